```python
import math
import jax
import jax.numpy as jnp
from jax import lax
import numpy as np

D_MODEL = 1024
BATCH = 4
SEQ = 8192
DEPTH = 2

GRID_W = 64
CTX_LEN = 256

N_MOD = 6
NORM_EPS = 1e-6
LN_EPS = 1e-5

A_WIDTH = 3 * D_MODEL // 4
A_HEAD_DIM = 64
A_HEADS = A_WIDTH // A_HEAD_DIM
A_RANK_W = 64
A_RANK_A = 64
A_RANK_G = 128
A_LNX_EPS = 64e-5
A_COLS = 3 * A_WIDTH + 2 * A_RANK_W + 2 * A_RANK_A + A_RANK_G
A_SPLITS = (A_WIDTH, 2 * A_WIDTH, 3 * A_WIDTH, 3 * A_WIDTH + 2 * A_RANK_W,
            3 * A_WIDTH + 2 * A_RANK_W + 2 * A_RANK_A)

B_WIDTH = D_MODEL - A_WIDTH
B_GROUPS = 4
B_GROUP_DIM = B_WIDTH // B_GROUPS
EVEN_IN_COLS = A_COLS + B_WIDTH

C_WIDTH = 3 * D_MODEL // 4
C_HEAD_DIM = 64
C_HEADS = C_WIDTH // C_HEAD_DIM
C_KV_HEADS = 4
C_GROUP = C_HEADS // C_KV_HEADS
KV_WIDTH = C_KV_HEADS * C_HEAD_DIM
ROPE_AXIS_DIM = C_HEAD_DIM // 2
ROPE_THETA = 10000.0
Q_BLOCK = 128
ATTN_SCALE = C_HEAD_DIM ** -0.5

D_WIDTH = D_MODEL - C_WIDTH
D_CONV_WIDTH = 31
D_PAD = D_CONV_WIDTH // 2
ODD_SPLITS = (C_WIDTH, C_WIDTH + KV_WIDTH, C_WIDTH + 2 * KV_WIDTH)
ODD_IN_COLS = C_WIDTH + 2 * KV_WIDTH + 2 * D_WIDTH

PK_HEADS = 8
PK_DIM = 256
PK_HALF = PK_DIM // 2
N_KEYS = 128
N_EXPERTS = N_KEYS * N_KEYS
PK_TOPK = 16
PEER_CHUNK = 256

kernel_name = "hybrid_rwkv7_fnet_gqa_conformer_peer_dit"


def rms_norm(x, g):
    xf = x.astype(jnp.float32)
    y = xf * lax.rsqrt(jnp.mean(xf * xf, axis=-1, keepdims=True) + NORM_EPS)
    return (y * g.astype(jnp.float32)).astype(x.dtype)


def layer_norm(x, g, b):
    xf = x.astype(jnp.float32)
    mu = jnp.mean(xf, axis=-1, keepdims=True)
    var = jnp.mean(jnp.square(xf - mu), axis=-1, keepdims=True)
    return ((xf - mu) * lax.rsqrt(var + LN_EPS) * g + b).astype(x.dtype)


def centred_shift(z, mu_prev, mu_next):
    z_prev = jnp.pad(z[:, :-1], ((0, 0), (1, 0), (0, 0)))
    z_next = jnp.pad(z[:, 1:], ((0, 0), (0, 1), (0, 0)))
    return z + mu_prev * (z_prev - z) + mu_next * (z_next - z)


def wkv7_scan(r, decay, k, v, a, b, s0, reverse):
    def step(state, inp):
        r_t, w_t, k_t, v_t, a_t, b_t = inp
        sa = jnp.einsum("bhvk,bhk->bhv", state, a_t)
        state = (state * w_t[:, :, None, :] + sa[..., None] * b_t[:, :, None, :]
                 + v_t[..., None] * k_t[:, :, None, :])
        return state, jnp.einsum("bhvk,bhk->bhv", state, r_t)
    xs = tuple(jnp.swapaxes(t, 0, 1) for t in (r, decay, k, v, a, b))
    state, y = lax.scan(step, s0, xs, reverse=reverse)
    return jnp.swapaxes(y, 0, 1), state


def rwkv7_terms(zr, w0, w2, a0, a2, g2, k_k, k_a, r_k):
    zr = zr.astype(jnp.float32)
    Bn, T, _ = zr.shape
    hd = lambda t: t.reshape(Bn, T, A_HEADS, A_HEAD_DIM)
    r, k, v, xw, xa, xg = jnp.split(zr, A_SPLITS, axis=-1)
    xw = xw.reshape(Bn, T, 2, A_RANK_W)
    xa = xa.reshape(Bn, T, 2, A_RANK_A)
    g = jax.nn.sigmoid(xg) @ g2.astype(jnp.float32)
    kk = hd(k * k_k)
    kk = kk / jnp.maximum(jnp.sqrt(jnp.sum(kk * kk, axis=-1, keepdims=True)), 1e-12)
    rh, vh = hd(r), hd(v)
    dirs = []
    bonus = 0.0
    for d in range(2):
        w_log = -jax.nn.softplus(-(w0[d] + jnp.tanh(xw[:, :, d]) @ w2[d])) - 0.5
        decay = jnp.exp(-jnp.exp(w_log))
        a_gate = jax.nn.sigmoid(a0[d] + xa[:, :, d] @ a2[d])
        k_d = hd(k * (1.0 + (a_gate - 1.0) * k_a))
        dirs.append((hd(decay), k_d, -kk, kk * hd(a_gate)))
        bonus = bonus + jnp.sum(rh * k_d * r_k, axis=-1, keepdims=True) * vh
    return rh, vh, g, dirs, bonus.reshape(Bn, T, A_WIDTH)


def head_group_norm(y, g, b):
    mu = jnp.mean(y, axis=-1, keepdims=True)
    var = jnp.mean(jnp.square(y - mu), axis=-1, keepdims=True)
    yn = (y - mu) * lax.rsqrt(var + A_LNX_EPS)
    Bn, T = y.shape[:2]
    return yn.reshape(Bn, T, A_WIDTH) * g + b


def fourier_mix(f):
    Bn, T, _ = f.shape
    fg = f.reshape(Bn, T, B_GROUPS, B_GROUP_DIM).astype(jnp.float32)
    y = jnp.fft.fftn(fg, axes=(1, 3), norm="ortho").real
    return y.reshape(Bn, T, B_WIDTH).astype(f.dtype)


def rwkv_fourier_mixer(h, hc, w_in, shift_prev, shift_next, w0, w2, a0, a2, g2,
                       k_k, k_a, r_k, lnx_g, lnx_b, w_out, need_ctx):
    def project(hs):
        z = hs @ w_in
        return centred_shift(z[..., :A_COLS], shift_prev, shift_next), z[..., A_COLS:]
    zc, fc = project(hc)
    zl, fl = project(h)
    rc, vc, gc, dirs_c, bonus_c = rwkv7_terms(zc, w0, w2, a0, a2, g2, k_k, k_a, r_k)
    rl, vl, gl, dirs_l, bonus_l = rwkv7_terms(zl, w0, w2, a0, a2, g2, k_k, k_a, r_k)
    s0 = jnp.zeros((h.shape[0], A_HEADS, A_HEAD_DIM, A_HEAD_DIM), jnp.float32)
    y_ctx = 0.0
    y_lat = 0.0
    for d, reverse in enumerate((False, True)):
        dec_c, k_c, a_c, b_c = dirs_c[d]
        dec_l, k_l, a_l, b_l = dirs_l[d]
        yc, s_ctx = wkv7_scan(rc, dec_c, k_c, vc, a_c, b_c, s0, reverse)
        yl, _ = wkv7_scan(rl, dec_l, k_l, vl, a_l, b_l, s_ctx, reverse)
        y_ctx = y_ctx + yc
        y_lat = y_lat + yl

    def finish(y, bonus, g, f):
        o = (head_group_norm(y, lnx_g, lnx_b) + bonus) * g
        return jnp.concatenate([o.astype(f.dtype), fourier_mix(f)], axis=-1) @ w_out

    out = finish(y_lat, bonus_l, gl, fl)
    if not need_ctx:
        return out, None
    return out, finish(y_ctx, bonus_c, gc, fc)


def axial_rope(n):
    rows = n // GRID_W
    row = jnp.broadcast_to(jnp.arange(rows)[:, None], (rows, GRID_W)).reshape(-1)
    col = jnp.broadcast_to(jnp.arange(GRID_W)[None, :], (rows, GRID_W)).reshape(-1)
    inv = ROPE_THETA ** (-jnp.arange(0, ROPE_AXIS_DIM, 2, dtype=jnp.float32) / ROPE_AXIS_DIM)
    ang = jnp.stack([row[:, None] * inv, col[:, None] * inv], axis=1)
    return jnp.cos(ang), jnp.sin(ang)


def apply_rope(x, cos, sin):
    Bn, T, H, _ = x.shape
    xr = x.astype(jnp.float32).reshape(Bn, T, H, 2, 2, ROPE_AXIS_DIM // 2)
    x1, x2 = xr[..., 0, :], xr[..., 1, :]
    c = cos[None, :, None]
    s = sin[None, :, None]
    out = jnp.stack([x1 * c - x2 * s, x2 * c + x1 * s], axis=-2)
    return out.reshape(Bn, T, H, C_HEAD_DIM).astype(x.dtype)


def attend(q, k, v):
    Bn, Tq, _, _ = q.shape
    nb = Tq // Q_BLOCK
    qb = q.reshape(Bn, nb, Q_BLOCK, C_KV_HEADS, C_GROUP, C_HEAD_DIM).transpose(1, 0, 2, 3, 4, 5)

    def block(qblk):
        s = jnp.einsum("bqkgd,blkd->bkgql", qblk, k).astype(jnp.float32) * ATTN_SCALE
        p = jax.nn.softmax(s, axis=-1).astype(v.dtype)
        return jnp.einsum("bkgql,blkd->bqkgd", p, v)

    o = lax.map(block, qb)
    return o.transpose(1, 0, 2, 3, 4, 5).reshape(Bn, Tq, C_WIDTH)


def conformer_conv(u, dw_w, dw_b, cn_g, cn_b):
    val, gate = jnp.split(u, 2, axis=-1)
    y = val * jax.nn.sigmoid(gate)
    y = lax.conv_general_dilated(y, dw_w[:, None, :], window_strides=(1,), padding=[(D_PAD, D_PAD)],
                                 dimension_numbers=("NWC", "WIO", "NWC"),
                                 feature_group_count=D_WIDTH) + dw_b
    return jax.nn.silu(layer_norm(y, cn_g, cn_b))


def attention_conv_mixer(h, hc, w_in, q_norm, k_norm, dw_w, dw_b, cn_g, cn_b, w_out, need_ctx):
    Bn, S, _ = h.shape
    CL = hc.shape[1]
    q, k, v, u = jnp.split(h @ w_in, ODD_SPLITS, axis=-1)
    q = rms_norm(q.reshape(Bn, S, C_HEADS, C_HEAD_DIM), q_norm)
    k = rms_norm(k.reshape(Bn, S, C_KV_HEADS, C_HEAD_DIM), k_norm)
    cos, sin = axial_rope(S)
    q = apply_rope(q, cos, sin)
    k = apply_rope(k, cos, sin)
    v = v.reshape(Bn, S, C_KV_HEADS, C_HEAD_DIM)
    kc, vc = jnp.split(hc @ w_in[:, C_WIDTH:C_WIDTH + 2 * KV_WIDTH], 2, axis=-1)
    kc = rms_norm(kc.reshape(Bn, CL, C_KV_HEADS, C_HEAD_DIM), k_norm)
    vc = vc.reshape(Bn, CL, C_KV_HEADS, C_HEAD_DIM)
    k_all = jnp.concatenate([k, kc], axis=1)
    v_all = jnp.concatenate([v, vc], axis=1)
    out = jnp.concatenate([attend(q, k_all, v_all),
                           conformer_conv(u, dw_w, dw_b, cn_g, cn_b)], axis=-1) @ w_out
    if not need_ctx:
        return out, None
    qc = rms_norm((hc @ w_in[:, :C_WIDTH]).reshape(Bn, CL, C_HEADS, C_HEAD_DIM), q_norm)
    uc = hc @ w_in[:, ODD_SPLITS[2]:]
    out_c = jnp.concatenate([attend(qc, kc, vc),
                             conformer_conv(uc, dw_w, dw_b, cn_g, cn_b)], axis=-1) @ w_out
    return out, out_c


def peer_ffn(h, pq, sk1, sk2, pu, pv):
    shape = h.shape
    D = shape[-1]
    t = h.reshape(-1, D)
    n = t.shape[0]
    t = jnp.pad(t, ((0, (-n) % PEER_CHUNK), (0, 0))).reshape(-1, PEER_CHUNK, D)

    def chunk(tc):
        q = (tc @ pq).reshape(PEER_CHUNK, PK_HEADS, 2, PK_HALF).astype(jnp.float32)
        s1 = jnp.einsum("thd,hnd->thn", q[:, :, 0], sk1.astype(jnp.float32))
        s2 = jnp.einsum("thd,hnd->thn", q[:, :, 1], sk2.astype(jnp.float32))
        v1, i1 = lax.top_k(s1, PK_TOPK)
        v2, i2 = lax.top_k(s2, PK_TOPK)
        cand = (v1[..., :, None] + v2[..., None, :]).reshape(PEER_CHUNK, PK_HEADS, PK_TOPK * PK_TOPK)
        score, ci = lax.top_k(cand, PK_TOPK)
        e1 = jnp.take_along_axis(i1, ci // PK_TOPK, axis=-1)
        e2 = jnp.take_along_axis(i2, ci % PK_TOPK, axis=-1)
        expert = e1 * N_KEYS + e2
        gate = jax.nn.softmax(score, axis=-1)
        act = jax.nn.gelu(jnp.einsum("thkd,td->thk", pu[expert], tc).astype(jnp.float32),
                          approximate=False) * gate
        return jnp.einsum("thk,thkd->td", act.astype(pv.dtype), pv[expert])

    out = lax.map(chunk, t).reshape(-1, D)[:n]
    return out.reshape(shape).astype(h.dtype)


def trunk_layer(x, ctx, c, c_ctx, mixer, mod_w, mod_b, norm1, mix_p, norm2, peer_p, last):
    sh1, sc1, g1, sh2, sc2, g2 = jnp.split((jax.nn.silu(c) @ mod_w + mod_b)[:, None, :], N_MOD, axis=-1)
    csh1, csc1, cg1, csh2, csc2, cg2 = jnp.split(jax.nn.silu(c_ctx) @ mod_w + mod_b, N_MOD, axis=-1)
    h = rms_norm(x, norm1) * (1.0 + sc1) + sh1
    hc = rms_norm(ctx, norm1) * (1.0 + csc1) + csh1
    mix, mix_c = mixer(h, hc, *mix_p, need_ctx=not last)
    x = x + g1 * mix
    x = x + g2 * peer_ffn(rms_norm(x, norm2) * (1.0 + sc2) + sh2, *peer_p)
    if last:
        return x, None
    ctx = ctx + cg1 * mix_c
    ctx = ctx + cg2 * peer_ffn(rms_norm(ctx, norm2) * (1.0 + csc2) + csh2, *peer_p)
    return x, ctx


def setup_inputs(seed: int = 0) -> dict:
    key = jax.random.key(seed)
    keys = iter(jax.random.split(key, 64))
    nrm = lambda shape, s: s * jax.random.normal(next(keys), shape, jnp.float32)
    uni = lambda shape, lo, hi: jax.random.uniform(next(keys), shape, jnp.float32, lo, hi)
    D = D_MODEL
    fan = D ** -0.5
    return {
        "x": nrm((BATCH, SEQ, D), 1.0),
        "c": nrm((BATCH, D), 1.0),
        "ctx": nrm((BATCH, CTX_LEN, D), 1.0),
        "c_ctx": nrm((D,), 1.0),
        "l0_mod_w": nrm((D, N_MOD * D), 0.5 * fan),
        "l0_mod_b": nrm((N_MOD * D,), 0.02),
        "l0_norm1": 1.0 + nrm((D,), 0.02),
        "l0_w_in": nrm((D, EVEN_IN_COLS), fan),
        "l0_shift_prev": uni((A_COLS,), 0.0, 0.5),
        "l0_shift_next": uni((A_COLS,), 0.0, 0.5),
        "l0_w0": uni((2, A_WIDTH), -4.0, 0.0),
        "l0_w2": nrm((2, A_RANK_W, A_WIDTH), 0.1 * A_RANK_W ** -0.5),
        "l0_a0": nrm((2, A_WIDTH), 0.1),
        "l0_a2": nrm((2, A_RANK_A, A_WIDTH), 0.3 * A_RANK_A ** -0.5),
        "l0_g2": nrm((A_RANK_G, A_WIDTH), A_RANK_G ** -0.5),
        "l0_k_k": 0.85 + nrm((A_WIDTH,), 0.05),
        "l0_k_a": 1.0 + nrm((A_WIDTH,), 0.05),
        "l0_r_k": nrm((A_HEADS, A_HEAD_DIM), 0.1),
        "l0_lnx_g": 1.0 + nrm((A_WIDTH,), 0.02),
        "l0_lnx_b": nrm((A_WIDTH,), 0.02),
        "l0_w_out": nrm((A_WIDTH + B_WIDTH, D), fan),
        "l0_norm2": 1.0 + nrm((D,), 0.02),
        "l0_pq": nrm((D, PK_HEADS * PK_DIM), fan),
        "l0_sk1": nrm((PK_HEADS, N_KEYS, PK_HALF), PK_HALF ** -0.5),
        "l0_sk2": nrm((PK_HEADS, N_KEYS, PK_HALF), PK_HALF ** -0.5),
        "l0_pu": nrm((N_EXPERTS, D), fan),
        "l0_pv": nrm((N_EXPERTS, D), 0.5),
        "l1_mod_w": nrm((D, N_MOD * D), 0.5 * fan),
        "l1_mod_b": nrm((N_MOD * D,), 0.02),
        "l1_norm1": 1.0 + nrm((D,), 0.02),
        "l1_w_in": nrm((D, ODD_IN_COLS), fan),
        "l1_q_norm": 1.0 + nrm((C_HEAD_DIM,), 0.02),
        "l1_k_norm": 1.0 + nrm((C_HEAD_DIM,), 0.02),
        "l1_dw_w": nrm((D_CONV_WIDTH, D_WIDTH), D_CONV_WIDTH ** -0.5),
        "l1_dw_b": nrm((D_WIDTH,), 0.02),
        "l1_cn_g": 1.0 + nrm((D_WIDTH,), 0.02),
        "l1_cn_b": nrm((D_WIDTH,), 0.02),
        "l1_w_out": nrm((C_WIDTH + D_WIDTH, D), fan),
        "l1_norm2": 1.0 + nrm((D,), 0.02),
        "l1_pq": nrm((D, PK_HEADS * PK_DIM), fan),
        "l1_sk1": nrm((PK_HEADS, N_KEYS, PK_HALF), PK_HALF ** -0.5),
        "l1_sk2": nrm((PK_HEADS, N_KEYS, PK_HALF), PK_HALF ** -0.5),
        "l1_pu": nrm((N_EXPERTS, D), fan),
        "l1_pv": nrm((N_EXPERTS, D), 0.5),
        "norm_f": 1.0 + nrm((D,), 0.02),
    }


def reference(x, c, ctx, c_ctx,
              l0_mod_w, l0_mod_b, l0_norm1, l0_w_in, l0_shift_prev, l0_shift_next,
              l0_w0, l0_w2, l0_a0, l0_a2, l0_g2, l0_k_k, l0_k_a, l0_r_k, l0_lnx_g, l0_lnx_b,
              l0_w_out, l0_norm2, l0_pq, l0_sk1, l0_sk2, l0_pu, l0_pv,
              l1_mod_w, l1_mod_b, l1_norm1, l1_w_in, l1_q_norm, l1_k_norm, l1_dw_w, l1_dw_b,
              l1_cn_g, l1_cn_b, l1_w_out, l1_norm2, l1_pq, l1_sk1, l1_sk2, l1_pu, l1_pv,
              norm_f):
    layers = (
        (rwkv_fourier_mixer, l0_mod_w, l0_mod_b, l0_norm1,
         (l0_w_in, l0_shift_prev, l0_shift_next, l0_w0, l0_w2, l0_a0, l0_a2, l0_g2,
          l0_k_k, l0_k_a, l0_r_k, l0_lnx_g, l0_lnx_b, l0_w_out),
         l0_norm2, (l0_pq, l0_sk1, l0_sk2, l0_pu, l0_pv)),
        (attention_conv_mixer, l1_mod_w, l1_mod_b, l1_norm1,
         (l1_w_in, l1_q_norm, l1_k_norm, l1_dw_w, l1_dw_b, l1_cn_g, l1_cn_b, l1_w_out),
         l1_norm2, (l1_pq, l1_sk1, l1_sk2, l1_pu, l1_pv)),
    )
    for i in range(DEPTH):
        mixer, mod_w, mod_b, norm1, mix_p, norm2, peer_p = layers[i]
        x, ctx = trunk_layer(x, ctx, c, c_ctx, mixer, mod_w, mod_b, norm1, mix_p, norm2, peer_p,
                             i == DEPTH - 1)
    return rms_norm(x, norm_f)
```

```python
import functools
import math

import jax
import jax.numpy as jnp
import numpy as np
from jax import lax
from jax.experimental import pallas as pl
from jax.experimental.pallas import tpu as pltpu

F32 = jnp.float32
BF16 = jnp.bfloat16
HIGHEST = lax.Precision.HIGHEST

LANES = 128
SUBLANES = 8
VMEM_LIMIT_BYTES = 56 * 1024 * 1024

N_MOD = 6
NORM_EPS = 1e-6
LN_EPS = 1e-5
GRID_W = 64
HEAD_DIM = 64
A_WIDTH = 768
A_HEADS = A_WIDTH // HEAD_DIM
A_RANK_W = 64
A_RANK_A = 64
A_RANK_G = 128
A_LNX_EPS = 64e-5
A_COLS = 3 * A_WIDTH + 2 * A_RANK_W + 2 * A_RANK_A + A_RANK_G
B_WIDTH = 256
B_GROUP_DIM = 64
C_WIDTH = 768
C_HEADS = 12
C_KV_HEADS = 4
C_GROUP = C_HEADS // C_KV_HEADS
KV_WIDTH = C_KV_HEADS * HEAD_DIM
ROPE_AXIS_DIM = HEAD_DIM // 2
ROPE_THETA = 10000.0
ATTN_SCALE = HEAD_DIM ** -0.5
D_WIDTH = 256
D_CONV_WIDTH = 31
D_PAD = D_CONV_WIDTH // 2
PK_HEADS = 8
PK_DIM = 256
PK_HALF = 128
N_KEYS = 128
PK_TOPK = 16

SCAN_CHUNK = 64
NEG_INF = float("-inf")


def _cparams(semantics):
    return pltpu.CompilerParams(dimension_semantics=semantics,
                                vmem_limit_bytes=VMEM_LIMIT_BYTES)


def _split_bf16(x):
    hi = x.astype(BF16)
    lo = (x - hi.astype(F32)).astype(BF16)
    return hi, lo


def _dot(a, b):
    return jnp.dot(a, b, preferred_element_type=F32)


def _dot_hp(a, b):
    return jnp.dot(a, b, preferred_element_type=F32, precision=HIGHEST)


def _dot_nt(a, b, precision=None):
    return lax.dot_general(a, b, (((1,), (1,)), ((), ())),
                           preferred_element_type=F32, precision=precision)


def _dot_tn(a, b, precision=None):
    return lax.dot_general(a, b, (((0,), (0,)), ((), ())),
                           preferred_element_type=F32, precision=precision)


def _segsum(x, ones_bd):
    hi, lo = _split_bf16(x)
    return _dot(hi, ones_bd) + _dot(lo, ones_bd)


def _block_ones(width, seg):
    r = np.arange(width) // seg
    return jnp.asarray((r[:, None] == r[None, :]).astype(np.float32), dtype=BF16)


def _mod_kernel(c_ref, w_ref, b_ref, o_ref):
    c = c_ref[...]
    s = c * jax.nn.sigmoid(c)
    o_ref[...] = _dot_hp(s, w_ref[...]) + b_ref[...]


def _modulation(c, c_ctx, mod_w, mod_b):
    bsz, d = c.shape
    rows = SUBLANES * pl.cdiv(bsz + 1, SUBLANES)
    cc = jnp.zeros((rows, d), F32).at[:bsz].set(c).at[bsz].set(c_ctx)
    n = mod_w.shape[1]
    tn = n // 4
    out = pl.pallas_call(
        _mod_kernel,
        grid=(n // tn,),
        in_specs=[pl.BlockSpec((rows, d), lambda j: (0, 0)),
                  pl.BlockSpec((d, tn), lambda j: (0, j)),
                  pl.BlockSpec((1, tn), lambda j: (0, j))],
        out_specs=pl.BlockSpec((rows, tn), lambda j: (0, j)),
        out_shape=jax.ShapeDtypeStruct((rows, n), F32),
        compiler_params=_cparams(("arbitrary",)),
        name="adaln_mod",
    )(cc, mod_w, mod_b.reshape(1, n))
    lat = out[:bsz].reshape(bsz, N_MOD, d)
    ctx = jnp.broadcast_to(out[bsz].reshape(1, N_MOD, d), (bsz, N_MOD, d))
    return jnp.stack([ctx, lat], axis=1)


def _norm_mod(x, g, mod, row):
    ms = jnp.mean(x * x, axis=-1, keepdims=True)
    y = x * lax.rsqrt(ms + NORM_EPS) * g
    return y * (1.0 + mod[row + 1:row + 2]) + mod[row:row + 1]


def _normmod_matmul_kernel(x_ref, g_ref, mod_ref, w_ref, o_ref):
    h = _norm_mod(x_ref[0], g_ref[...], mod_ref[0, 0], 0)
    o_ref[0] = _dot(h.astype(BF16), w_ref[...])


def _normmod_matmul(x, g, mod, w, tm, ctx_tiles):
    bsz, rows, d = x.shape
    n = w.shape[1]
    return pl.pallas_call(
        _normmod_matmul_kernel,
        grid=(bsz, rows // tm),
        in_specs=[pl.BlockSpec((1, tm, d), lambda b, t: (b, t, 0)),
                  pl.BlockSpec((1, d), lambda b, t: (0, 0)),
                  pl.BlockSpec((1, 1, N_MOD, d),
                               lambda b, t: (b, jnp.where(t >= ctx_tiles, 1, 0), 0, 0)),
                  pl.BlockSpec((d, n), lambda b, t: (0, 0))],
        out_specs=pl.BlockSpec((1, tm, n), lambda b, t: (b, t, 0)),
        out_shape=jax.ShapeDtypeStruct((bsz, rows, n), F32),
        compiler_params=_cparams(("parallel", "parallel")),
        name="normmod_proj",
    )(x, g.reshape(1, d), mod, w.astype(BF16))


def _rwkv_prep_kernel(z_ref, zp_ref, zn_ref, mup_ref, mun_ref, w0_ref, w2_ref, a0_ref, a2_ref,
                      g2_ref, kk_ref, ka_ref, rk_ref, ones_ref,
                      r_o, v_o, g_o, bonus_o, kkn_o, lwf_o, kdf_o, bf_o, lwr_o, kdr_o, br_o,
                      *, tm, ctx_tiles):
    t = pl.program_id(1)
    nt = pl.num_programs(1)
    z = z_ref[0][:, :A_COLS]
    zp_row = zp_ref[0][SUBLANES - 1:SUBLANES, :A_COLS]
    zn_row = zn_ref[0][0:1, :A_COLS]
    first = jnp.logical_or(t == 0, t == ctx_tiles)
    last = jnp.logical_or(t == ctx_tiles - 1, t == nt - 1)
    zp_row = jnp.where(first, 0.0, zp_row)
    zn_row = jnp.where(last, 0.0, zn_row)
    ridx = lax.broadcasted_iota(jnp.int32, (tm, 1), 0)
    z_prev = jnp.where(ridx == 0, zp_row, pltpu.roll(z, 1, 0))
    z_next = jnp.where(ridx == tm - 1, zn_row, pltpu.roll(z, tm - 1, 0))
    zs = z + mup_ref[...] * (z_prev - z) + mun_ref[...] * (z_next - z)

    w = A_WIDTH
    r = zs[:, 0:w]
    k = zs[:, w:2 * w]
    v = zs[:, 2 * w:3 * w]
    o = 3 * w
    xw = zs[:, o:o + 2 * A_RANK_W]
    o += 2 * A_RANK_W
    xa = zs[:, o:o + 2 * A_RANK_A]
    o += 2 * A_RANK_A
    xg = zs[:, o:o + A_RANK_G]
    ones_bd = ones_ref[...]

    g_o[0] = _dot_hp(jax.nn.sigmoid(xg), g2_ref[...])
    kk = k * kk_ref[...]
    nrm = jnp.sqrt(_segsum(kk * kk, ones_bd))
    kk = kk / jnp.maximum(nrm, 1e-12)
    r_o[0] = r
    v_o[0] = v
    kkn_o[0] = kk
    bonus = jnp.zeros_like(r)
    tw = jnp.tanh(xw)
    for d, (lw_o, kd_o, b_o) in enumerate(((lwf_o, kdf_o, bf_o), (lwr_o, kdr_o, br_o))):
        wl = w0_ref[d:d + 1, :] + _dot_hp(tw[:, d * A_RANK_W:(d + 1) * A_RANK_W], w2_ref[d])
        w_log = -jax.nn.softplus(-wl) - 0.5
        lw_o[0] = -jnp.exp(w_log)
        a_gate = jax.nn.sigmoid(
            a0_ref[d:d + 1, :] + _dot_hp(xa[:, d * A_RANK_A:(d + 1) * A_RANK_A], a2_ref[d]))
        k_d = k * (1.0 + (a_gate - 1.0) * ka_ref[...])
        kd_o[0] = k_d
        b_o[0] = kk * a_gate
        bonus = bonus + _segsum(r * k_d * rk_ref[...], ones_bd) * v
    bonus_o[0] = bonus


def _rwkv_prep(z, p, tm, ctx_tiles):
    bsz, rows, ncol = z.shape
    w = A_WIDTH
    hb = tm // SUBLANES
    nblk8 = rows // SUBLANES
    row_spec = lambda width: pl.BlockSpec((1, width), lambda b, t: (0, 0))
    full = lambda shape: pl.BlockSpec(shape, lambda b, t: (0,) * len(shape))
    out_spec = pl.BlockSpec((1, tm, w), lambda b, t: (b, t, 0))
    out_shape = jax.ShapeDtypeStruct((bsz, rows, w), F32)
    kern = functools.partial(_rwkv_prep_kernel, tm=tm, ctx_tiles=ctx_tiles)
    return pl.pallas_call(
        kern,
        grid=(bsz, rows // tm),
        in_specs=[pl.BlockSpec((1, tm, ncol), lambda b, t: (b, t, 0)),
                  pl.BlockSpec((1, SUBLANES, ncol),
                               lambda b, t: (b, jnp.maximum(t * hb - 1, 0), 0)),
                  pl.BlockSpec((1, SUBLANES, ncol),
                               lambda b, t: (b, jnp.minimum((t + 1) * hb, nblk8 - 1), 0)),
                  row_spec(A_COLS), row_spec(A_COLS),
                  full((2, w)), full((2, A_RANK_W, w)), full((2, w)), full((2, A_RANK_A, w)),
                  full((A_RANK_G, w)), row_spec(w), row_spec(w), row_spec(w), full((w, w))],
        out_specs=[out_spec] * 11,
        out_shape=[out_shape] * 11,
        compiler_params=_cparams(("parallel", "parallel")),
        name="rwkv_prep",
    )(z, z, z, p["shift_prev"].reshape(1, -1), p["shift_next"].reshape(1, -1),
      p["w0"], p["w2"], p["a0"], p["a2"], p["g2"], p["k_k"].reshape(1, w),
      p["k_a"].reshape(1, w), p["r_k"].reshape(1, w), _block_ones(w, HEAD_DIM))


def _scan_chunk(r, v, kk, lw, kd, b, s, tri_incl, tri_strict, edge_row):
    c = r.shape[0]
    lcum = _dot_hp(tri_incl, lw)
    p_in = jnp.exp(lcum)
    p_ex = jnp.exp(lcum - lw)
    p_inv = jnp.exp(-lcum)
    at = -kk * p_ex
    rt = r * p_in
    bt = b * p_inv
    kt = kd * p_inv
    lhs = jnp.concatenate([at, rt], axis=0)
    rhs = jnp.concatenate([bt, kt], axis=0)
    g = _dot_nt(lhs, rhs, HIGHEST)
    xs = _dot_nt(lhs, s, HIGHEST)
    a_ab = jnp.where(tri_strict, g[:c, :c], 0.0)
    a_ak = jnp.where(tri_strict, g[:c, c:], 0.0)
    a_rb = jnp.where(tri_incl > 0, g[c:, :c], 0.0)
    a_rk = jnp.where(tri_incl > 0, g[c:, c:], 0.0)
    x = xs[:c] + _dot_hp(a_ak, v)
    apow = a_ab
    steps = int(math.log2(c))
    for i in range(steps):
        x = x + _dot_hp(apow, x)
        if i + 1 < steps:
            apow = _dot_hp(apow, apow)
    sav = jnp.concatenate([x, v], axis=0)
    y = xs[c:] + _dot_hp(jnp.concatenate([a_rb, a_rk], axis=1), sav)
    p_end = p_in[edge_row:edge_row + 1, :]
    s_new = (s + _dot_tn(sav, rhs, HIGHEST)) * p_end
    return y, s_new


def _scan_kernel(rf, vf, kkf, lwf, kdf, bf, rr, vr, kkr, lwr, kdr, br, yf_o, yr_o, s_ref, *, c):
    @pl.when(pl.program_id(2) == 0)
    def _():
        s_ref[...] = jnp.zeros_like(s_ref)

    ri = lax.broadcasted_iota(jnp.int32, (c, c), 0)
    ci = lax.broadcasted_iota(jnp.int32, (c, c), 1)
    lower = (ci <= ri)
    upper = (ci >= ri)
    dirs = ((rf, vf, kkf, lwf, kdf, bf, yf_o, lower.astype(F32), ci < ri, c - 1),
            (rr, vr, kkr, lwr, kdr, br, yr_o, upper.astype(F32), ci > ri, 0))
    for d, (r_, v_, kk_, lw_, kd_, b_, y_o, tri_incl, tri_strict, edge) in enumerate(dirs):
        ys = []
        for h in range(LANES // HEAD_DIM):
            sl = slice(h * HEAD_DIM, (h + 1) * HEAD_DIM)
            y, s_new = _scan_chunk(r_[0][:, sl], v_[0][:, sl], kk_[0][:, sl], lw_[0][:, sl],
                                   kd_[0][:, sl], b_[0][:, sl], s_ref[d, h],
                                   tri_incl, tri_strict, edge)
            s_ref[d, h] = s_new
            ys.append(y)
        y_o[0] = jnp.concatenate(ys, axis=1)


def _rwkv_scan(r, v, kk, lwf, kdf, bf, lwr, kdr, br, ctx_len):
    bsz, rows, w = r.shape
    c = SCAN_CHUNK
    nch = rows // c
    cch = ctx_len // c
    fwd = lambda b, hp, i: (b, i, hp)
    rev = lambda b, hp, i: (b, jnp.where(i < cch, cch - 1 - i, nch - 1 + cch - i), hp)
    blk = lambda im: pl.BlockSpec((1, c, LANES), im)
    kern = functools.partial(_scan_kernel, c=c)
    out_shape = jax.ShapeDtypeStruct((bsz, rows, w), F32)
    return pl.pallas_call(
        kern,
        grid=(bsz, w // LANES, nch),
        in_specs=[blk(fwd)] * 6 + [blk(rev)] * 6,
        out_specs=[blk(fwd), blk(rev)],
        out_shape=[out_shape, out_shape],
        scratch_shapes=[pltpu.VMEM((2, LANES // HEAD_DIM, HEAD_DIM, HEAD_DIM), F32)],
        compiler_params=_cparams(("parallel", "parallel", "arbitrary")),
        name="rwkv_scan",
    )(r, v, kk, lwf, kdf, bf, r, v, kk, lwr, kdr, br)


def _rwkv_finish_kernel(yf, yr, bonus, g, lg, lb, ones_ref, o_ref):
    y = yf[0] + yr[0]
    ones_bd = ones_ref[...]
    inv = 1.0 / HEAD_DIM
    mu = _segsum(y, ones_bd) * inv
    yc = y - mu
    var = _segsum(yc * yc, ones_bd) * inv
    yn = yc * lax.rsqrt(var + A_LNX_EPS) * lg[...] + lb[...]
    o_ref[0] = ((yn + bonus[0]) * g[0]).astype(o_ref.dtype)


def _rwkv_finish(yf, yr, bonus, g, lnx_g, lnx_b, tm):
    bsz, rows, w = yf.shape
    blk = pl.BlockSpec((1, tm, w), lambda b, t: (b, t, 0))
    row = pl.BlockSpec((1, w), lambda b, t: (0, 0))
    return pl.pallas_call(
        _rwkv_finish_kernel,
        grid=(bsz, rows // tm),
        in_specs=[blk, blk, blk, blk, row, row, pl.BlockSpec((w, w), lambda b, t: (0, 0))],
        out_specs=blk,
        out_shape=jax.ShapeDtypeStruct((bsz, rows, w), BF16),
        compiler_params=_cparams(("parallel", "parallel")),
        name="rwkv_finish",
    )(yf, yr, bonus, g, lnx_g.reshape(1, w), lnx_b.reshape(1, w), _block_ones(w, HEAD_DIM))


def _fnet_chan_kernel(z_ref, m_ref, o_ref):
    f = z_ref[0][:, A_COLS:A_COLS + B_WIDTH]
    o_ref[0] = _dot(f.astype(BF16), m_ref[...]).astype(o_ref.dtype)


def _fnet_chan(z, tm):
    bsz, rows, ncol = z.shape
    j = np.arange(B_WIDTH)
    same = (j[:, None] // B_GROUP_DIM) == (j[None, :] // B_GROUP_DIM)
    ang = 2.0 * np.pi * ((j[:, None] % B_GROUP_DIM) * (j[None, :] % B_GROUP_DIM) % B_GROUP_DIM) / B_GROUP_DIM
    m = np.concatenate([np.where(same, np.cos(ang), 0.0), np.where(same, np.sin(ang), 0.0)], axis=1)
    return pl.pallas_call(
        _fnet_chan_kernel,
        grid=(bsz, rows // tm),
        in_specs=[pl.BlockSpec((1, tm, ncol), lambda b, t: (b, t, 0)),
                  pl.BlockSpec((B_WIDTH, 2 * B_WIDTH), lambda b, t: (0, 0))],
        out_specs=pl.BlockSpec((1, tm, 2 * B_WIDTH), lambda b, t: (b, t, 0)),
        out_shape=jax.ShapeDtypeStruct((bsz, rows, 2 * B_WIDTH), BF16),
        compiler_params=_cparams(("parallel", "parallel")),
        name="fnet_chan",
    )(z, jnp.asarray(m, dtype=BF16))


def _fnet_seq_kernel(fcs_ref, cb_ref, sb_ref, c0_ref, s0_ref, o_ref, acc_ref, *, scale, nb):
    k = pl.program_id(1)

    @pl.when(k == 0)
    def _():
        acc_ref[...] = jnp.zeros_like(acc_ref)

    cb = cb_ref[...]
    sb = sb_ref[...]
    c0 = c0_ref[0]
    s0 = s0_ref[0]
    cm = (c0 * cb - s0 * sb).astype(BF16)
    sm = (s0 * cb + c0 * sb).astype(BF16)
    for b in range(nb):
        fcs = fcs_ref[b]
        acc_ref[b] += _dot(cm, fcs[:, :B_WIDTH]) - _dot(sm, fcs[:, B_WIDTH:])

    @pl.when(k == pl.num_programs(1) - 1)
    def _():
        o_ref[...] = (acc_ref[...] * scale).astype(o_ref.dtype)


def _fnet_seq(fcs, row0, length, ts, tk):
    bsz = fcs.shape[0]
    ds = jnp.arange(ts, dtype=jnp.int32)[:, None]
    tt = jnp.arange(length, dtype=jnp.int32)[None, :]
    ang = ((ds * tt) % length).astype(F32) * (2.0 * math.pi / length)
    cb, sb = jnp.cos(ang), jnp.sin(ang)
    s0 = (jnp.arange(length // ts, dtype=jnp.int32) * ts)[:, None]
    ang0 = ((s0 * tt) % length).astype(F32) * (2.0 * math.pi / length)
    c0, sn0 = jnp.cos(ang0)[:, None, :], jnp.sin(ang0)[:, None, :]
    kern = functools.partial(_fnet_seq_kernel, scale=1.0 / math.sqrt(length * B_GROUP_DIM), nb=bsz)
    koff = row0 // tk
    return pl.pallas_call(
        kern,
        grid=(length // ts, length // tk),
        in_specs=[pl.BlockSpec((bsz, tk, 2 * B_WIDTH), lambda s, k: (0, k + koff, 0)),
                  pl.BlockSpec((ts, tk), lambda s, k: (0, k)),
                  pl.BlockSpec((ts, tk), lambda s, k: (0, k)),
                  pl.BlockSpec((1, 1, tk), lambda s, k: (s, 0, k)),
                  pl.BlockSpec((1, 1, tk), lambda s, k: (s, 0, k))],
        out_specs=pl.BlockSpec((bsz, ts, B_WIDTH), lambda s, k: (0, s, 0)),
        out_shape=jax.ShapeDtypeStruct((bsz, length, B_WIDTH), BF16),
        scratch_shapes=[pltpu.VMEM((bsz, ts, B_WIDTH), F32)],
        compiler_params=_cparams(("parallel", "arbitrary")),
        name="fnet_seq",
    )(fcs, cb, sb, c0, sn0)


def _proj_residual_kernel(o1_ref, o2_ref, x_ref, mod_ref, w1_ref, w2_ref, out_ref):
    mix = _dot(o1_ref[0], w1_ref[...]) + _dot(o2_ref[0], w2_ref[...])
    gate = mod_ref[0, 0][2:3]
    out_ref[0] = x_ref[0] + gate * mix


def _proj_residual(o1, o2, x, mod, w_out, tm, tile_off, ctx_tiles):
    bsz, rows, w1 = o1.shape
    w2 = o2.shape[2]
    d = x.shape[2]
    return pl.pallas_call(
        _proj_residual_kernel,
        grid=(bsz, rows // tm),
        in_specs=[pl.BlockSpec((1, tm, w1), lambda b, t: (b, t, 0)),
                  pl.BlockSpec((1, tm, w2), lambda b, t: (b, t, 0)),
                  pl.BlockSpec((1, tm, d), lambda b, t: (b, t + tile_off, 0)),
                  pl.BlockSpec((1, 1, N_MOD, d),
                               lambda b, t: (b, jnp.where(t + tile_off >= ctx_tiles, 1, 0), 0, 0)),
                  pl.BlockSpec((w1, d), lambda b, t: (0, 0)),
                  pl.BlockSpec((w2, d), lambda b, t: (0, 0))],
        out_specs=pl.BlockSpec((1, tm, d), lambda b, t: (b, t, 0)),
        out_shape=jax.ShapeDtypeStruct((bsz, rows, d), F32),
        compiler_params=_cparams(("parallel", "parallel")),
        name="proj_residual",
    )(o1, o2, x, mod, w_out[:w1].astype(BF16), w_out[w1:].astype(BF16))


def _top16_rows(s):
    rows = []
    for _ in range(PK_TOPK):
        m = jnp.max(s, axis=0, keepdims=True)
        rows.append(m)
        s = jnp.where(s >= m, NEG_INF, s)
    return rows


_PAIR_IDX = [(i, j) for i in range(PK_TOPK) for j in range(PK_TOPK) if (i + 1) * (j + 1) <= PK_TOPK]


def _peer_kernel(x_ref, g_ref, mod_ref, pqh_ref, pql_ref, sk1_ref, sk2_ref, pu_ref, pvt_ref, gf_ref,
                 out_ref, ht_ref, s1_ref, s2_ref, e2_ref, w1_ref, tau_ref, cand_ref, acc_ref,
                 *, tm, e1_per_blk, final_norm):
    eb = pl.program_id(2)

    @pl.when(eb == 0)
    def _():
        h = _norm_mod(x_ref[0], g_ref[...], mod_ref[0, 0], 3)
        ht = h.T
        hhi, hlo = _split_bf16(ht)
        ht_ref[...] = hhi
        qt = _dot(pqh_ref[...], hhi) + _dot(pqh_ref[...], hlo) + _dot(pql_ref[...], hhi)
        for hd in range(PK_HEADS):
            q1 = qt[hd * PK_DIM:hd * PK_DIM + PK_HALF]
            q2 = qt[hd * PK_DIM + PK_HALF:(hd + 1) * PK_DIM]
            s1 = _dot_hp(sk1_ref[hd], q1)
            s2 = _dot_hp(sk2_ref[hd], q2)
            v1 = _top16_rows(s1)
            v2 = _top16_rows(s2)
            cand_ref[...] = jnp.full_like(cand_ref, NEG_INF)
            for n, (i, j) in enumerate(_PAIR_IDX):
                cand_ref[n:n + 1, :] = v1[i] + v2[j]
            cand = cand_ref[...]
            top = v1[0] + v2[0]
            zsum = jnp.zeros_like(top)
            tau = top
            for _ in range(PK_TOPK):
                m = jnp.max(cand, axis=0, keepdims=True)
                zsum = zsum + jnp.exp(m - top)
                tau = m
                cand = jnp.where(cand >= m, NEG_INF, cand)
            s1_ref[hd] = s1
            s2_ref[hd] = s2
            e2_ref[hd] = jnp.exp(s2 - v2[0])
            w1_ref[hd] = jnp.exp(s1 - v1[0]) / zsum
            tau_ref[hd] = jnp.broadcast_to(tau, (SUBLANES, tm))
        acc_ref[...] = jnp.zeros_like(acc_ref)

    hu = _dot(pu_ref[...], ht_ref[...])
    acts = []
    for j in range(e1_per_blk):
        e1 = eb * e1_per_blk + j
        gate = jnp.zeros((N_KEYS, tm), F32)
        for hd in range(PK_HEADS):
            s1row = s1_ref[hd, pl.ds(e1, 1), :]
            w1row = w1_ref[hd, pl.ds(e1, 1), :]
            score = s2_ref[hd] + s1row
            gate = gate + jnp.where(score >= tau_ref[hd, 0:1, :], e2_ref[hd] * w1row, 0.0)
        u = hu[j * N_KEYS:(j + 1) * N_KEYS]
        act = 0.5 * u * (1.0 + lax.erf(u * (1.0 / math.sqrt(2.0))))
        acts.append((act * gate).astype(BF16))
    acc_ref[...] += _dot(pvt_ref[...], jnp.concatenate(acts, axis=0))

    @pl.when(eb == pl.num_programs(2) - 1)
    def _():
        y = x_ref[0] + mod_ref[0, 0][5:6] * acc_ref[...].T
        if final_norm:
            ms = jnp.mean(y * y, axis=-1, keepdims=True)
            y = y * lax.rsqrt(ms + NORM_EPS) * gf_ref[...]
        out_ref[0] = y


def _peer(x, g, mod, p, tm, tile_off, ctx_tiles, final_norm, norm_f):
    bsz, rows, d = x.shape
    n_exp = p["pu"].shape[0]
    e1_per_blk = 4
    eblk = e1_per_blk * N_KEYS
    pqt = p["pq"].T
    pqh = pqt.astype(BF16)
    pql = (pqt - pqh.astype(F32)).astype(BF16)
    nq = pqt.shape[0]
    kern = functools.partial(_peer_kernel, tm=tm, e1_per_blk=e1_per_blk, final_norm=final_norm)
    const = lambda shape: pl.BlockSpec(shape, lambda b, t, e: (0,) * len(shape))
    return pl.pallas_call(
        kern,
        grid=(bsz, rows // tm, n_exp // eblk),
        in_specs=[pl.BlockSpec((1, tm, d), lambda b, t, e: (b, t, 0)),
                  const((1, d)),
                  pl.BlockSpec((1, 1, N_MOD, d),
                               lambda b, t, e: (b, jnp.where(t + tile_off >= ctx_tiles, 1, 0), 0, 0)),
                  const((nq, d)), const((nq, d)),
                  const((PK_HEADS, N_KEYS, PK_HALF)), const((PK_HEADS, N_KEYS, PK_HALF)),
                  pl.BlockSpec((eblk, d), lambda b, t, e: (e, 0)),
                  pl.BlockSpec((d, eblk), lambda b, t, e: (0, e)),
                  const((1, d))],
        out_specs=pl.BlockSpec((1, tm, d), lambda b, t, e: (b, t, 0)),
        out_shape=jax.ShapeDtypeStruct((bsz, rows, d), F32),
        scratch_shapes=[pltpu.VMEM((d, tm), BF16),
                        pltpu.VMEM((PK_HEADS, N_KEYS, tm), F32),
                        pltpu.VMEM((PK_HEADS, N_KEYS, tm), F32),
                        pltpu.VMEM((PK_HEADS, N_KEYS, tm), F32),
                        pltpu.VMEM((PK_HEADS, N_KEYS, tm), F32),
                        pltpu.VMEM((PK_HEADS, SUBLANES, tm), F32),
                        pltpu.VMEM((SUBLANES * pl.cdiv(len(_PAIR_IDX), SUBLANES), tm), F32),
                        pltpu.VMEM((d, tm), F32)],
        compiler_params=_cparams(("parallel", "parallel", "arbitrary")),
        name="peer",
    )(x, g.reshape(1, d), mod, pqh, pql, p["sk1"], p["sk2"],
      p["pu"].astype(BF16), p["pv"].T.astype(BF16), norm_f.reshape(1, d))


def _rope_tables(rows, ctx_len):
    t = jnp.arange(rows - ctx_len, dtype=jnp.int32)
    inv = ROPE_THETA ** (-jnp.arange(0, ROPE_AXIS_DIM, 2, dtype=F32) / ROPE_AXIS_DIM)
    ang_r = (t // GRID_W).astype(F32)[:, None] * inv
    ang_c = (t % GRID_W).astype(F32)[:, None] * inv
    cos = jnp.concatenate([jnp.cos(ang_r)] * 2 + [jnp.cos(ang_c)] * 2, axis=1)
    sin = jnp.concatenate([-jnp.sin(ang_r), jnp.sin(ang_r), -jnp.sin(ang_c), jnp.sin(ang_c)], axis=1)
    cos = jnp.concatenate([jnp.ones((ctx_len, HEAD_DIM), F32), cos], axis=0)
    sin = jnp.concatenate([jnp.zeros((ctx_len, HEAD_DIM), F32), sin], axis=0)
    return jnp.tile(cos, (1, 2)), jnp.tile(sin, (1, 2))


def _head_norm_rope(x, gain, ones_bd, cos, sin):
    ms = _segsum(x * x, ones_bd) * (1.0 / HEAD_DIM)
    y = x * lax.rsqrt(ms + NORM_EPS) * gain
    outs = []
    half = ROPE_AXIS_DIM // 2
    lane = lax.broadcasted_iota(jnp.int32, (1, LANES), 1)
    first_half = (lane % ROPE_AXIS_DIM) < half
    for i in range(x.shape[1] // LANES):
        yc = y[:, i * LANES:(i + 1) * LANES]
        partner = jnp.where(first_half, pltpu.roll(yc, LANES - half, 1), pltpu.roll(yc, half, 1))
        outs.append(yc * cos + partner * sin)
    return jnp.concatenate(outs, axis=1)


def _attn_prep_kernel(z_ref, qn_ref, kn_ref, cos_ref, sin_ref, onesq_ref, onesk_ref,
                      q_o, k_o, v_o):
    z = z_ref[0]
    cos = cos_ref[...]
    sin = sin_ref[...]
    q = _head_norm_rope(z[:, :C_WIDTH], qn_ref[...], onesq_ref[...], cos, sin)
    k = _head_norm_rope(z[:, C_WIDTH:C_WIDTH + KV_WIDTH], kn_ref[...], onesk_ref[...], cos, sin)
    q_o[0] = (q * ATTN_SCALE).astype(q_o.dtype)
    k_o[0] = k.astype(k_o.dtype)
    v_o[0] = z[:, C_WIDTH + KV_WIDTH:C_WIDTH + 2 * KV_WIDTH].astype(v_o.dtype)


def _attn_prep(z, q_norm, k_norm, ctx_len, tm):
    bsz, rows, ncol = z.shape
    cos, sin = _rope_tables(rows, ctx_len)
    qn = jnp.tile(q_norm, C_HEADS).reshape(1, C_WIDTH)
    kn = jnp.tile(k_norm, C_KV_HEADS).reshape(1, KV_WIDTH)
    const = lambda shape: pl.BlockSpec(shape, lambda b, t: (0,) * len(shape))
    return pl.pallas_call(
        _attn_prep_kernel,
        grid=(bsz, rows // tm),
        in_specs=[pl.BlockSpec((1, tm, ncol), lambda b, t: (b, t, 0)),
                  const((1, C_WIDTH)), const((1, KV_WIDTH)),
                  pl.BlockSpec((tm, LANES), lambda b, t: (t, 0)),
                  pl.BlockSpec((tm, LANES), lambda b, t: (t, 0)),
                  const((C_WIDTH, C_WIDTH)), const((KV_WIDTH, KV_WIDTH))],
        out_specs=[pl.BlockSpec((1, tm, C_WIDTH), lambda b, t: (b, t, 0)),
                   pl.BlockSpec((1, tm, KV_WIDTH), lambda b, t: (b, t, 0)),
                   pl.BlockSpec((1, tm, KV_WIDTH), lambda b, t: (b, t, 0))],
        out_shape=[jax.ShapeDtypeStruct((bsz, rows, C_WIDTH), BF16),
                   jax.ShapeDtypeStruct((bsz, rows, KV_WIDTH), BF16),
                   jax.ShapeDtypeStruct((bsz, rows, KV_WIDTH), BF16)],
        compiler_params=_cparams(("parallel", "parallel")),
        name="attn_prep",
    )(z, qn, kn, cos, sin, _block_ones(C_WIDTH, HEAD_DIM), _block_ones(KV_WIDTH, HEAD_DIM))


def _attn_kernel(q_ref, k_ref, v_ref, o_ref, m_ref, l_ref, acc_ref):
    kt = pl.program_id(2)

    @pl.when(kt == 0)
    def _():
        m_ref[...] = jnp.full_like(m_ref, NEG_INF)
        l_ref[...] = jnp.zeros_like(l_ref)
        acc_ref[...] = jnp.zeros_like(acc_ref)

    q = q_ref[0]
    k = k_ref[0]
    v = v_ref[0]
    for hd in range(C_HEADS):
        kv = hd // C_GROUP
        qh = q[:, hd * HEAD_DIM:(hd + 1) * HEAD_DIM]
        kh = k[:, kv * HEAD_DIM:(kv + 1) * HEAD_DIM]
        vh = v[:, kv * HEAD_DIM:(kv + 1) * HEAD_DIM]
        s = _dot_nt(qh, kh)
        m_old = m_ref[hd]
        m_new = jnp.maximum(m_old, jnp.max(s, axis=-1, keepdims=True))
        alpha = jnp.exp(m_old - m_new)
        p = jnp.exp(s - m_new)
        l_ref[hd] = alpha * l_ref[hd] + jnp.sum(p, axis=-1, keepdims=True)
        acc_ref[hd] = alpha * acc_ref[hd] + _dot(p.astype(BF16), vh)
        m_ref[hd] = m_new

    @pl.when(kt == pl.num_programs(2) - 1)
    def _():
        o_ref[0] = jnp.concatenate(
            [acc_ref[hd] / l_ref[hd] for hd in range(C_HEADS)], axis=1).astype(o_ref.dtype)


def _attention(q, k, v, ctx_len, tq, tk):
    bsz, rows, _ = q.shape
    seq = rows - ctx_len
    qoff = ctx_len // tq
    return pl.pallas_call(
        _attn_kernel,
        grid=(bsz, seq // tq, rows // tk),
        in_specs=[pl.BlockSpec((1, tq, C_WIDTH), lambda b, i, j: (b, i + qoff, 0)),
                  pl.BlockSpec((1, tk, KV_WIDTH), lambda b, i, j: (b, j, 0)),
                  pl.BlockSpec((1, tk, KV_WIDTH), lambda b, i, j: (b, j, 0))],
        out_specs=pl.BlockSpec((1, tq, C_WIDTH), lambda b, i, j: (b, i, 0)),
        out_shape=jax.ShapeDtypeStruct((bsz, seq, C_WIDTH), BF16),
        scratch_shapes=[pltpu.VMEM((C_HEADS, tq, 1), F32),
                        pltpu.VMEM((C_HEADS, tq, 1), F32),
                        pltpu.VMEM((C_HEADS, tq, HEAD_DIM), F32)],
        compiler_params=_cparams(("parallel", "parallel", "arbitrary")),
        name="attention",
    )(q, k, v)


def _conv_kernel(zc_ref, zp_ref, zn_ref, w_ref, b_ref, lg_ref, lb_ref, o_ref, ybuf, *, tm, halo):
    t = pl.program_id(1)
    nt = pl.num_programs(1)
    ucol = C_WIDTH + 2 * KV_WIDTH

    def glu(z):
        u = z[:, ucol:ucol + 2 * D_WIDTH]
        return u[:, :D_WIDTH] * jax.nn.sigmoid(u[:, D_WIDTH:])

    ybuf[0:halo, :] = jnp.where(t == 0, 0.0, glu(zp_ref[0]))
    ybuf[halo:halo + tm, :] = glu(zc_ref[0])
    ybuf[halo + tm:, :] = jnp.where(t == nt - 1, 0.0, glu(zn_ref[0]))
    acc = jnp.zeros((tm, D_WIDTH), F32)
    for j in range(D_CONV_WIDTH):
        off = halo - D_PAD + j
        acc = acc + w_ref[j:j + 1, :] * ybuf[off:off + tm, :]
    y = acc + b_ref[...]
    mu = jnp.mean(y, axis=-1, keepdims=True)
    yc = y - mu
    var = jnp.mean(yc * yc, axis=-1, keepdims=True)
    yn = yc * lax.rsqrt(var + LN_EPS) * lg_ref[...] + lb_ref[...]
    o_ref[0] = (yn * jax.nn.sigmoid(yn)).astype(o_ref.dtype)


def _conformer_conv(z, dw_w, dw_b, cn_g, cn_b, ctx_len, tm):
    bsz, rows, ncol = z.shape
    seq = rows - ctx_len
    halo = 2 * SUBLANES
    hb = tm // halo
    off = ctx_len // tm
    offh = ctx_len // halo
    nh = seq // halo
    kern = functools.partial(_conv_kernel, tm=tm, halo=halo)
    row = lambda w: pl.BlockSpec((1, w), lambda b, t: (0, 0))
    return pl.pallas_call(
        kern,
        grid=(bsz, seq // tm),
        in_specs=[pl.BlockSpec((1, tm, ncol), lambda b, t: (b, t + off, 0)),
                  pl.BlockSpec((1, halo, ncol),
                               lambda b, t: (b, offh + jnp.maximum(t * hb - 1, 0), 0)),
                  pl.BlockSpec((1, halo, ncol),
                               lambda b, t: (b, offh + jnp.minimum((t + 1) * hb, nh - 1), 0)),
                  pl.BlockSpec((D_CONV_WIDTH, D_WIDTH), lambda b, t: (0, 0)),
                  row(D_WIDTH), row(D_WIDTH), row(D_WIDTH)],
        out_specs=pl.BlockSpec((1, tm, D_WIDTH), lambda b, t: (b, t, 0)),
        out_shape=jax.ShapeDtypeStruct((bsz, seq, D_WIDTH), BF16),
        scratch_shapes=[pltpu.VMEM((tm + 2 * halo, D_WIDTH), F32)],
        compiler_params=_cparams(("parallel", "parallel")),
        name="conformer_conv",
    )(z, z, z, dw_w, dw_b.reshape(1, -1), cn_g.reshape(1, -1), cn_b.reshape(1, -1))


def _forward(x, c, ctx, c_ctx, l0, l1, norm_f):
    bsz, seq, d = x.shape
    ctx_len = ctx.shape[1]
    tm = min(256, ctx_len)
    ctx_tiles = ctx_len // tm
    xs = jnp.concatenate([ctx, x], axis=1)

    mod = _modulation(c, c_ctx, l0["mod_w"], l0["mod_b"])
    z = _normmod_matmul(xs, l0["norm1"], mod, l0["w_in"], tm, ctx_tiles)
    r, v, g, bonus, kk, lwf, kdf, bf, lwr, kdr, br = _rwkv_prep(z, l0, tm, ctx_tiles)
    yf, yr = _rwkv_scan(r, v, kk, lwf, kdf, bf, lwr, kdr, br, ctx_len)
    o_rwkv = _rwkv_finish(yf, yr, bonus, g, l0["lnx_g"], l0["lnx_b"], tm)
    fcs = _fnet_chan(z, tm)
    ts_c = min(256, ctx_len)
    ts_l = min(512, seq)
    f_ctx = _fnet_seq(fcs, 0, ctx_len, ts_c, ts_c)
    f_lat = _fnet_seq(fcs, ctx_len, seq, ts_l, min(512, seq, ctx_len))
    o_fnet = jnp.concatenate([f_ctx, f_lat], axis=1)
    xs = _proj_residual(o_rwkv, o_fnet, xs, mod, l0["w_out"], tm, 0, ctx_tiles)
    xs = _peer(xs, l0["norm2"], mod, l0, tm, 0, ctx_tiles, False, norm_f)

    mod = _modulation(c, c_ctx, l1["mod_w"], l1["mod_b"])
    z = _normmod_matmul(xs, l1["norm1"], mod, l1["w_in"], tm, ctx_tiles)
    q, k, v = _attn_prep(z, l1["q_norm"], l1["k_norm"], ctx_len, tm)
    o_attn = _attention(q, k, v, ctx_len, tm, tm)
    o_conv = _conformer_conv(z, l1["dw_w"], l1["dw_b"], l1["cn_g"], l1["cn_b"], ctx_len, tm)
    xl = _proj_residual(o_attn, o_conv, xs, mod, l1["w_out"], tm, ctx_tiles, ctx_tiles)
    return _peer(xl, l1["norm2"], mod, l1, tm, ctx_tiles, ctx_tiles, True, norm_f)


def kernel(x, c, ctx, c_ctx, l0_mod_w, l0_mod_b, l0_norm1, l0_w_in, l0_shift_prev, l0_shift_next, l0_w0, l0_w2, l0_a0, l0_a2, l0_g2, l0_k_k, l0_k_a, l0_r_k, l0_lnx_g, l0_lnx_b, l0_w_out, l0_norm2, l0_pq, l0_sk1, l0_sk2, l0_pu, l0_pv, l1_mod_w, l1_mod_b, l1_norm1, l1_w_in, l1_q_norm, l1_k_norm, l1_dw_w, l1_dw_b, l1_cn_g, l1_cn_b, l1_w_out, l1_norm2, l1_pq, l1_sk1, l1_sk2, l1_pu, l1_pv, norm_f):
    l0 = dict(mod_w=l0_mod_w, mod_b=l0_mod_b, norm1=l0_norm1, w_in=l0_w_in,
              shift_prev=l0_shift_prev, shift_next=l0_shift_next, w0=l0_w0, w2=l0_w2,
              a0=l0_a0, a2=l0_a2, g2=l0_g2, k_k=l0_k_k, k_a=l0_k_a, r_k=l0_r_k,
              lnx_g=l0_lnx_g, lnx_b=l0_lnx_b, w_out=l0_w_out, norm2=l0_norm2,
              pq=l0_pq, sk1=l0_sk1, sk2=l0_sk2, pu=l0_pu, pv=l0_pv)
    l1 = dict(mod_w=l1_mod_w, mod_b=l1_mod_b, norm1=l1_norm1, w_in=l1_w_in,
              q_norm=l1_q_norm, k_norm=l1_k_norm, dw_w=l1_dw_w, dw_b=l1_dw_b,
              cn_g=l1_cn_g, cn_b=l1_cn_b, w_out=l1_w_out, norm2=l1_norm2,
              pq=l1_pq, sk1=l1_sk1, sk2=l1_sk2, pu=l1_pu, pv=l1_pv)
    return _forward(x, c, ctx, c_ctx, l0, l1, norm_f)
```

```python
import functools
import math

import jax
import jax.numpy as jnp
import numpy as np
from jax import lax
from jax.experimental import pallas as pl
from jax.experimental.pallas import tpu as pltpu

F32 = jnp.float32
BF16 = jnp.bfloat16
HIGHEST = lax.Precision.HIGHEST

LANES = 128
SUBLANES = 8
VMEM_LIMIT_BYTES = 56 * 1024 * 1024

N_MOD = 6
NORM_EPS = 1e-6
LN_EPS = 1e-5
GRID_W = 64
HEAD_DIM = 64
A_WIDTH = 768
A_HEADS = A_WIDTH // HEAD_DIM
A_RANK_W = 64
A_RANK_A = 64
A_RANK_G = 128
A_LNX_EPS = 64e-5
A_COLS = 3 * A_WIDTH + 2 * A_RANK_W + 2 * A_RANK_A + A_RANK_G
B_WIDTH = 256
B_GROUP_DIM = 64
C_WIDTH = 768
C_HEADS = 12
C_KV_HEADS = 4
C_GROUP = C_HEADS // C_KV_HEADS
KV_WIDTH = C_KV_HEADS * HEAD_DIM
ROPE_AXIS_DIM = HEAD_DIM // 2
ROPE_THETA = 10000.0
ATTN_SCALE = HEAD_DIM ** -0.5
D_WIDTH = 256
D_CONV_WIDTH = 31
D_PAD = D_CONV_WIDTH // 2
PK_HEADS = 8
PK_DIM = 256
PK_HALF = 128
N_KEYS = 128
PK_TOPK = 16

SCAN_CHUNK = 64
NEG_INF = float("-inf")


def _cparams(semantics):
    return pltpu.CompilerParams(dimension_semantics=semantics,
                                vmem_limit_bytes=VMEM_LIMIT_BYTES)


def _split_bf16(x):
    hi = x.astype(BF16)
    lo = (x - hi.astype(F32)).astype(BF16)
    return hi, lo


def _dot(a, b):
    return jnp.dot(a, b, preferred_element_type=F32)


def _dot_hp(a, b):
    return jnp.dot(a, b, preferred_element_type=F32, precision=HIGHEST)


def _dot_nt(a, b, precision=None):
    return lax.dot_general(a, b, (((1,), (1,)), ((), ())),
                           preferred_element_type=F32, precision=precision)


def _dot_tn(a, b, precision=None):
    return lax.dot_general(a, b, (((0,), (0,)), ((), ())),
                           preferred_element_type=F32, precision=precision)


_NN = (((1,), (0,)), ((), ()))
_NT = (((1,), (1,)), ((), ()))
_TN = (((0,), (0,)), ((), ()))


def _mm3(a, b, dims=_NN):
    (ah, al), (bh, bl) = a, b
    dg = lambda x, y: lax.dot_general(x, y, dims, preferred_element_type=F32)
    return dg(ah, bh) + (dg(ah, bl) + dg(al, bh))


def _segsum(x, ones_bd):
    hi, lo = _split_bf16(x)
    return _dot(hi, ones_bd) + _dot(lo, ones_bd)


def _block_ones(width, seg):
    r = np.arange(width) // seg
    return jnp.asarray((r[:, None] == r[None, :]).astype(np.float32), dtype=BF16)


def _mod_kernel(c_ref, w_ref, b_ref, o_ref):
    c = c_ref[...]
    s = c * jax.nn.sigmoid(c)
    o_ref[...] = _dot_hp(s, w_ref[...]) + b_ref[...]


def _modulation(c, c_ctx, mod_w, mod_b):
    bsz, d = c.shape
    rows = SUBLANES * pl.cdiv(bsz + 1, SUBLANES)
    cc = jnp.zeros((rows, d), F32).at[:bsz].set(c).at[bsz].set(c_ctx)
    n = mod_w.shape[1]
    tn = n // 4
    out = pl.pallas_call(
        _mod_kernel,
        grid=(n // tn,),
        in_specs=[pl.BlockSpec((rows, d), lambda j: (0, 0)),
                  pl.BlockSpec((d, tn), lambda j: (0, j)),
                  pl.BlockSpec((1, tn), lambda j: (0, j))],
        out_specs=pl.BlockSpec((rows, tn), lambda j: (0, j)),
        out_shape=jax.ShapeDtypeStruct((rows, n), F32),
        compiler_params=_cparams(("arbitrary",)),
        name="adaln_mod",
    )(cc, mod_w, mod_b.reshape(1, n))
    lat = out[:bsz].reshape(bsz, N_MOD, d)
    ctx = jnp.broadcast_to(out[bsz].reshape(1, N_MOD, d), (bsz, N_MOD, d))
    return jnp.stack([ctx, lat], axis=1)


def _norm_mod(x, g, mod, row):
    ms = jnp.mean(x * x, axis=-1, keepdims=True)
    y = x * lax.rsqrt(ms + NORM_EPS) * g
    return y * (1.0 + mod[row + 1:row + 2]) + mod[row:row + 1]


def _normmod_matmul_kernel(x_ref, g_ref, mod_ref, w_ref, o_ref):
    h = _norm_mod(x_ref[0], g_ref[...], mod_ref[0, 0], 0)
    o_ref[0] = _dot(h.astype(BF16), w_ref[...])


def _normmod_matmul(x, g, mod, w, tm, ctx_tiles):
    bsz, rows, d = x.shape
    n = w.shape[1]
    return pl.pallas_call(
        _normmod_matmul_kernel,
        grid=(bsz, rows // tm),
        in_specs=[pl.BlockSpec((1, tm, d), lambda b, t: (b, t, 0)),
                  pl.BlockSpec((1, d), lambda b, t: (0, 0)),
                  pl.BlockSpec((1, 1, N_MOD, d),
                               lambda b, t: (b, jnp.where(t >= ctx_tiles, 1, 0), 0, 0)),
                  pl.BlockSpec((d, n), lambda b, t: (0, 0))],
        out_specs=pl.BlockSpec((1, tm, n), lambda b, t: (b, t, 0)),
        out_shape=jax.ShapeDtypeStruct((bsz, rows, n), F32),
        compiler_params=_cparams(("parallel", "parallel")),
        name="normmod_proj",
    )(x, g.reshape(1, d), mod, w.astype(BF16))


def _rwkv_prep_kernel(z_ref, zp_ref, zn_ref, mup_ref, mun_ref, w0_ref, w2_ref, a0_ref, a2_ref,
                      g2_ref, kk_ref, ka_ref, rk_ref, ones_ref,
                      r_o, v_o, g_o, bonus_o, kkn_o, lwf_o, kdf_o, bf_o, lwr_o, kdr_o, br_o,
                      *, tm, ctx_tiles):
    t = pl.program_id(1)
    nt = pl.num_programs(1)
    z = z_ref[0][:, :A_COLS]
    zp_row = zp_ref[0][SUBLANES - 1:SUBLANES, :A_COLS]
    zn_row = zn_ref[0][0:1, :A_COLS]
    first = jnp.logical_or(t == 0, t == ctx_tiles)
    last = jnp.logical_or(t == ctx_tiles - 1, t == nt - 1)
    zp_row = jnp.where(first, 0.0, zp_row)
    zn_row = jnp.where(last, 0.0, zn_row)
    ridx = lax.broadcasted_iota(jnp.int32, (tm, 1), 0)
    z_prev = jnp.where(ridx == 0, zp_row, pltpu.roll(z, 1, 0))
    z_next = jnp.where(ridx == tm - 1, zn_row, pltpu.roll(z, tm - 1, 0))
    zs = z + mup_ref[...] * (z_prev - z) + mun_ref[...] * (z_next - z)

    w = A_WIDTH
    r = zs[:, 0:w]
    k = zs[:, w:2 * w]
    v = zs[:, 2 * w:3 * w]
    o = 3 * w
    xw = zs[:, o:o + 2 * A_RANK_W]
    o += 2 * A_RANK_W
    xa = zs[:, o:o + 2 * A_RANK_A]
    o += 2 * A_RANK_A
    xg = zs[:, o:o + A_RANK_G]
    ones_bd = ones_ref[...]

    g_o[0] = _dot_hp(jax.nn.sigmoid(xg), g2_ref[...])
    kk = k * kk_ref[...]
    nrm = jnp.sqrt(_segsum(kk * kk, ones_bd))
    kk = kk / jnp.maximum(nrm, 1e-12)
    r_o[0] = r
    v_o[0] = v
    kkn_o[0] = kk
    bonus = jnp.zeros_like(r)
    tw = jnp.tanh(xw)
    for d, (lw_o, kd_o, b_o) in enumerate(((lwf_o, kdf_o, bf_o), (lwr_o, kdr_o, br_o))):
        wl = w0_ref[d:d + 1, :] + _dot_hp(tw[:, d * A_RANK_W:(d + 1) * A_RANK_W], w2_ref[d])
        w_log = -jax.nn.softplus(-wl) - 0.5
        lw_o[0] = -jnp.exp(w_log)
        a_gate = jax.nn.sigmoid(
            a0_ref[d:d + 1, :] + _dot_hp(xa[:, d * A_RANK_A:(d + 1) * A_RANK_A], a2_ref[d]))
        k_d = k * (1.0 + (a_gate - 1.0) * ka_ref[...])
        kd_o[0] = k_d
        b_o[0] = kk * a_gate
        bonus = bonus + _segsum(r * k_d * rk_ref[...], ones_bd) * v
    bonus_o[0] = bonus


def _rwkv_prep(z, p, tm, ctx_tiles):
    bsz, rows, ncol = z.shape
    w = A_WIDTH
    hb = tm // SUBLANES
    nblk8 = rows // SUBLANES
    row_spec = lambda width: pl.BlockSpec((1, width), lambda b, t: (0, 0))
    full = lambda shape: pl.BlockSpec(shape, lambda b, t: (0,) * len(shape))
    out_spec = pl.BlockSpec((1, tm, w), lambda b, t: (b, t, 0))
    out_shape = jax.ShapeDtypeStruct((bsz, rows, w), F32)
    kern = functools.partial(_rwkv_prep_kernel, tm=tm, ctx_tiles=ctx_tiles)
    return pl.pallas_call(
        kern,
        grid=(bsz, rows // tm),
        in_specs=[pl.BlockSpec((1, tm, ncol), lambda b, t: (b, t, 0)),
                  pl.BlockSpec((1, SUBLANES, ncol),
                               lambda b, t: (b, jnp.maximum(t * hb - 1, 0), 0)),
                  pl.BlockSpec((1, SUBLANES, ncol),
                               lambda b, t: (b, jnp.minimum((t + 1) * hb, nblk8 - 1), 0)),
                  row_spec(A_COLS), row_spec(A_COLS),
                  full((2, w)), full((2, A_RANK_W, w)), full((2, w)), full((2, A_RANK_A, w)),
                  full((A_RANK_G, w)), row_spec(w), row_spec(w), row_spec(w), full((w, w))],
        out_specs=[out_spec] * 11,
        out_shape=[out_shape] * 11,
        compiler_params=_cparams(("parallel", "parallel")),
        name="rwkv_prep",
    )(z, z, z, p["shift_prev"].reshape(1, -1), p["shift_next"].reshape(1, -1),
      p["w0"], p["w2"], p["a0"], p["a2"], p["g2"], p["k_k"].reshape(1, w),
      p["k_a"].reshape(1, w), p["r_k"].reshape(1, w), _block_ones(w, HEAD_DIM))


SCAN_HEADS = 4


def _scan_kernel(rf, vf, kkf, lwf, kdf, bf, rr, vr, kkr, lwr, kdr, br, yf_o, yr_o, s_ref, *, c):
    @pl.when(pl.program_id(2) == 0)
    def _():
        s_ref[...] = jnp.zeros_like(s_ref)

    ri = lax.broadcasted_iota(jnp.int32, (c, c), 0)
    ci = lax.broadcasted_iota(jnp.int32, (c, c), 1)
    ri2 = lax.broadcasted_iota(jnp.int32, (c, 2 * c), 0)
    ci2 = lax.broadcasted_iota(jnp.int32, (c, 2 * c), 1)
    ci2 = jnp.where(ci2 >= c, ci2 - c, ci2)
    dirs = ((rf, vf, kkf, lwf, kdf, bf, ci <= ri, ci < ri, ci2 <= ri2, c - 1),
            (rr, vr, kkr, lwr, kdr, br, ci >= ri, ci > ri, ci2 >= ri2, 0))
    lhs, rhs, vs, ss, pend, incl, strict = [], [], [], [], [], [], []
    for d, (r_, v_, kk_, lw_, kd_, b_, m_incl, m_strict, m_incl2, edge) in enumerate(dirs):
        lw = lw_[0]
        tri = m_incl.astype(F32).astype(BF16)
        lw_h = lw.astype(BF16)
        lw_m, lw_l = _split_bf16(lw - lw_h.astype(F32))
        lcum = _dot(tri, lw_h) + (_dot(tri, lw_m) + _dot(tri, lw_l))
        p_in = jnp.exp(lcum)
        p_inv = jnp.exp(-lcum)
        lhs_all = jnp.concatenate([-kk_[0] * jnp.exp(lcum - lw), r_[0] * p_in], axis=0)
        rhs_all = jnp.concatenate([b_[0] * p_inv, kd_[0] * p_inv], axis=0)
        v_all = v_[0]
        for h in range(SCAN_HEADS):
            sl = slice(h * HEAD_DIM, (h + 1) * HEAD_DIM)
            lhs.append(lhs_all[:, sl])
            rhs.append(rhs_all[:, sl])
            vs.append(v_all[:, sl])
            ss.append(s_ref[d, h])
            pend.append(p_in[edge:edge + 1, sl])
            incl.append(m_incl2)
            strict.append(m_strict)
    n = len(lhs)
    idx = range(n)
    lhs2 = [_split_bf16(lhs[i]) for i in idx]
    rhs2 = [_split_bf16(rhs[i]) for i in idx]
    v2 = [_split_bf16(vs[i]) for i in idx]
    s2 = [_split_bf16(ss[i]) for i in idx]
    g = [_mm3(lhs2[i], rhs2[i], _NT) for i in idx]
    xs = [_mm3(lhs2[i], s2[i], _NT) for i in idx]
    a_ak = [_split_bf16(jnp.where(strict[i], g[i][:c, c:], 0.0)) for i in idx]
    x = [xs[i][:c] + _mm3(a_ak[i], v2[i]) for i in idx]
    apow = [_split_bf16(jnp.where(strict[i], g[i][:c, :c], 0.0)) for i in idx]
    steps = int(math.log2(c))
    for k in range(steps):
        x2 = [_split_bf16(x[i]) for i in idx]
        x = [x[i] + _mm3(apow[i], x2[i]) for i in idx]
        if k + 1 < steps:
            apow = [_split_bf16(_mm3(apow[i], apow[i])) for i in idx]
    x2 = [_split_bf16(x[i]) for i in idx]
    sav = [tuple(jnp.concatenate([x2[i][t], v2[i][t]], axis=0) for t in range(2)) for i in idx]
    a_r = [_split_bf16(jnp.where(incl[i], g[i][c:], 0.0)) for i in idx]
    y = [xs[i][c:] + _mm3(a_r[i], sav[i]) for i in idx]
    s_new = [(ss[i] + _mm3(sav[i], rhs2[i], _TN)) * pend[i] for i in idx]
    for d, y_o in enumerate((yf_o, yr_o)):
        for h in range(SCAN_HEADS):
            s_ref[d, h] = s_new[d * SCAN_HEADS + h]
        y_o[0] = jnp.concatenate(y[d * SCAN_HEADS:(d + 1) * SCAN_HEADS], axis=1)


def _rwkv_scan(r, v, kk, lwf, kdf, bf, lwr, kdr, br, ctx_len):
    bsz, rows, w = r.shape
    c = SCAN_CHUNK
    nch = rows // c
    cch = ctx_len // c
    lanes = SCAN_HEADS * HEAD_DIM
    fwd = lambda b, hp, i: (b, i, hp)
    rev = lambda b, hp, i: (b, jnp.where(i < cch, cch - 1 - i, nch - 1 + cch - i), hp)
    blk = lambda im: pl.BlockSpec((1, c, lanes), im)
    kern = functools.partial(_scan_kernel, c=c)
    out_shape = jax.ShapeDtypeStruct((bsz, rows, w), F32)
    return pl.pallas_call(
        kern,
        grid=(bsz, w // lanes, nch),
        in_specs=[blk(fwd)] * 6 + [blk(rev)] * 6,
        out_specs=[blk(fwd), blk(rev)],
        out_shape=[out_shape, out_shape],
        scratch_shapes=[pltpu.VMEM((2, SCAN_HEADS, HEAD_DIM, HEAD_DIM), F32)],
        compiler_params=_cparams(("parallel", "parallel", "arbitrary")),
        name="rwkv_scan",
    )(r, v, kk, lwf, kdf, bf, r, v, kk, lwr, kdr, br)


def _rwkv_finish_kernel(yf, yr, bonus, g, lg, lb, ones_ref, o_ref):
    y = yf[0] + yr[0]
    ones_bd = ones_ref[...]
    inv = 1.0 / HEAD_DIM
    mu = _segsum(y, ones_bd) * inv
    yc = y - mu
    var = _segsum(yc * yc, ones_bd) * inv
    yn = yc * lax.rsqrt(var + A_LNX_EPS) * lg[...] + lb[...]
    o_ref[0] = ((yn + bonus[0]) * g[0]).astype(o_ref.dtype)


def _rwkv_finish(yf, yr, bonus, g, lnx_g, lnx_b, tm):
    bsz, rows, w = yf.shape
    blk = pl.BlockSpec((1, tm, w), lambda b, t: (b, t, 0))
    row = pl.BlockSpec((1, w), lambda b, t: (0, 0))
    return pl.pallas_call(
        _rwkv_finish_kernel,
        grid=(bsz, rows // tm),
        in_specs=[blk, blk, blk, blk, row, row, pl.BlockSpec((w, w), lambda b, t: (0, 0))],
        out_specs=blk,
        out_shape=jax.ShapeDtypeStruct((bsz, rows, w), BF16),
        compiler_params=_cparams(("parallel", "parallel")),
        name="rwkv_finish",
    )(yf, yr, bonus, g, lnx_g.reshape(1, w), lnx_b.reshape(1, w), _block_ones(w, HEAD_DIM))


def _fnet_chan_kernel(z_ref, m_ref, o_ref):
    f = z_ref[0][:, A_COLS:A_COLS + B_WIDTH]
    o_ref[0] = _dot(f.astype(BF16), m_ref[...]).astype(o_ref.dtype)


def _fnet_chan(z, tm):
    bsz, rows, ncol = z.shape
    j = np.arange(B_WIDTH)
    same = (j[:, None] // B_GROUP_DIM) == (j[None, :] // B_GROUP_DIM)
    ang = 2.0 * np.pi * ((j[:, None] % B_GROUP_DIM) * (j[None, :] % B_GROUP_DIM) % B_GROUP_DIM) / B_GROUP_DIM
    m = np.concatenate([np.where(same, np.cos(ang), 0.0), np.where(same, np.sin(ang), 0.0)], axis=1)
    return pl.pallas_call(
        _fnet_chan_kernel,
        grid=(bsz, rows // tm),
        in_specs=[pl.BlockSpec((1, tm, ncol), lambda b, t: (b, t, 0)),
                  pl.BlockSpec((B_WIDTH, 2 * B_WIDTH), lambda b, t: (0, 0))],
        out_specs=pl.BlockSpec((1, tm, 2 * B_WIDTH), lambda b, t: (b, t, 0)),
        out_shape=jax.ShapeDtypeStruct((bsz, rows, 2 * B_WIDTH), BF16),
        compiler_params=_cparams(("parallel", "parallel")),
        name="fnet_chan",
    )(z, jnp.asarray(m, dtype=BF16))


def _fnet_seq_kernel(fcs_ref, cb_ref, sb_ref, c0_ref, s0_ref, o_ref, acc_ref, *, scale, nb):
    k = pl.program_id(1)

    @pl.when(k == 0)
    def _():
        acc_ref[...] = jnp.zeros_like(acc_ref)

    cb = cb_ref[...]
    sb = sb_ref[...]
    c0 = c0_ref[0]
    s0 = s0_ref[0]
    cm = (c0 * cb - s0 * sb).astype(BF16)
    sm = (s0 * cb + c0 * sb).astype(BF16)
    for b in range(nb):
        fcs = fcs_ref[b]
        acc_ref[b] += _dot(cm, fcs[:, :B_WIDTH]) - _dot(sm, fcs[:, B_WIDTH:])

    @pl.when(k == pl.num_programs(1) - 1)
    def _():
        o_ref[...] = (acc_ref[...] * scale).astype(o_ref.dtype)


def _fnet_seq(fcs, row0, length, ts, tk):
    bsz = fcs.shape[0]
    ds = jnp.arange(ts, dtype=jnp.int32)[:, None]
    tt = jnp.arange(length, dtype=jnp.int32)[None, :]
    ang = ((ds * tt) % length).astype(F32) * (2.0 * math.pi / length)
    cb, sb = jnp.cos(ang), jnp.sin(ang)
    s0 = (jnp.arange(length // ts, dtype=jnp.int32) * ts)[:, None]
    ang0 = ((s0 * tt) % length).astype(F32) * (2.0 * math.pi / length)
    c0, sn0 = jnp.cos(ang0)[:, None, :], jnp.sin(ang0)[:, None, :]
    kern = functools.partial(_fnet_seq_kernel, scale=1.0 / math.sqrt(length * B_GROUP_DIM), nb=bsz)
    koff = row0 // tk
    return pl.pallas_call(
        kern,
        grid=(length // ts, length // tk),
        in_specs=[pl.BlockSpec((bsz, tk, 2 * B_WIDTH), lambda s, k: (0, k + koff, 0)),
                  pl.BlockSpec((ts, tk), lambda s, k: (0, k)),
                  pl.BlockSpec((ts, tk), lambda s, k: (0, k)),
                  pl.BlockSpec((1, 1, tk), lambda s, k: (s, 0, k)),
                  pl.BlockSpec((1, 1, tk), lambda s, k: (s, 0, k))],
        out_specs=pl.BlockSpec((bsz, ts, B_WIDTH), lambda s, k: (0, s, 0)),
        out_shape=jax.ShapeDtypeStruct((bsz, length, B_WIDTH), BF16),
        scratch_shapes=[pltpu.VMEM((bsz, ts, B_WIDTH), F32)],
        compiler_params=_cparams(("parallel", "arbitrary")),
        name="fnet_seq",
    )(fcs, cb, sb, c0, sn0)


def _proj_residual_kernel(o1_ref, o2_ref, x_ref, mod_ref, w1_ref, w2_ref, out_ref, *, o1_transposed):
    o1 = o1_ref[0]
    if o1_transposed:
        o1 = o1.astype(F32).T.astype(BF16)
    mix = _dot(o1, w1_ref[...]) + _dot(o2_ref[0], w2_ref[...])
    gate = mod_ref[0, 0][2:3]
    out_ref[0] = x_ref[0] + gate * mix


def _proj_residual(o1, o2, x, mod, w_out, tm, tile_off, ctx_tiles, o1_transposed=False):
    bsz, rows, w2 = o2.shape
    w1 = o1.shape[1] if o1_transposed else o1.shape[2]
    d = x.shape[2]
    o1_spec = (pl.BlockSpec((1, w1, tm), lambda b, t: (b, 0, t)) if o1_transposed
               else pl.BlockSpec((1, tm, w1), lambda b, t: (b, t, 0)))
    kern = functools.partial(_proj_residual_kernel, o1_transposed=o1_transposed)
    return pl.pallas_call(
        kern,
        grid=(bsz, rows // tm),
        in_specs=[o1_spec,
                  pl.BlockSpec((1, tm, w2), lambda b, t: (b, t, 0)),
                  pl.BlockSpec((1, tm, d), lambda b, t: (b, t + tile_off, 0)),
                  pl.BlockSpec((1, 1, N_MOD, d),
                               lambda b, t: (b, jnp.where(t + tile_off >= ctx_tiles, 1, 0), 0, 0)),
                  pl.BlockSpec((w1, d), lambda b, t: (0, 0)),
                  pl.BlockSpec((w2, d), lambda b, t: (0, 0))],
        out_specs=pl.BlockSpec((1, tm, d), lambda b, t: (b, t, 0)),
        out_shape=jax.ShapeDtypeStruct((bsz, rows, d), F32),
        compiler_params=_cparams(("parallel", "parallel")),
        name="proj_residual",
    )(o1, o2, x, mod, w_out[:w1].astype(BF16), w_out[w1:].astype(BF16))


def _top16_rows(s):
    rows = []
    for _ in range(PK_TOPK):
        m = jnp.max(s, axis=0, keepdims=True)
        rows.append(m)
        s = jnp.where(s >= m, NEG_INF, s)
    return rows


_PAIR_IDX = [(i, j) for i in range(PK_TOPK) for j in range(PK_TOPK) if (i + 1) * (j + 1) <= PK_TOPK]


def _peer_kernel(x_ref, g_ref, mod_ref, pqh_ref, pql_ref, sk1_ref, sk2_ref, pu_ref, pvt_ref, gf_ref,
                 out_ref, ht_ref, s1_ref, s2_ref, e2_ref, w1_ref, tau_ref, cand_ref, acc_ref,
                 *, tm, e1_per_blk, final_norm):
    eb = pl.program_id(2)

    @pl.when(eb == 0)
    def _():
        h = _norm_mod(x_ref[0], g_ref[...], mod_ref[0, 0], 3)
        ht = h.T
        hhi, hlo = _split_bf16(ht)
        ht_ref[...] = hhi
        qt = _dot(pqh_ref[...], hhi) + _dot(pqh_ref[...], hlo) + _dot(pql_ref[...], hhi)
        for hd in range(PK_HEADS):
            q1 = qt[hd * PK_DIM:hd * PK_DIM + PK_HALF]
            q2 = qt[hd * PK_DIM + PK_HALF:(hd + 1) * PK_DIM]
            s1 = _dot_hp(sk1_ref[hd], q1)
            s2 = _dot_hp(sk2_ref[hd], q2)
            v1 = _top16_rows(s1)
            v2 = _top16_rows(s2)
            cand_ref[...] = jnp.full_like(cand_ref, NEG_INF)
            for n, (i, j) in enumerate(_PAIR_IDX):
                cand_ref[n:n + 1, :] = v1[i] + v2[j]
            cand = cand_ref[...]
            top = v1[0] + v2[0]
            zsum = jnp.zeros_like(top)
            tau = top
            for _ in range(PK_TOPK):
                m = jnp.max(cand, axis=0, keepdims=True)
                zsum = zsum + jnp.exp(m - top)
                tau = m
                cand = jnp.where(cand >= m, NEG_INF, cand)
            s1_ref[hd] = s1
            s2_ref[hd] = s2
            e2_ref[hd] = jnp.exp(s2 - v2[0])
            w1_ref[hd] = jnp.exp(s1 - v1[0]) / zsum
            tau_ref[hd] = jnp.broadcast_to(tau, (SUBLANES, tm))
        acc_ref[...] = jnp.zeros_like(acc_ref)

    sub = 2 * N_KEYS
    nsub = e1_per_blk // 2
    ht = ht_ref[...]

    def up(i):
        return _dot(pu_ref[i * sub:(i + 1) * sub, :], ht)

    def activation(i, hu):
        acts = []
        for jj in range(2):
            e1 = eb * e1_per_blk + 2 * i + jj
            gate = jnp.zeros((N_KEYS, tm), F32)
            for hd in range(PK_HEADS):
                s1row = s1_ref[hd, pl.ds(e1, 1), :]
                w1row = w1_ref[hd, pl.ds(e1, 1), :]
                score = s2_ref[hd] + s1row
                gate = gate + jnp.where(score >= tau_ref[hd, 0:1, :], e2_ref[hd] * w1row, 0.0)
            u = hu[jj * N_KEYS:(jj + 1) * N_KEYS]
            act = 0.5 * u * (1.0 + lax.erf(u * (1.0 / math.sqrt(2.0))))
            acts.append((act * gate).astype(BF16))
        return jnp.concatenate(acts, axis=0)

    hu_next = up(0)
    down = None
    for i in range(nsub):
        hu = hu_next
        if i + 1 < nsub:
            hu_next = up(i + 1)
        part = _dot(pvt_ref[:, i * sub:(i + 1) * sub], activation(i, hu))
        down = part if down is None else down + part
    acc_ref[...] += down

    @pl.when(eb == pl.num_programs(2) - 1)
    def _():
        y = x_ref[0] + mod_ref[0, 0][5:6] * acc_ref[...].T
        if final_norm:
            ms = jnp.mean(y * y, axis=-1, keepdims=True)
            y = y * lax.rsqrt(ms + NORM_EPS) * gf_ref[...]
        out_ref[0] = y


def _peer(x, g, mod, p, tm, tile_off, ctx_tiles, final_norm, norm_f):
    bsz, rows, d = x.shape
    n_exp = p["pu"].shape[0]
    e1_per_blk = 8
    eblk = e1_per_blk * N_KEYS
    pqt = p["pq"].T
    pqh = pqt.astype(BF16)
    pql = (pqt - pqh.astype(F32)).astype(BF16)
    nq = pqt.shape[0]
    kern = functools.partial(_peer_kernel, tm=tm, e1_per_blk=e1_per_blk, final_norm=final_norm)
    const = lambda shape: pl.BlockSpec(shape, lambda b, t, e: (0,) * len(shape))
    return pl.pallas_call(
        kern,
        grid=(bsz, rows // tm, n_exp // eblk),
        in_specs=[pl.BlockSpec((1, tm, d), lambda b, t, e: (b, t, 0)),
                  const((1, d)),
                  pl.BlockSpec((1, 1, N_MOD, d),
                               lambda b, t, e: (b, jnp.where(t + tile_off >= ctx_tiles, 1, 0), 0, 0)),
                  const((nq, d)), const((nq, d)),
                  const((PK_HEADS, N_KEYS, PK_HALF)), const((PK_HEADS, N_KEYS, PK_HALF)),
                  pl.BlockSpec((eblk, d), lambda b, t, e: (e, 0)),
                  pl.BlockSpec((d, eblk), lambda b, t, e: (0, e)),
                  const((1, d))],
        out_specs=pl.BlockSpec((1, tm, d), lambda b, t, e: (b, t, 0)),
        out_shape=jax.ShapeDtypeStruct((bsz, rows, d), F32),
        scratch_shapes=[pltpu.VMEM((d, tm), BF16),
                        pltpu.VMEM((PK_HEADS, N_KEYS, tm), F32),
                        pltpu.VMEM((PK_HEADS, N_KEYS, tm), F32),
                        pltpu.VMEM((PK_HEADS, N_KEYS, tm), F32),
                        pltpu.VMEM((PK_HEADS, N_KEYS, tm), F32),
                        pltpu.VMEM((PK_HEADS, SUBLANES, tm), F32),
                        pltpu.VMEM((SUBLANES * pl.cdiv(len(_PAIR_IDX), SUBLANES), tm), F32),
                        pltpu.VMEM((d, tm), F32)],
        compiler_params=_cparams(("parallel", "parallel", "arbitrary")),
        name="peer",
    )(x, g.reshape(1, d), mod, pqh, pql, p["sk1"], p["sk2"],
      p["pu"].astype(BF16), p["pv"].T.astype(BF16), norm_f.reshape(1, d))


def _rope_tables(rows, ctx_len):
    t = jnp.arange(rows - ctx_len, dtype=jnp.int32)
    inv = ROPE_THETA ** (-jnp.arange(0, ROPE_AXIS_DIM, 2, dtype=F32) / ROPE_AXIS_DIM)
    ang_r = (t // GRID_W).astype(F32)[:, None] * inv
    ang_c = (t % GRID_W).astype(F32)[:, None] * inv
    cos = jnp.concatenate([jnp.cos(ang_r)] * 2 + [jnp.cos(ang_c)] * 2, axis=1)
    sin = jnp.concatenate([-jnp.sin(ang_r), jnp.sin(ang_r), -jnp.sin(ang_c), jnp.sin(ang_c)], axis=1)
    cos = jnp.concatenate([jnp.ones((ctx_len, HEAD_DIM), F32), cos], axis=0)
    sin = jnp.concatenate([jnp.zeros((ctx_len, HEAD_DIM), F32), sin], axis=0)
    return jnp.tile(cos, (1, 2)), jnp.tile(sin, (1, 2))


def _head_norm_rope(x, gain, ones_bd, cos, sin):
    ms = _segsum(x * x, ones_bd) * (1.0 / HEAD_DIM)
    y = x * lax.rsqrt(ms + NORM_EPS) * gain
    outs = []
    half = ROPE_AXIS_DIM // 2
    lane = lax.broadcasted_iota(jnp.int32, (1, LANES), 1)
    first_half = (lane % ROPE_AXIS_DIM) < half
    for i in range(x.shape[1] // LANES):
        yc = y[:, i * LANES:(i + 1) * LANES]
        partner = jnp.where(first_half, pltpu.roll(yc, LANES - half, 1), pltpu.roll(yc, half, 1))
        outs.append(yc * cos + partner * sin)
    return jnp.concatenate(outs, axis=1)


def _attn_prep_kernel(z_ref, qn_ref, kn_ref, cos_ref, sin_ref, onesq_ref, onesk_ref,
                      q_o, k_o, v_o):
    z = z_ref[0]
    cos = cos_ref[...]
    sin = sin_ref[...]
    q = _head_norm_rope(z[:, :C_WIDTH], qn_ref[...], onesq_ref[...], cos, sin)
    k = _head_norm_rope(z[:, C_WIDTH:C_WIDTH + KV_WIDTH], kn_ref[...], onesk_ref[...], cos, sin)
    q_o[0] = (q * (ATTN_SCALE * math.log2(math.e))).T.astype(q_o.dtype)
    for j in range(C_KV_HEADS):
        k_o[0, j] = k[:, j * HEAD_DIM:(j + 1) * HEAD_DIM].astype(k_o.dtype)
    v_o[0] = z[:, C_WIDTH + KV_WIDTH:C_WIDTH + 2 * KV_WIDTH].T.astype(v_o.dtype)


def _attn_prep(z, q_norm, k_norm, ctx_len, tm):
    bsz, rows, ncol = z.shape
    cos, sin = _rope_tables(rows, ctx_len)
    qn = jnp.tile(q_norm, C_HEADS).reshape(1, C_WIDTH)
    kn = jnp.tile(k_norm, C_KV_HEADS).reshape(1, KV_WIDTH)
    const = lambda shape: pl.BlockSpec(shape, lambda b, t: (0,) * len(shape))
    return pl.pallas_call(
        _attn_prep_kernel,
        grid=(bsz, rows // tm),
        in_specs=[pl.BlockSpec((1, tm, ncol), lambda b, t: (b, t, 0)),
                  const((1, C_WIDTH)), const((1, KV_WIDTH)),
                  pl.BlockSpec((tm, LANES), lambda b, t: (t, 0)),
                  pl.BlockSpec((tm, LANES), lambda b, t: (t, 0)),
                  const((C_WIDTH, C_WIDTH)), const((KV_WIDTH, KV_WIDTH))],
        out_specs=[pl.BlockSpec((1, C_WIDTH, tm), lambda b, t: (b, 0, t)),
                   pl.BlockSpec((1, C_KV_HEADS, tm, HEAD_DIM), lambda b, t: (b, 0, t, 0)),
                   pl.BlockSpec((1, KV_WIDTH, tm), lambda b, t: (b, 0, t))],
        out_shape=[jax.ShapeDtypeStruct((bsz, C_WIDTH, rows), BF16),
                   jax.ShapeDtypeStruct((bsz, C_KV_HEADS, rows, HEAD_DIM), BF16),
                   jax.ShapeDtypeStruct((bsz, KV_WIDTH, rows), BF16)],
        compiler_params=_cparams(("parallel", "parallel")),
        name="attn_prep",
    )(z, qn, kn, cos, sin, _block_ones(C_WIDTH, HEAD_DIM), _block_ones(KV_WIDTH, HEAD_DIM))


def _attn_kernel(qt_ref, k_ref, vt_ref, o_ref, m_ref, l_ref, acc_ref):
    kt = pl.program_id(3)

    @pl.when(kt == 0)
    def _():
        m_ref[...] = jnp.full_like(m_ref, NEG_INF)
        l_ref[...] = jnp.zeros_like(l_ref)
        acc_ref[...] = jnp.zeros_like(acc_ref)

    kb = k_ref[0, 0]
    vt = vt_ref[0]
    grp = range(C_GROUP)
    s = [_dot(kb, qt_ref[0, g * HEAD_DIM:(g + 1) * HEAD_DIM, :]) for g in grp]
    m_old = [m_ref[g] for g in grp]
    m_new = [jnp.maximum(m_old[g], jnp.max(s[g], axis=0, keepdims=True)) for g in grp]
    p = [jnp.exp2(s[g] - m_new[g]) for g in grp]
    pv = [_dot(vt, p[g].astype(BF16)) for g in grp]
    for g in grp:
        alpha = jnp.exp2(m_old[g] - m_new[g])
        l_ref[g] = alpha * l_ref[g] + jnp.sum(p[g], axis=0, keepdims=True)
        acc_ref[g] = alpha * acc_ref[g] + pv[g]
        m_ref[g] = m_new[g]

    @pl.when(kt == pl.num_programs(3) - 1)
    def _():
        o_ref[0] = jnp.concatenate(
            [acc_ref[g] / l_ref[g] for g in grp], axis=0).astype(o_ref.dtype)


ATTN_MAX_KEY_TILE = 2816


def _key_tile(rows):
    return max(t for t in range(LANES, min(rows, ATTN_MAX_KEY_TILE) + 1, LANES) if rows % t == 0)


def _attention(qt, k, vt, ctx_len, tq, tk):
    bsz, _, rows = qt.shape
    seq = rows - ctx_len
    qoff = ctx_len // tq
    gw = C_GROUP * HEAD_DIM
    return pl.pallas_call(
        _attn_kernel,
        grid=(bsz, C_KV_HEADS, seq // tq, rows // tk),
        in_specs=[pl.BlockSpec((1, gw, tq), lambda b, j, i, kk: (b, j, i + qoff)),
                  pl.BlockSpec((1, 1, tk, HEAD_DIM), lambda b, j, i, kk: (b, j, kk, 0)),
                  pl.BlockSpec((1, HEAD_DIM, tk), lambda b, j, i, kk: (b, j, kk))],
        out_specs=pl.BlockSpec((1, gw, tq), lambda b, j, i, kk: (b, j, i)),
        out_shape=jax.ShapeDtypeStruct((bsz, C_WIDTH, seq), BF16),
        scratch_shapes=[pltpu.VMEM((C_GROUP, 1, tq), F32),
                        pltpu.VMEM((C_GROUP, 1, tq), F32),
                        pltpu.VMEM((C_GROUP, HEAD_DIM, tq), F32)],
        compiler_params=_cparams(("parallel", "parallel", "parallel", "arbitrary")),
        name="attention",
    )(qt, k, vt)


def _conv_kernel(zc_ref, zp_ref, zn_ref, w_ref, b_ref, lg_ref, lb_ref, o_ref, ybuf, *, tm, halo):
    t = pl.program_id(1)
    nt = pl.num_programs(1)
    ucol = C_WIDTH + 2 * KV_WIDTH

    def glu(z):
        u = z[:, ucol:ucol + 2 * D_WIDTH]
        return u[:, :D_WIDTH] * jax.nn.sigmoid(u[:, D_WIDTH:])

    ybuf[0:halo, :] = jnp.where(t == 0, 0.0, glu(zp_ref[0]))
    ybuf[halo:halo + tm, :] = glu(zc_ref[0])
    ybuf[halo + tm:, :] = jnp.where(t == nt - 1, 0.0, glu(zn_ref[0]))
    acc = jnp.zeros((tm, D_WIDTH), F32)
    for j in range(D_CONV_WIDTH):
        off = halo - D_PAD + j
        acc = acc + w_ref[j:j + 1, :] * ybuf[off:off + tm, :]
    y = acc + b_ref[...]
    mu = jnp.mean(y, axis=-1, keepdims=True)
    yc = y - mu
    var = jnp.mean(yc * yc, axis=-1, keepdims=True)
    yn = yc * lax.rsqrt(var + LN_EPS) * lg_ref[...] + lb_ref[...]
    o_ref[0] = (yn * jax.nn.sigmoid(yn)).astype(o_ref.dtype)


def _conformer_conv(z, dw_w, dw_b, cn_g, cn_b, ctx_len, tm):
    bsz, rows, ncol = z.shape
    seq = rows - ctx_len
    halo = 2 * SUBLANES
    hb = tm // halo
    off = ctx_len // tm
    offh = ctx_len // halo
    nh = seq // halo
    kern = functools.partial(_conv_kernel, tm=tm, halo=halo)
    row = lambda w: pl.BlockSpec((1, w), lambda b, t: (0, 0))
    return pl.pallas_call(
        kern,
        grid=(bsz, seq // tm),
        in_specs=[pl.BlockSpec((1, tm, ncol), lambda b, t: (b, t + off, 0)),
                  pl.BlockSpec((1, halo, ncol),
                               lambda b, t: (b, offh + jnp.maximum(t * hb - 1, 0), 0)),
                  pl.BlockSpec((1, halo, ncol),
                               lambda b, t: (b, offh + jnp.minimum((t + 1) * hb, nh - 1), 0)),
                  pl.BlockSpec((D_CONV_WIDTH, D_WIDTH), lambda b, t: (0, 0)),
                  row(D_WIDTH), row(D_WIDTH), row(D_WIDTH)],
        out_specs=pl.BlockSpec((1, tm, D_WIDTH), lambda b, t: (b, t, 0)),
        out_shape=jax.ShapeDtypeStruct((bsz, seq, D_WIDTH), BF16),
        scratch_shapes=[pltpu.VMEM((tm + 2 * halo, D_WIDTH), F32)],
        compiler_params=_cparams(("parallel", "parallel")),
        name="conformer_conv",
    )(z, z, z, dw_w, dw_b.reshape(1, -1), cn_g.reshape(1, -1), cn_b.reshape(1, -1))


def _forward(x, c, ctx, c_ctx, l0, l1, norm_f):
    bsz, seq, d = x.shape
    ctx_len = ctx.shape[1]
    tm = min(256, ctx_len)
    ctx_tiles = ctx_len // tm
    xs = jnp.concatenate([ctx, x], axis=1)

    mod = _modulation(c, c_ctx, l0["mod_w"], l0["mod_b"])
    z = _normmod_matmul(xs, l0["norm1"], mod, l0["w_in"], tm, ctx_tiles)
    r, v, g, bonus, kk, lwf, kdf, bf, lwr, kdr, br = _rwkv_prep(z, l0, tm, ctx_tiles)
    yf, yr = _rwkv_scan(r, v, kk, lwf, kdf, bf, lwr, kdr, br, ctx_len)
    o_rwkv = _rwkv_finish(yf, yr, bonus, g, l0["lnx_g"], l0["lnx_b"], tm)
    fcs = _fnet_chan(z, tm)
    ts_c = min(256, ctx_len)
    ts_l = min(512, seq)
    f_ctx = _fnet_seq(fcs, 0, ctx_len, ts_c, ts_c)
    f_lat = _fnet_seq(fcs, ctx_len, seq, ts_l, min(512, seq, ctx_len))
    o_fnet = jnp.concatenate([f_ctx, f_lat], axis=1)
    xs = _proj_residual(o_rwkv, o_fnet, xs, mod, l0["w_out"], tm, 0, ctx_tiles)
    xs = _peer(xs, l0["norm2"], mod, l0, tm, 0, ctx_tiles, False, norm_f)

    mod = _modulation(c, c_ctx, l1["mod_w"], l1["mod_b"])
    z = _normmod_matmul(xs, l1["norm1"], mod, l1["w_in"], tm, ctx_tiles)
    qt, k, vt = _attn_prep(z, l1["q_norm"], l1["k_norm"], ctx_len, tm)
    o_attn = _attention(qt, k, vt, ctx_len, tm, _key_tile(ctx_len + seq))
    o_conv = _conformer_conv(z, l1["dw_w"], l1["dw_b"], l1["cn_g"], l1["cn_b"], ctx_len, tm)
    xl = _proj_residual(o_attn, o_conv, xs, mod, l1["w_out"], tm, ctx_tiles, ctx_tiles,
                        o1_transposed=True)
    return _peer(xl, l1["norm2"], mod, l1, tm, ctx_tiles, ctx_tiles, True, norm_f)


def kernel(x, c, ctx, c_ctx, l0_mod_w, l0_mod_b, l0_norm1, l0_w_in, l0_shift_prev, l0_shift_next, l0_w0, l0_w2, l0_a0, l0_a2, l0_g2, l0_k_k, l0_k_a, l0_r_k, l0_lnx_g, l0_lnx_b, l0_w_out, l0_norm2, l0_pq, l0_sk1, l0_sk2, l0_pu, l0_pv, l1_mod_w, l1_mod_b, l1_norm1, l1_w_in, l1_q_norm, l1_k_norm, l1_dw_w, l1_dw_b, l1_cn_g, l1_cn_b, l1_w_out, l1_norm2, l1_pq, l1_sk1, l1_sk2, l1_pu, l1_pv, norm_f):
    l0 = dict(mod_w=l0_mod_w, mod_b=l0_mod_b, norm1=l0_norm1, w_in=l0_w_in,
              shift_prev=l0_shift_prev, shift_next=l0_shift_next, w0=l0_w0, w2=l0_w2,
              a0=l0_a0, a2=l0_a2, g2=l0_g2, k_k=l0_k_k, k_a=l0_k_a, r_k=l0_r_k,
              lnx_g=l0_lnx_g, lnx_b=l0_lnx_b, w_out=l0_w_out, norm2=l0_norm2,
              pq=l0_pq, sk1=l0_sk1, sk2=l0_sk2, pu=l0_pu, pv=l0_pv)
    l1 = dict(mod_w=l1_mod_w, mod_b=l1_mod_b, norm1=l1_norm1, w_in=l1_w_in,
              q_norm=l1_q_norm, k_norm=l1_k_norm, dw_w=l1_dw_w, dw_b=l1_dw_b,
              cn_g=l1_cn_g, cn_b=l1_cn_b, w_out=l1_w_out, norm2=l1_norm2,
              pq=l1_pq, sk1=l1_sk1, sk2=l1_sk2, pu=l1_pu, pv=l1_pv)
    return _forward(x, c, ctx, c_ctx, l0, l1, norm_f)
```

```python
import functools
import math

import jax
import jax.numpy as jnp
import numpy as np
from jax import lax
from jax.experimental import pallas as pl
from jax.experimental.pallas import tpu as pltpu

F32 = jnp.float32
BF16 = jnp.bfloat16
HIGHEST = lax.Precision.HIGHEST

LANES = 128
SUBLANES = 8
VMEM_LIMIT_BYTES = 56 * 1024 * 1024

N_MOD = 6
NORM_EPS = 1e-6
LN_EPS = 1e-5
GRID_W = 64
HEAD_DIM = 64
A_WIDTH = 768
A_HEADS = A_WIDTH // HEAD_DIM
A_RANK_W = 64
A_RANK_A = 64
A_RANK_G = 128
A_LNX_EPS = 64e-5
A_COLS = 3 * A_WIDTH + 2 * A_RANK_W + 2 * A_RANK_A + A_RANK_G
B_WIDTH = 256
B_GROUP_DIM = 64
C_WIDTH = 768
C_HEADS = 12
C_KV_HEADS = 4
C_GROUP = C_HEADS // C_KV_HEADS
KV_WIDTH = C_KV_HEADS * HEAD_DIM
ROPE_AXIS_DIM = HEAD_DIM // 2
ROPE_THETA = 10000.0
ATTN_SCALE = HEAD_DIM ** -0.5
D_WIDTH = 256
D_CONV_WIDTH = 31
D_PAD = D_CONV_WIDTH // 2
PK_HEADS = 8
PK_DIM = 256
PK_HALF = 128
N_KEYS = 128
PK_TOPK = 16

SCAN_CHUNK = 64
NEG_INF = float("-inf")


def _cparams(semantics):
    return pltpu.CompilerParams(dimension_semantics=semantics,
                                vmem_limit_bytes=VMEM_LIMIT_BYTES)


def _split_bf16(x):
    hi = x.astype(BF16)
    lo = (x - hi.astype(F32)).astype(BF16)
    return hi, lo


def _dot(a, b):
    return jnp.dot(a, b, preferred_element_type=F32)


def _dot_hp(a, b):
    return jnp.dot(a, b, preferred_element_type=F32, precision=HIGHEST)


def _dot_nt(a, b, precision=None):
    return lax.dot_general(a, b, (((1,), (1,)), ((), ())),
                           preferred_element_type=F32, precision=precision)


def _dot_tn(a, b, precision=None):
    return lax.dot_general(a, b, (((0,), (0,)), ((), ())),
                           preferred_element_type=F32, precision=precision)


_NN = (((1,), (0,)), ((), ()))
_NT = (((1,), (1,)), ((), ()))
_TN = (((0,), (0,)), ((), ()))


def _mm3(a, b, dims=_NN):
    (ah, al), (bh, bl) = a, b
    dg = lambda x, y: lax.dot_general(x, y, dims, preferred_element_type=F32)
    return dg(ah, bh) + (dg(ah, bl) + dg(al, bh))


def _segsum(x, ones_bd):
    hi, lo = _split_bf16(x)
    return _dot(hi, ones_bd) + _dot(lo, ones_bd)


def _block_ones(width, seg):
    r = np.arange(width) // seg
    return jnp.asarray((r[:, None] == r[None, :]).astype(np.float32), dtype=BF16)


def _mod_kernel(c_ref, w_ref, b_ref, o_ref):
    c = c_ref[...]
    s = c * jax.nn.sigmoid(c)
    o_ref[...] = _dot_hp(s, w_ref[...]) + b_ref[...]


def _modulation(c, c_ctx, mod_w, mod_b):
    bsz, d = c.shape
    rows = SUBLANES * pl.cdiv(bsz + 1, SUBLANES)
    cc = jnp.zeros((rows, d), F32).at[:bsz].set(c).at[bsz].set(c_ctx)
    n = mod_w.shape[1]
    tn = n // 4
    out = pl.pallas_call(
        _mod_kernel,
        grid=(n // tn,),
        in_specs=[pl.BlockSpec((rows, d), lambda j: (0, 0)),
                  pl.BlockSpec((d, tn), lambda j: (0, j)),
                  pl.BlockSpec((1, tn), lambda j: (0, j))],
        out_specs=pl.BlockSpec((rows, tn), lambda j: (0, j)),
        out_shape=jax.ShapeDtypeStruct((rows, n), F32),
        compiler_params=_cparams(("arbitrary",)),
        name="adaln_mod",
    )(cc, mod_w, mod_b.reshape(1, n))
    lat = out[:bsz].reshape(bsz, N_MOD, d)
    ctx = jnp.broadcast_to(out[bsz].reshape(1, N_MOD, d), (bsz, N_MOD, d))
    return jnp.stack([ctx, lat], axis=1)


def _norm_mod(x, g, mod, row):
    ms = jnp.mean(x * x, axis=-1, keepdims=True)
    y = x * lax.rsqrt(ms + NORM_EPS) * g
    return y * (1.0 + mod[row + 1:row + 2]) + mod[row:row + 1]


def _normmod_matmul_kernel(x_ref, g_ref, mod_ref, w_ref, o_ref):
    h = _norm_mod(x_ref[0], g_ref[...], mod_ref[0, 0], 0)
    o_ref[0] = _dot(h.astype(BF16), w_ref[...])


def _normmod_matmul(x, g, mod, w, tm, ctx_tiles):
    bsz, rows, d = x.shape
    n = w.shape[1]
    return pl.pallas_call(
        _normmod_matmul_kernel,
        grid=(bsz, rows // tm),
        in_specs=[pl.BlockSpec((1, tm, d), lambda b, t: (b, t, 0)),
                  pl.BlockSpec((1, d), lambda b, t: (0, 0)),
                  pl.BlockSpec((1, 1, N_MOD, d),
                               lambda b, t: (b, jnp.where(t >= ctx_tiles, 1, 0), 0, 0)),
                  pl.BlockSpec((d, n), lambda b, t: (0, 0))],
        out_specs=pl.BlockSpec((1, tm, n), lambda b, t: (b, t, 0)),
        out_shape=jax.ShapeDtypeStruct((bsz, rows, n), F32),
        compiler_params=_cparams(("parallel", "parallel")),
        name="normmod_proj",
    )(x, g.reshape(1, d), mod, w.astype(BF16))


def _rwkv_prep_kernel(z_ref, zp_ref, zn_ref, mup_ref, mun_ref, w0_ref, w2_ref, a0_ref, a2_ref,
                      g2_ref, kk_ref, ka_ref, rk_ref, ones_ref,
                      r_o, v_o, g_o, bonus_o, kkn_o, lwf_o, kdf_o, bf_o, lwr_o, kdr_o, br_o,
                      *, tm, ctx_tiles):
    t = pl.program_id(1)
    nt = pl.num_programs(1)
    z = z_ref[0][:, :A_COLS]
    zp_row = zp_ref[0][SUBLANES - 1:SUBLANES, :A_COLS]
    zn_row = zn_ref[0][0:1, :A_COLS]
    first = jnp.logical_or(t == 0, t == ctx_tiles)
    last = jnp.logical_or(t == ctx_tiles - 1, t == nt - 1)
    zp_row = jnp.where(first, 0.0, zp_row)
    zn_row = jnp.where(last, 0.0, zn_row)
    ridx = lax.broadcasted_iota(jnp.int32, (tm, 1), 0)
    z_prev = jnp.where(ridx == 0, zp_row, pltpu.roll(z, 1, 0))
    z_next = jnp.where(ridx == tm - 1, zn_row, pltpu.roll(z, tm - 1, 0))
    zs = z + mup_ref[...] * (z_prev - z) + mun_ref[...] * (z_next - z)

    w = A_WIDTH
    r = zs[:, 0:w]
    k = zs[:, w:2 * w]
    v = zs[:, 2 * w:3 * w]
    o = 3 * w
    xw = zs[:, o:o + 2 * A_RANK_W]
    o += 2 * A_RANK_W
    xa = zs[:, o:o + 2 * A_RANK_A]
    o += 2 * A_RANK_A
    xg = zs[:, o:o + A_RANK_G]
    ones_bd = ones_ref[...]

    g_o[0] = _dot_hp(jax.nn.sigmoid(xg), g2_ref[...])
    kk = k * kk_ref[...]
    nrm = jnp.sqrt(_segsum(kk * kk, ones_bd))
    kk = kk / jnp.maximum(nrm, 1e-12)
    r_o[0] = r
    v_o[0] = v
    kkn_o[0] = kk
    bonus = jnp.zeros_like(r)
    tw = jnp.tanh(xw)
    for d, (lw_o, kd_o, b_o) in enumerate(((lwf_o, kdf_o, bf_o), (lwr_o, kdr_o, br_o))):
        wl = w0_ref[d:d + 1, :] + _dot_hp(tw[:, d * A_RANK_W:(d + 1) * A_RANK_W], w2_ref[d])
        w_log = -jax.nn.softplus(-wl) - 0.5
        lw_o[0] = -jnp.exp(w_log)
        a_gate = jax.nn.sigmoid(
            a0_ref[d:d + 1, :] + _dot_hp(xa[:, d * A_RANK_A:(d + 1) * A_RANK_A], a2_ref[d]))
        k_d = k * (1.0 + (a_gate - 1.0) * ka_ref[...])
        kd_o[0] = k_d
        b_o[0] = kk * a_gate
        bonus = bonus + _segsum(r * k_d * rk_ref[...], ones_bd) * v
    bonus_o[0] = bonus


def _rwkv_prep(z, p, tm, ctx_tiles):
    bsz, rows, ncol = z.shape
    w = A_WIDTH
    hb = tm // SUBLANES
    nblk8 = rows // SUBLANES
    row_spec = lambda width: pl.BlockSpec((1, width), lambda b, t: (0, 0))
    full = lambda shape: pl.BlockSpec(shape, lambda b, t: (0,) * len(shape))
    out_spec = pl.BlockSpec((1, tm, w), lambda b, t: (b, t, 0))
    out_shape = jax.ShapeDtypeStruct((bsz, rows, w), F32)
    kern = functools.partial(_rwkv_prep_kernel, tm=tm, ctx_tiles=ctx_tiles)
    return pl.pallas_call(
        kern,
        grid=(bsz, rows // tm),
        in_specs=[pl.BlockSpec((1, tm, ncol), lambda b, t: (b, t, 0)),
                  pl.BlockSpec((1, SUBLANES, ncol),
                               lambda b, t: (b, jnp.maximum(t * hb - 1, 0), 0)),
                  pl.BlockSpec((1, SUBLANES, ncol),
                               lambda b, t: (b, jnp.minimum((t + 1) * hb, nblk8 - 1), 0)),
                  row_spec(A_COLS), row_spec(A_COLS),
                  full((2, w)), full((2, A_RANK_W, w)), full((2, w)), full((2, A_RANK_A, w)),
                  full((A_RANK_G, w)), row_spec(w), row_spec(w), row_spec(w), full((w, w))],
        out_specs=[out_spec] * 11,
        out_shape=[out_shape] * 11,
        compiler_params=_cparams(("parallel", "parallel")),
        name="rwkv_prep",
    )(z, z, z, p["shift_prev"].reshape(1, -1), p["shift_next"].reshape(1, -1),
      p["w0"], p["w2"], p["a0"], p["a2"], p["g2"], p["k_k"].reshape(1, w),
      p["k_a"].reshape(1, w), p["r_k"].reshape(1, w), _block_ones(w, HEAD_DIM))


SCAN_HEADS = 4


def _scan_kernel(rf, vf, kkf, lwf, kdf, bf, rr, vr, kkr, lwr, kdr, br, yf_o, yr_o, s_ref, *, c):
    @pl.when(pl.program_id(2) == 0)
    def _():
        s_ref[...] = jnp.zeros_like(s_ref)

    ri = lax.broadcasted_iota(jnp.int32, (c, c), 0)
    ci = lax.broadcasted_iota(jnp.int32, (c, c), 1)
    ri2 = lax.broadcasted_iota(jnp.int32, (c, 2 * c), 0)
    ci2 = lax.broadcasted_iota(jnp.int32, (c, 2 * c), 1)
    ci2 = jnp.where(ci2 >= c, ci2 - c, ci2)
    dirs = ((rf, vf, kkf, lwf, kdf, bf, ci <= ri, ci < ri, ci2 <= ri2, c - 1),
            (rr, vr, kkr, lwr, kdr, br, ci >= ri, ci > ri, ci2 >= ri2, 0))
    lhs, rhs, vs, ss, pend, incl, strict = [], [], [], [], [], [], []
    for d, (r_, v_, kk_, lw_, kd_, b_, m_incl, m_strict, m_incl2, edge) in enumerate(dirs):
        lw = lw_[0]
        tri = m_incl.astype(F32).astype(BF16)
        lw_h = lw.astype(BF16)
        lw_m, lw_l = _split_bf16(lw - lw_h.astype(F32))
        lcum = _dot(tri, lw_h) + (_dot(tri, lw_m) + _dot(tri, lw_l))
        p_in = jnp.exp(lcum)
        p_inv = jnp.exp(-lcum)
        lhs_all = jnp.concatenate([-kk_[0] * jnp.exp(lcum - lw), r_[0] * p_in], axis=0)
        rhs_all = jnp.concatenate([b_[0] * p_inv, kd_[0] * p_inv], axis=0)
        v_all = v_[0]
        for h in range(SCAN_HEADS):
            sl = slice(h * HEAD_DIM, (h + 1) * HEAD_DIM)
            lhs.append(lhs_all[:, sl])
            rhs.append(rhs_all[:, sl])
            vs.append(v_all[:, sl])
            ss.append(s_ref[d, h])
            pend.append(p_in[edge:edge + 1, sl])
            incl.append(m_incl2)
            strict.append(m_strict)
    n = len(lhs)
    idx = range(n)
    lhs2 = [_split_bf16(lhs[i]) for i in idx]
    rhs2 = [_split_bf16(rhs[i]) for i in idx]
    v2 = [_split_bf16(vs[i]) for i in idx]
    s2 = [_split_bf16(ss[i]) for i in idx]
    g = [_mm3(lhs2[i], rhs2[i], _NT) for i in idx]
    xs = [_mm3(lhs2[i], s2[i], _NT) for i in idx]
    a_ak = [_split_bf16(jnp.where(strict[i], g[i][:c, c:], 0.0)) for i in idx]
    x = [xs[i][:c] + _mm3(a_ak[i], v2[i]) for i in idx]
    apow = [_split_bf16(jnp.where(strict[i], g[i][:c, :c], 0.0)) for i in idx]
    steps = int(math.log2(c))
    for k in range(steps):
        x2 = [_split_bf16(x[i]) for i in idx]
        x = [x[i] + _mm3(apow[i], x2[i]) for i in idx]
        if k + 1 < steps:
            apow = [_split_bf16(_mm3(apow[i], apow[i])) for i in idx]
    x2 = [_split_bf16(x[i]) for i in idx]
    sav = [tuple(jnp.concatenate([x2[i][t], v2[i][t]], axis=0) for t in range(2)) for i in idx]
    a_r = [_split_bf16(jnp.where(incl[i], g[i][c:], 0.0)) for i in idx]
    y = [xs[i][c:] + _mm3(a_r[i], sav[i]) for i in idx]
    s_new = [(ss[i] + _mm3(sav[i], rhs2[i], _TN)) * pend[i] for i in idx]
    for d, y_o in enumerate((yf_o, yr_o)):
        for h in range(SCAN_HEADS):
            s_ref[d, h] = s_new[d * SCAN_HEADS + h]
        y_o[0] = jnp.concatenate(y[d * SCAN_HEADS:(d + 1) * SCAN_HEADS], axis=1)


def _rwkv_scan(r, v, kk, lwf, kdf, bf, lwr, kdr, br, ctx_len):
    bsz, rows, w = r.shape
    c = SCAN_CHUNK
    nch = rows // c
    cch = ctx_len // c
    lanes = SCAN_HEADS * HEAD_DIM
    fwd = lambda b, hp, i: (b, i, hp)
    rev = lambda b, hp, i: (b, jnp.where(i < cch, cch - 1 - i, nch - 1 + cch - i), hp)
    blk = lambda im: pl.BlockSpec((1, c, lanes), im)
    kern = functools.partial(_scan_kernel, c=c)
    out_shape = jax.ShapeDtypeStruct((bsz, rows, w), F32)
    return pl.pallas_call(
        kern,
        grid=(bsz, w // lanes, nch),
        in_specs=[blk(fwd)] * 6 + [blk(rev)] * 6,
        out_specs=[blk(fwd), blk(rev)],
        out_shape=[out_shape, out_shape],
        scratch_shapes=[pltpu.VMEM((2, SCAN_HEADS, HEAD_DIM, HEAD_DIM), F32)],
        compiler_params=_cparams(("parallel", "parallel", "arbitrary")),
        name="rwkv_scan",
    )(r, v, kk, lwf, kdf, bf, r, v, kk, lwr, kdr, br)


def _rwkv_finish_kernel(yf, yr, bonus, g, lg, lb, ones_ref, o_ref):
    y = yf[0] + yr[0]
    ones_bd = ones_ref[...]
    inv = 1.0 / HEAD_DIM
    mu = _segsum(y, ones_bd) * inv
    yc = y - mu
    var = _segsum(yc * yc, ones_bd) * inv
    yn = yc * lax.rsqrt(var + A_LNX_EPS) * lg[...] + lb[...]
    o_ref[0] = ((yn + bonus[0]) * g[0]).astype(o_ref.dtype)


def _rwkv_finish(yf, yr, bonus, g, lnx_g, lnx_b, tm):
    bsz, rows, w = yf.shape
    blk = pl.BlockSpec((1, tm, w), lambda b, t: (b, t, 0))
    row = pl.BlockSpec((1, w), lambda b, t: (0, 0))
    return pl.pallas_call(
        _rwkv_finish_kernel,
        grid=(bsz, rows // tm),
        in_specs=[blk, blk, blk, blk, row, row, pl.BlockSpec((w, w), lambda b, t: (0, 0))],
        out_specs=blk,
        out_shape=jax.ShapeDtypeStruct((bsz, rows, w), BF16),
        compiler_params=_cparams(("parallel", "parallel")),
        name="rwkv_finish",
    )(yf, yr, bonus, g, lnx_g.reshape(1, w), lnx_b.reshape(1, w), _block_ones(w, HEAD_DIM))


def _fnet_chan_kernel(z_ref, m_ref, o_ref):
    f = z_ref[0][:, A_COLS:A_COLS + B_WIDTH]
    o_ref[0] = _dot(f.astype(BF16), m_ref[...]).astype(o_ref.dtype)


def _fnet_chan(z, tm):
    bsz, rows, ncol = z.shape
    j = np.arange(B_WIDTH)
    same = (j[:, None] // B_GROUP_DIM) == (j[None, :] // B_GROUP_DIM)
    ang = 2.0 * np.pi * ((j[:, None] % B_GROUP_DIM) * (j[None, :] % B_GROUP_DIM) % B_GROUP_DIM) / B_GROUP_DIM
    m = np.concatenate([np.where(same, np.cos(ang), 0.0), np.where(same, np.sin(ang), 0.0)], axis=1)
    return pl.pallas_call(
        _fnet_chan_kernel,
        grid=(bsz, rows // tm),
        in_specs=[pl.BlockSpec((1, tm, ncol), lambda b, t: (b, t, 0)),
                  pl.BlockSpec((B_WIDTH, 2 * B_WIDTH), lambda b, t: (0, 0))],
        out_specs=pl.BlockSpec((1, tm, 2 * B_WIDTH), lambda b, t: (b, t, 0)),
        out_shape=jax.ShapeDtypeStruct((bsz, rows, 2 * B_WIDTH), BF16),
        compiler_params=_cparams(("parallel", "parallel")),
        name="fnet_chan",
    )(z, jnp.asarray(m, dtype=BF16))


def _fnet_seq_kernel(fcs_ref, cb_ref, sb_ref, c0_ref, s0_ref, o_ref, acc_ref, *, scale, nb):
    k = pl.program_id(1)

    @pl.when(k == 0)
    def _():
        acc_ref[...] = jnp.zeros_like(acc_ref)

    cb = cb_ref[...]
    sb = sb_ref[...]
    c0 = c0_ref[0]
    s0 = s0_ref[0]
    cm = (c0 * cb - s0 * sb).astype(BF16)
    sm = (s0 * cb + c0 * sb).astype(BF16)
    for b in range(nb):
        fcs = fcs_ref[b]
        acc_ref[b] += _dot(cm, fcs[:, :B_WIDTH]) - _dot(sm, fcs[:, B_WIDTH:])

    @pl.when(k == pl.num_programs(1) - 1)
    def _():
        o_ref[...] = (acc_ref[...] * scale).astype(o_ref.dtype)


def _fnet_seq(fcs, row0, length, ts, tk):
    bsz = fcs.shape[0]
    ds = jnp.arange(ts, dtype=jnp.int32)[:, None]
    tt = jnp.arange(length, dtype=jnp.int32)[None, :]
    ang = ((ds * tt) % length).astype(F32) * (2.0 * math.pi / length)
    cb, sb = jnp.cos(ang), jnp.sin(ang)
    s0 = (jnp.arange(length // ts, dtype=jnp.int32) * ts)[:, None]
    ang0 = ((s0 * tt) % length).astype(F32) * (2.0 * math.pi / length)
    c0, sn0 = jnp.cos(ang0)[:, None, :], jnp.sin(ang0)[:, None, :]
    kern = functools.partial(_fnet_seq_kernel, scale=1.0 / math.sqrt(length * B_GROUP_DIM), nb=bsz)
    koff = row0 // tk
    return pl.pallas_call(
        kern,
        grid=(length // ts, length // tk),
        in_specs=[pl.BlockSpec((bsz, tk, 2 * B_WIDTH), lambda s, k: (0, k + koff, 0)),
                  pl.BlockSpec((ts, tk), lambda s, k: (0, k)),
                  pl.BlockSpec((ts, tk), lambda s, k: (0, k)),
                  pl.BlockSpec((1, 1, tk), lambda s, k: (s, 0, k)),
                  pl.BlockSpec((1, 1, tk), lambda s, k: (s, 0, k))],
        out_specs=pl.BlockSpec((bsz, ts, B_WIDTH), lambda s, k: (0, s, 0)),
        out_shape=jax.ShapeDtypeStruct((bsz, length, B_WIDTH), BF16),
        scratch_shapes=[pltpu.VMEM((bsz, ts, B_WIDTH), F32)],
        compiler_params=_cparams(("parallel", "arbitrary")),
        name="fnet_seq",
    )(fcs, cb, sb, c0, sn0)


def _proj_residual_kernel(o1_ref, o2_ref, x_ref, mod_ref, w1_ref, w2_ref, out_ref, *, o1_transposed):
    o1 = o1_ref[0]
    if o1_transposed:
        o1 = o1.astype(F32).T.astype(BF16)
    mix = _dot(o1, w1_ref[...]) + _dot(o2_ref[0], w2_ref[...])
    gate = mod_ref[0, 0][2:3]
    out_ref[0] = x_ref[0] + gate * mix


def _proj_residual(o1, o2, x, mod, w_out, tm, tile_off, ctx_tiles, o1_transposed=False):
    bsz, rows, w2 = o2.shape
    w1 = o1.shape[1] if o1_transposed else o1.shape[2]
    d = x.shape[2]
    o1_spec = (pl.BlockSpec((1, w1, tm), lambda b, t: (b, 0, t)) if o1_transposed
               else pl.BlockSpec((1, tm, w1), lambda b, t: (b, t, 0)))
    kern = functools.partial(_proj_residual_kernel, o1_transposed=o1_transposed)
    return pl.pallas_call(
        kern,
        grid=(bsz, rows // tm),
        in_specs=[o1_spec,
                  pl.BlockSpec((1, tm, w2), lambda b, t: (b, t, 0)),
                  pl.BlockSpec((1, tm, d), lambda b, t: (b, t + tile_off, 0)),
                  pl.BlockSpec((1, 1, N_MOD, d),
                               lambda b, t: (b, jnp.where(t + tile_off >= ctx_tiles, 1, 0), 0, 0)),
                  pl.BlockSpec((w1, d), lambda b, t: (0, 0)),
                  pl.BlockSpec((w2, d), lambda b, t: (0, 0))],
        out_specs=pl.BlockSpec((1, tm, d), lambda b, t: (b, t, 0)),
        out_shape=jax.ShapeDtypeStruct((bsz, rows, d), F32),
        compiler_params=_cparams(("parallel", "parallel")),
        name="proj_residual",
    )(o1, o2, x, mod, w_out[:w1].astype(BF16), w_out[w1:].astype(BF16))


def _merge_exchange_network(n):
    pairs = []
    p = 1
    while p < n:
        k = p
        while k >= 1:
            for j in range(k % p, n - k, 2 * k):
                for i in range(min(k, n - j - k)):
                    if (i + j) // (2 * p) == (i + j + k) // (2 * p):
                        pairs.append((i + j, i + j + k))
            k //= 2
        p *= 2
    return pairs


def _top16_rows(s):
    nslab = s.shape[0] // SUBLANES
    slabs = [s[k * SUBLANES:(k + 1) * SUBLANES] for k in range(nslab)]
    for a, b in _merge_exchange_network(nslab):
        slabs[a], slabs[b] = jnp.maximum(slabs[a], slabs[b]), jnp.minimum(slabs[a], slabs[b])
    rows = []
    for t in range(PK_TOPK):
        m = jnp.max(slabs[0], axis=0, keepdims=True)
        rows.append(m)
        hit = slabs[0] >= m
        for k in range(nslab - 1 - t):
            slabs[k] = jnp.where(hit, slabs[k + 1], slabs[k])
    return rows


_PAIR_IDX = [(i, j) for i in range(PK_TOPK) for j in range(PK_TOPK) if (i + 1) * (j + 1) <= PK_TOPK]
BF16_SUBLANES = 2 * SUBLANES


def _rows_bf16(row):
    packed = jnp.broadcast_to(row, (BF16_SUBLANES, row.shape[1])).astype(BF16)
    return jnp.tile(packed, (N_KEYS // BF16_SUBLANES, 1))


def _peer_kernel(x_ref, g_ref, mod_ref, pqh_ref, pql_ref, sk1_ref, sk2_ref, pu_ref, pvt_ref, gf_ref,
                 out_ref, ht_ref, r2_ref, e2_ref, c1_ref, w1_ref, cand_ref, acc_ref,
                 *, tm, e1_per_blk, final_norm):
    eb = pl.program_id(2)

    @pl.when(eb == 0)
    def _():
        h = _norm_mod(x_ref[0], g_ref[...], mod_ref[0, 0], 3)
        ht = h.T
        hhi, hlo = _split_bf16(ht)
        ht_ref[...] = hhi
        qt = _dot(pqh_ref[...], hhi) + _dot(pqh_ref[...], hlo) + _dot(pql_ref[...], hhi)
        for hd in range(PK_HEADS):
            q1 = qt[hd * PK_DIM:hd * PK_DIM + PK_HALF]
            q2 = qt[hd * PK_DIM + PK_HALF:(hd + 1) * PK_DIM]
            s1 = _dot_hp(sk1_ref[hd], q1)
            s2 = _dot_hp(sk2_ref[hd], q2)
            v1 = _top16_rows(s1)
            v2 = _top16_rows(s2)
            cand_ref[...] = jnp.full_like(cand_ref, NEG_INF)
            for n, (i, j) in enumerate(_PAIR_IDX):
                cand_ref[n:n + 1, :] = v1[i] + v2[j]
            cand = cand_ref[...]
            top = v1[0] + v2[0]
            zsum = jnp.zeros_like(top)
            tau = top
            for _ in range(PK_TOPK):
                m = jnp.max(cand, axis=0, keepdims=True)
                zsum = zsum + jnp.exp(m - top)
                tau = m
                cand = jnp.where(cand >= m, NEG_INF, cand)
            rank2 = jnp.full((N_KEYS, tm), float(PK_TOPK), F32)
            for j in reversed(range(PK_TOPK)):
                rank2 = jnp.where(s2 >= v2[j], float(j), rank2)
            count1 = jnp.zeros((N_KEYS, tm), F32)
            for i in reversed(range(PK_TOPK)):
                cnt = jnp.zeros_like(top)
                for j in range(PK_TOPK // (i + 1)):
                    cnt = cnt + jnp.where(v1[i] + v2[j] >= tau, 1.0, 0.0)
                count1 = jnp.where(s1 >= v1[i], cnt, count1)
            r2_ref[hd] = rank2.astype(BF16)
            c1_ref[hd] = count1
            e2_ref[hd] = jnp.exp(s2 - v2[0]).astype(BF16)
            w1_ref[hd] = jnp.exp(s1 - v1[0]) / zsum
        acc_ref[...] = jnp.zeros_like(acc_ref)

    sub = 2 * N_KEYS
    nsub = e1_per_blk // 2
    ht = ht_ref[...]

    def up(i):
        return _dot(pu_ref[i * sub:(i + 1) * sub, :], ht)

    def activation(i, hu):
        acts = []
        for jj in range(2):
            e1 = eb * e1_per_blk + 2 * i + jj
            gate = jnp.zeros((N_KEYS, tm), BF16)
            for hd in range(PK_HEADS):
                count = _rows_bf16(c1_ref[hd, pl.ds(e1, 1), :])
                weight = _rows_bf16(w1_ref[hd, pl.ds(e1, 1), :])
                gate = gate + jnp.where(r2_ref[hd] < count, e2_ref[hd] * weight, 0)
            u = hu[jj * N_KEYS:(jj + 1) * N_KEYS]
            act = 0.5 * u * (1.0 + lax.erf(u * (1.0 / math.sqrt(2.0))))
            acts.append(act.astype(BF16) * gate)
        return jnp.concatenate(acts, axis=0)

    hu_next = up(0)
    down = None
    for i in range(nsub):
        hu = hu_next
        if i + 1 < nsub:
            hu_next = up(i + 1)
        part = _dot(pvt_ref[:, i * sub:(i + 1) * sub], activation(i, hu))
        down = part if down is None else down + part
    acc_ref[...] += down

    @pl.when(eb == pl.num_programs(2) - 1)
    def _():
        y = x_ref[0] + mod_ref[0, 0][5:6] * acc_ref[...].T
        if final_norm:
            ms = jnp.mean(y * y, axis=-1, keepdims=True)
            y = y * lax.rsqrt(ms + NORM_EPS) * gf_ref[...]
        out_ref[0] = y


def _peer(x, g, mod, p, tm, tile_off, ctx_tiles, final_norm, norm_f):
    bsz, rows, d = x.shape
    n_exp = p["pu"].shape[0]
    e1_per_blk = 16
    eblk = e1_per_blk * N_KEYS
    pqt = p["pq"].T
    pqh = pqt.astype(BF16)
    pql = (pqt - pqh.astype(F32)).astype(BF16)
    nq = pqt.shape[0]
    kern = functools.partial(_peer_kernel, tm=tm, e1_per_blk=e1_per_blk, final_norm=final_norm)
    const = lambda shape: pl.BlockSpec(shape, lambda b, t, e: (0,) * len(shape))
    return pl.pallas_call(
        kern,
        grid=(bsz, rows // tm, n_exp // eblk),
        in_specs=[pl.BlockSpec((1, tm, d), lambda b, t, e: (b, t, 0)),
                  const((1, d)),
                  pl.BlockSpec((1, 1, N_MOD, d),
                               lambda b, t, e: (b, jnp.where(t + tile_off >= ctx_tiles, 1, 0), 0, 0)),
                  const((nq, d)), const((nq, d)),
                  const((PK_HEADS, N_KEYS, PK_HALF)), const((PK_HEADS, N_KEYS, PK_HALF)),
                  pl.BlockSpec((eblk, d), lambda b, t, e: (e, 0)),
                  pl.BlockSpec((d, eblk), lambda b, t, e: (0, e)),
                  const((1, d))],
        out_specs=pl.BlockSpec((1, tm, d), lambda b, t, e: (b, t, 0)),
        out_shape=jax.ShapeDtypeStruct((bsz, rows, d), F32),
        scratch_shapes=[pltpu.VMEM((d, tm), BF16),
                        pltpu.VMEM((PK_HEADS, N_KEYS, tm), BF16),
                        pltpu.VMEM((PK_HEADS, N_KEYS, tm), BF16),
                        pltpu.VMEM((PK_HEADS, N_KEYS, tm), F32),
                        pltpu.VMEM((PK_HEADS, N_KEYS, tm), F32),
                        pltpu.VMEM((SUBLANES * pl.cdiv(len(_PAIR_IDX), SUBLANES), tm), F32),
                        pltpu.VMEM((d, tm), F32)],
        compiler_params=_cparams(("parallel", "parallel", "arbitrary")),
        name="peer",
    )(x, g.reshape(1, d), mod, pqh, pql, p["sk1"], p["sk2"],
      p["pu"].astype(BF16), p["pv"].T.astype(BF16), norm_f.reshape(1, d))


def _rope_tables(rows, ctx_len):
    t = jnp.arange(rows - ctx_len, dtype=jnp.int32)
    inv = ROPE_THETA ** (-jnp.arange(0, ROPE_AXIS_DIM, 2, dtype=F32) / ROPE_AXIS_DIM)
    ang_r = (t // GRID_W).astype(F32)[:, None] * inv
    ang_c = (t % GRID_W).astype(F32)[:, None] * inv
    cos = jnp.concatenate([jnp.cos(ang_r)] * 2 + [jnp.cos(ang_c)] * 2, axis=1)
    sin = jnp.concatenate([-jnp.sin(ang_r), jnp.sin(ang_r), -jnp.sin(ang_c), jnp.sin(ang_c)], axis=1)
    cos = jnp.concatenate([jnp.ones((ctx_len, HEAD_DIM), F32), cos], axis=0)
    sin = jnp.concatenate([jnp.zeros((ctx_len, HEAD_DIM), F32), sin], axis=0)
    return jnp.tile(cos, (1, 2)), jnp.tile(sin, (1, 2))


def _head_norm_rope(x, gain, ones_bd, cos, sin):
    ms = _segsum(x * x, ones_bd) * (1.0 / HEAD_DIM)
    y = x * lax.rsqrt(ms + NORM_EPS) * gain
    outs = []
    half = ROPE_AXIS_DIM // 2
    lane = lax.broadcasted_iota(jnp.int32, (1, LANES), 1)
    first_half = (lane % ROPE_AXIS_DIM) < half
    for i in range(x.shape[1] // LANES):
        yc = y[:, i * LANES:(i + 1) * LANES]
        partner = jnp.where(first_half, pltpu.roll(yc, LANES - half, 1), pltpu.roll(yc, half, 1))
        outs.append(yc * cos + partner * sin)
    return jnp.concatenate(outs, axis=1)


def _attn_prep_kernel(z_ref, qn_ref, kn_ref, cos_ref, sin_ref, onesq_ref, onesk_ref,
                      q_o, k_o, v_o):
    z = z_ref[0]
    cos = cos_ref[...]
    sin = sin_ref[...]
    q = _head_norm_rope(z[:, :C_WIDTH], qn_ref[...], onesq_ref[...], cos, sin)
    k = _head_norm_rope(z[:, C_WIDTH:C_WIDTH + KV_WIDTH], kn_ref[...], onesk_ref[...], cos, sin)
    q_o[0] = (q * (ATTN_SCALE * math.log2(math.e))).T.astype(q_o.dtype)
    for j in range(C_KV_HEADS):
        k_o[0, j] = k[:, j * HEAD_DIM:(j + 1) * HEAD_DIM].astype(k_o.dtype)
    v_o[0] = z[:, C_WIDTH + KV_WIDTH:C_WIDTH + 2 * KV_WIDTH].T.astype(v_o.dtype)


def _attn_prep(z, q_norm, k_norm, ctx_len, tm):
    bsz, rows, ncol = z.shape
    cos, sin = _rope_tables(rows, ctx_len)
    qn = jnp.tile(q_norm, C_HEADS).reshape(1, C_WIDTH)
    kn = jnp.tile(k_norm, C_KV_HEADS).reshape(1, KV_WIDTH)
    const = lambda shape: pl.BlockSpec(shape, lambda b, t: (0,) * len(shape))
    return pl.pallas_call(
        _attn_prep_kernel,
        grid=(bsz, rows // tm),
        in_specs=[pl.BlockSpec((1, tm, ncol), lambda b, t: (b, t, 0)),
                  const((1, C_WIDTH)), const((1, KV_WIDTH)),
                  pl.BlockSpec((tm, LANES), lambda b, t: (t, 0)),
                  pl.BlockSpec((tm, LANES), lambda b, t: (t, 0)),
                  const((C_WIDTH, C_WIDTH)), const((KV_WIDTH, KV_WIDTH))],
        out_specs=[pl.BlockSpec((1, C_WIDTH, tm), lambda b, t: (b, 0, t)),
                   pl.BlockSpec((1, C_KV_HEADS, tm, HEAD_DIM), lambda b, t: (b, 0, t, 0)),
                   pl.BlockSpec((1, KV_WIDTH, tm), lambda b, t: (b, 0, t))],
        out_shape=[jax.ShapeDtypeStruct((bsz, C_WIDTH, rows), BF16),
                   jax.ShapeDtypeStruct((bsz, C_KV_HEADS, rows, HEAD_DIM), BF16),
                   jax.ShapeDtypeStruct((bsz, KV_WIDTH, rows), BF16)],
        compiler_params=_cparams(("parallel", "parallel")),
        name="attn_prep",
    )(z, qn, kn, cos, sin, _block_ones(C_WIDTH, HEAD_DIM), _block_ones(KV_WIDTH, HEAD_DIM))


def _attn_kernel(qt_ref, k_ref, vt_ref, o_ref, m_ref, l_ref, acc_ref):
    kt = pl.program_id(3)

    @pl.when(kt == 0)
    def _():
        m_ref[...] = jnp.full_like(m_ref, NEG_INF)
        l_ref[...] = jnp.zeros_like(l_ref)
        acc_ref[...] = jnp.zeros_like(acc_ref)

    kb = k_ref[0, 0]
    vt = vt_ref[0]
    grp = range(C_GROUP)
    s = [_dot(kb, qt_ref[0, g * HEAD_DIM:(g + 1) * HEAD_DIM, :]) for g in grp]
    m_old = [m_ref[g] for g in grp]
    m_new = [jnp.maximum(m_old[g], jnp.max(s[g], axis=0, keepdims=True)) for g in grp]
    p = [jnp.exp2(s[g] - m_new[g]) for g in grp]
    pv = [_dot(vt, p[g].astype(BF16)) for g in grp]
    for g in grp:
        alpha = jnp.exp2(m_old[g] - m_new[g])
        l_ref[g] = alpha * l_ref[g] + jnp.sum(p[g], axis=0, keepdims=True)
        acc_ref[g] = alpha * acc_ref[g] + pv[g]
        m_ref[g] = m_new[g]

    @pl.when(kt == pl.num_programs(3) - 1)
    def _():
        o_ref[0] = jnp.concatenate(
            [acc_ref[g] / l_ref[g] for g in grp], axis=0).astype(o_ref.dtype)


ATTN_MAX_KEY_TILE = 2816


def _key_tile(rows):
    return max(t for t in range(LANES, min(rows, ATTN_MAX_KEY_TILE) + 1, LANES) if rows % t == 0)


def _attention(qt, k, vt, ctx_len, tq, tk):
    bsz, _, rows = qt.shape
    seq = rows - ctx_len
    qoff = ctx_len // tq
    gw = C_GROUP * HEAD_DIM
    return pl.pallas_call(
        _attn_kernel,
        grid=(bsz, C_KV_HEADS, seq // tq, rows // tk),
        in_specs=[pl.BlockSpec((1, gw, tq), lambda b, j, i, kk: (b, j, i + qoff)),
                  pl.BlockSpec((1, 1, tk, HEAD_DIM), lambda b, j, i, kk: (b, j, kk, 0)),
                  pl.BlockSpec((1, HEAD_DIM, tk), lambda b, j, i, kk: (b, j, kk))],
        out_specs=pl.BlockSpec((1, gw, tq), lambda b, j, i, kk: (b, j, i)),
        out_shape=jax.ShapeDtypeStruct((bsz, C_WIDTH, seq), BF16),
        scratch_shapes=[pltpu.VMEM((C_GROUP, 1, tq), F32),
                        pltpu.VMEM((C_GROUP, 1, tq), F32),
                        pltpu.VMEM((C_GROUP, HEAD_DIM, tq), F32)],
        compiler_params=_cparams(("parallel", "parallel", "parallel", "arbitrary")),
        name="attention",
    )(qt, k, vt)


def _conv_kernel(zc_ref, zp_ref, zn_ref, w_ref, b_ref, lg_ref, lb_ref, o_ref, ybuf, *, tm, halo):
    t = pl.program_id(1)
    nt = pl.num_programs(1)
    ucol = C_WIDTH + 2 * KV_WIDTH

    def glu(z):
        u = z[:, ucol:ucol + 2 * D_WIDTH]
        return u[:, :D_WIDTH] * jax.nn.sigmoid(u[:, D_WIDTH:])

    ybuf[0:halo, :] = jnp.where(t == 0, 0.0, glu(zp_ref[0]))
    ybuf[halo:halo + tm, :] = glu(zc_ref[0])
    ybuf[halo + tm:, :] = jnp.where(t == nt - 1, 0.0, glu(zn_ref[0]))
    acc = jnp.zeros((tm, D_WIDTH), F32)
    for j in range(D_CONV_WIDTH):
        off = halo - D_PAD + j
        acc = acc + w_ref[j:j + 1, :] * ybuf[off:off + tm, :]
    y = acc + b_ref[...]
    mu = jnp.mean(y, axis=-1, keepdims=True)
    yc = y - mu
    var = jnp.mean(yc * yc, axis=-1, keepdims=True)
    yn = yc * lax.rsqrt(var + LN_EPS) * lg_ref[...] + lb_ref[...]
    o_ref[0] = (yn * jax.nn.sigmoid(yn)).astype(o_ref.dtype)


def _conformer_conv(z, dw_w, dw_b, cn_g, cn_b, ctx_len, tm):
    bsz, rows, ncol = z.shape
    seq = rows - ctx_len
    halo = 2 * SUBLANES
    hb = tm // halo
    off = ctx_len // tm
    offh = ctx_len // halo
    nh = seq // halo
    kern = functools.partial(_conv_kernel, tm=tm, halo=halo)
    row = lambda w: pl.BlockSpec((1, w), lambda b, t: (0, 0))
    return pl.pallas_call(
        kern,
        grid=(bsz, seq // tm),
        in_specs=[pl.BlockSpec((1, tm, ncol), lambda b, t: (b, t + off, 0)),
                  pl.BlockSpec((1, halo, ncol),
                               lambda b, t: (b, offh + jnp.maximum(t * hb - 1, 0), 0)),
                  pl.BlockSpec((1, halo, ncol),
                               lambda b, t: (b, offh + jnp.minimum((t + 1) * hb, nh - 1), 0)),
                  pl.BlockSpec((D_CONV_WIDTH, D_WIDTH), lambda b, t: (0, 0)),
                  row(D_WIDTH), row(D_WIDTH), row(D_WIDTH)],
        out_specs=pl.BlockSpec((1, tm, D_WIDTH), lambda b, t: (b, t, 0)),
        out_shape=jax.ShapeDtypeStruct((bsz, seq, D_WIDTH), BF16),
        scratch_shapes=[pltpu.VMEM((tm + 2 * halo, D_WIDTH), F32)],
        compiler_params=_cparams(("parallel", "parallel")),
        name="conformer_conv",
    )(z, z, z, dw_w, dw_b.reshape(1, -1), cn_g.reshape(1, -1), cn_b.reshape(1, -1))


def _forward(x, c, ctx, c_ctx, l0, l1, norm_f):
    bsz, seq, d = x.shape
    ctx_len = ctx.shape[1]
    tm = min(256, ctx_len)
    ctx_tiles = ctx_len // tm
    xs = jnp.concatenate([ctx, x], axis=1)

    mod = _modulation(c, c_ctx, l0["mod_w"], l0["mod_b"])
    z = _normmod_matmul(xs, l0["norm1"], mod, l0["w_in"], tm, ctx_tiles)
    r, v, g, bonus, kk, lwf, kdf, bf, lwr, kdr, br = _rwkv_prep(z, l0, tm, ctx_tiles)
    yf, yr = _rwkv_scan(r, v, kk, lwf, kdf, bf, lwr, kdr, br, ctx_len)
    o_rwkv = _rwkv_finish(yf, yr, bonus, g, l0["lnx_g"], l0["lnx_b"], tm)
    fcs = _fnet_chan(z, tm)
    ts_c = min(256, ctx_len)
    ts_l = min(512, seq)
    f_ctx = _fnet_seq(fcs, 0, ctx_len, ts_c, ts_c)
    f_lat = _fnet_seq(fcs, ctx_len, seq, ts_l, min(512, seq, ctx_len))
    o_fnet = jnp.concatenate([f_ctx, f_lat], axis=1)
    xs = _proj_residual(o_rwkv, o_fnet, xs, mod, l0["w_out"], tm, 0, ctx_tiles)
    xs = _peer(xs, l0["norm2"], mod, l0, tm, 0, ctx_tiles, False, norm_f)

    mod = _modulation(c, c_ctx, l1["mod_w"], l1["mod_b"])
    z = _normmod_matmul(xs, l1["norm1"], mod, l1["w_in"], tm, ctx_tiles)
    qt, k, vt = _attn_prep(z, l1["q_norm"], l1["k_norm"], ctx_len, tm)
    o_attn = _attention(qt, k, vt, ctx_len, tm, _key_tile(ctx_len + seq))
    o_conv = _conformer_conv(z, l1["dw_w"], l1["dw_b"], l1["cn_g"], l1["cn_b"], ctx_len, tm)
    xl = _proj_residual(o_attn, o_conv, xs, mod, l1["w_out"], tm, ctx_tiles, ctx_tiles,
                        o1_transposed=True)
    return _peer(xl, l1["norm2"], mod, l1, tm, ctx_tiles, ctx_tiles, True, norm_f)


def kernel(x, c, ctx, c_ctx, l0_mod_w, l0_mod_b, l0_norm1, l0_w_in, l0_shift_prev, l0_shift_next, l0_w0, l0_w2, l0_a0, l0_a2, l0_g2, l0_k_k, l0_k_a, l0_r_k, l0_lnx_g, l0_lnx_b, l0_w_out, l0_norm2, l0_pq, l0_sk1, l0_sk2, l0_pu, l0_pv, l1_mod_w, l1_mod_b, l1_norm1, l1_w_in, l1_q_norm, l1_k_norm, l1_dw_w, l1_dw_b, l1_cn_g, l1_cn_b, l1_w_out, l1_norm2, l1_pq, l1_sk1, l1_sk2, l1_pu, l1_pv, norm_f):
    l0 = dict(mod_w=l0_mod_w, mod_b=l0_mod_b, norm1=l0_norm1, w_in=l0_w_in,
              shift_prev=l0_shift_prev, shift_next=l0_shift_next, w0=l0_w0, w2=l0_w2,
              a0=l0_a0, a2=l0_a2, g2=l0_g2, k_k=l0_k_k, k_a=l0_k_a, r_k=l0_r_k,
              lnx_g=l0_lnx_g, lnx_b=l0_lnx_b, w_out=l0_w_out, norm2=l0_norm2,
              pq=l0_pq, sk1=l0_sk1, sk2=l0_sk2, pu=l0_pu, pv=l0_pv)
    l1 = dict(mod_w=l1_mod_w, mod_b=l1_mod_b, norm1=l1_norm1, w_in=l1_w_in,
              q_norm=l1_q_norm, k_norm=l1_k_norm, dw_w=l1_dw_w, dw_b=l1_dw_b,
              cn_g=l1_cn_g, cn_b=l1_cn_b, w_out=l1_w_out, norm2=l1_norm2,
              pq=l1_pq, sk1=l1_sk1, sk2=l1_sk2, pu=l1_pu, pv=l1_pv)
    return _forward(x, c, ctx, c_ctx, l0, l1, norm_f)
```

```python
import functools
import math

import jax
import jax.numpy as jnp
import numpy as np
from jax import lax
from jax.experimental import pallas as pl
from jax.experimental.pallas import tpu as pltpu

F32 = jnp.float32
BF16 = jnp.bfloat16
HIGHEST = lax.Precision.HIGHEST

LANES = 128
SUBLANES = 8
VMEM_LIMIT_BYTES = 56 * 1024 * 1024

N_MOD = 6
NORM_EPS = 1e-6
LN_EPS = 1e-5
GRID_W = 64
HEAD_DIM = 64
A_WIDTH = 768
A_HEADS = A_WIDTH // HEAD_DIM
A_RANK_W = 64
A_RANK_A = 64
A_RANK_G = 128
A_LNX_EPS = 64e-5
A_COLS = 3 * A_WIDTH + 2 * A_RANK_W + 2 * A_RANK_A + A_RANK_G
B_WIDTH = 256
B_GROUP_DIM = 64
C_WIDTH = 768
C_HEADS = 12
C_KV_HEADS = 4
C_GROUP = C_HEADS // C_KV_HEADS
KV_WIDTH = C_KV_HEADS * HEAD_DIM
ROPE_AXIS_DIM = HEAD_DIM // 2
ROPE_THETA = 10000.0
ATTN_SCALE = HEAD_DIM ** -0.5
D_WIDTH = 256
D_CONV_WIDTH = 31
D_PAD = D_CONV_WIDTH // 2
PK_HEADS = 8
PK_DIM = 256
PK_HALF = 128
N_KEYS = 128
PK_TOPK = 16

SCAN_CHUNK = 64
NEG_INF = float("-inf")


def _cparams(semantics):
    return pltpu.CompilerParams(dimension_semantics=semantics,
                                vmem_limit_bytes=VMEM_LIMIT_BYTES)


def _split_bf16(x):
    hi = x.astype(BF16)
    lo = (x - hi.astype(F32)).astype(BF16)
    return hi, lo


def _dot(a, b):
    return jnp.dot(a, b, preferred_element_type=F32)


def _dot_hp(a, b):
    return jnp.dot(a, b, preferred_element_type=F32, precision=HIGHEST)


def _dot_nt(a, b, precision=None):
    return lax.dot_general(a, b, (((1,), (1,)), ((), ())),
                           preferred_element_type=F32, precision=precision)


def _dot_tn(a, b, precision=None):
    return lax.dot_general(a, b, (((0,), (0,)), ((), ())),
                           preferred_element_type=F32, precision=precision)


_NN = (((1,), (0,)), ((), ()))
_NT = (((1,), (1,)), ((), ()))
_TN = (((0,), (0,)), ((), ()))


def _mm3(a, b, dims=_NN):
    (ah, al), (bh, bl) = a, b
    dg = lambda x, y: lax.dot_general(x, y, dims, preferred_element_type=F32)
    return dg(ah, bh) + (dg(ah, bl) + dg(al, bh))


def _segsum(x, ones_bd):
    hi, lo = _split_bf16(x)
    return _dot(hi, ones_bd) + _dot(lo, ones_bd)


def _block_ones(width, seg):
    r = np.arange(width) // seg
    return jnp.asarray((r[:, None] == r[None, :]).astype(np.float32), dtype=BF16)


def _mod_kernel(c_ref, w_ref, b_ref, o_ref):
    c = c_ref[...]
    s = c * jax.nn.sigmoid(c)
    o_ref[...] = _dot_hp(s, w_ref[...]) + b_ref[...]


def _modulation(c, c_ctx, mod_w, mod_b):
    bsz, d = c.shape
    rows = SUBLANES * pl.cdiv(bsz + 1, SUBLANES)
    cc = jnp.zeros((rows, d), F32).at[:bsz].set(c).at[bsz].set(c_ctx)
    n = mod_w.shape[1]
    tn = n // 4
    out = pl.pallas_call(
        _mod_kernel,
        grid=(n // tn,),
        in_specs=[pl.BlockSpec((rows, d), lambda j: (0, 0)),
                  pl.BlockSpec((d, tn), lambda j: (0, j)),
                  pl.BlockSpec((1, tn), lambda j: (0, j))],
        out_specs=pl.BlockSpec((rows, tn), lambda j: (0, j)),
        out_shape=jax.ShapeDtypeStruct((rows, n), F32),
        compiler_params=_cparams(("arbitrary",)),
        name="adaln_mod",
    )(cc, mod_w, mod_b.reshape(1, n))
    lat = out[:bsz].reshape(bsz, N_MOD, d)
    ctx = jnp.broadcast_to(out[bsz].reshape(1, N_MOD, d), (bsz, N_MOD, d))
    return jnp.stack([ctx, lat], axis=1)


def _norm_mod(x, g, mod, row):
    ms = jnp.mean(x * x, axis=-1, keepdims=True)
    y = x * lax.rsqrt(ms + NORM_EPS) * g
    return y * (1.0 + mod[row + 1:row + 2]) + mod[row:row + 1]


def _normmod_matmul_kernel(x_ref, g_ref, mod_ref, w_ref, o_ref):
    h = _norm_mod(x_ref[0], g_ref[...], mod_ref[0, 0], 0)
    o_ref[0] = _dot(h.astype(BF16), w_ref[...])


def _normmod_matmul(x, g, mod, w, tm, ctx_tiles):
    bsz, rows, d = x.shape
    n = w.shape[1]
    return pl.pallas_call(
        _normmod_matmul_kernel,
        grid=(bsz, rows // tm),
        in_specs=[pl.BlockSpec((1, tm, d), lambda b, t: (b, t, 0)),
                  pl.BlockSpec((1, d), lambda b, t: (0, 0)),
                  pl.BlockSpec((1, 1, N_MOD, d),
                               lambda b, t: (b, jnp.where(t >= ctx_tiles, 1, 0), 0, 0)),
                  pl.BlockSpec((d, n), lambda b, t: (0, 0))],
        out_specs=pl.BlockSpec((1, tm, n), lambda b, t: (b, t, 0)),
        out_shape=jax.ShapeDtypeStruct((bsz, rows, n), F32),
        compiler_params=_cparams(("parallel", "parallel")),
        name="normmod_proj",
    )(x, g.reshape(1, d), mod, w.astype(BF16))


def _rwkv_prep_kernel(z_ref, zp_ref, zn_ref, mup_ref, mun_ref, w0_ref, w2_ref, a0_ref, a2_ref,
                      g2_ref, kk_ref, ka_ref, rk_ref, ones_ref,
                      r_o, v_o, g_o, bonus_o, kkn_o, lwf_o, kdf_o, bf_o, lwr_o, kdr_o, br_o,
                      *, tm, ctx_tiles):
    t = pl.program_id(1)
    nt = pl.num_programs(1)
    z = z_ref[0][:, :A_COLS]
    zp_row = zp_ref[0][SUBLANES - 1:SUBLANES, :A_COLS]
    zn_row = zn_ref[0][0:1, :A_COLS]
    first = jnp.logical_or(t == 0, t == ctx_tiles)
    last = jnp.logical_or(t == ctx_tiles - 1, t == nt - 1)
    zp_row = jnp.where(first, 0.0, zp_row)
    zn_row = jnp.where(last, 0.0, zn_row)
    ridx = lax.broadcasted_iota(jnp.int32, (tm, 1), 0)
    z_prev = jnp.where(ridx == 0, zp_row, pltpu.roll(z, 1, 0))
    z_next = jnp.where(ridx == tm - 1, zn_row, pltpu.roll(z, tm - 1, 0))
    zs = z + mup_ref[...] * (z_prev - z) + mun_ref[...] * (z_next - z)

    w = A_WIDTH
    r = zs[:, 0:w]
    k = zs[:, w:2 * w]
    v = zs[:, 2 * w:3 * w]
    o = 3 * w
    xw = zs[:, o:o + 2 * A_RANK_W]
    o += 2 * A_RANK_W
    xa = zs[:, o:o + 2 * A_RANK_A]
    o += 2 * A_RANK_A
    xg = zs[:, o:o + A_RANK_G]
    ones_bd = ones_ref[...]

    g_o[0] = _dot_hp(jax.nn.sigmoid(xg), g2_ref[...])
    kk = k * kk_ref[...]
    nrm = jnp.sqrt(_segsum(kk * kk, ones_bd))
    kk = kk / jnp.maximum(nrm, 1e-12)
    r_o[0] = r
    v_o[0] = v
    kkn_o[0] = kk
    bonus = jnp.zeros_like(r)
    tw = jnp.tanh(xw)
    for d, (lw_o, kd_o, b_o) in enumerate(((lwf_o, kdf_o, bf_o), (lwr_o, kdr_o, br_o))):
        wl = w0_ref[d:d + 1, :] + _dot_hp(tw[:, d * A_RANK_W:(d + 1) * A_RANK_W], w2_ref[d])
        w_log = -jax.nn.softplus(-wl) - 0.5
        lw_o[0] = -jnp.exp(w_log)
        a_gate = jax.nn.sigmoid(
            a0_ref[d:d + 1, :] + _dot_hp(xa[:, d * A_RANK_A:(d + 1) * A_RANK_A], a2_ref[d]))
        k_d = k * (1.0 + (a_gate - 1.0) * ka_ref[...])
        kd_o[0] = k_d
        b_o[0] = kk * a_gate
        bonus = bonus + _segsum(r * k_d * rk_ref[...], ones_bd) * v
    bonus_o[0] = bonus


def _rwkv_prep(z, p, tm, ctx_tiles):
    bsz, rows, ncol = z.shape
    w = A_WIDTH
    hb = tm // SUBLANES
    nblk8 = rows // SUBLANES
    row_spec = lambda width: pl.BlockSpec((1, width), lambda b, t: (0, 0))
    full = lambda shape: pl.BlockSpec(shape, lambda b, t: (0,) * len(shape))
    out_spec = pl.BlockSpec((1, tm, w), lambda b, t: (b, t, 0))
    out_shape = jax.ShapeDtypeStruct((bsz, rows, w), F32)
    kern = functools.partial(_rwkv_prep_kernel, tm=tm, ctx_tiles=ctx_tiles)
    return pl.pallas_call(
        kern,
        grid=(bsz, rows // tm),
        in_specs=[pl.BlockSpec((1, tm, ncol), lambda b, t: (b, t, 0)),
                  pl.BlockSpec((1, SUBLANES, ncol),
                               lambda b, t: (b, jnp.maximum(t * hb - 1, 0), 0)),
                  pl.BlockSpec((1, SUBLANES, ncol),
                               lambda b, t: (b, jnp.minimum((t + 1) * hb, nblk8 - 1), 0)),
                  row_spec(A_COLS), row_spec(A_COLS),
                  full((2, w)), full((2, A_RANK_W, w)), full((2, w)), full((2, A_RANK_A, w)),
                  full((A_RANK_G, w)), row_spec(w), row_spec(w), row_spec(w), full((w, w))],
        out_specs=[out_spec] * 11,
        out_shape=[out_shape] * 11,
        compiler_params=_cparams(("parallel", "parallel")),
        name="rwkv_prep",
    )(z, z, z, p["shift_prev"].reshape(1, -1), p["shift_next"].reshape(1, -1),
      p["w0"], p["w2"], p["a0"], p["a2"], p["g2"], p["k_k"].reshape(1, w),
      p["k_a"].reshape(1, w), p["r_k"].reshape(1, w), _block_ones(w, HEAD_DIM))


SCAN_HEADS = 4


def _scan_kernel(rf, vf, kkf, lwf, kdf, bf, rr, vr, kkr, lwr, kdr, br, yf_o, yr_o, s_ref, *, c):
    @pl.when(pl.program_id(2) == 0)
    def _():
        s_ref[...] = jnp.zeros_like(s_ref)

    ri = lax.broadcasted_iota(jnp.int32, (c, c), 0)
    ci = lax.broadcasted_iota(jnp.int32, (c, c), 1)
    ri2 = lax.broadcasted_iota(jnp.int32, (c, 2 * c), 0)
    ci2 = lax.broadcasted_iota(jnp.int32, (c, 2 * c), 1)
    ci2 = jnp.where(ci2 >= c, ci2 - c, ci2)
    dirs = ((rf, vf, kkf, lwf, kdf, bf, ci <= ri, ci < ri, ci2 <= ri2, c - 1),
            (rr, vr, kkr, lwr, kdr, br, ci >= ri, ci > ri, ci2 >= ri2, 0))
    lhs, rhs, vs, ss, pend, incl, strict = [], [], [], [], [], [], []
    for d, (r_, v_, kk_, lw_, kd_, b_, m_incl, m_strict, m_incl2, edge) in enumerate(dirs):
        lw = lw_[0]
        tri = m_incl.astype(F32).astype(BF16)
        lw_h = lw.astype(BF16)
        lw_m, lw_l = _split_bf16(lw - lw_h.astype(F32))
        lcum = _dot(tri, lw_h) + (_dot(tri, lw_m) + _dot(tri, lw_l))
        p_in = jnp.exp(lcum)
        p_inv = jnp.exp(-lcum)
        lhs_all = jnp.concatenate([-kk_[0] * jnp.exp(lcum - lw), r_[0] * p_in], axis=0)
        rhs_all = jnp.concatenate([b_[0] * p_inv, kd_[0] * p_inv], axis=0)
        v_all = v_[0]
        for h in range(SCAN_HEADS):
            sl = slice(h * HEAD_DIM, (h + 1) * HEAD_DIM)
            lhs.append(lhs_all[:, sl])
            rhs.append(rhs_all[:, sl])
            vs.append(v_all[:, sl])
            ss.append(s_ref[d, h])
            pend.append(p_in[edge:edge + 1, sl])
            incl.append(m_incl2)
            strict.append(m_strict)
    n = len(lhs)
    idx = range(n)
    lhs2 = [_split_bf16(lhs[i]) for i in idx]
    rhs2 = [_split_bf16(rhs[i]) for i in idx]
    v2 = [_split_bf16(vs[i]) for i in idx]
    s2 = [_split_bf16(ss[i]) for i in idx]
    g = [_mm3(lhs2[i], rhs2[i], _NT) for i in idx]
    xs = [_mm3(lhs2[i], s2[i], _NT) for i in idx]
    a_ak = [_split_bf16(jnp.where(strict[i], g[i][:c, c:], 0.0)) for i in idx]
    x = [xs[i][:c] + _mm3(a_ak[i], v2[i]) for i in idx]
    apow = [_split_bf16(jnp.where(strict[i], g[i][:c, :c], 0.0)) for i in idx]
    steps = int(math.log2(c))
    for k in range(steps):
        x2 = [_split_bf16(x[i]) for i in idx]
        x = [x[i] + _mm3(apow[i], x2[i]) for i in idx]
        if k + 1 < steps:
            apow = [_split_bf16(_mm3(apow[i], apow[i])) for i in idx]
    x2 = [_split_bf16(x[i]) for i in idx]
    sav = [tuple(jnp.concatenate([x2[i][t], v2[i][t]], axis=0) for t in range(2)) for i in idx]
    a_r = [_split_bf16(jnp.where(incl[i], g[i][c:], 0.0)) for i in idx]
    y = [xs[i][c:] + _mm3(a_r[i], sav[i]) for i in idx]
    s_new = [(ss[i] + _mm3(sav[i], rhs2[i], _TN)) * pend[i] for i in idx]
    for d, y_o in enumerate((yf_o, yr_o)):
        for h in range(SCAN_HEADS):
            s_ref[d, h] = s_new[d * SCAN_HEADS + h]
        y_o[0] = jnp.concatenate(y[d * SCAN_HEADS:(d + 1) * SCAN_HEADS], axis=1)


def _rwkv_scan(r, v, kk, lwf, kdf, bf, lwr, kdr, br, ctx_len):
    bsz, rows, w = r.shape
    c = SCAN_CHUNK
    nch = rows // c
    cch = ctx_len // c
    lanes = SCAN_HEADS * HEAD_DIM
    fwd = lambda b, hp, i: (b, i, hp)
    rev = lambda b, hp, i: (b, jnp.where(i < cch, cch - 1 - i, nch - 1 + cch - i), hp)
    blk = lambda im: pl.BlockSpec((1, c, lanes), im)
    kern = functools.partial(_scan_kernel, c=c)
    out_shape = jax.ShapeDtypeStruct((bsz, rows, w), F32)
    return pl.pallas_call(
        kern,
        grid=(bsz, w // lanes, nch),
        in_specs=[blk(fwd)] * 6 + [blk(rev)] * 6,
        out_specs=[blk(fwd), blk(rev)],
        out_shape=[out_shape, out_shape],
        scratch_shapes=[pltpu.VMEM((2, SCAN_HEADS, HEAD_DIM, HEAD_DIM), F32)],
        compiler_params=_cparams(("parallel", "parallel", "arbitrary")),
        name="rwkv_scan",
    )(r, v, kk, lwf, kdf, bf, r, v, kk, lwr, kdr, br)


def _rwkv_finish_kernel(yf, yr, bonus, g, lg, lb, ones_ref, o_ref):
    y = yf[0] + yr[0]
    ones_bd = ones_ref[...]
    inv = 1.0 / HEAD_DIM
    mu = _segsum(y, ones_bd) * inv
    yc = y - mu
    var = _segsum(yc * yc, ones_bd) * inv
    yn = yc * lax.rsqrt(var + A_LNX_EPS) * lg[...] + lb[...]
    o_ref[0] = ((yn + bonus[0]) * g[0]).astype(o_ref.dtype)


def _rwkv_finish(yf, yr, bonus, g, lnx_g, lnx_b, tm):
    bsz, rows, w = yf.shape
    blk = pl.BlockSpec((1, tm, w), lambda b, t: (b, t, 0))
    row = pl.BlockSpec((1, w), lambda b, t: (0, 0))
    return pl.pallas_call(
        _rwkv_finish_kernel,
        grid=(bsz, rows // tm),
        in_specs=[blk, blk, blk, blk, row, row, pl.BlockSpec((w, w), lambda b, t: (0, 0))],
        out_specs=blk,
        out_shape=jax.ShapeDtypeStruct((bsz, rows, w), BF16),
        compiler_params=_cparams(("parallel", "parallel")),
        name="rwkv_finish",
    )(yf, yr, bonus, g, lnx_g.reshape(1, w), lnx_b.reshape(1, w), _block_ones(w, HEAD_DIM))


def _fnet_chan_kernel(z_ref, m_ref, o_ref):
    f = z_ref[0][:, A_COLS:A_COLS + B_WIDTH]
    o_ref[0] = _dot(f.astype(BF16), m_ref[...]).astype(o_ref.dtype)


def _fnet_chan(z, tm):
    bsz, rows, ncol = z.shape
    j = np.arange(B_WIDTH)
    same = (j[:, None] // B_GROUP_DIM) == (j[None, :] // B_GROUP_DIM)
    ang = 2.0 * np.pi * ((j[:, None] % B_GROUP_DIM) * (j[None, :] % B_GROUP_DIM) % B_GROUP_DIM) / B_GROUP_DIM
    m = np.concatenate([np.where(same, np.cos(ang), 0.0), np.where(same, np.sin(ang), 0.0)], axis=1)
    return pl.pallas_call(
        _fnet_chan_kernel,
        grid=(bsz, rows // tm),
        in_specs=[pl.BlockSpec((1, tm, ncol), lambda b, t: (b, t, 0)),
                  pl.BlockSpec((B_WIDTH, 2 * B_WIDTH), lambda b, t: (0, 0))],
        out_specs=pl.BlockSpec((1, tm, 2 * B_WIDTH), lambda b, t: (b, t, 0)),
        out_shape=jax.ShapeDtypeStruct((bsz, rows, 2 * B_WIDTH), BF16),
        compiler_params=_cparams(("parallel", "parallel")),
        name="fnet_chan",
    )(z, jnp.asarray(m, dtype=BF16))


def _fnet_seq_kernel(fcs_ref, cb_ref, sb_ref, c0_ref, s0_ref, o_ref, acc_ref, *, scale, nb):
    k = pl.program_id(1)

    @pl.when(k == 0)
    def _():
        acc_ref[...] = jnp.zeros_like(acc_ref)

    cb = cb_ref[...]
    sb = sb_ref[...]
    c0 = c0_ref[0]
    s0 = s0_ref[0]
    cm = (c0 * cb - s0 * sb).astype(BF16)
    sm = (s0 * cb + c0 * sb).astype(BF16)
    for b in range(nb):
        fcs = fcs_ref[b]
        acc_ref[b] += _dot(cm, fcs[:, :B_WIDTH]) - _dot(sm, fcs[:, B_WIDTH:])

    @pl.when(k == pl.num_programs(1) - 1)
    def _():
        o_ref[...] = (acc_ref[...] * scale).astype(o_ref.dtype)


def _fnet_seq(fcs, row0, length, ts, tk):
    bsz = fcs.shape[0]
    ds = jnp.arange(ts, dtype=jnp.int32)[:, None]
    tt = jnp.arange(length, dtype=jnp.int32)[None, :]
    ang = ((ds * tt) % length).astype(F32) * (2.0 * math.pi / length)
    cb, sb = jnp.cos(ang), jnp.sin(ang)
    s0 = (jnp.arange(length // ts, dtype=jnp.int32) * ts)[:, None]
    ang0 = ((s0 * tt) % length).astype(F32) * (2.0 * math.pi / length)
    c0, sn0 = jnp.cos(ang0)[:, None, :], jnp.sin(ang0)[:, None, :]
    kern = functools.partial(_fnet_seq_kernel, scale=1.0 / math.sqrt(length * B_GROUP_DIM), nb=bsz)
    koff = row0 // tk
    return pl.pallas_call(
        kern,
        grid=(length // ts, length // tk),
        in_specs=[pl.BlockSpec((bsz, tk, 2 * B_WIDTH), lambda s, k: (0, k + koff, 0)),
                  pl.BlockSpec((ts, tk), lambda s, k: (0, k)),
                  pl.BlockSpec((ts, tk), lambda s, k: (0, k)),
                  pl.BlockSpec((1, 1, tk), lambda s, k: (s, 0, k)),
                  pl.BlockSpec((1, 1, tk), lambda s, k: (s, 0, k))],
        out_specs=pl.BlockSpec((bsz, ts, B_WIDTH), lambda s, k: (0, s, 0)),
        out_shape=jax.ShapeDtypeStruct((bsz, length, B_WIDTH), BF16),
        scratch_shapes=[pltpu.VMEM((bsz, ts, B_WIDTH), F32)],
        compiler_params=_cparams(("parallel", "arbitrary")),
        name="fnet_seq",
    )(fcs, cb, sb, c0, sn0)


def _proj_residual_kernel(o1_ref, o2_ref, x_ref, mod_ref, w1_ref, w2_ref, out_ref, *, o1_transposed):
    o1 = o1_ref[0]
    if o1_transposed:
        o1 = o1.astype(F32).T.astype(BF16)
    mix = _dot(o1, w1_ref[...]) + _dot(o2_ref[0], w2_ref[...])
    gate = mod_ref[0, 0][2:3]
    out_ref[0] = x_ref[0] + gate * mix


def _proj_residual(o1, o2, x, mod, w_out, tm, tile_off, ctx_tiles, o1_transposed=False):
    bsz, rows, w2 = o2.shape
    w1 = o1.shape[1] if o1_transposed else o1.shape[2]
    d = x.shape[2]
    o1_spec = (pl.BlockSpec((1, w1, tm), lambda b, t: (b, 0, t)) if o1_transposed
               else pl.BlockSpec((1, tm, w1), lambda b, t: (b, t, 0)))
    kern = functools.partial(_proj_residual_kernel, o1_transposed=o1_transposed)
    return pl.pallas_call(
        kern,
        grid=(bsz, rows // tm),
        in_specs=[o1_spec,
                  pl.BlockSpec((1, tm, w2), lambda b, t: (b, t, 0)),
                  pl.BlockSpec((1, tm, d), lambda b, t: (b, t + tile_off, 0)),
                  pl.BlockSpec((1, 1, N_MOD, d),
                               lambda b, t: (b, jnp.where(t + tile_off >= ctx_tiles, 1, 0), 0, 0)),
                  pl.BlockSpec((w1, d), lambda b, t: (0, 0)),
                  pl.BlockSpec((w2, d), lambda b, t: (0, 0))],
        out_specs=pl.BlockSpec((1, tm, d), lambda b, t: (b, t, 0)),
        out_shape=jax.ShapeDtypeStruct((bsz, rows, d), F32),
        compiler_params=_cparams(("parallel", "parallel")),
        name="proj_residual",
    )(o1, o2, x, mod, w_out[:w1].astype(BF16), w_out[w1:].astype(BF16))


def _merge_exchange_network(n):
    pairs = []
    p = 1
    while p < n:
        k = p
        while k >= 1:
            for j in range(k % p, n - k, 2 * k):
                for i in range(min(k, n - j - k)):
                    if (i + j) // (2 * p) == (i + j + k) // (2 * p):
                        pairs.append((i + j, i + j + k))
            k //= 2
        p *= 2
    return pairs


def _top16_rows(s):
    nslab = s.shape[0] // SUBLANES
    slabs = [s[k * SUBLANES:(k + 1) * SUBLANES] for k in range(nslab)]
    for a, b in _merge_exchange_network(nslab):
        slabs[a], slabs[b] = jnp.maximum(slabs[a], slabs[b]), jnp.minimum(slabs[a], slabs[b])
    rows = []
    for t in range(PK_TOPK):
        m = jnp.max(slabs[0], axis=0, keepdims=True)
        rows.append(m)
        hit = slabs[0] >= m
        for k in range(nslab - 1 - t):
            slabs[k] = jnp.where(hit, slabs[k + 1], slabs[k])
    return rows


_PAIR_IDX = [(i, j) for i in range(PK_TOPK) for j in range(PK_TOPK) if (i + 1) * (j + 1) <= PK_TOPK]
BF16_SUBLANES = 2 * SUBLANES
PEER_SUB_E1 = 2
PEER_STEP_E1 = 16
PEER_LOOKAHEAD = 1


def _rows_bf16(row):
    packed = jnp.broadcast_to(row, (BF16_SUBLANES, row.shape[1])).astype(BF16)
    return jnp.tile(packed, (N_KEYS // BF16_SUBLANES, 1))


def _peer_kernel(x_ref, g_ref, mod_ref, pqh_ref, pql_ref, sk1h_ref, sk1l_ref, sk2h_ref, sk2l_ref,
                 pu_ref, pvt_ref, gf_ref,
                 out_ref, ht_ref, r2_ref, e2_ref, c1_ref, w1_ref, s_ref, cand_ref, acc_ref,
                 *, tm, e1_per_blk, final_norm):
    eb = pl.program_id(2)

    @pl.when(eb == 0)
    def _():
        h = _norm_mod(x_ref[0], g_ref[...], mod_ref[0, 0], 3)
        ht = h.T
        hhi, hlo = _split_bf16(ht)
        ht_ref[...] = hhi
        qt = _mm3((pqh_ref[...], pql_ref[...]), (hhi, hlo))
        qs = [_split_bf16(qt[i * PK_HALF:(i + 1) * PK_HALF]) for i in range(2 * PK_HEADS)]
        for hd in range(PK_HEADS):
            s_ref[2 * hd] = _mm3((sk1h_ref[hd], sk1l_ref[hd]), qs[2 * hd])
            s_ref[2 * hd + 1] = _mm3((sk2h_ref[hd], sk2l_ref[hd]), qs[2 * hd + 1])
        for hd in range(PK_HEADS):
            s1 = s_ref[2 * hd]
            s2 = s_ref[2 * hd + 1]
            v1 = _top16_rows(s1)
            v2 = _top16_rows(s2)
            cand_ref[...] = jnp.full_like(cand_ref, NEG_INF)
            for n, (i, j) in enumerate(_PAIR_IDX):
                cand_ref[n:n + 1, :] = v1[i] + v2[j]
            cand = cand_ref[...]
            top = v1[0] + v2[0]
            zsum = jnp.zeros_like(top)
            tau = top
            for _ in range(PK_TOPK):
                m = jnp.max(cand, axis=0, keepdims=True)
                zsum = zsum + jnp.exp(m - top)
                tau = m
                cand = jnp.where(cand >= m, NEG_INF, cand)
            rank2 = jnp.full((N_KEYS, tm), float(PK_TOPK), F32)
            for j in reversed(range(PK_TOPK)):
                rank2 = jnp.where(s2 >= v2[j], float(j), rank2)
            count1 = jnp.zeros((N_KEYS, tm), F32)
            for i in reversed(range(PK_TOPK)):
                cnt = jnp.zeros_like(top)
                for j in range(PK_TOPK // (i + 1)):
                    cnt = cnt + jnp.where(v1[i] + v2[j] >= tau, 1.0, 0.0)
                count1 = jnp.where(s1 >= v1[i], cnt, count1)
            r2_ref[hd] = rank2.astype(BF16)
            c1_ref[hd] = count1
            e2_ref[hd] = jnp.exp(s2 - v2[0]).astype(BF16)
            w1_ref[hd] = jnp.exp(s1 - v1[0]) / zsum
        acc_ref[...] = jnp.zeros_like(acc_ref)

    sub = PEER_SUB_E1 * N_KEYS
    nsub = e1_per_blk // PEER_SUB_E1
    ht = ht_ref[...]

    def up(i):
        return _dot(pu_ref[i * sub:(i + 1) * sub, :], ht)

    def activation(i, hu):
        acts = []
        for jj in range(PEER_SUB_E1):
            e1 = eb * e1_per_blk + PEER_SUB_E1 * i + jj
            gate = jnp.zeros((N_KEYS, tm), BF16)
            for hd in range(PK_HEADS):
                count = _rows_bf16(c1_ref[hd, pl.ds(e1, 1), :])
                weight = _rows_bf16(w1_ref[hd, pl.ds(e1, 1), :])
                gate = gate + jnp.where(r2_ref[hd] < count, e2_ref[hd] * weight, 0)
            u = hu[jj * N_KEYS:(jj + 1) * N_KEYS]
            act = 0.5 * u * (1.0 + lax.erf(u * (1.0 / math.sqrt(2.0))))
            acts.append(act.astype(BF16) * gate)
        return jnp.concatenate(acts, axis=0)

    hus = [up(i) for i in range(min(PEER_LOOKAHEAD, nsub))]
    down = None
    for i in range(nsub):
        if i + PEER_LOOKAHEAD < nsub:
            hus.append(up(i + PEER_LOOKAHEAD))
        part = _dot(pvt_ref[:, i * sub:(i + 1) * sub], activation(i, hus[i]))
        down = part if down is None else down + part
    acc_ref[...] += down

    @pl.when(eb == pl.num_programs(2) - 1)
    def _():
        y = x_ref[0] + mod_ref[0, 0][5:6] * acc_ref[...].T
        if final_norm:
            ms = jnp.mean(y * y, axis=-1, keepdims=True)
            y = y * lax.rsqrt(ms + NORM_EPS) * gf_ref[...]
        out_ref[0] = y


def _peer(x, g, mod, p, tm, tile_off, ctx_tiles, final_norm, norm_f):
    bsz, rows, d = x.shape
    n_exp = p["pu"].shape[0]
    e1_per_blk = PEER_STEP_E1
    eblk = e1_per_blk * N_KEYS
    pqt = p["pq"].T
    pqh = pqt.astype(BF16)
    pql = (pqt - pqh.astype(F32)).astype(BF16)
    nq = pqt.shape[0]
    kern = functools.partial(_peer_kernel, tm=tm, e1_per_blk=e1_per_blk, final_norm=final_norm)
    const = lambda shape: pl.BlockSpec(shape, lambda b, t, e: (0,) * len(shape))
    return pl.pallas_call(
        kern,
        grid=(bsz, rows // tm, n_exp // eblk),
        in_specs=[pl.BlockSpec((1, tm, d), lambda b, t, e: (b, t, 0)),
                  const((1, d)),
                  pl.BlockSpec((1, 1, N_MOD, d),
                               lambda b, t, e: (b, jnp.where(t + tile_off >= ctx_tiles, 1, 0), 0, 0)),
                  const((nq, d)), const((nq, d)),
                  const((PK_HEADS, N_KEYS, PK_HALF)), const((PK_HEADS, N_KEYS, PK_HALF)),
                  const((PK_HEADS, N_KEYS, PK_HALF)), const((PK_HEADS, N_KEYS, PK_HALF)),
                  pl.BlockSpec((eblk, d), lambda b, t, e: (e, 0)),
                  pl.BlockSpec((d, eblk), lambda b, t, e: (0, e)),
                  const((1, d))],
        out_specs=pl.BlockSpec((1, tm, d), lambda b, t, e: (b, t, 0)),
        out_shape=jax.ShapeDtypeStruct((bsz, rows, d), F32),
        scratch_shapes=[pltpu.VMEM((d, tm), BF16),
                        pltpu.VMEM((PK_HEADS, N_KEYS, tm), BF16),
                        pltpu.VMEM((PK_HEADS, N_KEYS, tm), BF16),
                        pltpu.VMEM((PK_HEADS, N_KEYS, tm), F32),
                        pltpu.VMEM((PK_HEADS, N_KEYS, tm), F32),
                        pltpu.VMEM((2 * PK_HEADS, N_KEYS, tm), F32),
                        pltpu.VMEM((SUBLANES * pl.cdiv(len(_PAIR_IDX), SUBLANES), tm), F32),
                        pltpu.VMEM((d, tm), F32)],
        compiler_params=_cparams(("parallel", "parallel", "arbitrary")),
        name="peer",
    )(x, g.reshape(1, d), mod, pqh, pql, *_split_bf16(p["sk1"]), *_split_bf16(p["sk2"]),
      p["pu"].astype(BF16), p["pv"].T.astype(BF16), norm_f.reshape(1, d))


def _rope_tables(rows, ctx_len):
    t = jnp.arange(rows - ctx_len, dtype=jnp.int32)
    inv = ROPE_THETA ** (-jnp.arange(0, ROPE_AXIS_DIM, 2, dtype=F32) / ROPE_AXIS_DIM)
    ang_r = (t // GRID_W).astype(F32)[:, None] * inv
    ang_c = (t % GRID_W).astype(F32)[:, None] * inv
    cos = jnp.concatenate([jnp.cos(ang_r)] * 2 + [jnp.cos(ang_c)] * 2, axis=1)
    sin = jnp.concatenate([-jnp.sin(ang_r), jnp.sin(ang_r), -jnp.sin(ang_c), jnp.sin(ang_c)], axis=1)
    cos = jnp.concatenate([jnp.ones((ctx_len, HEAD_DIM), F32), cos], axis=0)
    sin = jnp.concatenate([jnp.zeros((ctx_len, HEAD_DIM), F32), sin], axis=0)
    return jnp.tile(cos, (1, 2)), jnp.tile(sin, (1, 2))


def _head_norm_rope(x, gain, ones_bd, cos, sin):
    ms = _segsum(x * x, ones_bd) * (1.0 / HEAD_DIM)
    y = x * lax.rsqrt(ms + NORM_EPS) * gain
    outs = []
    half = ROPE_AXIS_DIM // 2
    lane = lax.broadcasted_iota(jnp.int32, (1, LANES), 1)
    first_half = (lane % ROPE_AXIS_DIM) < half
    for i in range(x.shape[1] // LANES):
        yc = y[:, i * LANES:(i + 1) * LANES]
        partner = jnp.where(first_half, pltpu.roll(yc, LANES - half, 1), pltpu.roll(yc, half, 1))
        outs.append(yc * cos + partner * sin)
    return jnp.concatenate(outs, axis=1)


def _attn_prep_kernel(z_ref, qn_ref, kn_ref, cos_ref, sin_ref, onesq_ref, onesk_ref,
                      q_o, k_o, v_o):
    z = z_ref[0]
    cos = cos_ref[...]
    sin = sin_ref[...]
    q = _head_norm_rope(z[:, :C_WIDTH], qn_ref[...], onesq_ref[...], cos, sin)
    k = _head_norm_rope(z[:, C_WIDTH:C_WIDTH + KV_WIDTH], kn_ref[...], onesk_ref[...], cos, sin)
    q_o[0] = (q * (ATTN_SCALE * math.log2(math.e))).T.astype(q_o.dtype)
    for j in range(C_KV_HEADS):
        k_o[0, j] = k[:, j * HEAD_DIM:(j + 1) * HEAD_DIM].astype(k_o.dtype)
    v_o[0] = z[:, C_WIDTH + KV_WIDTH:C_WIDTH + 2 * KV_WIDTH].T.astype(v_o.dtype)


def _attn_prep(z, q_norm, k_norm, ctx_len, tm):
    bsz, rows, ncol = z.shape
    cos, sin = _rope_tables(rows, ctx_len)
    qn = jnp.tile(q_norm, C_HEADS).reshape(1, C_WIDTH)
    kn = jnp.tile(k_norm, C_KV_HEADS).reshape(1, KV_WIDTH)
    const = lambda shape: pl.BlockSpec(shape, lambda b, t: (0,) * len(shape))
    return pl.pallas_call(
        _attn_prep_kernel,
        grid=(bsz, rows // tm),
        in_specs=[pl.BlockSpec((1, tm, ncol), lambda b, t: (b, t, 0)),
                  const((1, C_WIDTH)), const((1, KV_WIDTH)),
                  pl.BlockSpec((tm, LANES), lambda b, t: (t, 0)),
                  pl.BlockSpec((tm, LANES), lambda b, t: (t, 0)),
                  const((C_WIDTH, C_WIDTH)), const((KV_WIDTH, KV_WIDTH))],
        out_specs=[pl.BlockSpec((1, C_WIDTH, tm), lambda b, t: (b, 0, t)),
                   pl.BlockSpec((1, C_KV_HEADS, tm, HEAD_DIM), lambda b, t: (b, 0, t, 0)),
                   pl.BlockSpec((1, KV_WIDTH, tm), lambda b, t: (b, 0, t))],
        out_shape=[jax.ShapeDtypeStruct((bsz, C_WIDTH, rows), BF16),
                   jax.ShapeDtypeStruct((bsz, C_KV_HEADS, rows, HEAD_DIM), BF16),
                   jax.ShapeDtypeStruct((bsz, KV_WIDTH, rows), BF16)],
        compiler_params=_cparams(("parallel", "parallel")),
        name="attn_prep",
    )(z, qn, kn, cos, sin, _block_ones(C_WIDTH, HEAD_DIM), _block_ones(KV_WIDTH, HEAD_DIM))


def _attn_kernel(qt_ref, k_ref, vt_ref, o_ref, m_ref, l_ref, acc_ref, *, kc):
    kt = pl.program_id(3)

    @pl.when(kt == 0)
    def _():
        m_ref[...] = jnp.full_like(m_ref, NEG_INF)
        l_ref[...] = jnp.zeros_like(l_ref)
        acc_ref[...] = jnp.zeros_like(acc_ref)

    grp = range(C_GROUP)
    qts = [qt_ref[0, g * HEAD_DIM:(g + 1) * HEAD_DIM, :] for g in grp]
    nck = k_ref.shape[2] // kc

    def scores(c):
        kb = k_ref[0, 0, c * kc:(c + 1) * kc, :]
        return [_dot(kb, qts[g]) for g in grp]

    m = [m_ref[g] for g in grp]
    l = [l_ref[g] for g in grp]
    acc = [acc_ref[g] for g in grp]
    s_next = scores(0)
    for c in range(nck):
        s = s_next
        if c + 1 < nck:
            s_next = scores(c + 1)
        vt = vt_ref[0, :, c * kc:(c + 1) * kc]
        m_new = [jnp.maximum(m[g], jnp.max(s[g], axis=0, keepdims=True)) for g in grp]
        p = [jnp.exp2(s[g] - m_new[g]) for g in grp]
        pv = [_dot(vt, p[g].astype(BF16)) for g in grp]
        for g in grp:
            alpha = jnp.exp2(m[g] - m_new[g])
            l[g] = alpha * l[g] + jnp.sum(p[g], axis=0, keepdims=True)
            acc[g] = alpha * acc[g] + pv[g]
        m = m_new
    for g in grp:
        m_ref[g] = m[g]
        l_ref[g] = l[g]
        acc_ref[g] = acc[g]

    @pl.when(kt == pl.num_programs(3) - 1)
    def _():
        o_ref[0] = jnp.concatenate(
            [acc_ref[g] / l_ref[g] for g in grp], axis=0).astype(o_ref.dtype)


ATTN_MAX_KEY_TILE = 2816
ATTN_MAX_KEY_CHUNK = 256


def _key_tile(rows, limit=ATTN_MAX_KEY_TILE):
    return max(t for t in range(LANES, min(rows, limit) + 1, LANES) if rows % t == 0)


def _attention(qt, k, vt, ctx_len, tq, tk):
    bsz, _, rows = qt.shape
    seq = rows - ctx_len
    qoff = ctx_len // tq
    gw = C_GROUP * HEAD_DIM
    kern = functools.partial(_attn_kernel, kc=_key_tile(tk, ATTN_MAX_KEY_CHUNK))
    return pl.pallas_call(
        kern,
        grid=(bsz, C_KV_HEADS, seq // tq, rows // tk),
        in_specs=[pl.BlockSpec((1, gw, tq), lambda b, j, i, kk: (b, j, i + qoff)),
                  pl.BlockSpec((1, 1, tk, HEAD_DIM), lambda b, j, i, kk: (b, j, kk, 0)),
                  pl.BlockSpec((1, HEAD_DIM, tk), lambda b, j, i, kk: (b, j, kk))],
        out_specs=pl.BlockSpec((1, gw, tq), lambda b, j, i, kk: (b, j, i)),
        out_shape=jax.ShapeDtypeStruct((bsz, C_WIDTH, seq), BF16),
        scratch_shapes=[pltpu.VMEM((C_GROUP, 1, tq), F32),
                        pltpu.VMEM((C_GROUP, 1, tq), F32),
                        pltpu.VMEM((C_GROUP, HEAD_DIM, tq), F32)],
        compiler_params=_cparams(("parallel", "parallel", "parallel", "arbitrary")),
        name="attention",
    )(qt, k, vt)


def _conv_kernel(zc_ref, zp_ref, zn_ref, w_ref, b_ref, lg_ref, lb_ref, o_ref, ybuf, *, tm, halo):
    t = pl.program_id(1)
    nt = pl.num_programs(1)
    ucol = C_WIDTH + 2 * KV_WIDTH

    def glu(z):
        u = z[:, ucol:ucol + 2 * D_WIDTH]
        return u[:, :D_WIDTH] * jax.nn.sigmoid(u[:, D_WIDTH:])

    ybuf[0:halo, :] = jnp.where(t == 0, 0.0, glu(zp_ref[0]))
    ybuf[halo:halo + tm, :] = glu(zc_ref[0])
    ybuf[halo + tm:, :] = jnp.where(t == nt - 1, 0.0, glu(zn_ref[0]))
    acc = jnp.zeros((tm, D_WIDTH), F32)
    for j in range(D_CONV_WIDTH):
        off = halo - D_PAD + j
        acc = acc + w_ref[j:j + 1, :] * ybuf[off:off + tm, :]
    y = acc + b_ref[...]
    mu = jnp.mean(y, axis=-1, keepdims=True)
    yc = y - mu
    var = jnp.mean(yc * yc, axis=-1, keepdims=True)
    yn = yc * lax.rsqrt(var + LN_EPS) * lg_ref[...] + lb_ref[...]
    o_ref[0] = (yn * jax.nn.sigmoid(yn)).astype(o_ref.dtype)


def _conformer_conv(z, dw_w, dw_b, cn_g, cn_b, ctx_len, tm):
    bsz, rows, ncol = z.shape
    seq = rows - ctx_len
    halo = 2 * SUBLANES
    hb = tm // halo
    off = ctx_len // tm
    offh = ctx_len // halo
    nh = seq // halo
    kern = functools.partial(_conv_kernel, tm=tm, halo=halo)
    row = lambda w: pl.BlockSpec((1, w), lambda b, t: (0, 0))
    return pl.pallas_call(
        kern,
        grid=(bsz, seq // tm),
        in_specs=[pl.BlockSpec((1, tm, ncol), lambda b, t: (b, t + off, 0)),
                  pl.BlockSpec((1, halo, ncol),
                               lambda b, t: (b, offh + jnp.maximum(t * hb - 1, 0), 0)),
                  pl.BlockSpec((1, halo, ncol),
                               lambda b, t: (b, offh + jnp.minimum((t + 1) * hb, nh - 1), 0)),
                  pl.BlockSpec((D_CONV_WIDTH, D_WIDTH), lambda b, t: (0, 0)),
                  row(D_WIDTH), row(D_WIDTH), row(D_WIDTH)],
        out_specs=pl.BlockSpec((1, tm, D_WIDTH), lambda b, t: (b, t, 0)),
        out_shape=jax.ShapeDtypeStruct((bsz, seq, D_WIDTH), BF16),
        scratch_shapes=[pltpu.VMEM((tm + 2 * halo, D_WIDTH), F32)],
        compiler_params=_cparams(("parallel", "parallel")),
        name="conformer_conv",
    )(z, z, z, dw_w, dw_b.reshape(1, -1), cn_g.reshape(1, -1), cn_b.reshape(1, -1))


def _forward(x, c, ctx, c_ctx, l0, l1, norm_f):
    bsz, seq, d = x.shape
    ctx_len = ctx.shape[1]
    tm = min(256, ctx_len)
    ctx_tiles = ctx_len // tm
    xs = jnp.concatenate([ctx, x], axis=1)

    mod = _modulation(c, c_ctx, l0["mod_w"], l0["mod_b"])
    z = _normmod_matmul(xs, l0["norm1"], mod, l0["w_in"], tm, ctx_tiles)
    r, v, g, bonus, kk, lwf, kdf, bf, lwr, kdr, br = _rwkv_prep(z, l0, tm, ctx_tiles)
    yf, yr = _rwkv_scan(r, v, kk, lwf, kdf, bf, lwr, kdr, br, ctx_len)
    o_rwkv = _rwkv_finish(yf, yr, bonus, g, l0["lnx_g"], l0["lnx_b"], tm)
    fcs = _fnet_chan(z, tm)
    ts_c = min(256, ctx_len)
    ts_l = min(512, seq)
    f_ctx = _fnet_seq(fcs, 0, ctx_len, ts_c, ts_c)
    f_lat = _fnet_seq(fcs, ctx_len, seq, ts_l, min(512, seq, ctx_len))
    o_fnet = jnp.concatenate([f_ctx, f_lat], axis=1)
    xs = _proj_residual(o_rwkv, o_fnet, xs, mod, l0["w_out"], tm, 0, ctx_tiles)
    xs = _peer(xs, l0["norm2"], mod, l0, tm, 0, ctx_tiles, False, norm_f)

    mod = _modulation(c, c_ctx, l1["mod_w"], l1["mod_b"])
    z = _normmod_matmul(xs, l1["norm1"], mod, l1["w_in"], tm, ctx_tiles)
    qt, k, vt = _attn_prep(z, l1["q_norm"], l1["k_norm"], ctx_len, tm)
    o_attn = _attention(qt, k, vt, ctx_len, tm, _key_tile(ctx_len + seq))
    o_conv = _conformer_conv(z, l1["dw_w"], l1["dw_b"], l1["cn_g"], l1["cn_b"], ctx_len, tm)
    xl = _proj_residual(o_attn, o_conv, xs, mod, l1["w_out"], tm, ctx_tiles, ctx_tiles,
                        o1_transposed=True)
    return _peer(xl, l1["norm2"], mod, l1, tm, ctx_tiles, ctx_tiles, True, norm_f)


def kernel(x, c, ctx, c_ctx, l0_mod_w, l0_mod_b, l0_norm1, l0_w_in, l0_shift_prev, l0_shift_next, l0_w0, l0_w2, l0_a0, l0_a2, l0_g2, l0_k_k, l0_k_a, l0_r_k, l0_lnx_g, l0_lnx_b, l0_w_out, l0_norm2, l0_pq, l0_sk1, l0_sk2, l0_pu, l0_pv, l1_mod_w, l1_mod_b, l1_norm1, l1_w_in, l1_q_norm, l1_k_norm, l1_dw_w, l1_dw_b, l1_cn_g, l1_cn_b, l1_w_out, l1_norm2, l1_pq, l1_sk1, l1_sk2, l1_pu, l1_pv, norm_f):
    l0 = dict(mod_w=l0_mod_w, mod_b=l0_mod_b, norm1=l0_norm1, w_in=l0_w_in,
              shift_prev=l0_shift_prev, shift_next=l0_shift_next, w0=l0_w0, w2=l0_w2,
              a0=l0_a0, a2=l0_a2, g2=l0_g2, k_k=l0_k_k, k_a=l0_k_a, r_k=l0_r_k,
              lnx_g=l0_lnx_g, lnx_b=l0_lnx_b, w_out=l0_w_out, norm2=l0_norm2,
              pq=l0_pq, sk1=l0_sk1, sk2=l0_sk2, pu=l0_pu, pv=l0_pv)
    l1 = dict(mod_w=l1_mod_w, mod_b=l1_mod_b, norm1=l1_norm1, w_in=l1_w_in,
              q_norm=l1_q_norm, k_norm=l1_k_norm, dw_w=l1_dw_w, dw_b=l1_dw_b,
              cn_g=l1_cn_g, cn_b=l1_cn_b, w_out=l1_w_out, norm2=l1_norm2,
              pq=l1_pq, sk1=l1_sk1, sk2=l1_sk2, pu=l1_pu, pv=l1_pv)
    return _forward(x, c, ctx, c_ctx, l0, l1, norm_f)
```

```python
import functools
import math

import jax
import jax.numpy as jnp
import numpy as np
from jax import lax
from jax.experimental import pallas as pl
from jax.experimental.pallas import tpu as pltpu

F32 = jnp.float32
BF16 = jnp.bfloat16
HIGHEST = lax.Precision.HIGHEST

LANES = 128
SUBLANES = 8
VMEM_LIMIT_BYTES = 56 * 1024 * 1024

N_MOD = 6
NORM_EPS = 1e-6
LN_EPS = 1e-5
GRID_W = 64
HEAD_DIM = 64
A_WIDTH = 768
A_HEADS = A_WIDTH // HEAD_DIM
A_RANK_W = 64
A_RANK_A = 64
A_RANK_G = 128
A_LNX_EPS = 64e-5
A_COLS = 3 * A_WIDTH + 2 * A_RANK_W + 2 * A_RANK_A + A_RANK_G
B_WIDTH = 256
B_GROUP_DIM = 64
C_WIDTH = 768
C_HEADS = 12
C_KV_HEADS = 4
C_GROUP = C_HEADS // C_KV_HEADS
KV_WIDTH = C_KV_HEADS * HEAD_DIM
ROPE_AXIS_DIM = HEAD_DIM // 2
ROPE_THETA = 10000.0
ATTN_SCALE = HEAD_DIM ** -0.5
D_WIDTH = 256
D_CONV_WIDTH = 31
D_PAD = D_CONV_WIDTH // 2
PK_HEADS = 8
PK_DIM = 256
PK_HALF = 128
N_KEYS = 128
PK_TOPK = 16

SCAN_CHUNK = 64
NEG_INF = float("-inf")


def _cparams(semantics):
    return pltpu.CompilerParams(dimension_semantics=semantics,
                                vmem_limit_bytes=VMEM_LIMIT_BYTES)


def _split_bf16(x):
    hi = x.astype(BF16)
    lo = (x - hi.astype(F32)).astype(BF16)
    return hi, lo


def _dot(a, b):
    return jnp.dot(a, b, preferred_element_type=F32)


def _dot_hp(a, b):
    return _mm3(_split_bf16(a), _split_bf16(b))


def _dot_nt(a, b, precision=None):
    return lax.dot_general(a, b, (((1,), (1,)), ((), ())),
                           preferred_element_type=F32, precision=precision)


def _dot_tn(a, b, precision=None):
    return lax.dot_general(a, b, (((0,), (0,)), ((), ())),
                           preferred_element_type=F32, precision=precision)


_NN = (((1,), (0,)), ((), ()))
_NT = (((1,), (1,)), ((), ()))
_TN = (((0,), (0,)), ((), ()))


def _mm3(a, b, dims=_NN):
    (ah, al), (bh, bl) = a, b
    dg = lambda x, y: lax.dot_general(x, y, dims, preferred_element_type=F32)
    return dg(ah, bh) + (dg(ah, bl) + dg(al, bh))


def _segsum(x, ones_bd):
    hi, lo = _split_bf16(x)
    return _dot(hi, ones_bd) + _dot(lo, ones_bd)


def _block_ones(width, seg):
    r = np.arange(width) // seg
    return jnp.asarray((r[:, None] == r[None, :]).astype(np.float32), dtype=BF16)


def _mod_kernel(c_ref, w_ref, b_ref, o_ref):
    c = c_ref[...]
    s = c * jax.nn.sigmoid(c)
    o_ref[...] = _dot_hp(s, w_ref[...]) + b_ref[...]


def _modulation(c, c_ctx, mod_w, mod_b):
    bsz, d = c.shape
    rows = SUBLANES * pl.cdiv(bsz + 1, SUBLANES)
    cc = jnp.zeros((rows, d), F32).at[:bsz].set(c).at[bsz].set(c_ctx)
    n = mod_w.shape[1]
    tn = n // 4
    out = pl.pallas_call(
        _mod_kernel,
        grid=(n // tn,),
        in_specs=[pl.BlockSpec((rows, d), lambda j: (0, 0)),
                  pl.BlockSpec((d, tn), lambda j: (0, j)),
                  pl.BlockSpec((1, tn), lambda j: (0, j))],
        out_specs=pl.BlockSpec((rows, tn), lambda j: (0, j)),
        out_shape=jax.ShapeDtypeStruct((rows, n), F32),
        compiler_params=_cparams(("arbitrary",)),
        name="adaln_mod",
    )(cc, mod_w, mod_b.reshape(1, n))
    lat = out[:bsz].reshape(bsz, N_MOD, d)
    ctx = jnp.broadcast_to(out[bsz].reshape(1, N_MOD, d), (bsz, N_MOD, d))
    return jnp.stack([ctx, lat], axis=1)


def _norm_mod(x, g, mod, row):
    ms = jnp.mean(x * x, axis=-1, keepdims=True)
    y = x * lax.rsqrt(ms + NORM_EPS) * g
    return y * (1.0 + mod[row + 1:row + 2]) + mod[row:row + 1]


def _normmod_matmul_kernel(x_ref, g_ref, mod_ref, w_ref, o1_ref, o2_ref):
    h = _norm_mod(x_ref[0], g_ref[...], mod_ref[0, 0], 0)
    z = _dot(h.astype(BF16), w_ref[...])
    n1 = o1_ref.shape[2]
    o1_ref[0] = z[:, :n1]
    o2_ref[0] = z[:, n1:]


def _normmod_matmul(x, g, mod, w, tm, ctx_tiles, n1):
    bsz, rows, d = x.shape
    n = w.shape[1]
    widths = (n1, n - n1)
    return pl.pallas_call(
        _normmod_matmul_kernel,
        grid=(bsz, rows // tm),
        in_specs=[pl.BlockSpec((1, tm, d), lambda b, t: (b, t, 0)),
                  pl.BlockSpec((1, d), lambda b, t: (0, 0)),
                  pl.BlockSpec((1, 1, N_MOD, d),
                               lambda b, t: (b, jnp.where(t >= ctx_tiles, 1, 0), 0, 0)),
                  pl.BlockSpec((d, n), lambda b, t: (0, 0))],
        out_specs=[pl.BlockSpec((1, tm, wd), lambda b, t: (b, t, 0)) for wd in widths],
        out_shape=[jax.ShapeDtypeStruct((bsz, rows, wd), F32) for wd in widths],
        compiler_params=_cparams(("parallel", "parallel")),
        name="normmod_proj",
    )(x, g.reshape(1, d), mod, w.astype(BF16))


def _rwkv_prep_kernel(z_ref, zp_ref, zn_ref, mup_ref, mun_ref, w0_ref, w2_ref, a0_ref, a2_ref,
                      g2_ref, kk_ref, ka_ref, rk_ref, ones_ref,
                      r_o, v_o, g_o, bonus_o, kkn_o, lwf_o, kdf_o, bf_o, lwr_o, kdr_o, br_o,
                      *, tm, ctx_tiles):
    t = pl.program_id(1)
    nt = pl.num_programs(1)
    z = z_ref[0][:, :A_COLS]
    zp_row = zp_ref[0][SUBLANES - 1:SUBLANES, :A_COLS]
    zn_row = zn_ref[0][0:1, :A_COLS]
    first = jnp.logical_or(t == 0, t == ctx_tiles)
    last = jnp.logical_or(t == ctx_tiles - 1, t == nt - 1)
    zp_row = jnp.where(first, 0.0, zp_row)
    zn_row = jnp.where(last, 0.0, zn_row)
    ridx = lax.broadcasted_iota(jnp.int32, (tm, 1), 0)
    z_prev = jnp.where(ridx == 0, zp_row, pltpu.roll(z, 1, 0))
    z_next = jnp.where(ridx == tm - 1, zn_row, pltpu.roll(z, tm - 1, 0))
    zs = z + mup_ref[...] * (z_prev - z) + mun_ref[...] * (z_next - z)

    w = A_WIDTH
    r = zs[:, 0:w]
    k = zs[:, w:2 * w]
    v = zs[:, 2 * w:3 * w]
    o = 3 * w
    xw = zs[:, o:o + 2 * A_RANK_W]
    o += 2 * A_RANK_W
    xa = zs[:, o:o + 2 * A_RANK_A]
    o += 2 * A_RANK_A
    xg = zs[:, o:o + A_RANK_G]
    ones_bd = ones_ref[...]

    g_o[0] = _dot_hp(jax.nn.sigmoid(xg), g2_ref[...])
    kk = k * kk_ref[...]
    nrm = jnp.sqrt(_segsum(kk * kk, ones_bd))
    kk = kk / jnp.maximum(nrm, 1e-12)
    r_o[0] = r
    v_o[0] = v
    kkn_o[0] = kk
    bonus = jnp.zeros_like(r)
    tw = jnp.tanh(xw)
    for d, (lw_o, kd_o, b_o) in enumerate(((lwf_o, kdf_o, bf_o), (lwr_o, kdr_o, br_o))):
        wl = w0_ref[d:d + 1, :] + _dot_hp(tw[:, d * A_RANK_W:(d + 1) * A_RANK_W], w2_ref[d])
        w_log = -jax.nn.softplus(-wl) - 0.5
        lw_o[0] = -jnp.exp(w_log)
        a_gate = jax.nn.sigmoid(
            a0_ref[d:d + 1, :] + _dot_hp(xa[:, d * A_RANK_A:(d + 1) * A_RANK_A], a2_ref[d]))
        k_d = k * (1.0 + (a_gate - 1.0) * ka_ref[...])
        kd_o[0] = k_d
        b_o[0] = kk * a_gate
        bonus = bonus + _segsum(r * k_d * rk_ref[...], ones_bd) * v
    bonus_o[0] = bonus


def _rwkv_prep(z, p, tm, ctx_tiles):
    bsz, rows, ncol = z.shape
    w = A_WIDTH
    hb = tm // SUBLANES
    nblk8 = rows // SUBLANES
    row_spec = lambda width: pl.BlockSpec((1, width), lambda b, t: (0, 0))
    full = lambda shape: pl.BlockSpec(shape, lambda b, t: (0,) * len(shape))
    out_spec = pl.BlockSpec((1, tm, w), lambda b, t: (b, t, 0))
    out_shape = jax.ShapeDtypeStruct((bsz, rows, w), F32)
    kern = functools.partial(_rwkv_prep_kernel, tm=tm, ctx_tiles=ctx_tiles)
    return pl.pallas_call(
        kern,
        grid=(bsz, rows // tm),
        in_specs=[pl.BlockSpec((1, tm, ncol), lambda b, t: (b, t, 0)),
                  pl.BlockSpec((1, SUBLANES, ncol),
                               lambda b, t: (b, jnp.maximum(t * hb - 1, 0), 0)),
                  pl.BlockSpec((1, SUBLANES, ncol),
                               lambda b, t: (b, jnp.minimum((t + 1) * hb, nblk8 - 1), 0)),
                  row_spec(A_COLS), row_spec(A_COLS),
                  full((2, w)), full((2, A_RANK_W, w)), full((2, w)), full((2, A_RANK_A, w)),
                  full((A_RANK_G, w)), row_spec(w), row_spec(w), row_spec(w), full((w, w))],
        out_specs=[out_spec] * 11,
        out_shape=[out_shape] * 11,
        compiler_params=_cparams(("parallel", "parallel")),
        name="rwkv_prep",
    )(z, z, z, p["shift_prev"].reshape(1, -1), p["shift_next"].reshape(1, -1),
      p["w0"], p["w2"], p["a0"], p["a2"], p["g2"], p["k_k"].reshape(1, w),
      p["k_a"].reshape(1, w), p["r_k"].reshape(1, w), _block_ones(w, HEAD_DIM))


SCAN_HEADS = 4


def _scan_kernel(rf, vf, kkf, lwf, kdf, bf, rr, vr, kkr, lwr, kdr, br, yf_o, yr_o, s_ref, *, c):
    @pl.when(pl.program_id(2) == 0)
    def _():
        s_ref[...] = jnp.zeros_like(s_ref)

    ri = lax.broadcasted_iota(jnp.int32, (c, c), 0)
    ci = lax.broadcasted_iota(jnp.int32, (c, c), 1)
    ri2 = lax.broadcasted_iota(jnp.int32, (c, 2 * c), 0)
    ci2 = lax.broadcasted_iota(jnp.int32, (c, 2 * c), 1)
    ci2 = jnp.where(ci2 >= c, ci2 - c, ci2)
    dirs = ((rf, vf, kkf, lwf, kdf, bf, ci <= ri, ci < ri, ci2 <= ri2, c - 1),
            (rr, vr, kkr, lwr, kdr, br, ci >= ri, ci > ri, ci2 >= ri2, 0))
    lhs, rhs, vs, ss, pend, incl, strict = [], [], [], [], [], [], []
    for d, (r_, v_, kk_, lw_, kd_, b_, m_incl, m_strict, m_incl2, edge) in enumerate(dirs):
        lw = lw_[0]
        tri = m_incl.astype(F32).astype(BF16)
        lw_h = lw.astype(BF16)
        lw_m, lw_l = _split_bf16(lw - lw_h.astype(F32))
        lcum = _dot(tri, lw_h) + (_dot(tri, lw_m) + _dot(tri, lw_l))
        p_in = jnp.exp(lcum)
        p_inv = jnp.exp(-lcum)
        lhs_all = jnp.concatenate([-kk_[0] * jnp.exp(lcum - lw), r_[0] * p_in], axis=0)
        rhs_all = jnp.concatenate([b_[0] * p_inv, kd_[0] * p_inv], axis=0)
        v_all = v_[0]
        for h in range(SCAN_HEADS):
            sl = slice(h * HEAD_DIM, (h + 1) * HEAD_DIM)
            lhs.append(lhs_all[:, sl])
            rhs.append(rhs_all[:, sl])
            vs.append(v_all[:, sl])
            ss.append(s_ref[d, h])
            pend.append(p_in[edge:edge + 1, sl])
            incl.append(m_incl2)
            strict.append(m_strict)
    n = len(lhs)
    idx = range(n)
    lhs2 = [_split_bf16(lhs[i]) for i in idx]
    rhs2 = [_split_bf16(rhs[i]) for i in idx]
    v2 = [_split_bf16(vs[i]) for i in idx]
    s2 = [_split_bf16(ss[i]) for i in idx]
    g = [_mm3(lhs2[i], rhs2[i], _NT) for i in idx]
    xs = [_mm3(lhs2[i], s2[i], _NT) for i in idx]
    a_ak = [_split_bf16(jnp.where(strict[i], g[i][:c, c:], 0.0)) for i in idx]
    x = [xs[i][:c] + _mm3(a_ak[i], v2[i]) for i in idx]
    apow = [_split_bf16(jnp.where(strict[i], g[i][:c, :c], 0.0)) for i in idx]
    steps = int(math.log2(c))
    for k in range(steps):
        x2 = [_split_bf16(x[i]) for i in idx]
        x = [x[i] + _mm3(apow[i], x2[i]) for i in idx]
        if k + 1 < steps:
            apow = [_split_bf16(_mm3(apow[i], apow[i])) for i in idx]
    x2 = [_split_bf16(x[i]) for i in idx]
    sav = [tuple(jnp.concatenate([x2[i][t], v2[i][t]], axis=0) for t in range(2)) for i in idx]
    a_r = [_split_bf16(jnp.where(incl[i], g[i][c:], 0.0)) for i in idx]
    y = [xs[i][c:] + _mm3(a_r[i], sav[i]) for i in idx]
    s_new = [(ss[i] + _mm3(sav[i], rhs2[i], _TN)) * pend[i] for i in idx]
    for d, y_o in enumerate((yf_o, yr_o)):
        for h in range(SCAN_HEADS):
            s_ref[d, h] = s_new[d * SCAN_HEADS + h]
        y_o[0] = jnp.concatenate(y[d * SCAN_HEADS:(d + 1) * SCAN_HEADS], axis=1)


def _rwkv_scan(r, v, kk, lwf, kdf, bf, lwr, kdr, br, ctx_len):
    bsz, rows, w = r.shape
    c = SCAN_CHUNK
    nch = rows // c
    cch = ctx_len // c
    lanes = SCAN_HEADS * HEAD_DIM
    fwd = lambda b, hp, i: (b, i, hp)
    rev = lambda b, hp, i: (b, jnp.where(i < cch, cch - 1 - i, nch - 1 + cch - i), hp)
    blk = lambda im: pl.BlockSpec((1, c, lanes), im)
    kern = functools.partial(_scan_kernel, c=c)
    out_shape = jax.ShapeDtypeStruct((bsz, rows, w), F32)
    return pl.pallas_call(
        kern,
        grid=(bsz, w // lanes, nch),
        in_specs=[blk(fwd)] * 6 + [blk(rev)] * 6,
        out_specs=[blk(fwd), blk(rev)],
        out_shape=[out_shape, out_shape],
        scratch_shapes=[pltpu.VMEM((2, SCAN_HEADS, HEAD_DIM, HEAD_DIM), F32)],
        compiler_params=_cparams(("parallel", "parallel", "arbitrary")),
        name="rwkv_scan",
    )(r, v, kk, lwf, kdf, bf, r, v, kk, lwr, kdr, br)


def _rwkv_finish_kernel(yf, yr, bonus, g, lg, lb, ones_ref, o_ref):
    y = yf[0] + yr[0]
    ones_bd = ones_ref[...]
    inv = 1.0 / HEAD_DIM
    mu = _segsum(y, ones_bd) * inv
    yc = y - mu
    var = _segsum(yc * yc, ones_bd) * inv
    yn = yc * lax.rsqrt(var + A_LNX_EPS) * lg[...] + lb[...]
    o_ref[0] = ((yn + bonus[0]) * g[0]).astype(o_ref.dtype)


def _rwkv_finish(yf, yr, bonus, g, lnx_g, lnx_b, tm):
    bsz, rows, w = yf.shape
    blk = pl.BlockSpec((1, tm, w), lambda b, t: (b, t, 0))
    row = pl.BlockSpec((1, w), lambda b, t: (0, 0))
    return pl.pallas_call(
        _rwkv_finish_kernel,
        grid=(bsz, rows // tm),
        in_specs=[blk, blk, blk, blk, row, row, pl.BlockSpec((w, w), lambda b, t: (0, 0))],
        out_specs=blk,
        out_shape=jax.ShapeDtypeStruct((bsz, rows, w), BF16),
        compiler_params=_cparams(("parallel", "parallel")),
        name="rwkv_finish",
    )(yf, yr, bonus, g, lnx_g.reshape(1, w), lnx_b.reshape(1, w), _block_ones(w, HEAD_DIM))


def _fnet_chan_kernel(z_ref, m_ref, o_ref):
    o_ref[0] = _dot(z_ref[0].astype(BF16), m_ref[...]).astype(o_ref.dtype)


def _fnet_chan(z, tm):
    bsz, rows, ncol = z.shape
    j = np.arange(B_WIDTH)
    same = (j[:, None] // B_GROUP_DIM) == (j[None, :] // B_GROUP_DIM)
    ang = 2.0 * np.pi * ((j[:, None] % B_GROUP_DIM) * (j[None, :] % B_GROUP_DIM) % B_GROUP_DIM) / B_GROUP_DIM
    m = np.concatenate([np.where(same, np.cos(ang), 0.0), np.where(same, np.sin(ang), 0.0)], axis=1)
    return pl.pallas_call(
        _fnet_chan_kernel,
        grid=(bsz, rows // tm),
        in_specs=[pl.BlockSpec((1, tm, ncol), lambda b, t: (b, t, 0)),
                  pl.BlockSpec((B_WIDTH, 2 * B_WIDTH), lambda b, t: (0, 0))],
        out_specs=pl.BlockSpec((1, tm, 2 * B_WIDTH), lambda b, t: (b, t, 0)),
        out_shape=jax.ShapeDtypeStruct((bsz, rows, 2 * B_WIDTH), BF16),
        compiler_params=_cparams(("parallel", "parallel")),
        name="fnet_chan",
    )(z, jnp.asarray(m, dtype=BF16))


def _fnet_seq_kernel(fcs_ref, cb_ref, sb_ref, c0_ref, s0_ref, o_ref, acc_ref, *, scale, nb):
    k = pl.program_id(1)

    @pl.when(k == 0)
    def _():
        acc_ref[...] = jnp.zeros_like(acc_ref)

    cb = cb_ref[...]
    sb = sb_ref[...]
    c0 = c0_ref[0]
    s0 = s0_ref[0]
    cm = (c0 * cb - s0 * sb).astype(BF16)
    sm = (s0 * cb + c0 * sb).astype(BF16)
    for b in range(nb):
        fcs = fcs_ref[b]
        acc_ref[b] += _dot(cm, fcs[:, :B_WIDTH]) - _dot(sm, fcs[:, B_WIDTH:])

    @pl.when(k == pl.num_programs(1) - 1)
    def _():
        o_ref[...] = (acc_ref[...] * scale).astype(o_ref.dtype)


def _fnet_seq(fcs, row0, length, ts, tk):
    bsz = fcs.shape[0]
    ds = jnp.arange(ts, dtype=jnp.int32)[:, None]
    tt = jnp.arange(length, dtype=jnp.int32)[None, :]
    ang = ((ds * tt) % length).astype(F32) * (2.0 * math.pi / length)
    cb, sb = jnp.cos(ang), jnp.sin(ang)
    s0 = (jnp.arange(length // ts, dtype=jnp.int32) * ts)[:, None]
    ang0 = ((s0 * tt) % length).astype(F32) * (2.0 * math.pi / length)
    c0, sn0 = jnp.cos(ang0)[:, None, :], jnp.sin(ang0)[:, None, :]
    kern = functools.partial(_fnet_seq_kernel, scale=1.0 / math.sqrt(length * B_GROUP_DIM), nb=bsz)
    koff = row0 // tk
    return pl.pallas_call(
        kern,
        grid=(length // ts, length // tk),
        in_specs=[pl.BlockSpec((bsz, tk, 2 * B_WIDTH), lambda s, k: (0, k + koff, 0)),
                  pl.BlockSpec((ts, tk), lambda s, k: (0, k)),
                  pl.BlockSpec((ts, tk), lambda s, k: (0, k)),
                  pl.BlockSpec((1, 1, tk), lambda s, k: (s, 0, k)),
                  pl.BlockSpec((1, 1, tk), lambda s, k: (s, 0, k))],
        out_specs=pl.BlockSpec((bsz, ts, B_WIDTH), lambda s, k: (0, s, 0)),
        out_shape=jax.ShapeDtypeStruct((bsz, length, B_WIDTH), BF16),
        scratch_shapes=[pltpu.VMEM((bsz, ts, B_WIDTH), F32)],
        compiler_params=_cparams(("parallel", "arbitrary")),
        name="fnet_seq",
    )(fcs, cb, sb, c0, sn0)


def _proj_residual_kernel(o1_ref, o2_ref, x_ref, mod_ref, w1_ref, w2_ref, out_ref, *, o1_transposed):
    o1 = o1_ref[0]
    if o1_transposed:
        o1 = o1.astype(F32).T.astype(BF16)
    mix = _dot(o1, w1_ref[...]) + _dot(o2_ref[0], w2_ref[...])
    gate = mod_ref[0, 0][2:3]
    out_ref[0] = x_ref[0] + gate * mix


def _proj_residual(o1, o2, x, mod, w_out, tm, tile_off, ctx_tiles, o1_transposed=False):
    bsz, rows, w2 = o2.shape
    w1 = o1.shape[1] if o1_transposed else o1.shape[2]
    d = x.shape[2]
    o1_spec = (pl.BlockSpec((1, w1, tm), lambda b, t: (b, 0, t)) if o1_transposed
               else pl.BlockSpec((1, tm, w1), lambda b, t: (b, t, 0)))
    kern = functools.partial(_proj_residual_kernel, o1_transposed=o1_transposed)
    return pl.pallas_call(
        kern,
        grid=(bsz, rows // tm),
        in_specs=[o1_spec,
                  pl.BlockSpec((1, tm, w2), lambda b, t: (b, t, 0)),
                  pl.BlockSpec((1, tm, d), lambda b, t: (b, t + tile_off, 0)),
                  pl.BlockSpec((1, 1, N_MOD, d),
                               lambda b, t: (b, jnp.where(t + tile_off >= ctx_tiles, 1, 0), 0, 0)),
                  pl.BlockSpec((w1, d), lambda b, t: (0, 0)),
                  pl.BlockSpec((w2, d), lambda b, t: (0, 0))],
        out_specs=pl.BlockSpec((1, tm, d), lambda b, t: (b, t, 0)),
        out_shape=jax.ShapeDtypeStruct((bsz, rows, d), F32),
        compiler_params=_cparams(("parallel", "parallel")),
        name="proj_residual",
    )(o1, o2, x, mod, w_out[:w1].astype(BF16), w_out[w1:].astype(BF16))


def _merge_exchange_network(n):
    pairs = []
    p = 1
    while p < n:
        k = p
        while k >= 1:
            for j in range(k % p, n - k, 2 * k):
                for i in range(min(k, n - j - k)):
                    if (i + j) // (2 * p) == (i + j + k) // (2 * p):
                        pairs.append((i + j, i + j + k))
            k //= 2
        p *= 2
    return pairs


def _top16_rows(s):
    nslab = s.shape[0] // SUBLANES
    slabs = [s[k * SUBLANES:(k + 1) * SUBLANES] for k in range(nslab)]
    for a, b in _merge_exchange_network(nslab):
        slabs[a], slabs[b] = jnp.maximum(slabs[a], slabs[b]), jnp.minimum(slabs[a], slabs[b])
    rows = []
    for t in range(PK_TOPK):
        m = jnp.max(slabs[0], axis=0, keepdims=True)
        rows.append(m)
        hit = slabs[0] >= m
        for k in range(nslab - 1 - t):
            slabs[k] = jnp.where(hit, slabs[k + 1], slabs[k])
    return rows


_PAIR_IDX = [(i, j) for i in range(PK_TOPK) for j in range(PK_TOPK) if (i + 1) * (j + 1) <= PK_TOPK]
BF16_SUBLANES = 2 * SUBLANES
PEER_SUB_E1 = 1
PEER_STEP_E1 = 16


def _rows_bf16(row):
    packed = jnp.broadcast_to(row, (BF16_SUBLANES, row.shape[1])).astype(BF16)
    return jnp.tile(packed, (N_KEYS // BF16_SUBLANES, 1))


def _peer_kernel(x_ref, g_ref, mod_ref, pqh_ref, pql_ref, sk1h_ref, sk1l_ref, sk2h_ref, sk2l_ref,
                 pu_ref, pvt_ref, gf_ref,
                 out_ref, ht_ref, r2_ref, e2_ref, c1_ref, w1_ref, s_ref, cand_ref, acc_ref,
                 *, tm, e1_per_blk, final_norm):
    eb = pl.program_id(2)

    @pl.when(eb == 0)
    def _():
        h = _norm_mod(x_ref[0], g_ref[...], mod_ref[0, 0], 3)
        ht = h.T
        hhi, hlo = _split_bf16(ht)
        ht_ref[...] = hhi
        qt = _mm3((pqh_ref[...], pql_ref[...]), (hhi, hlo))
        qs = [_split_bf16(qt[i * PK_HALF:(i + 1) * PK_HALF]) for i in range(2 * PK_HEADS)]
        for hd in range(PK_HEADS):
            s_ref[2 * hd] = _mm3((sk1h_ref[hd], sk1l_ref[hd]), qs[2 * hd])
            s_ref[2 * hd + 1] = _mm3((sk2h_ref[hd], sk2l_ref[hd]), qs[2 * hd + 1])
        for hd in range(PK_HEADS):
            s1 = s_ref[2 * hd]
            s2 = s_ref[2 * hd + 1]
            v1 = _top16_rows(s1)
            v2 = _top16_rows(s2)
            cand_ref[...] = jnp.full_like(cand_ref, NEG_INF)
            for n, (i, j) in enumerate(_PAIR_IDX):
                cand_ref[n:n + 1, :] = v1[i] + v2[j]
            cand = cand_ref[...]
            top = v1[0] + v2[0]
            zsum = jnp.zeros_like(top)
            tau = top
            for _ in range(PK_TOPK):
                m = jnp.max(cand, axis=0, keepdims=True)
                zsum = zsum + jnp.exp(m - top)
                tau = m
                cand = jnp.where(cand >= m, NEG_INF, cand)
            rank2 = jnp.full((N_KEYS, tm), float(PK_TOPK), F32)
            for j in reversed(range(PK_TOPK)):
                rank2 = jnp.where(s2 >= v2[j], float(j), rank2)
            count1 = jnp.zeros((N_KEYS, tm), F32)
            for i in reversed(range(PK_TOPK)):
                cnt = jnp.zeros_like(top)
                for j in range(PK_TOPK // (i + 1)):
                    cnt = cnt + jnp.where(v1[i] + v2[j] >= tau, 1.0, 0.0)
                count1 = jnp.where(s1 >= v1[i], cnt, count1)
            r2_ref[hd] = rank2.astype(BF16)
            c1_ref[hd] = count1
            e2_ref[hd] = jnp.exp(s2 - v2[0]).astype(BF16)
            w1_ref[hd] = jnp.exp(s1 - v1[0]) / zsum
        acc_ref[...] = jnp.zeros_like(acc_ref)

    sub = PEER_SUB_E1 * N_KEYS
    nsub = e1_per_blk // PEER_SUB_E1
    ht = ht_ref[...]

    def up(i):
        return _dot(pu_ref[i * sub:(i + 1) * sub, :], ht)

    def activation(i, hu):
        acts = []
        for jj in range(PEER_SUB_E1):
            e1 = eb * e1_per_blk + PEER_SUB_E1 * i + jj
            gate = jnp.zeros((N_KEYS, tm), BF16)
            for hd in range(PK_HEADS):
                count = _rows_bf16(c1_ref[hd, pl.ds(e1, 1), :])
                weight = _rows_bf16(w1_ref[hd, pl.ds(e1, 1), :])
                gate = gate + jnp.where(r2_ref[hd] < count, e2_ref[hd] * weight, 0)
            u = hu[jj * N_KEYS:(jj + 1) * N_KEYS]
            act = 0.5 * u * (1.0 + lax.erf(u * (1.0 / math.sqrt(2.0))))
            acts.append(act.astype(BF16) * gate)
        return jnp.concatenate(acts, axis=0)

    hus = [up(i) for i in range(nsub)]
    acts = [activation(i, hus[i]) for i in range(nsub)]
    acc_ref[...] += _dot(pvt_ref[...], jnp.concatenate(acts, axis=0))

    @pl.when(eb == pl.num_programs(2) - 1)
    def _():
        y = x_ref[0] + mod_ref[0, 0][5:6] * acc_ref[...].T
        if final_norm:
            ms = jnp.mean(y * y, axis=-1, keepdims=True)
            y = y * lax.rsqrt(ms + NORM_EPS) * gf_ref[...]
        out_ref[0] = y


def _peer(x, g, mod, p, tm, tile_off, ctx_tiles, final_norm, norm_f):
    bsz, rows, d = x.shape
    n_exp = p["pu"].shape[0]
    e1_per_blk = PEER_STEP_E1
    eblk = e1_per_blk * N_KEYS
    pqt = p["pq"].T
    pqh = pqt.astype(BF16)
    pql = (pqt - pqh.astype(F32)).astype(BF16)
    nq = pqt.shape[0]
    kern = functools.partial(_peer_kernel, tm=tm, e1_per_blk=e1_per_blk, final_norm=final_norm)
    const = lambda shape: pl.BlockSpec(shape, lambda b, t, e: (0,) * len(shape))
    return pl.pallas_call(
        kern,
        grid=(bsz, rows // tm, n_exp // eblk),
        in_specs=[pl.BlockSpec((1, tm, d), lambda b, t, e: (b, t, 0)),
                  const((1, d)),
                  pl.BlockSpec((1, 1, N_MOD, d),
                               lambda b, t, e: (b, jnp.where(t + tile_off >= ctx_tiles, 1, 0), 0, 0)),
                  const((nq, d)), const((nq, d)),
                  const((PK_HEADS, N_KEYS, PK_HALF)), const((PK_HEADS, N_KEYS, PK_HALF)),
                  const((PK_HEADS, N_KEYS, PK_HALF)), const((PK_HEADS, N_KEYS, PK_HALF)),
                  pl.BlockSpec((eblk, d), lambda b, t, e: (e, 0)),
                  pl.BlockSpec((d, eblk), lambda b, t, e: (0, e)),
                  const((1, d))],
        out_specs=pl.BlockSpec((1, tm, d), lambda b, t, e: (b, t, 0)),
        out_shape=jax.ShapeDtypeStruct((bsz, rows, d), F32),
        scratch_shapes=[pltpu.VMEM((d, tm), BF16),
                        pltpu.VMEM((PK_HEADS, N_KEYS, tm), BF16),
                        pltpu.VMEM((PK_HEADS, N_KEYS, tm), BF16),
                        pltpu.VMEM((PK_HEADS, N_KEYS, tm), F32),
                        pltpu.VMEM((PK_HEADS, N_KEYS, tm), F32),
                        pltpu.VMEM((2 * PK_HEADS, N_KEYS, tm), F32),
                        pltpu.VMEM((SUBLANES * pl.cdiv(len(_PAIR_IDX), SUBLANES), tm), F32),
                        pltpu.VMEM((d, tm), F32)],
        compiler_params=_cparams(("parallel", "parallel", "arbitrary")),
        name="peer",
    )(x, g.reshape(1, d), mod, pqh, pql, *_split_bf16(p["sk1"]), *_split_bf16(p["sk2"]),
      p["pu"].astype(BF16), p["pv"].T.astype(BF16), norm_f.reshape(1, d))


def _rope_tables(rows, ctx_len):
    t = jnp.arange(rows - ctx_len, dtype=jnp.int32)
    inv = ROPE_THETA ** (-jnp.arange(0, ROPE_AXIS_DIM, 2, dtype=F32) / ROPE_AXIS_DIM)
    ang_r = (t // GRID_W).astype(F32)[:, None] * inv
    ang_c = (t % GRID_W).astype(F32)[:, None] * inv
    cos = jnp.concatenate([jnp.cos(ang_r)] * 2 + [jnp.cos(ang_c)] * 2, axis=1)
    sin = jnp.concatenate([-jnp.sin(ang_r), jnp.sin(ang_r), -jnp.sin(ang_c), jnp.sin(ang_c)], axis=1)
    cos = jnp.concatenate([jnp.ones((ctx_len, HEAD_DIM), F32), cos], axis=0)
    sin = jnp.concatenate([jnp.zeros((ctx_len, HEAD_DIM), F32), sin], axis=0)
    return jnp.tile(cos, (1, 2)), jnp.tile(sin, (1, 2))


def _head_norm_rope(x, gain, ones_bd, cos, sin):
    ms = _segsum(x * x, ones_bd) * (1.0 / HEAD_DIM)
    y = x * lax.rsqrt(ms + NORM_EPS) * gain
    outs = []
    half = ROPE_AXIS_DIM // 2
    lane = lax.broadcasted_iota(jnp.int32, (1, LANES), 1)
    first_half = (lane % ROPE_AXIS_DIM) < half
    for i in range(x.shape[1] // LANES):
        yc = y[:, i * LANES:(i + 1) * LANES]
        partner = jnp.where(first_half, pltpu.roll(yc, LANES - half, 1), pltpu.roll(yc, half, 1))
        outs.append(yc * cos + partner * sin)
    return jnp.concatenate(outs, axis=1)


def _attn_prep_kernel(z_ref, qn_ref, kn_ref, cos_ref, sin_ref, onesq_ref, onesk_ref,
                      q_o, k_o, v_o):
    z = z_ref[0]
    cos = cos_ref[...]
    sin = sin_ref[...]
    q = _head_norm_rope(z[:, :C_WIDTH], qn_ref[...], onesq_ref[...], cos, sin)
    k = _head_norm_rope(z[:, C_WIDTH:C_WIDTH + KV_WIDTH], kn_ref[...], onesk_ref[...], cos, sin)
    q_o[0] = (q * (ATTN_SCALE * math.log2(math.e))).T.astype(q_o.dtype)
    for j in range(C_KV_HEADS):
        k_o[0, j] = k[:, j * HEAD_DIM:(j + 1) * HEAD_DIM].astype(k_o.dtype)
    v_o[0] = z[:, C_WIDTH + KV_WIDTH:C_WIDTH + 2 * KV_WIDTH].T.astype(v_o.dtype)


def _attn_prep(z, q_norm, k_norm, ctx_len, tm):
    bsz, rows, ncol = z.shape
    cos, sin = _rope_tables(rows, ctx_len)
    qn = jnp.tile(q_norm, C_HEADS).reshape(1, C_WIDTH)
    kn = jnp.tile(k_norm, C_KV_HEADS).reshape(1, KV_WIDTH)
    const = lambda shape: pl.BlockSpec(shape, lambda b, t: (0,) * len(shape))
    return pl.pallas_call(
        _attn_prep_kernel,
        grid=(bsz, rows // tm),
        in_specs=[pl.BlockSpec((1, tm, ncol), lambda b, t: (b, t, 0)),
                  const((1, C_WIDTH)), const((1, KV_WIDTH)),
                  pl.BlockSpec((tm, LANES), lambda b, t: (t, 0)),
                  pl.BlockSpec((tm, LANES), lambda b, t: (t, 0)),
                  const((C_WIDTH, C_WIDTH)), const((KV_WIDTH, KV_WIDTH))],
        out_specs=[pl.BlockSpec((1, C_WIDTH, tm), lambda b, t: (b, 0, t)),
                   pl.BlockSpec((1, C_KV_HEADS, tm, HEAD_DIM), lambda b, t: (b, 0, t, 0)),
                   pl.BlockSpec((1, KV_WIDTH, tm), lambda b, t: (b, 0, t))],
        out_shape=[jax.ShapeDtypeStruct((bsz, C_WIDTH, rows), BF16),
                   jax.ShapeDtypeStruct((bsz, C_KV_HEADS, rows, HEAD_DIM), BF16),
                   jax.ShapeDtypeStruct((bsz, KV_WIDTH, rows), BF16)],
        compiler_params=_cparams(("parallel", "parallel")),
        name="attn_prep",
    )(z, qn, kn, cos, sin, _block_ones(C_WIDTH, HEAD_DIM), _block_ones(KV_WIDTH, HEAD_DIM))


def _attn_kernel(qt_ref, k_ref, vt_ref, o_ref, m_ref, l_ref, acc_ref, *, kc):
    kt = pl.program_id(3)

    @pl.when(kt == 0)
    def _():
        m_ref[...] = jnp.full_like(m_ref, NEG_INF)
        l_ref[...] = jnp.zeros_like(l_ref)
        acc_ref[...] = jnp.zeros_like(acc_ref)

    grp = range(C_GROUP)
    qts = [qt_ref[0, g * HEAD_DIM:(g + 1) * HEAD_DIM, :] for g in grp]
    nck = k_ref.shape[2] // kc

    def scores(c):
        kb = k_ref[0, 0, c * kc:(c + 1) * kc, :]
        return [_dot(kb, qts[g]) for g in grp]

    m = [m_ref[g] for g in grp]
    l = [l_ref[g] for g in grp]
    acc = [acc_ref[g] for g in grp]
    s_next = scores(0)
    for c in range(nck):
        s = s_next
        if c + 1 < nck:
            s_next = scores(c + 1)
        vt = vt_ref[0, :, c * kc:(c + 1) * kc]
        m_new = [jnp.maximum(m[g], jnp.max(s[g], axis=0, keepdims=True)) for g in grp]
        p = [jnp.exp2(s[g] - m_new[g]) for g in grp]
        pv = [_dot(vt, p[g].astype(BF16)) for g in grp]
        for g in grp:
            alpha = jnp.exp2(m[g] - m_new[g])
            l[g] = alpha * l[g] + jnp.sum(p[g], axis=0, keepdims=True)
            acc[g] = alpha * acc[g] + pv[g]
        m = m_new
    for g in grp:
        m_ref[g] = m[g]
        l_ref[g] = l[g]
        acc_ref[g] = acc[g]

    @pl.when(kt == pl.num_programs(3) - 1)
    def _():
        o_ref[0] = jnp.concatenate(
            [acc_ref[g] / l_ref[g] for g in grp], axis=0).astype(o_ref.dtype)


ATTN_MAX_KEY_TILE = 2816
ATTN_MAX_KEY_CHUNK = 256


def _key_tile(rows, limit=ATTN_MAX_KEY_TILE):
    return max(t for t in range(LANES, min(rows, limit) + 1, LANES) if rows % t == 0)


def _attention(qt, k, vt, ctx_len, tq, tk):
    bsz, _, rows = qt.shape
    seq = rows - ctx_len
    qoff = ctx_len // tq
    gw = C_GROUP * HEAD_DIM
    kern = functools.partial(_attn_kernel, kc=_key_tile(tk, ATTN_MAX_KEY_CHUNK))
    return pl.pallas_call(
        kern,
        grid=(bsz, C_KV_HEADS, seq // tq, rows // tk),
        in_specs=[pl.BlockSpec((1, gw, tq), lambda b, j, i, kk: (b, j, i + qoff)),
                  pl.BlockSpec((1, 1, tk, HEAD_DIM), lambda b, j, i, kk: (b, j, kk, 0)),
                  pl.BlockSpec((1, HEAD_DIM, tk), lambda b, j, i, kk: (b, j, kk))],
        out_specs=pl.BlockSpec((1, gw, tq), lambda b, j, i, kk: (b, j, i)),
        out_shape=jax.ShapeDtypeStruct((bsz, C_WIDTH, seq), BF16),
        scratch_shapes=[pltpu.VMEM((C_GROUP, 1, tq), F32),
                        pltpu.VMEM((C_GROUP, 1, tq), F32),
                        pltpu.VMEM((C_GROUP, HEAD_DIM, tq), F32)],
        compiler_params=_cparams(("parallel", "parallel", "parallel", "arbitrary")),
        name="attention",
    )(qt, k, vt)


def _conv_kernel(zc_ref, zp_ref, zn_ref, w_ref, b_ref, lg_ref, lb_ref, o_ref, ybuf, *, tm, halo):
    t = pl.program_id(1)
    nt = pl.num_programs(1)

    def glu(u):
        return u[:, :D_WIDTH] * jax.nn.sigmoid(u[:, D_WIDTH:])

    ybuf[0:halo, :] = jnp.where(t == 0, 0.0, glu(zp_ref[0]))
    ybuf[halo:halo + tm, :] = glu(zc_ref[0])
    ybuf[halo + tm:, :] = jnp.where(t == nt - 1, 0.0, glu(zn_ref[0]))
    acc = jnp.zeros((tm, D_WIDTH), F32)
    for j in range(D_CONV_WIDTH):
        off = halo - D_PAD + j
        acc = acc + w_ref[j:j + 1, :] * ybuf[off:off + tm, :]
    y = acc + b_ref[...]
    mu = jnp.mean(y, axis=-1, keepdims=True)
    yc = y - mu
    var = jnp.mean(yc * yc, axis=-1, keepdims=True)
    yn = yc * lax.rsqrt(var + LN_EPS) * lg_ref[...] + lb_ref[...]
    o_ref[0] = (yn * jax.nn.sigmoid(yn)).astype(o_ref.dtype)


def _conformer_conv(z, dw_w, dw_b, cn_g, cn_b, ctx_len, tm):
    bsz, rows, ncol = z.shape
    seq = rows - ctx_len
    halo = 2 * SUBLANES
    hb = tm // halo
    off = ctx_len // tm
    offh = ctx_len // halo
    nh = seq // halo
    kern = functools.partial(_conv_kernel, tm=tm, halo=halo)
    row = lambda w: pl.BlockSpec((1, w), lambda b, t: (0, 0))
    return pl.pallas_call(
        kern,
        grid=(bsz, seq // tm),
        in_specs=[pl.BlockSpec((1, tm, ncol), lambda b, t: (b, t + off, 0)),
                  pl.BlockSpec((1, halo, ncol),
                               lambda b, t: (b, offh + jnp.maximum(t * hb - 1, 0), 0)),
                  pl.BlockSpec((1, halo, ncol),
                               lambda b, t: (b, offh + jnp.minimum((t + 1) * hb, nh - 1), 0)),
                  pl.BlockSpec((D_CONV_WIDTH, D_WIDTH), lambda b, t: (0, 0)),
                  row(D_WIDTH), row(D_WIDTH), row(D_WIDTH)],
        out_specs=pl.BlockSpec((1, tm, D_WIDTH), lambda b, t: (b, t, 0)),
        out_shape=jax.ShapeDtypeStruct((bsz, seq, D_WIDTH), BF16),
        scratch_shapes=[pltpu.VMEM((tm + 2 * halo, D_WIDTH), F32)],
        compiler_params=_cparams(("parallel", "parallel")),
        name="conformer_conv",
    )(z, z, z, dw_w, dw_b.reshape(1, -1), cn_g.reshape(1, -1), cn_b.reshape(1, -1))


def _forward(x, c, ctx, c_ctx, l0, l1, norm_f):
    bsz, seq, d = x.shape
    ctx_len = ctx.shape[1]
    tm = min(256, ctx_len)
    ctx_tiles = ctx_len // tm
    xs = jnp.concatenate([ctx, x], axis=1)

    mod = _modulation(c, c_ctx, l0["mod_w"], l0["mod_b"])
    z, zf = _normmod_matmul(xs, l0["norm1"], mod, l0["w_in"], tm, ctx_tiles, A_COLS)
    r, v, g, bonus, kk, lwf, kdf, bf, lwr, kdr, br = _rwkv_prep(z, l0, tm, ctx_tiles)
    yf, yr = _rwkv_scan(r, v, kk, lwf, kdf, bf, lwr, kdr, br, ctx_len)
    o_rwkv = _rwkv_finish(yf, yr, bonus, g, l0["lnx_g"], l0["lnx_b"], tm)
    fcs = _fnet_chan(zf, tm)
    ts_c = min(256, ctx_len)
    ts_l = min(512, seq)
    f_ctx = _fnet_seq(fcs, 0, ctx_len, ts_c, ts_c)
    f_lat = _fnet_seq(fcs[:, ctx_len:], 0, seq, ts_l, ts_l)
    o_fnet = jnp.concatenate([f_ctx, f_lat], axis=1)
    xs = _proj_residual(o_rwkv, o_fnet, xs, mod, l0["w_out"], tm, 0, ctx_tiles)
    xs = _peer(xs, l0["norm2"], mod, l0, tm, 0, ctx_tiles, False, norm_f)

    mod = _modulation(c, c_ctx, l1["mod_w"], l1["mod_b"])
    z, zu = _normmod_matmul(xs, l1["norm1"], mod, l1["w_in"], tm, ctx_tiles,
                            C_WIDTH + 2 * KV_WIDTH)
    qt, k, vt = _attn_prep(z, l1["q_norm"], l1["k_norm"], ctx_len, tm)
    o_attn = _attention(qt, k, vt, ctx_len, tm, _key_tile(ctx_len + seq))
    o_conv = _conformer_conv(zu, l1["dw_w"], l1["dw_b"], l1["cn_g"], l1["cn_b"], ctx_len, tm)
    xl = _proj_residual(o_attn, o_conv, xs, mod, l1["w_out"], tm, ctx_tiles, ctx_tiles,
                        o1_transposed=True)
    return _peer(xl, l1["norm2"], mod, l1, tm, ctx_tiles, ctx_tiles, True, norm_f)


def kernel(x, c, ctx, c_ctx, l0_mod_w, l0_mod_b, l0_norm1, l0_w_in, l0_shift_prev, l0_shift_next, l0_w0, l0_w2, l0_a0, l0_a2, l0_g2, l0_k_k, l0_k_a, l0_r_k, l0_lnx_g, l0_lnx_b, l0_w_out, l0_norm2, l0_pq, l0_sk1, l0_sk2, l0_pu, l0_pv, l1_mod_w, l1_mod_b, l1_norm1, l1_w_in, l1_q_norm, l1_k_norm, l1_dw_w, l1_dw_b, l1_cn_g, l1_cn_b, l1_w_out, l1_norm2, l1_pq, l1_sk1, l1_sk2, l1_pu, l1_pv, norm_f):
    l0 = dict(mod_w=l0_mod_w, mod_b=l0_mod_b, norm1=l0_norm1, w_in=l0_w_in,
              shift_prev=l0_shift_prev, shift_next=l0_shift_next, w0=l0_w0, w2=l0_w2,
              a0=l0_a0, a2=l0_a2, g2=l0_g2, k_k=l0_k_k, k_a=l0_k_a, r_k=l0_r_k,
              lnx_g=l0_lnx_g, lnx_b=l0_lnx_b, w_out=l0_w_out, norm2=l0_norm2,
              pq=l0_pq, sk1=l0_sk1, sk2=l0_sk2, pu=l0_pu, pv=l0_pv)
    l1 = dict(mod_w=l1_mod_w, mod_b=l1_mod_b, norm1=l1_norm1, w_in=l1_w_in,
              q_norm=l1_q_norm, k_norm=l1_k_norm, dw_w=l1_dw_w, dw_b=l1_dw_b,
              cn_g=l1_cn_g, cn_b=l1_cn_b, w_out=l1_w_out, norm2=l1_norm2,
              pq=l1_pq, sk1=l1_sk1, sk2=l1_sk2, pu=l1_pu, pv=l1_pv)
    return _forward(x, c, ctx, c_ctx, l0, l1, norm_f)
```

```python
import functools
import math

import jax
import jax.numpy as jnp
import numpy as np
from jax import lax
from jax.experimental import pallas as pl
from jax.experimental.pallas import tpu as pltpu

F32 = jnp.float32
BF16 = jnp.bfloat16
HIGHEST = lax.Precision.HIGHEST

LANES = 128
SUBLANES = 8
VMEM_LIMIT_BYTES = 56 * 1024 * 1024

N_MOD = 6
NORM_EPS = 1e-6
LN_EPS = 1e-5
GRID_W = 64
HEAD_DIM = 64
A_WIDTH = 768
A_HEADS = A_WIDTH // HEAD_DIM
A_RANK_W = 64
A_RANK_A = 64
A_RANK_G = 128
A_LNX_EPS = 64e-5
A_COLS = 3 * A_WIDTH + 2 * A_RANK_W + 2 * A_RANK_A + A_RANK_G
B_WIDTH = 256
B_GROUP_DIM = 64
C_WIDTH = 768
C_HEADS = 12
C_KV_HEADS = 4
C_GROUP = C_HEADS // C_KV_HEADS
KV_WIDTH = C_KV_HEADS * HEAD_DIM
ROPE_AXIS_DIM = HEAD_DIM // 2
ROPE_THETA = 10000.0
ATTN_SCALE = HEAD_DIM ** -0.5
D_WIDTH = 256
D_CONV_WIDTH = 31
D_PAD = D_CONV_WIDTH // 2
PK_HEADS = 8
PK_DIM = 256
PK_HALF = 128
N_KEYS = 128
PK_TOPK = 16

SCAN_CHUNK = 64
NEG_INF = float("-inf")


def _cparams(semantics):
    return pltpu.CompilerParams(dimension_semantics=semantics,
                                vmem_limit_bytes=VMEM_LIMIT_BYTES)


def _split_bf16(x):
    hi = x.astype(BF16)
    lo = (x - hi.astype(F32)).astype(BF16)
    return hi, lo


def _dot(a, b):
    return jnp.dot(a, b, preferred_element_type=F32)


def _dot_hp(a, b):
    return _mm3(_split_bf16(a), _split_bf16(b))


def _dot_nt(a, b, precision=None):
    return lax.dot_general(a, b, (((1,), (1,)), ((), ())),
                           preferred_element_type=F32, precision=precision)


def _dot_tn(a, b, precision=None):
    return lax.dot_general(a, b, (((0,), (0,)), ((), ())),
                           preferred_element_type=F32, precision=precision)


_NN = (((1,), (0,)), ((), ()))
_NT = (((1,), (1,)), ((), ()))
_TN = (((0,), (0,)), ((), ()))


def _mm3(a, b, dims=_NN):
    (ah, al), (bh, bl) = a, b
    dg = lambda x, y: lax.dot_general(x, y, dims, preferred_element_type=F32)
    return dg(ah, bh) + (dg(ah, bl) + dg(al, bh))


def _segsum(x, ones_bd):
    hi, lo = _split_bf16(x)
    return _dot(hi, ones_bd) + _dot(lo, ones_bd)


def _block_ones(width, seg):
    r = np.arange(width) // seg
    return jnp.asarray((r[:, None] == r[None, :]).astype(np.float32), dtype=BF16)


def _mod_kernel(c_ref, w_ref, b_ref, o_ref):
    c = c_ref[...]
    s = c * jax.nn.sigmoid(c)
    o_ref[...] = _dot_hp(s, w_ref[...]) + b_ref[...]


def _modulation(c, c_ctx, mod_w, mod_b):
    bsz, d = c.shape
    rows = SUBLANES * pl.cdiv(bsz + 1, SUBLANES)
    cc = jnp.zeros((rows, d), F32).at[:bsz].set(c).at[bsz].set(c_ctx)
    n = mod_w.shape[1]
    tn = n // 4
    out = pl.pallas_call(
        _mod_kernel,
        grid=(n // tn,),
        in_specs=[pl.BlockSpec((rows, d), lambda j: (0, 0)),
                  pl.BlockSpec((d, tn), lambda j: (0, j)),
                  pl.BlockSpec((1, tn), lambda j: (0, j))],
        out_specs=pl.BlockSpec((rows, tn), lambda j: (0, j)),
        out_shape=jax.ShapeDtypeStruct((rows, n), F32),
        compiler_params=_cparams(("arbitrary",)),
        name="adaln_mod",
    )(cc, mod_w, mod_b.reshape(1, n))
    lat = out[:bsz].reshape(bsz, N_MOD, d)
    ctx = jnp.broadcast_to(out[bsz].reshape(1, N_MOD, d), (bsz, N_MOD, d))
    return jnp.stack([ctx, lat], axis=1)


def _norm_mod(x, g, mod, row):
    ms = jnp.mean(x * x, axis=-1, keepdims=True)
    y = x * lax.rsqrt(ms + NORM_EPS) * g
    return y * (1.0 + mod[row + 1:row + 2]) + mod[row:row + 1]


def _normmod_matmul_kernel(x_ref, g_ref, mod_ref, w_ref, o1_ref, o2_ref):
    h = _norm_mod(x_ref[0], g_ref[...], mod_ref[0, 0], 0)
    z = _dot(h.astype(BF16), w_ref[...])
    n1 = o1_ref.shape[2]
    o1_ref[0] = z[:, :n1]
    o2_ref[0] = z[:, n1:]


def _normmod_matmul(x, g, mod, w, tm, ctx_tiles, n1):
    bsz, rows, d = x.shape
    n = w.shape[1]
    widths = (n1, n - n1)
    return pl.pallas_call(
        _normmod_matmul_kernel,
        grid=(bsz, rows // tm),
        in_specs=[pl.BlockSpec((1, tm, d), lambda b, t: (b, t, 0)),
                  pl.BlockSpec((1, d), lambda b, t: (0, 0)),
                  pl.BlockSpec((1, 1, N_MOD, d),
                               lambda b, t: (b, jnp.where(t >= ctx_tiles, 1, 0), 0, 0)),
                  pl.BlockSpec((d, n), lambda b, t: (0, 0))],
        out_specs=[pl.BlockSpec((1, tm, wd), lambda b, t: (b, t, 0)) for wd in widths],
        out_shape=[jax.ShapeDtypeStruct((bsz, rows, wd), F32) for wd in widths],
        compiler_params=_cparams(("parallel", "parallel")),
        name="normmod_proj",
    )(x, g.reshape(1, d), mod, w.astype(BF16))


def _rwkv_prep_kernel(z_ref, zp_ref, zn_ref, mup_ref, mun_ref, w0_ref, w2_ref, a0_ref, a2_ref,
                      g2_ref, kk_ref, ka_ref, rk_ref, ones_ref,
                      r_o, v_o, g_o, bonus_o, kkn_o, lwf_o, kdf_o, bf_o, lwr_o, kdr_o, br_o,
                      *, tm, ctx_tiles):
    t = pl.program_id(1)
    nt = pl.num_programs(1)
    z = z_ref[0][:, :A_COLS]
    zp_row = zp_ref[0][SUBLANES - 1:SUBLANES, :A_COLS]
    zn_row = zn_ref[0][0:1, :A_COLS]
    first = jnp.logical_or(t == 0, t == ctx_tiles)
    last = jnp.logical_or(t == ctx_tiles - 1, t == nt - 1)
    zp_row = jnp.where(first, 0.0, zp_row)
    zn_row = jnp.where(last, 0.0, zn_row)
    ridx = lax.broadcasted_iota(jnp.int32, (tm, 1), 0)
    z_prev = jnp.where(ridx == 0, zp_row, pltpu.roll(z, 1, 0))
    z_next = jnp.where(ridx == tm - 1, zn_row, pltpu.roll(z, tm - 1, 0))
    zs = z + mup_ref[...] * (z_prev - z) + mun_ref[...] * (z_next - z)

    w = A_WIDTH
    r = zs[:, 0:w]
    k = zs[:, w:2 * w]
    v = zs[:, 2 * w:3 * w]
    o = 3 * w
    xw = zs[:, o:o + 2 * A_RANK_W]
    o += 2 * A_RANK_W
    xa = zs[:, o:o + 2 * A_RANK_A]
    o += 2 * A_RANK_A
    xg = zs[:, o:o + A_RANK_G]
    ones_bd = ones_ref[...]

    g_o[0] = _dot_hp(jax.nn.sigmoid(xg), g2_ref[...])
    kk = k * kk_ref[...]
    nrm = jnp.sqrt(_segsum(kk * kk, ones_bd))
    kk = kk / jnp.maximum(nrm, 1e-12)
    r_o[0] = r
    v_o[0] = v
    kkn_o[0] = kk
    bonus = jnp.zeros_like(r)
    tw = jnp.tanh(xw)
    for d, (lw_o, kd_o, b_o) in enumerate(((lwf_o, kdf_o, bf_o), (lwr_o, kdr_o, br_o))):
        wl = w0_ref[d:d + 1, :] + _dot_hp(tw[:, d * A_RANK_W:(d + 1) * A_RANK_W], w2_ref[d])
        w_log = -jax.nn.softplus(-wl) - 0.5
        lw_o[0] = -jnp.exp(w_log)
        a_gate = jax.nn.sigmoid(
            a0_ref[d:d + 1, :] + _dot_hp(xa[:, d * A_RANK_A:(d + 1) * A_RANK_A], a2_ref[d]))
        k_d = k * (1.0 + (a_gate - 1.0) * ka_ref[...])
        kd_o[0] = k_d
        b_o[0] = kk * a_gate
        bonus = bonus + _segsum(r * k_d * rk_ref[...], ones_bd) * v
    bonus_o[0] = bonus


def _rwkv_prep(z, p, tm, ctx_tiles):
    bsz, rows, ncol = z.shape
    w = A_WIDTH
    hb = tm // SUBLANES
    nblk8 = rows // SUBLANES
    row_spec = lambda width: pl.BlockSpec((1, width), lambda b, t: (0, 0))
    full = lambda shape: pl.BlockSpec(shape, lambda b, t: (0,) * len(shape))
    out_spec = pl.BlockSpec((1, tm, w), lambda b, t: (b, t, 0))
    out_shape = jax.ShapeDtypeStruct((bsz, rows, w), F32)
    kern = functools.partial(_rwkv_prep_kernel, tm=tm, ctx_tiles=ctx_tiles)
    return pl.pallas_call(
        kern,
        grid=(bsz, rows // tm),
        in_specs=[pl.BlockSpec((1, tm, ncol), lambda b, t: (b, t, 0)),
                  pl.BlockSpec((1, SUBLANES, ncol),
                               lambda b, t: (b, jnp.maximum(t * hb - 1, 0), 0)),
                  pl.BlockSpec((1, SUBLANES, ncol),
                               lambda b, t: (b, jnp.minimum((t + 1) * hb, nblk8 - 1), 0)),
                  row_spec(A_COLS), row_spec(A_COLS),
                  full((2, w)), full((2, A_RANK_W, w)), full((2, w)), full((2, A_RANK_A, w)),
                  full((A_RANK_G, w)), row_spec(w), row_spec(w), row_spec(w), full((w, w))],
        out_specs=[out_spec] * 11,
        out_shape=[out_shape] * 11,
        compiler_params=_cparams(("parallel", "parallel")),
        name="rwkv_prep",
    )(z, z, z, p["shift_prev"].reshape(1, -1), p["shift_next"].reshape(1, -1),
      p["w0"], p["w2"], p["a0"], p["a2"], p["g2"], p["k_k"].reshape(1, w),
      p["k_a"].reshape(1, w), p["r_k"].reshape(1, w), _block_ones(w, HEAD_DIM))


SCAN_HEADS = 4


def _scan_kernel(rf, vf, kkf, lwf, kdf, bf, rr, vr, kkr, lwr, kdr, br, yf_o, yr_o, s_ref, *, c):
    @pl.when(pl.program_id(2) == 0)
    def _():
        s_ref[...] = jnp.zeros_like(s_ref)

    ri = lax.broadcasted_iota(jnp.int32, (c, c), 0)
    ci = lax.broadcasted_iota(jnp.int32, (c, c), 1)
    ri2 = lax.broadcasted_iota(jnp.int32, (c, 2 * c), 0)
    ci2 = lax.broadcasted_iota(jnp.int32, (c, 2 * c), 1)
    ci2 = jnp.where(ci2 >= c, ci2 - c, ci2)
    dirs = ((rf, vf, kkf, lwf, kdf, bf, ci <= ri, ci < ri, ci2 <= ri2, c - 1),
            (rr, vr, kkr, lwr, kdr, br, ci >= ri, ci > ri, ci2 >= ri2, 0))
    lhs, rhs, vs, ss, pend, incl, strict = [], [], [], [], [], [], []
    for d, (r_, v_, kk_, lw_, kd_, b_, m_incl, m_strict, m_incl2, edge) in enumerate(dirs):
        lw = lw_[0]
        tri = m_incl.astype(F32).astype(BF16)
        lw_h = lw.astype(BF16)
        lw_m, lw_l = _split_bf16(lw - lw_h.astype(F32))
        lcum = _dot(tri, lw_h) + (_dot(tri, lw_m) + _dot(tri, lw_l))
        p_in = jnp.exp(lcum)
        p_inv = jnp.exp(-lcum)
        lhs_all = jnp.concatenate([-kk_[0] * jnp.exp(lcum - lw), r_[0] * p_in], axis=0)
        rhs_all = jnp.concatenate([b_[0] * p_inv, kd_[0] * p_inv], axis=0)
        v_all = v_[0]
        for h in range(SCAN_HEADS):
            sl = slice(h * HEAD_DIM, (h + 1) * HEAD_DIM)
            lhs.append(lhs_all[:, sl])
            rhs.append(rhs_all[:, sl])
            vs.append(v_all[:, sl])
            ss.append(s_ref[d, h])
            pend.append(p_in[edge:edge + 1, sl])
            incl.append(m_incl2)
            strict.append(m_strict)
    n = len(lhs)
    idx = range(n)
    lhs2 = [_split_bf16(lhs[i]) for i in idx]
    rhs2 = [_split_bf16(rhs[i]) for i in idx]
    v2 = [_split_bf16(vs[i]) for i in idx]
    s2 = [_split_bf16(ss[i]) for i in idx]
    g = [_mm3(lhs2[i], rhs2[i], _NT) for i in idx]
    xs = [_mm3(lhs2[i], s2[i], _NT) for i in idx]
    a_ak = [_split_bf16(jnp.where(strict[i], g[i][:c, c:], 0.0)) for i in idx]
    x = [xs[i][:c] + _mm3(a_ak[i], v2[i]) for i in idx]
    apow = [_split_bf16(jnp.where(strict[i], g[i][:c, :c], 0.0)) for i in idx]
    steps = int(math.log2(c))
    for k in range(steps):
        x2 = [_split_bf16(x[i]) for i in idx]
        x = [x[i] + _mm3(apow[i], x2[i]) for i in idx]
        if k + 1 < steps:
            apow = [_split_bf16(_mm3(apow[i], apow[i])) for i in idx]
    x2 = [_split_bf16(x[i]) for i in idx]
    sav = [tuple(jnp.concatenate([x2[i][t], v2[i][t]], axis=0) for t in range(2)) for i in idx]
    a_r = [_split_bf16(jnp.where(incl[i], g[i][c:], 0.0)) for i in idx]
    y = [xs[i][c:] + _mm3(a_r[i], sav[i]) for i in idx]
    s_new = [(ss[i] + _mm3(sav[i], rhs2[i], _TN)) * pend[i] for i in idx]
    for d, y_o in enumerate((yf_o, yr_o)):
        for h in range(SCAN_HEADS):
            s_ref[d, h] = s_new[d * SCAN_HEADS + h]
        y_o[0] = jnp.concatenate(y[d * SCAN_HEADS:(d + 1) * SCAN_HEADS], axis=1)


def _rwkv_scan(r, v, kk, lwf, kdf, bf, lwr, kdr, br, ctx_len):
    bsz, rows, w = r.shape
    c = SCAN_CHUNK
    nch = rows // c
    cch = ctx_len // c
    lanes = SCAN_HEADS * HEAD_DIM
    fwd = lambda b, hp, i: (b, i, hp)
    rev = lambda b, hp, i: (b, jnp.where(i < cch, cch - 1 - i, nch - 1 + cch - i), hp)
    blk = lambda im: pl.BlockSpec((1, c, lanes), im)
    kern = functools.partial(_scan_kernel, c=c)
    out_shape = jax.ShapeDtypeStruct((bsz, rows, w), F32)
    return pl.pallas_call(
        kern,
        grid=(bsz, w // lanes, nch),
        in_specs=[blk(fwd)] * 6 + [blk(rev)] * 6,
        out_specs=[blk(fwd), blk(rev)],
        out_shape=[out_shape, out_shape],
        scratch_shapes=[pltpu.VMEM((2, SCAN_HEADS, HEAD_DIM, HEAD_DIM), F32)],
        compiler_params=_cparams(("parallel", "parallel", "arbitrary")),
        name="rwkv_scan",
    )(r, v, kk, lwf, kdf, bf, r, v, kk, lwr, kdr, br)


def _rwkv_finish_kernel(yf, yr, bonus, g, lg, lb, ones_ref, o_ref):
    y = yf[0] + yr[0]
    ones_bd = ones_ref[...]
    inv = 1.0 / HEAD_DIM
    mu = _segsum(y, ones_bd) * inv
    yc = y - mu
    var = _segsum(yc * yc, ones_bd) * inv
    yn = yc * lax.rsqrt(var + A_LNX_EPS) * lg[...] + lb[...]
    o_ref[0] = ((yn + bonus[0]) * g[0]).astype(o_ref.dtype)


def _rwkv_finish(yf, yr, bonus, g, lnx_g, lnx_b, tm):
    bsz, rows, w = yf.shape
    blk = pl.BlockSpec((1, tm, w), lambda b, t: (b, t, 0))
    row = pl.BlockSpec((1, w), lambda b, t: (0, 0))
    return pl.pallas_call(
        _rwkv_finish_kernel,
        grid=(bsz, rows // tm),
        in_specs=[blk, blk, blk, blk, row, row, pl.BlockSpec((w, w), lambda b, t: (0, 0))],
        out_specs=blk,
        out_shape=jax.ShapeDtypeStruct((bsz, rows, w), BF16),
        compiler_params=_cparams(("parallel", "parallel")),
        name="rwkv_finish",
    )(yf, yr, bonus, g, lnx_g.reshape(1, w), lnx_b.reshape(1, w), _block_ones(w, HEAD_DIM))


def _fnet_chan_kernel(z_ref, m_ref, o_ref):
    o_ref[0] = _dot(z_ref[0].astype(BF16), m_ref[...]).astype(o_ref.dtype)


def _fnet_chan(z, tm):
    bsz, rows, ncol = z.shape
    j = np.arange(B_WIDTH)
    same = (j[:, None] // B_GROUP_DIM) == (j[None, :] // B_GROUP_DIM)
    ang = 2.0 * np.pi * ((j[:, None] % B_GROUP_DIM) * (j[None, :] % B_GROUP_DIM) % B_GROUP_DIM) / B_GROUP_DIM
    m = np.concatenate([np.where(same, np.cos(ang), 0.0), np.where(same, np.sin(ang), 0.0)], axis=1)
    return pl.pallas_call(
        _fnet_chan_kernel,
        grid=(bsz, rows // tm),
        in_specs=[pl.BlockSpec((1, tm, ncol), lambda b, t: (b, t, 0)),
                  pl.BlockSpec((B_WIDTH, 2 * B_WIDTH), lambda b, t: (0, 0))],
        out_specs=pl.BlockSpec((1, tm, 2 * B_WIDTH), lambda b, t: (b, t, 0)),
        out_shape=jax.ShapeDtypeStruct((bsz, rows, 2 * B_WIDTH), BF16),
        compiler_params=_cparams(("parallel", "parallel")),
        name="fnet_chan",
    )(z, jnp.asarray(m, dtype=BF16))


def _fnet_seq_kernel(fcs_ref, cb_ref, sb_ref, c0_ref, s0_ref, o_ref, acc_ref, *, scale, nb):
    k = pl.program_id(1)

    @pl.when(k == 0)
    def _():
        acc_ref[...] = jnp.zeros_like(acc_ref)

    cb = cb_ref[...]
    sb = sb_ref[...]
    c0 = c0_ref[0]
    s0 = s0_ref[0]
    cm = (c0 * cb - s0 * sb).astype(BF16)
    sm = (s0 * cb + c0 * sb).astype(BF16)
    for b in range(nb):
        fcs = fcs_ref[b]
        acc_ref[b] += _dot(cm, fcs[:, :B_WIDTH]) - _dot(sm, fcs[:, B_WIDTH:])

    @pl.when(k == pl.num_programs(1) - 1)
    def _():
        o_ref[...] = (acc_ref[...] * scale).astype(o_ref.dtype)


def _fnet_seq(fcs, row0, length, ts, tk):
    bsz = fcs.shape[0]
    ds = jnp.arange(ts, dtype=jnp.int32)[:, None]
    tt = jnp.arange(length, dtype=jnp.int32)[None, :]
    ang = ((ds * tt) % length).astype(F32) * (2.0 * math.pi / length)
    cb, sb = jnp.cos(ang), jnp.sin(ang)
    s0 = (jnp.arange(length // ts, dtype=jnp.int32) * ts)[:, None]
    ang0 = ((s0 * tt) % length).astype(F32) * (2.0 * math.pi / length)
    c0, sn0 = jnp.cos(ang0)[:, None, :], jnp.sin(ang0)[:, None, :]
    kern = functools.partial(_fnet_seq_kernel, scale=1.0 / math.sqrt(length * B_GROUP_DIM), nb=bsz)
    koff = row0 // tk
    return pl.pallas_call(
        kern,
        grid=(length // ts, length // tk),
        in_specs=[pl.BlockSpec((bsz, tk, 2 * B_WIDTH), lambda s, k: (0, k + koff, 0)),
                  pl.BlockSpec((ts, tk), lambda s, k: (0, k)),
                  pl.BlockSpec((ts, tk), lambda s, k: (0, k)),
                  pl.BlockSpec((1, 1, tk), lambda s, k: (s, 0, k)),
                  pl.BlockSpec((1, 1, tk), lambda s, k: (s, 0, k))],
        out_specs=pl.BlockSpec((bsz, ts, B_WIDTH), lambda s, k: (0, s, 0)),
        out_shape=jax.ShapeDtypeStruct((bsz, length, B_WIDTH), BF16),
        scratch_shapes=[pltpu.VMEM((bsz, ts, B_WIDTH), F32)],
        compiler_params=_cparams(("parallel", "arbitrary")),
        name="fnet_seq",
    )(fcs, cb, sb, c0, sn0)


def _proj_residual_kernel(o1_ref, o2_ref, x_ref, mod_ref, w1_ref, w2_ref, out_ref, *, o1_transposed):
    o1 = o1_ref[0]
    if o1_transposed:
        o1 = o1.astype(F32).T.astype(BF16)
    mix = _dot(o1, w1_ref[...]) + _dot(o2_ref[0], w2_ref[...])
    gate = mod_ref[0, 0][2:3]
    out_ref[0] = x_ref[0] + gate * mix


def _proj_residual(o1, o2, x, mod, w_out, tm, tile_off, ctx_tiles, o1_transposed=False):
    bsz, rows, w2 = o2.shape
    w1 = o1.shape[1] if o1_transposed else o1.shape[2]
    d = x.shape[2]
    o1_spec = (pl.BlockSpec((1, w1, tm), lambda b, t: (b, 0, t)) if o1_transposed
               else pl.BlockSpec((1, tm, w1), lambda b, t: (b, t, 0)))
    kern = functools.partial(_proj_residual_kernel, o1_transposed=o1_transposed)
    return pl.pallas_call(
        kern,
        grid=(bsz, rows // tm),
        in_specs=[o1_spec,
                  pl.BlockSpec((1, tm, w2), lambda b, t: (b, t, 0)),
                  pl.BlockSpec((1, tm, d), lambda b, t: (b, t + tile_off, 0)),
                  pl.BlockSpec((1, 1, N_MOD, d),
                               lambda b, t: (b, jnp.where(t + tile_off >= ctx_tiles, 1, 0), 0, 0)),
                  pl.BlockSpec((w1, d), lambda b, t: (0, 0)),
                  pl.BlockSpec((w2, d), lambda b, t: (0, 0))],
        out_specs=pl.BlockSpec((1, tm, d), lambda b, t: (b, t, 0)),
        out_shape=jax.ShapeDtypeStruct((bsz, rows, d), F32),
        compiler_params=_cparams(("parallel", "parallel")),
        name="proj_residual",
    )(o1, o2, x, mod, w_out[:w1].astype(BF16), w_out[w1:].astype(BF16))


def _merge_exchange_network(n):
    pairs = []
    p = 1
    while p < n:
        k = p
        while k >= 1:
            for j in range(k % p, n - k, 2 * k):
                for i in range(min(k, n - j - k)):
                    if (i + j) // (2 * p) == (i + j + k) // (2 * p):
                        pairs.append((i + j, i + j + k))
            k //= 2
        p *= 2
    return pairs


def _top16_rows(s):
    nslab = s.shape[0] // SUBLANES
    slabs = [s[k * SUBLANES:(k + 1) * SUBLANES] for k in range(nslab)]
    for a, b in _merge_exchange_network(nslab):
        slabs[a], slabs[b] = jnp.maximum(slabs[a], slabs[b]), jnp.minimum(slabs[a], slabs[b])
    rows = []
    for t in range(PK_TOPK):
        m = jnp.max(slabs[0], axis=0, keepdims=True)
        rows.append(m)
        hit = slabs[0] >= m
        for k in range(nslab - 1 - t):
            slabs[k] = jnp.where(hit, slabs[k + 1], slabs[k])
    return rows


_PAIR_IDX = [(i, j) for i in range(PK_TOPK) for j in range(PK_TOPK) if (i + 1) * (j + 1) <= PK_TOPK]
BF16_SUBLANES = 2 * SUBLANES
PEER_SUB_E1 = 1
PEER_STEP_E1 = 16


def _rows_bf16(row):
    packed = jnp.broadcast_to(row, (BF16_SUBLANES, row.shape[1])).astype(BF16)
    return jnp.tile(packed, (N_KEYS // BF16_SUBLANES, 1))


def _peer_kernel(x_ref, g_ref, mod_ref, pqh_ref, pql_ref, sk1h_ref, sk1l_ref, sk2h_ref, sk2l_ref,
                 pu_ref, pvt_ref, gf_ref,
                 out_ref, ht_ref, r2_ref, e2_ref, c1_ref, w1_ref, s_ref, cand_ref, act_ref, acc_ref,
                 *, tm, e1_per_blk, final_norm):
    eb = pl.program_id(2)
    last = pl.num_programs(2) - 1
    sub = PEER_SUB_E1 * N_KEYS
    nsub = e1_per_blk // PEER_SUB_E1

    def activation(blk, i, hu):
        acts = []
        for jj in range(PEER_SUB_E1):
            e1 = blk * e1_per_blk + PEER_SUB_E1 * i + jj
            gate = jnp.zeros((N_KEYS, tm), BF16)
            for hd in range(PK_HEADS):
                count = _rows_bf16(c1_ref[hd, pl.ds(e1, 1), :])
                weight = _rows_bf16(w1_ref[hd, pl.ds(e1, 1), :])
                gate = gate + jnp.where(r2_ref[hd] < count, e2_ref[hd] * weight, 0)
            u = hu[jj * N_KEYS:(jj + 1) * N_KEYS]
            act = 0.5 * u * (1.0 + lax.erf(u * (1.0 / math.sqrt(2.0))))
            acts.append(act.astype(BF16) * gate)
        return jnp.concatenate(acts, axis=0)

    def up_projections(ht):
        return [_dot(pu_ref[i * sub:(i + 1) * sub, :], ht) for i in range(nsub)]

    def store_activations(blk, hus):
        for i in range(nsub):
            act_ref[(blk + 1) % 2, i * sub:(i + 1) * sub, :] = activation(blk, i, hus[i])

    @pl.when(eb == 0)
    def _():
        h = _norm_mod(x_ref[0], g_ref[...], mod_ref[0, 0], 3)
        ht = h.T
        hhi, hlo = _split_bf16(ht)
        ht_ref[...] = hhi
        qt = _mm3((pqh_ref[...], pql_ref[...]), (hhi, hlo))
        qs = [_split_bf16(qt[i * PK_HALF:(i + 1) * PK_HALF]) for i in range(2 * PK_HEADS)]
        for hd in range(PK_HEADS):
            s_ref[2 * hd] = _mm3((sk1h_ref[hd], sk1l_ref[hd]), qs[2 * hd])
            s_ref[2 * hd + 1] = _mm3((sk2h_ref[hd], sk2l_ref[hd]), qs[2 * hd + 1])
        hus = up_projections(hhi)
        for hd in range(PK_HEADS):
            s1 = s_ref[2 * hd]
            s2 = s_ref[2 * hd + 1]
            v1 = _top16_rows(s1)
            v2 = _top16_rows(s2)
            cand_ref[...] = jnp.full_like(cand_ref, NEG_INF)
            for n, (i, j) in enumerate(_PAIR_IDX):
                cand_ref[n:n + 1, :] = v1[i] + v2[j]
            cand = cand_ref[...]
            top = v1[0] + v2[0]
            zsum = jnp.zeros_like(top)
            tau = top
            for _ in range(PK_TOPK):
                m = jnp.max(cand, axis=0, keepdims=True)
                zsum = zsum + jnp.exp(m - top)
                tau = m
                cand = jnp.where(cand >= m, NEG_INF, cand)
            rank2 = jnp.full((N_KEYS, tm), float(PK_TOPK), F32)
            for j in reversed(range(PK_TOPK)):
                rank2 = jnp.where(s2 >= v2[j], float(j), rank2)
            count1 = jnp.zeros((N_KEYS, tm), F32)
            for i in reversed(range(PK_TOPK)):
                cnt = jnp.zeros_like(top)
                for j in range(PK_TOPK // (i + 1)):
                    cnt = cnt + jnp.where(v1[i] + v2[j] >= tau, 1.0, 0.0)
                count1 = jnp.where(s1 >= v1[i], cnt, count1)
            r2_ref[hd] = rank2.astype(BF16)
            c1_ref[hd] = count1
            e2_ref[hd] = jnp.exp(s2 - v2[0]).astype(BF16)
            w1_ref[hd] = jnp.exp(s1 - v1[0]) / zsum
        acc_ref[...] = jnp.zeros_like(acc_ref)
        store_activations(0, hus)

    @pl.when(jnp.logical_and(eb > 0, eb < last))
    def _():
        down = _dot(pvt_ref[...], act_ref[eb % 2])
        store_activations(eb, up_projections(ht_ref[...]))
        acc_ref[...] += down

    @pl.when(eb == last)
    def _():
        acc = acc_ref[...] + _dot(pvt_ref[...], act_ref[eb % 2])
        y = x_ref[0] + mod_ref[0, 0][5:6] * acc.T
        if final_norm:
            ms = jnp.mean(y * y, axis=-1, keepdims=True)
            y = y * lax.rsqrt(ms + NORM_EPS) * gf_ref[...]
        out_ref[0] = y


def _peer(x, g, mod, p, tm, tile_off, ctx_tiles, final_norm, norm_f):
    bsz, rows, d = x.shape
    n_exp = p["pu"].shape[0]
    e1_per_blk = PEER_STEP_E1
    eblk = e1_per_blk * N_KEYS
    nblk = n_exp // eblk
    pqt = p["pq"].T
    pqh = pqt.astype(BF16)
    pql = (pqt - pqh.astype(F32)).astype(BF16)
    nq = pqt.shape[0]
    kern = functools.partial(_peer_kernel, tm=tm, e1_per_blk=e1_per_blk, final_norm=final_norm)
    const = lambda shape: pl.BlockSpec(shape, lambda b, t, e: (0,) * len(shape))
    return pl.pallas_call(
        kern,
        grid=(bsz, rows // tm, nblk + 1),
        in_specs=[pl.BlockSpec((1, tm, d), lambda b, t, e: (b, t, 0)),
                  const((1, d)),
                  pl.BlockSpec((1, 1, N_MOD, d),
                               lambda b, t, e: (b, jnp.where(t + tile_off >= ctx_tiles, 1, 0), 0, 0)),
                  const((nq, d)), const((nq, d)),
                  const((PK_HEADS, N_KEYS, PK_HALF)), const((PK_HEADS, N_KEYS, PK_HALF)),
                  const((PK_HEADS, N_KEYS, PK_HALF)), const((PK_HEADS, N_KEYS, PK_HALF)),
                  pl.BlockSpec((eblk, d), lambda b, t, e: (jnp.minimum(e, nblk - 1), 0)),
                  pl.BlockSpec((d, eblk), lambda b, t, e: (0, jnp.maximum(e - 1, 0))),
                  const((1, d))],
        out_specs=pl.BlockSpec((1, tm, d), lambda b, t, e: (b, t, 0)),
        out_shape=jax.ShapeDtypeStruct((bsz, rows, d), F32),
        scratch_shapes=[pltpu.VMEM((d, tm), BF16),
                        pltpu.VMEM((PK_HEADS, N_KEYS, tm), BF16),
                        pltpu.VMEM((PK_HEADS, N_KEYS, tm), BF16),
                        pltpu.VMEM((PK_HEADS, N_KEYS, tm), F32),
                        pltpu.VMEM((PK_HEADS, N_KEYS, tm), F32),
                        pltpu.VMEM((2 * PK_HEADS, N_KEYS, tm), F32),
                        pltpu.VMEM((SUBLANES * pl.cdiv(len(_PAIR_IDX), SUBLANES), tm), F32),
                        pltpu.VMEM((2, eblk, tm), BF16),
                        pltpu.VMEM((d, tm), F32)],
        compiler_params=_cparams(("parallel", "parallel", "arbitrary")),
        name="peer",
    )(x, g.reshape(1, d), mod, pqh, pql, *_split_bf16(p["sk1"]), *_split_bf16(p["sk2"]),
      p["pu"].astype(BF16), p["pv"].T.astype(BF16), norm_f.reshape(1, d))


def _rope_tables(rows, ctx_len):
    t = jnp.arange(rows - ctx_len, dtype=jnp.int32)
    inv = ROPE_THETA ** (-jnp.arange(0, ROPE_AXIS_DIM, 2, dtype=F32) / ROPE_AXIS_DIM)
    ang_r = (t // GRID_W).astype(F32)[:, None] * inv
    ang_c = (t % GRID_W).astype(F32)[:, None] * inv
    cos = jnp.concatenate([jnp.cos(ang_r)] * 2 + [jnp.cos(ang_c)] * 2, axis=1)
    sin = jnp.concatenate([-jnp.sin(ang_r), jnp.sin(ang_r), -jnp.sin(ang_c), jnp.sin(ang_c)], axis=1)
    cos = jnp.concatenate([jnp.ones((ctx_len, HEAD_DIM), F32), cos], axis=0)
    sin = jnp.concatenate([jnp.zeros((ctx_len, HEAD_DIM), F32), sin], axis=0)
    return jnp.tile(cos, (1, 2)), jnp.tile(sin, (1, 2))


def _head_norm_rope(x, gain, ones_bd, cos, sin):
    ms = _segsum(x * x, ones_bd) * (1.0 / HEAD_DIM)
    y = x * lax.rsqrt(ms + NORM_EPS) * gain
    outs = []
    half = ROPE_AXIS_DIM // 2
    lane = lax.broadcasted_iota(jnp.int32, (1, LANES), 1)
    first_half = (lane % ROPE_AXIS_DIM) < half
    for i in range(x.shape[1] // LANES):
        yc = y[:, i * LANES:(i + 1) * LANES]
        partner = jnp.where(first_half, pltpu.roll(yc, LANES - half, 1), pltpu.roll(yc, half, 1))
        outs.append(yc * cos + partner * sin)
    return jnp.concatenate(outs, axis=1)


def _attn_prep_kernel(z_ref, qn_ref, kn_ref, cos_ref, sin_ref, onesq_ref, onesk_ref,
                      q_o, k_o, v_o):
    z = z_ref[0]
    cos = cos_ref[...]
    sin = sin_ref[...]
    q = _head_norm_rope(z[:, :C_WIDTH], qn_ref[...], onesq_ref[...], cos, sin)
    k = _head_norm_rope(z[:, C_WIDTH:C_WIDTH + KV_WIDTH], kn_ref[...], onesk_ref[...], cos, sin)
    q_o[0] = (q * (ATTN_SCALE * math.log2(math.e))).T.astype(q_o.dtype)
    for j in range(C_KV_HEADS):
        k_o[0, j] = k[:, j * HEAD_DIM:(j + 1) * HEAD_DIM].astype(k_o.dtype)
    v_o[0] = z[:, C_WIDTH + KV_WIDTH:C_WIDTH + 2 * KV_WIDTH].T.astype(v_o.dtype)


def _attn_prep(z, q_norm, k_norm, ctx_len, tm):
    bsz, rows, ncol = z.shape
    cos, sin = _rope_tables(rows, ctx_len)
    qn = jnp.tile(q_norm, C_HEADS).reshape(1, C_WIDTH)
    kn = jnp.tile(k_norm, C_KV_HEADS).reshape(1, KV_WIDTH)
    const = lambda shape: pl.BlockSpec(shape, lambda b, t: (0,) * len(shape))
    return pl.pallas_call(
        _attn_prep_kernel,
        grid=(bsz, rows // tm),
        in_specs=[pl.BlockSpec((1, tm, ncol), lambda b, t: (b, t, 0)),
                  const((1, C_WIDTH)), const((1, KV_WIDTH)),
                  pl.BlockSpec((tm, LANES), lambda b, t: (t, 0)),
                  pl.BlockSpec((tm, LANES), lambda b, t: (t, 0)),
                  const((C_WIDTH, C_WIDTH)), const((KV_WIDTH, KV_WIDTH))],
        out_specs=[pl.BlockSpec((1, C_WIDTH, tm), lambda b, t: (b, 0, t)),
                   pl.BlockSpec((1, C_KV_HEADS, tm, HEAD_DIM), lambda b, t: (b, 0, t, 0)),
                   pl.BlockSpec((1, KV_WIDTH, tm), lambda b, t: (b, 0, t))],
        out_shape=[jax.ShapeDtypeStruct((bsz, C_WIDTH, rows), BF16),
                   jax.ShapeDtypeStruct((bsz, C_KV_HEADS, rows, HEAD_DIM), BF16),
                   jax.ShapeDtypeStruct((bsz, KV_WIDTH, rows), BF16)],
        compiler_params=_cparams(("parallel", "parallel")),
        name="attn_prep",
    )(z, qn, kn, cos, sin, _block_ones(C_WIDTH, HEAD_DIM), _block_ones(KV_WIDTH, HEAD_DIM))


def _attn_kernel(qt_ref, k_ref, vt_ref, o_ref, m_ref, l_ref, acc_ref, *, kc):
    kt = pl.program_id(3)

    @pl.when(kt == 0)
    def _():
        m_ref[...] = jnp.full_like(m_ref, NEG_INF)
        l_ref[...] = jnp.zeros_like(l_ref)
        acc_ref[...] = jnp.zeros_like(acc_ref)

    grp = range(C_GROUP)
    qts = [qt_ref[0, g * HEAD_DIM:(g + 1) * HEAD_DIM, :] for g in grp]
    nck = k_ref.shape[2] // kc

    def scores(c):
        kb = k_ref[0, 0, c * kc:(c + 1) * kc, :]
        return [_dot(kb, qts[g]) for g in grp]

    m = [m_ref[g] for g in grp]
    l = [l_ref[g] for g in grp]
    acc = [acc_ref[g] for g in grp]
    s_next = scores(0)
    for c in range(nck):
        s = s_next
        if c + 1 < nck:
            s_next = scores(c + 1)
        vt = vt_ref[0, :, c * kc:(c + 1) * kc]
        m_new = [jnp.maximum(m[g], jnp.max(s[g], axis=0, keepdims=True)) for g in grp]
        p = [jnp.exp2(s[g] - m_new[g]) for g in grp]
        pv = [_dot(vt, p[g].astype(BF16)) for g in grp]
        for g in grp:
            alpha = jnp.exp2(m[g] - m_new[g])
            l[g] = alpha * l[g] + jnp.sum(p[g], axis=0, keepdims=True)
            acc[g] = alpha * acc[g] + pv[g]
        m = m_new
    for g in grp:
        m_ref[g] = m[g]
        l_ref[g] = l[g]
        acc_ref[g] = acc[g]

    @pl.when(kt == pl.num_programs(3) - 1)
    def _():
        o_ref[0] = jnp.concatenate(
            [acc_ref[g] / l_ref[g] for g in grp], axis=0).astype(o_ref.dtype)


ATTN_MAX_KEY_TILE = 2816
ATTN_MAX_KEY_CHUNK = 256


def _key_tile(rows, limit=ATTN_MAX_KEY_TILE):
    return max(t for t in range(LANES, min(rows, limit) + 1, LANES) if rows % t == 0)


def _attention(qt, k, vt, ctx_len, tq, tk):
    bsz, _, rows = qt.shape
    seq = rows - ctx_len
    qoff = ctx_len // tq
    gw = C_GROUP * HEAD_DIM
    kern = functools.partial(_attn_kernel, kc=_key_tile(tk, ATTN_MAX_KEY_CHUNK))
    return pl.pallas_call(
        kern,
        grid=(bsz, C_KV_HEADS, seq // tq, rows // tk),
        in_specs=[pl.BlockSpec((1, gw, tq), lambda b, j, i, kk: (b, j, i + qoff)),
                  pl.BlockSpec((1, 1, tk, HEAD_DIM), lambda b, j, i, kk: (b, j, kk, 0)),
                  pl.BlockSpec((1, HEAD_DIM, tk), lambda b, j, i, kk: (b, j, kk))],
        out_specs=pl.BlockSpec((1, gw, tq), lambda b, j, i, kk: (b, j, i)),
        out_shape=jax.ShapeDtypeStruct((bsz, C_WIDTH, seq), BF16),
        scratch_shapes=[pltpu.VMEM((C_GROUP, 1, tq), F32),
                        pltpu.VMEM((C_GROUP, 1, tq), F32),
                        pltpu.VMEM((C_GROUP, HEAD_DIM, tq), F32)],
        compiler_params=_cparams(("parallel", "parallel", "parallel", "arbitrary")),
        name="attention",
    )(qt, k, vt)


def _conv_kernel(zc_ref, zp_ref, zn_ref, w_ref, b_ref, lg_ref, lb_ref, o_ref, ybuf, *, tm, halo):
    t = pl.program_id(1)
    nt = pl.num_programs(1)

    def glu(u):
        return u[:, :D_WIDTH] * jax.nn.sigmoid(u[:, D_WIDTH:])

    ybuf[0:halo, :] = jnp.where(t == 0, 0.0, glu(zp_ref[0]))
    ybuf[halo:halo + tm, :] = glu(zc_ref[0])
    ybuf[halo + tm:, :] = jnp.where(t == nt - 1, 0.0, glu(zn_ref[0]))
    acc = jnp.zeros((tm, D_WIDTH), F32)
    for j in range(D_CONV_WIDTH):
        off = halo - D_PAD + j
        acc = acc + w_ref[j:j + 1, :] * ybuf[off:off + tm, :]
    y = acc + b_ref[...]
    mu = jnp.mean(y, axis=-1, keepdims=True)
    yc = y - mu
    var = jnp.mean(yc * yc, axis=-1, keepdims=True)
    yn = yc * lax.rsqrt(var + LN_EPS) * lg_ref[...] + lb_ref[...]
    o_ref[0] = (yn * jax.nn.sigmoid(yn)).astype(o_ref.dtype)


def _conformer_conv(z, dw_w, dw_b, cn_g, cn_b, ctx_len, tm):
    bsz, rows, ncol = z.shape
    seq = rows - ctx_len
    halo = 2 * SUBLANES
    hb = tm // halo
    off = ctx_len // tm
    offh = ctx_len // halo
    nh = seq // halo
    kern = functools.partial(_conv_kernel, tm=tm, halo=halo)
    row = lambda w: pl.BlockSpec((1, w), lambda b, t: (0, 0))
    return pl.pallas_call(
        kern,
        grid=(bsz, seq // tm),
        in_specs=[pl.BlockSpec((1, tm, ncol), lambda b, t: (b, t + off, 0)),
                  pl.BlockSpec((1, halo, ncol),
                               lambda b, t: (b, offh + jnp.maximum(t * hb - 1, 0), 0)),
                  pl.BlockSpec((1, halo, ncol),
                               lambda b, t: (b, offh + jnp.minimum((t + 1) * hb, nh - 1), 0)),
                  pl.BlockSpec((D_CONV_WIDTH, D_WIDTH), lambda b, t: (0, 0)),
                  row(D_WIDTH), row(D_WIDTH), row(D_WIDTH)],
        out_specs=pl.BlockSpec((1, tm, D_WIDTH), lambda b, t: (b, t, 0)),
        out_shape=jax.ShapeDtypeStruct((bsz, seq, D_WIDTH), BF16),
        scratch_shapes=[pltpu.VMEM((tm + 2 * halo, D_WIDTH), F32)],
        compiler_params=_cparams(("parallel", "parallel")),
        name="conformer_conv",
    )(z, z, z, dw_w, dw_b.reshape(1, -1), cn_g.reshape(1, -1), cn_b.reshape(1, -1))


def _forward(x, c, ctx, c_ctx, l0, l1, norm_f):
    bsz, seq, d = x.shape
    ctx_len = ctx.shape[1]
    tm = min(256, ctx_len)
    ctx_tiles = ctx_len // tm
    xs = jnp.concatenate([ctx, x], axis=1)

    mod = _modulation(c, c_ctx, l0["mod_w"], l0["mod_b"])
    z, zf = _normmod_matmul(xs, l0["norm1"], mod, l0["w_in"], tm, ctx_tiles, A_COLS)
    r, v, g, bonus, kk, lwf, kdf, bf, lwr, kdr, br = _rwkv_prep(z, l0, tm, ctx_tiles)
    yf, yr = _rwkv_scan(r, v, kk, lwf, kdf, bf, lwr, kdr, br, ctx_len)
    o_rwkv = _rwkv_finish(yf, yr, bonus, g, l0["lnx_g"], l0["lnx_b"], tm)
    fcs = _fnet_chan(zf, tm)
    ts_c = min(256, ctx_len)
    ts_l = min(512, seq)
    f_ctx = _fnet_seq(fcs, 0, ctx_len, ts_c, ts_c)
    f_lat = _fnet_seq(fcs[:, ctx_len:], 0, seq, ts_l, ts_l)
    o_fnet = jnp.concatenate([f_ctx, f_lat], axis=1)
    xs = _proj_residual(o_rwkv, o_fnet, xs, mod, l0["w_out"], tm, 0, ctx_tiles)
    xs = _peer(xs, l0["norm2"], mod, l0, tm, 0, ctx_tiles, False, norm_f)

    mod = _modulation(c, c_ctx, l1["mod_w"], l1["mod_b"])
    z, zu = _normmod_matmul(xs, l1["norm1"], mod, l1["w_in"], tm, ctx_tiles,
                            C_WIDTH + 2 * KV_WIDTH)
    qt, k, vt = _attn_prep(z, l1["q_norm"], l1["k_norm"], ctx_len, tm)
    o_attn = _attention(qt, k, vt, ctx_len, tm, _key_tile(ctx_len + seq))
    o_conv = _conformer_conv(zu, l1["dw_w"], l1["dw_b"], l1["cn_g"], l1["cn_b"], ctx_len, tm)
    xl = _proj_residual(o_attn, o_conv, xs, mod, l1["w_out"], tm, ctx_tiles, ctx_tiles,
                        o1_transposed=True)
    return _peer(xl, l1["norm2"], mod, l1, tm, ctx_tiles, ctx_tiles, True, norm_f)


def kernel(x, c, ctx, c_ctx, l0_mod_w, l0_mod_b, l0_norm1, l0_w_in, l0_shift_prev, l0_shift_next, l0_w0, l0_w2, l0_a0, l0_a2, l0_g2, l0_k_k, l0_k_a, l0_r_k, l0_lnx_g, l0_lnx_b, l0_w_out, l0_norm2, l0_pq, l0_sk1, l0_sk2, l0_pu, l0_pv, l1_mod_w, l1_mod_b, l1_norm1, l1_w_in, l1_q_norm, l1_k_norm, l1_dw_w, l1_dw_b, l1_cn_g, l1_cn_b, l1_w_out, l1_norm2, l1_pq, l1_sk1, l1_sk2, l1_pu, l1_pv, norm_f):
    l0 = dict(mod_w=l0_mod_w, mod_b=l0_mod_b, norm1=l0_norm1, w_in=l0_w_in,
              shift_prev=l0_shift_prev, shift_next=l0_shift_next, w0=l0_w0, w2=l0_w2,
              a0=l0_a0, a2=l0_a2, g2=l0_g2, k_k=l0_k_k, k_a=l0_k_a, r_k=l0_r_k,
              lnx_g=l0_lnx_g, lnx_b=l0_lnx_b, w_out=l0_w_out, norm2=l0_norm2,
              pq=l0_pq, sk1=l0_sk1, sk2=l0_sk2, pu=l0_pu, pv=l0_pv)
    l1 = dict(mod_w=l1_mod_w, mod_b=l1_mod_b, norm1=l1_norm1, w_in=l1_w_in,
              q_norm=l1_q_norm, k_norm=l1_k_norm, dw_w=l1_dw_w, dw_b=l1_dw_b,
              cn_g=l1_cn_g, cn_b=l1_cn_b, w_out=l1_w_out, norm2=l1_norm2,
              pq=l1_pq, sk1=l1_sk1, sk2=l1_sk2, pu=l1_pu, pv=l1_pv)
    return _forward(x, c, ctx, c_ctx, l0, l1, norm_f)
```

```python
import functools
import math

import jax
import jax.numpy as jnp
import numpy as np
from jax import lax
from jax.experimental import pallas as pl
from jax.experimental.pallas import tpu as pltpu

F32 = jnp.float32
BF16 = jnp.bfloat16
HIGHEST = lax.Precision.HIGHEST

LANES = 128
SUBLANES = 8
VMEM_LIMIT_BYTES = 56 * 1024 * 1024

N_MOD = 6
NORM_EPS = 1e-6
LN_EPS = 1e-5
GRID_W = 64
HEAD_DIM = 64
A_WIDTH = 768
A_HEADS = A_WIDTH // HEAD_DIM
A_RANK_W = 64
A_RANK_A = 64
A_RANK_G = 128
A_LNX_EPS = 64e-5
A_COLS = 3 * A_WIDTH + 2 * A_RANK_W + 2 * A_RANK_A + A_RANK_G
B_WIDTH = 256
B_GROUP_DIM = 64
C_WIDTH = 768
C_HEADS = 12
C_KV_HEADS = 4
C_GROUP = C_HEADS // C_KV_HEADS
KV_WIDTH = C_KV_HEADS * HEAD_DIM
ROPE_AXIS_DIM = HEAD_DIM // 2
ROPE_THETA = 10000.0
ATTN_SCALE = HEAD_DIM ** -0.5
D_WIDTH = 256
D_CONV_WIDTH = 31
D_PAD = D_CONV_WIDTH // 2
PK_HEADS = 8
PK_DIM = 256
PK_HALF = 128
N_KEYS = 128
PK_TOPK = 16

SCAN_CHUNK = 64
NEG_INF = float("-inf")


def _cparams(semantics):
    return pltpu.CompilerParams(dimension_semantics=semantics,
                                vmem_limit_bytes=VMEM_LIMIT_BYTES)


def _split_bf16(x):
    hi = x.astype(BF16)
    lo = (x - hi.astype(F32)).astype(BF16)
    return hi, lo


def _dot(a, b):
    return jnp.dot(a, b, preferred_element_type=F32)


def _dot_hp(a, b):
    return _mm3(_split_bf16(a), _split_bf16(b))


def _dot_nt(a, b, precision=None):
    return lax.dot_general(a, b, (((1,), (1,)), ((), ())),
                           preferred_element_type=F32, precision=precision)


def _dot_tn(a, b, precision=None):
    return lax.dot_general(a, b, (((0,), (0,)), ((), ())),
                           preferred_element_type=F32, precision=precision)


_NN = (((1,), (0,)), ((), ()))
_NT = (((1,), (1,)), ((), ()))
_TN = (((0,), (0,)), ((), ()))


def _mm3(a, b, dims=_NN):
    (ah, al), (bh, bl) = a, b
    dg = lambda x, y: lax.dot_general(x, y, dims, preferred_element_type=F32)
    return dg(ah, bh) + (dg(ah, bl) + dg(al, bh))


def _segsum(x, ones_bd):
    hi, lo = _split_bf16(x)
    return _dot(hi, ones_bd) + _dot(lo, ones_bd)


def _block_ones(width, seg):
    r = np.arange(width) // seg
    return jnp.asarray((r[:, None] == r[None, :]).astype(np.float32), dtype=BF16)


def _mod_kernel(c_ref, w_ref, b_ref, o_ref):
    c = c_ref[...]
    s = c * jax.nn.sigmoid(c)
    o_ref[...] = _dot_hp(s, w_ref[...]) + b_ref[...]


def _modulation(c, c_ctx, mod_w, mod_b):
    bsz, d = c.shape
    rows = SUBLANES * pl.cdiv(bsz + 1, SUBLANES)
    cc = jnp.zeros((rows, d), F32).at[:bsz].set(c).at[bsz].set(c_ctx)
    n = mod_w.shape[1]
    tn = n // 4
    out = pl.pallas_call(
        _mod_kernel,
        grid=(n // tn,),
        in_specs=[pl.BlockSpec((rows, d), lambda j: (0, 0)),
                  pl.BlockSpec((d, tn), lambda j: (0, j)),
                  pl.BlockSpec((1, tn), lambda j: (0, j))],
        out_specs=pl.BlockSpec((rows, tn), lambda j: (0, j)),
        out_shape=jax.ShapeDtypeStruct((rows, n), F32),
        compiler_params=_cparams(("arbitrary",)),
        name="adaln_mod",
    )(cc, mod_w, mod_b.reshape(1, n))
    lat = out[:bsz].reshape(bsz, N_MOD, d)
    ctx = jnp.broadcast_to(out[bsz].reshape(1, N_MOD, d), (bsz, N_MOD, d))
    return jnp.stack([ctx, lat], axis=1)


def _norm_mod(x, g, mod, row):
    ms = jnp.mean(x * x, axis=-1, keepdims=True)
    y = x * lax.rsqrt(ms + NORM_EPS) * g
    return y * (1.0 + mod[row + 1:row + 2]) + mod[row:row + 1]


def _normmod_matmul_kernel(x_ref, g_ref, mod_ref, w_ref, o1_ref, o2_ref):
    h = _norm_mod(x_ref[0], g_ref[...], mod_ref[0, 0], 0)
    z = _dot(h.astype(BF16), w_ref[...])
    n1 = o1_ref.shape[2]
    o1_ref[0] = z[:, :n1]
    o2_ref[0] = z[:, n1:]


def _normmod_matmul(x, g, mod, w, tm, ctx_tiles, n1):
    bsz, rows, d = x.shape
    n = w.shape[1]
    widths = (n1, n - n1)
    return pl.pallas_call(
        _normmod_matmul_kernel,
        grid=(bsz, rows // tm),
        in_specs=[pl.BlockSpec((1, tm, d), lambda b, t: (b, t, 0)),
                  pl.BlockSpec((1, d), lambda b, t: (0, 0)),
                  pl.BlockSpec((1, 1, N_MOD, d),
                               lambda b, t: (b, jnp.where(t >= ctx_tiles, 1, 0), 0, 0)),
                  pl.BlockSpec((d, n), lambda b, t: (0, 0))],
        out_specs=[pl.BlockSpec((1, tm, wd), lambda b, t: (b, t, 0)) for wd in widths],
        out_shape=[jax.ShapeDtypeStruct((bsz, rows, wd), F32) for wd in widths],
        compiler_params=_cparams(("parallel", "parallel")),
        name="normmod_proj",
    )(x, g.reshape(1, d), mod, w.astype(BF16))


def _rwkv_prep_kernel(z_ref, zp_ref, zn_ref, mup_ref, mun_ref, w0_ref, w2_ref, a0_ref, a2_ref,
                      g2_ref, kk_ref, ka_ref, rk_ref, ones_ref,
                      r_o, v_o, g_o, bonus_o, kkn_o, lwf_o, kdf_o, bf_o, lwr_o, kdr_o, br_o,
                      *, tm, ctx_tiles):
    t = pl.program_id(1)
    nt = pl.num_programs(1)
    z = z_ref[0][:, :A_COLS]
    zp_row = zp_ref[0][SUBLANES - 1:SUBLANES, :A_COLS]
    zn_row = zn_ref[0][0:1, :A_COLS]
    first = jnp.logical_or(t == 0, t == ctx_tiles)
    last = jnp.logical_or(t == ctx_tiles - 1, t == nt - 1)
    zp_row = jnp.where(first, 0.0, zp_row)
    zn_row = jnp.where(last, 0.0, zn_row)
    ridx = lax.broadcasted_iota(jnp.int32, (tm, 1), 0)
    z_prev = jnp.where(ridx == 0, zp_row, pltpu.roll(z, 1, 0))
    z_next = jnp.where(ridx == tm - 1, zn_row, pltpu.roll(z, tm - 1, 0))
    zs = z + mup_ref[...] * (z_prev - z) + mun_ref[...] * (z_next - z)

    w = A_WIDTH
    r = zs[:, 0:w]
    k = zs[:, w:2 * w]
    v = zs[:, 2 * w:3 * w]
    o = 3 * w
    xw = zs[:, o:o + 2 * A_RANK_W]
    o += 2 * A_RANK_W
    xa = zs[:, o:o + 2 * A_RANK_A]
    o += 2 * A_RANK_A
    xg = zs[:, o:o + A_RANK_G]
    ones_bd = ones_ref[...]

    g_o[0] = _dot_hp(jax.nn.sigmoid(xg), g2_ref[...])
    kk = k * kk_ref[...]
    nrm = jnp.sqrt(_segsum(kk * kk, ones_bd))
    kk = kk / jnp.maximum(nrm, 1e-12)
    r_o[0] = r
    v_o[0] = v
    kkn_o[0] = kk
    bonus = jnp.zeros_like(r)
    tw = jnp.tanh(xw)
    for d, (lw_o, kd_o, b_o) in enumerate(((lwf_o, kdf_o, bf_o), (lwr_o, kdr_o, br_o))):
        wl = w0_ref[d:d + 1, :] + _dot_hp(tw[:, d * A_RANK_W:(d + 1) * A_RANK_W], w2_ref[d])
        w_log = -jax.nn.softplus(-wl) - 0.5
        lw_o[0] = -jnp.exp(w_log)
        a_gate = jax.nn.sigmoid(
            a0_ref[d:d + 1, :] + _dot_hp(xa[:, d * A_RANK_A:(d + 1) * A_RANK_A], a2_ref[d]))
        k_d = k * (1.0 + (a_gate - 1.0) * ka_ref[...])
        kd_o[0] = k_d
        b_o[0] = kk * a_gate
        bonus = bonus + _segsum(r * k_d * rk_ref[...], ones_bd) * v
    bonus_o[0] = bonus


def _rwkv_prep(z, p, tm, ctx_tiles):
    bsz, rows, ncol = z.shape
    w = A_WIDTH
    hb = tm // SUBLANES
    nblk8 = rows // SUBLANES
    row_spec = lambda width: pl.BlockSpec((1, width), lambda b, t: (0, 0))
    full = lambda shape: pl.BlockSpec(shape, lambda b, t: (0,) * len(shape))
    out_spec = pl.BlockSpec((1, tm, w), lambda b, t: (b, t, 0))
    out_shape = jax.ShapeDtypeStruct((bsz, rows, w), F32)
    kern = functools.partial(_rwkv_prep_kernel, tm=tm, ctx_tiles=ctx_tiles)
    return pl.pallas_call(
        kern,
        grid=(bsz, rows // tm),
        in_specs=[pl.BlockSpec((1, tm, ncol), lambda b, t: (b, t, 0)),
                  pl.BlockSpec((1, SUBLANES, ncol),
                               lambda b, t: (b, jnp.maximum(t * hb - 1, 0), 0)),
                  pl.BlockSpec((1, SUBLANES, ncol),
                               lambda b, t: (b, jnp.minimum((t + 1) * hb, nblk8 - 1), 0)),
                  row_spec(A_COLS), row_spec(A_COLS),
                  full((2, w)), full((2, A_RANK_W, w)), full((2, w)), full((2, A_RANK_A, w)),
                  full((A_RANK_G, w)), row_spec(w), row_spec(w), row_spec(w), full((w, w))],
        out_specs=[out_spec] * 11,
        out_shape=[out_shape] * 11,
        compiler_params=_cparams(("parallel", "parallel")),
        name="rwkv_prep",
    )(z, z, z, p["shift_prev"].reshape(1, -1), p["shift_next"].reshape(1, -1),
      p["w0"], p["w2"], p["a0"], p["a2"], p["g2"], p["k_k"].reshape(1, w),
      p["k_a"].reshape(1, w), p["r_k"].reshape(1, w), _block_ones(w, HEAD_DIM))


SCAN_HEADS = 4


def _scan_kernel(rf, vf, kkf, lwf, kdf, bf, rr, vr, kkr, lwr, kdr, br, yf_o, yr_o, s_ref, *, c):
    @pl.when(pl.program_id(2) == 0)
    def _():
        s_ref[...] = jnp.zeros_like(s_ref)

    ri = lax.broadcasted_iota(jnp.int32, (c, c), 0)
    ci = lax.broadcasted_iota(jnp.int32, (c, c), 1)
    ri2 = lax.broadcasted_iota(jnp.int32, (c, 2 * c), 0)
    ci2 = lax.broadcasted_iota(jnp.int32, (c, 2 * c), 1)
    ci2 = jnp.where(ci2 >= c, ci2 - c, ci2)
    dirs = ((rf, vf, kkf, lwf, kdf, bf, ci <= ri, ci < ri, ci2 <= ri2, c - 1),
            (rr, vr, kkr, lwr, kdr, br, ci >= ri, ci > ri, ci2 >= ri2, 0))
    lhs, rhs, vs, ss, pend, incl, strict = [], [], [], [], [], [], []
    for d, (r_, v_, kk_, lw_, kd_, b_, m_incl, m_strict, m_incl2, edge) in enumerate(dirs):
        lw = lw_[0]
        tri = m_incl.astype(F32).astype(BF16)
        lw_h = lw.astype(BF16)
        lw_m, lw_l = _split_bf16(lw - lw_h.astype(F32))
        lcum = _dot(tri, lw_h) + (_dot(tri, lw_m) + _dot(tri, lw_l))
        p_in = jnp.exp(lcum)
        p_inv = jnp.exp(-lcum)
        lhs_all = jnp.concatenate([-kk_[0] * jnp.exp(lcum - lw), r_[0] * p_in], axis=0)
        rhs_all = jnp.concatenate([b_[0] * p_inv, kd_[0] * p_inv], axis=0)
        v_all = v_[0]
        for h in range(SCAN_HEADS):
            sl = slice(h * HEAD_DIM, (h + 1) * HEAD_DIM)
            lhs.append(lhs_all[:, sl])
            rhs.append(rhs_all[:, sl])
            vs.append(v_all[:, sl])
            ss.append(s_ref[d, h])
            pend.append(p_in[edge:edge + 1, sl])
            incl.append(m_incl2)
            strict.append(m_strict)
    n = len(lhs)
    idx = range(n)
    lhs2 = [_split_bf16(lhs[i]) for i in idx]
    rhs2 = [_split_bf16(rhs[i]) for i in idx]
    v2 = [_split_bf16(vs[i]) for i in idx]
    s2 = [_split_bf16(ss[i]) for i in idx]
    g = [_mm3(lhs2[i], rhs2[i], _NT) for i in idx]
    xs = [_mm3(lhs2[i], s2[i], _NT) for i in idx]
    a_ak = [_split_bf16(jnp.where(strict[i], g[i][:c, c:], 0.0)) for i in idx]
    x = [xs[i][:c] + _mm3(a_ak[i], v2[i]) for i in idx]
    apow = [_split_bf16(jnp.where(strict[i], g[i][:c, :c], 0.0)) for i in idx]
    steps = int(math.log2(c))
    for k in range(steps):
        x2 = [_split_bf16(x[i]) for i in idx]
        x = [x[i] + _mm3(apow[i], x2[i]) for i in idx]
        if k + 1 < steps:
            apow = [_split_bf16(_mm3(apow[i], apow[i])) for i in idx]
    x2 = [_split_bf16(x[i]) for i in idx]
    sav = [tuple(jnp.concatenate([x2[i][t], v2[i][t]], axis=0) for t in range(2)) for i in idx]
    a_r = [_split_bf16(jnp.where(incl[i], g[i][c:], 0.0)) for i in idx]
    y = [xs[i][c:] + _mm3(a_r[i], sav[i]) for i in idx]
    s_new = [(ss[i] + _mm3(sav[i], rhs2[i], _TN)) * pend[i] for i in idx]
    for d, y_o in enumerate((yf_o, yr_o)):
        for h in range(SCAN_HEADS):
            s_ref[d, h] = s_new[d * SCAN_HEADS + h]
        y_o[0] = jnp.concatenate(y[d * SCAN_HEADS:(d + 1) * SCAN_HEADS], axis=1)


def _rwkv_scan(r, v, kk, lwf, kdf, bf, lwr, kdr, br, ctx_len):
    bsz, rows, w = r.shape
    c = SCAN_CHUNK
    nch = rows // c
    cch = ctx_len // c
    lanes = SCAN_HEADS * HEAD_DIM
    fwd = lambda b, hp, i: (b, i, hp)
    rev = lambda b, hp, i: (b, jnp.where(i < cch, cch - 1 - i, nch - 1 + cch - i), hp)
    blk = lambda im: pl.BlockSpec((1, c, lanes), im)
    kern = functools.partial(_scan_kernel, c=c)
    out_shape = jax.ShapeDtypeStruct((bsz, rows, w), F32)
    return pl.pallas_call(
        kern,
        grid=(bsz, w // lanes, nch),
        in_specs=[blk(fwd)] * 6 + [blk(rev)] * 6,
        out_specs=[blk(fwd), blk(rev)],
        out_shape=[out_shape, out_shape],
        scratch_shapes=[pltpu.VMEM((2, SCAN_HEADS, HEAD_DIM, HEAD_DIM), F32)],
        compiler_params=_cparams(("parallel", "parallel", "arbitrary")),
        name="rwkv_scan",
    )(r, v, kk, lwf, kdf, bf, r, v, kk, lwr, kdr, br)


def _rwkv_finish_kernel(yf, yr, bonus, g, lg, lb, ones_ref, o_ref):
    y = yf[0] + yr[0]
    ones_bd = ones_ref[...]
    inv = 1.0 / HEAD_DIM
    mu = _segsum(y, ones_bd) * inv
    yc = y - mu
    var = _segsum(yc * yc, ones_bd) * inv
    yn = yc * lax.rsqrt(var + A_LNX_EPS) * lg[...] + lb[...]
    o_ref[0] = ((yn + bonus[0]) * g[0]).astype(o_ref.dtype)


def _rwkv_finish(yf, yr, bonus, g, lnx_g, lnx_b, tm):
    bsz, rows, w = yf.shape
    blk = pl.BlockSpec((1, tm, w), lambda b, t: (b, t, 0))
    row = pl.BlockSpec((1, w), lambda b, t: (0, 0))
    return pl.pallas_call(
        _rwkv_finish_kernel,
        grid=(bsz, rows // tm),
        in_specs=[blk, blk, blk, blk, row, row, pl.BlockSpec((w, w), lambda b, t: (0, 0))],
        out_specs=blk,
        out_shape=jax.ShapeDtypeStruct((bsz, rows, w), BF16),
        compiler_params=_cparams(("parallel", "parallel")),
        name="rwkv_finish",
    )(yf, yr, bonus, g, lnx_g.reshape(1, w), lnx_b.reshape(1, w), _block_ones(w, HEAD_DIM))


def _fnet_chan_kernel(z_ref, m_ref, o_ref):
    o_ref[0] = _dot(z_ref[0].astype(BF16), m_ref[...]).astype(o_ref.dtype)


def _fnet_chan(z, tm):
    bsz, rows, ncol = z.shape
    j = np.arange(B_WIDTH)
    same = (j[:, None] // B_GROUP_DIM) == (j[None, :] // B_GROUP_DIM)
    ang = 2.0 * np.pi * ((j[:, None] % B_GROUP_DIM) * (j[None, :] % B_GROUP_DIM) % B_GROUP_DIM) / B_GROUP_DIM
    m = np.concatenate([np.where(same, np.cos(ang), 0.0), np.where(same, np.sin(ang), 0.0)], axis=1)
    return pl.pallas_call(
        _fnet_chan_kernel,
        grid=(bsz, rows // tm),
        in_specs=[pl.BlockSpec((1, tm, ncol), lambda b, t: (b, t, 0)),
                  pl.BlockSpec((B_WIDTH, 2 * B_WIDTH), lambda b, t: (0, 0))],
        out_specs=pl.BlockSpec((1, tm, 2 * B_WIDTH), lambda b, t: (b, t, 0)),
        out_shape=jax.ShapeDtypeStruct((bsz, rows, 2 * B_WIDTH), BF16),
        compiler_params=_cparams(("parallel", "parallel")),
        name="fnet_chan",
    )(z, jnp.asarray(m, dtype=BF16))


def _fnet_seq_kernel(fcs_ref, cb_ref, sb_ref, c0_ref, s0_ref, o_ref, acc_ref, *, scale, nb):
    k = pl.program_id(1)

    @pl.when(k == 0)
    def _():
        acc_ref[...] = jnp.zeros_like(acc_ref)

    cb = cb_ref[...]
    sb = sb_ref[...]
    c0 = c0_ref[0]
    s0 = s0_ref[0]
    cm = (c0 * cb - s0 * sb).astype(BF16)
    sm = (s0 * cb + c0 * sb).astype(BF16)
    for b in range(nb):
        fcs = fcs_ref[b]
        acc_ref[b] += _dot(cm, fcs[:, :B_WIDTH]) - _dot(sm, fcs[:, B_WIDTH:])

    @pl.when(k == pl.num_programs(1) - 1)
    def _():
        o_ref[...] = (acc_ref[...] * scale).astype(o_ref.dtype)


def _fnet_seq(fcs, row0, length, ts, tk):
    bsz = fcs.shape[0]
    ds = jnp.arange(ts, dtype=jnp.int32)[:, None]
    tt = jnp.arange(length, dtype=jnp.int32)[None, :]
    ang = ((ds * tt) % length).astype(F32) * (2.0 * math.pi / length)
    cb, sb = jnp.cos(ang), jnp.sin(ang)
    s0 = (jnp.arange(length // ts, dtype=jnp.int32) * ts)[:, None]
    ang0 = ((s0 * tt) % length).astype(F32) * (2.0 * math.pi / length)
    c0, sn0 = jnp.cos(ang0)[:, None, :], jnp.sin(ang0)[:, None, :]
    kern = functools.partial(_fnet_seq_kernel, scale=1.0 / math.sqrt(length * B_GROUP_DIM), nb=bsz)
    koff = row0 // tk
    return pl.pallas_call(
        kern,
        grid=(length // ts, length // tk),
        in_specs=[pl.BlockSpec((bsz, tk, 2 * B_WIDTH), lambda s, k: (0, k + koff, 0)),
                  pl.BlockSpec((ts, tk), lambda s, k: (0, k)),
                  pl.BlockSpec((ts, tk), lambda s, k: (0, k)),
                  pl.BlockSpec((1, 1, tk), lambda s, k: (s, 0, k)),
                  pl.BlockSpec((1, 1, tk), lambda s, k: (s, 0, k))],
        out_specs=pl.BlockSpec((bsz, ts, B_WIDTH), lambda s, k: (0, s, 0)),
        out_shape=jax.ShapeDtypeStruct((bsz, length, B_WIDTH), BF16),
        scratch_shapes=[pltpu.VMEM((bsz, ts, B_WIDTH), F32)],
        compiler_params=_cparams(("parallel", "arbitrary")),
        name="fnet_seq",
    )(fcs, cb, sb, c0, sn0)


def _proj_residual_kernel(o1_ref, o2_ref, x_ref, mod_ref, w1_ref, w2_ref, out_ref, *, o1_transposed):
    o1 = o1_ref[0]
    if o1_transposed:
        o1 = o1.astype(F32).T.astype(BF16)
    mix = _dot(o1, w1_ref[...]) + _dot(o2_ref[0], w2_ref[...])
    gate = mod_ref[0, 0][2:3]
    out_ref[0] = x_ref[0] + gate * mix


def _proj_residual(o1, o2, x, mod, w_out, tm, tile_off, ctx_tiles, o1_transposed=False):
    bsz, rows, w2 = o2.shape
    w1 = o1.shape[1] if o1_transposed else o1.shape[2]
    d = x.shape[2]
    o1_spec = (pl.BlockSpec((1, w1, tm), lambda b, t: (b, 0, t)) if o1_transposed
               else pl.BlockSpec((1, tm, w1), lambda b, t: (b, t, 0)))
    kern = functools.partial(_proj_residual_kernel, o1_transposed=o1_transposed)
    return pl.pallas_call(
        kern,
        grid=(bsz, rows // tm),
        in_specs=[o1_spec,
                  pl.BlockSpec((1, tm, w2), lambda b, t: (b, t, 0)),
                  pl.BlockSpec((1, tm, d), lambda b, t: (b, t + tile_off, 0)),
                  pl.BlockSpec((1, 1, N_MOD, d),
                               lambda b, t: (b, jnp.where(t + tile_off >= ctx_tiles, 1, 0), 0, 0)),
                  pl.BlockSpec((w1, d), lambda b, t: (0, 0)),
                  pl.BlockSpec((w2, d), lambda b, t: (0, 0))],
        out_specs=pl.BlockSpec((1, tm, d), lambda b, t: (b, t, 0)),
        out_shape=jax.ShapeDtypeStruct((bsz, rows, d), F32),
        compiler_params=_cparams(("parallel", "parallel")),
        name="proj_residual",
    )(o1, o2, x, mod, w_out[:w1].astype(BF16), w_out[w1:].astype(BF16))


def _merge_exchange_network(n):
    pairs = []
    p = 1
    while p < n:
        k = p
        while k >= 1:
            for j in range(k % p, n - k, 2 * k):
                for i in range(min(k, n - j - k)):
                    if (i + j) // (2 * p) == (i + j + k) // (2 * p):
                        pairs.append((i + j, i + j + k))
            k //= 2
        p *= 2
    return pairs


def _top16_rows(s):
    nslab = s.shape[0] // SUBLANES
    slabs = [s[k * SUBLANES:(k + 1) * SUBLANES] for k in range(nslab)]
    for a, b in _merge_exchange_network(nslab):
        slabs[a], slabs[b] = jnp.maximum(slabs[a], slabs[b]), jnp.minimum(slabs[a], slabs[b])
    rows = []
    for t in range(PK_TOPK):
        m = jnp.max(slabs[0], axis=0, keepdims=True)
        rows.append(m)
        hit = slabs[0] >= m
        for k in range(nslab - 1 - t):
            slabs[k] = jnp.where(hit, slabs[k + 1], slabs[k])
    return rows


_PAIR_IDX = [(i, j) for i in range(PK_TOPK) for j in range(PK_TOPK) if (i + 1) * (j + 1) <= PK_TOPK]
BF16_SUBLANES = 2 * SUBLANES
PEER_SUB_E1 = 1
PEER_STEP_E1 = 8
PEER_BATCH_ROWS = 2


def _rows_bf16(row):
    packed = jnp.broadcast_to(row, (BF16_SUBLANES, row.shape[1])).astype(BF16)
    return jnp.tile(packed, (N_KEYS // BF16_SUBLANES, 1))


def _peer_kernel(x_ref, g_ref, mod_ref, pqh_ref, pql_ref, sk1h_ref, sk1l_ref, sk2h_ref, sk2l_ref,
                 pu_ref, pvt_ref, gf_ref,
                 out_ref, ht_ref, r2_ref, e2_ref, c1_ref, w1_ref, s_ref, cand_ref, act_ref, acc_ref,
                 *, nb, tm, e1_per_blk, final_norm):
    eb = pl.program_id(2)
    last = pl.num_programs(2) - 1
    lanes = nb * tm
    sub = PEER_SUB_E1 * N_KEYS
    nsub = e1_per_blk // PEER_SUB_E1

    def activation(blk, i, hu):
        acts = []
        for jj in range(PEER_SUB_E1):
            e1 = blk * e1_per_blk + PEER_SUB_E1 * i + jj
            gate = jnp.zeros((N_KEYS, lanes), BF16)
            for hd in range(PK_HEADS):
                count = _rows_bf16(c1_ref[hd, pl.ds(e1, 1), :])
                weight = _rows_bf16(w1_ref[hd, pl.ds(e1, 1), :])
                gate = gate + jnp.where(r2_ref[hd] < count, e2_ref[hd] * weight, 0)
            u = hu[jj * N_KEYS:(jj + 1) * N_KEYS]
            act = 0.5 * u * (1.0 + lax.erf(u * (1.0 / math.sqrt(2.0))))
            acts.append(act.astype(BF16) * gate)
        return jnp.concatenate(acts, axis=0)

    def up_projections(ht):
        return [_dot(pu_ref[i * sub:(i + 1) * sub, :], ht) for i in range(nsub)]

    def store_activations(blk, hus):
        for i in range(nsub):
            act_ref[(blk + 1) % 2, i * sub:(i + 1) * sub, :] = activation(blk, i, hus[i])

    @pl.when(eb == 0)
    def _():
        h = jnp.concatenate([_norm_mod(x_ref[i], g_ref[...], mod_ref[i, 0], 3) for i in range(nb)],
                            axis=0)
        ht = h.T
        hhi, hlo = _split_bf16(ht)
        ht_ref[...] = hhi
        qt = _mm3((pqh_ref[...], pql_ref[...]), (hhi, hlo))
        qs = [_split_bf16(qt[i * PK_HALF:(i + 1) * PK_HALF]) for i in range(2 * PK_HEADS)]
        for hd in range(PK_HEADS):
            s_ref[2 * hd] = _mm3((sk1h_ref[hd], sk1l_ref[hd]), qs[2 * hd])
            s_ref[2 * hd + 1] = _mm3((sk2h_ref[hd], sk2l_ref[hd]), qs[2 * hd + 1])
        hus = up_projections(hhi)
        for hd in range(PK_HEADS):
            s1 = s_ref[2 * hd]
            s2 = s_ref[2 * hd + 1]
            v1 = _top16_rows(s1)
            v2 = _top16_rows(s2)
            cand_ref[...] = jnp.full_like(cand_ref, NEG_INF)
            for n, (i, j) in enumerate(_PAIR_IDX):
                cand_ref[n:n + 1, :] = v1[i] + v2[j]
            cand = cand_ref[...]
            top = v1[0] + v2[0]
            zsum = jnp.zeros_like(top)
            tau = top
            for _ in range(PK_TOPK):
                m = jnp.max(cand, axis=0, keepdims=True)
                zsum = zsum + jnp.exp(m - top)
                tau = m
                cand = jnp.where(cand >= m, NEG_INF, cand)
            rank2 = jnp.full((N_KEYS, lanes), float(PK_TOPK), F32)
            for j in reversed(range(PK_TOPK)):
                rank2 = jnp.where(s2 >= v2[j], float(j), rank2)
            count1 = jnp.zeros((N_KEYS, lanes), F32)
            for i in reversed(range(PK_TOPK)):
                cnt = jnp.zeros_like(top)
                for j in range(PK_TOPK // (i + 1)):
                    cnt = cnt + jnp.where(v1[i] + v2[j] >= tau, 1.0, 0.0)
                count1 = jnp.where(s1 >= v1[i], cnt, count1)
            r2_ref[hd] = rank2.astype(BF16)
            c1_ref[hd] = count1
            e2_ref[hd] = jnp.exp(s2 - v2[0]).astype(BF16)
            w1_ref[hd] = jnp.exp(s1 - v1[0]) / zsum
        acc_ref[...] = jnp.zeros_like(acc_ref)
        store_activations(0, hus)

    @pl.when(jnp.logical_and(eb > 0, eb < last))
    def _():
        down = _dot(pvt_ref[...], act_ref[eb % 2])
        store_activations(eb, up_projections(ht_ref[...]))
        acc_ref[...] += down

    @pl.when(eb == last)
    def _():
        acc_t = (acc_ref[...] + _dot(pvt_ref[...], act_ref[eb % 2])).T
        for i in range(nb):
            y = x_ref[i] + mod_ref[i, 0][5:6] * acc_t[i * tm:(i + 1) * tm]
            if final_norm:
                ms = jnp.mean(y * y, axis=-1, keepdims=True)
                y = y * lax.rsqrt(ms + NORM_EPS) * gf_ref[...]
            out_ref[i] = y


def _peer(x, g, mod, p, tm, tile_off, ctx_tiles, final_norm, norm_f):
    bsz, rows, d = x.shape
    n_exp = p["pu"].shape[0]
    e1_per_blk = PEER_STEP_E1
    eblk = e1_per_blk * N_KEYS
    nblk = n_exp // eblk
    pqt = p["pq"].T
    pqh = pqt.astype(BF16)
    pql = (pqt - pqh.astype(F32)).astype(BF16)
    nq = pqt.shape[0]
    nb = PEER_BATCH_ROWS if bsz % PEER_BATCH_ROWS == 0 else 1
    lanes = nb * tm
    kern = functools.partial(_peer_kernel, nb=nb, tm=tm, e1_per_blk=e1_per_blk,
                             final_norm=final_norm)
    const = lambda shape: pl.BlockSpec(shape, lambda b, t, e: (0,) * len(shape))
    return pl.pallas_call(
        kern,
        grid=(bsz // nb, rows // tm, nblk + 1),
        in_specs=[pl.BlockSpec((nb, tm, d), lambda b, t, e: (b, t, 0)),
                  const((1, d)),
                  pl.BlockSpec((nb, 1, N_MOD, d),
                               lambda b, t, e: (b, jnp.where(t + tile_off >= ctx_tiles, 1, 0), 0, 0)),
                  const((nq, d)), const((nq, d)),
                  const((PK_HEADS, N_KEYS, PK_HALF)), const((PK_HEADS, N_KEYS, PK_HALF)),
                  const((PK_HEADS, N_KEYS, PK_HALF)), const((PK_HEADS, N_KEYS, PK_HALF)),
                  pl.BlockSpec((eblk, d), lambda b, t, e: (jnp.minimum(e, nblk - 1), 0)),
                  pl.BlockSpec((d, eblk), lambda b, t, e: (0, jnp.maximum(e - 1, 0))),
                  const((1, d))],
        out_specs=pl.BlockSpec((nb, tm, d), lambda b, t, e: (b, t, 0)),
        out_shape=jax.ShapeDtypeStruct((bsz, rows, d), F32),
        scratch_shapes=[pltpu.VMEM((d, lanes), BF16),
                        pltpu.VMEM((PK_HEADS, N_KEYS, lanes), BF16),
                        pltpu.VMEM((PK_HEADS, N_KEYS, lanes), BF16),
                        pltpu.VMEM((PK_HEADS, N_KEYS, lanes), F32),
                        pltpu.VMEM((PK_HEADS, N_KEYS, lanes), F32),
                        pltpu.VMEM((2 * PK_HEADS, N_KEYS, lanes), F32),
                        pltpu.VMEM((SUBLANES * pl.cdiv(len(_PAIR_IDX), SUBLANES), lanes), F32),
                        pltpu.VMEM((2, eblk, lanes), BF16),
                        pltpu.VMEM((d, lanes), F32)],
        compiler_params=_cparams(("parallel", "parallel", "arbitrary")),
        name="peer",
    )(x, g.reshape(1, d), mod, pqh, pql, *_split_bf16(p["sk1"]), *_split_bf16(p["sk2"]),
      p["pu"].astype(BF16), p["pv"].T.astype(BF16), norm_f.reshape(1, d))


def _rope_tables(rows, ctx_len):
    t = jnp.arange(rows - ctx_len, dtype=jnp.int32)
    inv = ROPE_THETA ** (-jnp.arange(0, ROPE_AXIS_DIM, 2, dtype=F32) / ROPE_AXIS_DIM)
    ang_r = (t // GRID_W).astype(F32)[:, None] * inv
    ang_c = (t % GRID_W).astype(F32)[:, None] * inv
    cos = jnp.concatenate([jnp.cos(ang_r)] * 2 + [jnp.cos(ang_c)] * 2, axis=1)
    sin = jnp.concatenate([-jnp.sin(ang_r), jnp.sin(ang_r), -jnp.sin(ang_c), jnp.sin(ang_c)], axis=1)
    cos = jnp.concatenate([jnp.ones((ctx_len, HEAD_DIM), F32), cos], axis=0)
    sin = jnp.concatenate([jnp.zeros((ctx_len, HEAD_DIM), F32), sin], axis=0)
    return jnp.tile(cos, (1, 2)), jnp.tile(sin, (1, 2))


def _head_norm_rope(x, gain, ones_bd, cos, sin):
    ms = _segsum(x * x, ones_bd) * (1.0 / HEAD_DIM)
    y = x * lax.rsqrt(ms + NORM_EPS) * gain
    outs = []
    half = ROPE_AXIS_DIM // 2
    lane = lax.broadcasted_iota(jnp.int32, (1, LANES), 1)
    first_half = (lane % ROPE_AXIS_DIM) < half
    for i in range(x.shape[1] // LANES):
        yc = y[:, i * LANES:(i + 1) * LANES]
        partner = jnp.where(first_half, pltpu.roll(yc, LANES - half, 1), pltpu.roll(yc, half, 1))
        outs.append(yc * cos + partner * sin)
    return jnp.concatenate(outs, axis=1)


def _attn_prep_kernel(z_ref, qn_ref, kn_ref, cos_ref, sin_ref, onesq_ref, onesk_ref,
                      q_o, k_o, v_o):
    z = z_ref[0]
    cos = cos_ref[...]
    sin = sin_ref[...]
    q = _head_norm_rope(z[:, :C_WIDTH], qn_ref[...], onesq_ref[...], cos, sin)
    k = _head_norm_rope(z[:, C_WIDTH:C_WIDTH + KV_WIDTH], kn_ref[...], onesk_ref[...], cos, sin)
    q_o[0] = (q * (ATTN_SCALE * math.log2(math.e))).T.astype(q_o.dtype)
    for j in range(C_KV_HEADS):
        k_o[0, j] = k[:, j * HEAD_DIM:(j + 1) * HEAD_DIM].astype(k_o.dtype)
    v_o[0] = z[:, C_WIDTH + KV_WIDTH:C_WIDTH + 2 * KV_WIDTH].T.astype(v_o.dtype)


def _attn_prep(z, q_norm, k_norm, ctx_len, tm):
    bsz, rows, ncol = z.shape
    cos, sin = _rope_tables(rows, ctx_len)
    qn = jnp.tile(q_norm, C_HEADS).reshape(1, C_WIDTH)
    kn = jnp.tile(k_norm, C_KV_HEADS).reshape(1, KV_WIDTH)
    const = lambda shape: pl.BlockSpec(shape, lambda b, t: (0,) * len(shape))
    return pl.pallas_call(
        _attn_prep_kernel,
        grid=(bsz, rows // tm),
        in_specs=[pl.BlockSpec((1, tm, ncol), lambda b, t: (b, t, 0)),
                  const((1, C_WIDTH)), const((1, KV_WIDTH)),
                  pl.BlockSpec((tm, LANES), lambda b, t: (t, 0)),
                  pl.BlockSpec((tm, LANES), lambda b, t: (t, 0)),
                  const((C_WIDTH, C_WIDTH)), const((KV_WIDTH, KV_WIDTH))],
        out_specs=[pl.BlockSpec((1, C_WIDTH, tm), lambda b, t: (b, 0, t)),
                   pl.BlockSpec((1, C_KV_HEADS, tm, HEAD_DIM), lambda b, t: (b, 0, t, 0)),
                   pl.BlockSpec((1, KV_WIDTH, tm), lambda b, t: (b, 0, t))],
        out_shape=[jax.ShapeDtypeStruct((bsz, C_WIDTH, rows), BF16),
                   jax.ShapeDtypeStruct((bsz, C_KV_HEADS, rows, HEAD_DIM), BF16),
                   jax.ShapeDtypeStruct((bsz, KV_WIDTH, rows), BF16)],
        compiler_params=_cparams(("parallel", "parallel")),
        name="attn_prep",
    )(z, qn, kn, cos, sin, _block_ones(C_WIDTH, HEAD_DIM), _block_ones(KV_WIDTH, HEAD_DIM))


def _attn_kernel(qt_ref, k_ref, vt_ref, o_ref, m_ref, l_ref, acc_ref, *, kc):
    kt = pl.program_id(3)

    @pl.when(kt == 0)
    def _():
        m_ref[...] = jnp.full_like(m_ref, NEG_INF)
        l_ref[...] = jnp.zeros_like(l_ref)
        acc_ref[...] = jnp.zeros_like(acc_ref)

    grp = range(C_GROUP)
    qts = [qt_ref[0, g * HEAD_DIM:(g + 1) * HEAD_DIM, :] for g in grp]
    nck = k_ref.shape[2] // kc

    def scores(c):
        kb = k_ref[0, 0, c * kc:(c + 1) * kc, :]
        return [_dot(kb, qts[g]) for g in grp]

    m = [m_ref[g] for g in grp]
    l = [l_ref[g] for g in grp]
    acc = [acc_ref[g] for g in grp]
    s_next = scores(0)
    for c in range(nck):
        s = s_next
        if c + 1 < nck:
            s_next = scores(c + 1)
        vt = vt_ref[0, :, c * kc:(c + 1) * kc]
        m_new = [jnp.maximum(m[g], jnp.max(s[g], axis=0, keepdims=True)) for g in grp]
        p = [jnp.exp2(s[g] - m_new[g]) for g in grp]
        pv = [_dot(vt, p[g].astype(BF16)) for g in grp]
        for g in grp:
            alpha = jnp.exp2(m[g] - m_new[g])
            l[g] = alpha * l[g] + jnp.sum(p[g], axis=0, keepdims=True)
            acc[g] = alpha * acc[g] + pv[g]
        m = m_new
    for g in grp:
        m_ref[g] = m[g]
        l_ref[g] = l[g]
        acc_ref[g] = acc[g]

    @pl.when(kt == pl.num_programs(3) - 1)
    def _():
        o_ref[0] = jnp.concatenate(
            [acc_ref[g] / l_ref[g] for g in grp], axis=0).astype(o_ref.dtype)


ATTN_MAX_KEY_TILE = 2816
ATTN_MAX_KEY_CHUNK = 256


def _key_tile(rows, limit=ATTN_MAX_KEY_TILE):
    return max(t for t in range(LANES, min(rows, limit) + 1, LANES) if rows % t == 0)


def _attention(qt, k, vt, ctx_len, tq, tk):
    bsz, _, rows = qt.shape
    seq = rows - ctx_len
    qoff = ctx_len // tq
    gw = C_GROUP * HEAD_DIM
    kern = functools.partial(_attn_kernel, kc=_key_tile(tk, ATTN_MAX_KEY_CHUNK))
    return pl.pallas_call(
        kern,
        grid=(bsz, C_KV_HEADS, seq // tq, rows // tk),
        in_specs=[pl.BlockSpec((1, gw, tq), lambda b, j, i, kk: (b, j, i + qoff)),
                  pl.BlockSpec((1, 1, tk, HEAD_DIM), lambda b, j, i, kk: (b, j, kk, 0)),
                  pl.BlockSpec((1, HEAD_DIM, tk), lambda b, j, i, kk: (b, j, kk))],
        out_specs=pl.BlockSpec((1, gw, tq), lambda b, j, i, kk: (b, j, i)),
        out_shape=jax.ShapeDtypeStruct((bsz, C_WIDTH, seq), BF16),
        scratch_shapes=[pltpu.VMEM((C_GROUP, 1, tq), F32),
                        pltpu.VMEM((C_GROUP, 1, tq), F32),
                        pltpu.VMEM((C_GROUP, HEAD_DIM, tq), F32)],
        compiler_params=_cparams(("parallel", "parallel", "parallel", "arbitrary")),
        name="attention",
    )(qt, k, vt)


def _conv_kernel(zc_ref, zp_ref, zn_ref, w_ref, b_ref, lg_ref, lb_ref, o_ref, ybuf, *, tm, halo):
    t = pl.program_id(1)
    nt = pl.num_programs(1)

    def glu(u):
        return u[:, :D_WIDTH] * jax.nn.sigmoid(u[:, D_WIDTH:])

    ybuf[0:halo, :] = jnp.where(t == 0, 0.0, glu(zp_ref[0]))
    ybuf[halo:halo + tm, :] = glu(zc_ref[0])
    ybuf[halo + tm:, :] = jnp.where(t == nt - 1, 0.0, glu(zn_ref[0]))
    acc = jnp.zeros((tm, D_WIDTH), F32)
    for j in range(D_CONV_WIDTH):
        off = halo - D_PAD + j
        acc = acc + w_ref[j:j + 1, :] * ybuf[off:off + tm, :]
    y = acc + b_ref[...]
    mu = jnp.mean(y, axis=-1, keepdims=True)
    yc = y - mu
    var = jnp.mean(yc * yc, axis=-1, keepdims=True)
    yn = yc * lax.rsqrt(var + LN_EPS) * lg_ref[...] + lb_ref[...]
    o_ref[0] = (yn * jax.nn.sigmoid(yn)).astype(o_ref.dtype)


def _conformer_conv(z, dw_w, dw_b, cn_g, cn_b, ctx_len, tm):
    bsz, rows, ncol = z.shape
    seq = rows - ctx_len
    halo = 2 * SUBLANES
    hb = tm // halo
    off = ctx_len // tm
    offh = ctx_len // halo
    nh = seq // halo
    kern = functools.partial(_conv_kernel, tm=tm, halo=halo)
    row = lambda w: pl.BlockSpec((1, w), lambda b, t: (0, 0))
    return pl.pallas_call(
        kern,
        grid=(bsz, seq // tm),
        in_specs=[pl.BlockSpec((1, tm, ncol), lambda b, t: (b, t + off, 0)),
                  pl.BlockSpec((1, halo, ncol),
                               lambda b, t: (b, offh + jnp.maximum(t * hb - 1, 0), 0)),
                  pl.BlockSpec((1, halo, ncol),
                               lambda b, t: (b, offh + jnp.minimum((t + 1) * hb, nh - 1), 0)),
                  pl.BlockSpec((D_CONV_WIDTH, D_WIDTH), lambda b, t: (0, 0)),
                  row(D_WIDTH), row(D_WIDTH), row(D_WIDTH)],
        out_specs=pl.BlockSpec((1, tm, D_WIDTH), lambda b, t: (b, t, 0)),
        out_shape=jax.ShapeDtypeStruct((bsz, seq, D_WIDTH), BF16),
        scratch_shapes=[pltpu.VMEM((tm + 2 * halo, D_WIDTH), F32)],
        compiler_params=_cparams(("parallel", "parallel")),
        name="conformer_conv",
    )(z, z, z, dw_w, dw_b.reshape(1, -1), cn_g.reshape(1, -1), cn_b.reshape(1, -1))


def _forward(x, c, ctx, c_ctx, l0, l1, norm_f):
    bsz, seq, d = x.shape
    ctx_len = ctx.shape[1]
    tm = min(256, ctx_len)
    ctx_tiles = ctx_len // tm
    xs = jnp.concatenate([ctx, x], axis=1)

    mod = _modulation(c, c_ctx, l0["mod_w"], l0["mod_b"])
    z, zf = _normmod_matmul(xs, l0["norm1"], mod, l0["w_in"], tm, ctx_tiles, A_COLS)
    r, v, g, bonus, kk, lwf, kdf, bf, lwr, kdr, br = _rwkv_prep(z, l0, tm, ctx_tiles)
    yf, yr = _rwkv_scan(r, v, kk, lwf, kdf, bf, lwr, kdr, br, ctx_len)
    o_rwkv = _rwkv_finish(yf, yr, bonus, g, l0["lnx_g"], l0["lnx_b"], tm)
    fcs = _fnet_chan(zf, tm)
    ts_c = min(256, ctx_len)
    ts_l = min(512, seq)
    f_ctx = _fnet_seq(fcs, 0, ctx_len, ts_c, ts_c)
    f_lat = _fnet_seq(fcs[:, ctx_len:], 0, seq, ts_l, ts_l)
    o_fnet = jnp.concatenate([f_ctx, f_lat], axis=1)
    xs = _proj_residual(o_rwkv, o_fnet, xs, mod, l0["w_out"], tm, 0, ctx_tiles)
    xs = _peer(xs, l0["norm2"], mod, l0, tm, 0, ctx_tiles, False, norm_f)

    mod = _modulation(c, c_ctx, l1["mod_w"], l1["mod_b"])
    z, zu = _normmod_matmul(xs, l1["norm1"], mod, l1["w_in"], tm, ctx_tiles,
                            C_WIDTH + 2 * KV_WIDTH)
    qt, k, vt = _attn_prep(z, l1["q_norm"], l1["k_norm"], ctx_len, tm)
    o_attn = _attention(qt, k, vt, ctx_len, tm, _key_tile(ctx_len + seq))
    o_conv = _conformer_conv(zu, l1["dw_w"], l1["dw_b"], l1["cn_g"], l1["cn_b"], ctx_len, tm)
    xl = _proj_residual(o_attn, o_conv, xs, mod, l1["w_out"], tm, ctx_tiles, ctx_tiles,
                        o1_transposed=True)
    return _peer(xl, l1["norm2"], mod, l1, tm, ctx_tiles, ctx_tiles, True, norm_f)


def kernel(x, c, ctx, c_ctx, l0_mod_w, l0_mod_b, l0_norm1, l0_w_in, l0_shift_prev, l0_shift_next, l0_w0, l0_w2, l0_a0, l0_a2, l0_g2, l0_k_k, l0_k_a, l0_r_k, l0_lnx_g, l0_lnx_b, l0_w_out, l0_norm2, l0_pq, l0_sk1, l0_sk2, l0_pu, l0_pv, l1_mod_w, l1_mod_b, l1_norm1, l1_w_in, l1_q_norm, l1_k_norm, l1_dw_w, l1_dw_b, l1_cn_g, l1_cn_b, l1_w_out, l1_norm2, l1_pq, l1_sk1, l1_sk2, l1_pu, l1_pv, norm_f):
    l0 = dict(mod_w=l0_mod_w, mod_b=l0_mod_b, norm1=l0_norm1, w_in=l0_w_in,
              shift_prev=l0_shift_prev, shift_next=l0_shift_next, w0=l0_w0, w2=l0_w2,
              a0=l0_a0, a2=l0_a2, g2=l0_g2, k_k=l0_k_k, k_a=l0_k_a, r_k=l0_r_k,
              lnx_g=l0_lnx_g, lnx_b=l0_lnx_b, w_out=l0_w_out, norm2=l0_norm2,
              pq=l0_pq, sk1=l0_sk1, sk2=l0_sk2, pu=l0_pu, pv=l0_pv)
    l1 = dict(mod_w=l1_mod_w, mod_b=l1_mod_b, norm1=l1_norm1, w_in=l1_w_in,
              q_norm=l1_q_norm, k_norm=l1_k_norm, dw_w=l1_dw_w, dw_b=l1_dw_b,
              cn_g=l1_cn_g, cn_b=l1_cn_b, w_out=l1_w_out, norm2=l1_norm2,
              pq=l1_pq, sk1=l1_sk1, sk2=l1_sk2, pu=l1_pu, pv=l1_pv)
    return _forward(x, c, ctx, c_ctx, l0, l1, norm_f)
```

```python
import functools
import math

import jax
import jax.numpy as jnp
import numpy as np
from jax import lax
from jax.experimental import pallas as pl
from jax.experimental.pallas import tpu as pltpu

F32 = jnp.float32
BF16 = jnp.bfloat16
HIGHEST = lax.Precision.HIGHEST

LANES = 128
SUBLANES = 8
VMEM_LIMIT_BYTES = 56 * 1024 * 1024

N_MOD = 6
NORM_EPS = 1e-6
LN_EPS = 1e-5
GRID_W = 64
HEAD_DIM = 64
A_WIDTH = 768
A_HEADS = A_WIDTH // HEAD_DIM
A_RANK_W = 64
A_RANK_A = 64
A_RANK_G = 128
A_LNX_EPS = 64e-5
A_COLS = 3 * A_WIDTH + 2 * A_RANK_W + 2 * A_RANK_A + A_RANK_G
B_WIDTH = 256
B_GROUP_DIM = 64
C_WIDTH = 768
C_HEADS = 12
C_KV_HEADS = 4
C_GROUP = C_HEADS // C_KV_HEADS
KV_WIDTH = C_KV_HEADS * HEAD_DIM
ROPE_AXIS_DIM = HEAD_DIM // 2
ROPE_THETA = 10000.0
ATTN_SCALE = HEAD_DIM ** -0.5
D_WIDTH = 256
D_CONV_WIDTH = 31
D_PAD = D_CONV_WIDTH // 2
PK_HEADS = 8
PK_DIM = 256
PK_HALF = 128
N_KEYS = 128
PK_TOPK = 16

SCAN_CHUNK = 64
NEG_INF = float("-inf")


def _cparams(semantics):
    return pltpu.CompilerParams(dimension_semantics=semantics,
                                vmem_limit_bytes=VMEM_LIMIT_BYTES)


def _split_bf16(x):
    hi = x.astype(BF16)
    lo = (x - hi.astype(F32)).astype(BF16)
    return hi, lo


def _dot(a, b):
    return jnp.dot(a, b, preferred_element_type=F32)


def _dot_hp(a, b):
    return _mm3(_split_bf16(a), _split_bf16(b))


def _dot_nt(a, b, precision=None):
    return lax.dot_general(a, b, (((1,), (1,)), ((), ())),
                           preferred_element_type=F32, precision=precision)


def _dot_tn(a, b, precision=None):
    return lax.dot_general(a, b, (((0,), (0,)), ((), ())),
                           preferred_element_type=F32, precision=precision)


_NN = (((1,), (0,)), ((), ()))
_NT = (((1,), (1,)), ((), ()))
_TN = (((0,), (0,)), ((), ()))


def _mm3(a, b, dims=_NN):
    (ah, al), (bh, bl) = a, b
    dg = lambda x, y: lax.dot_general(x, y, dims, preferred_element_type=F32)
    return dg(ah, bh) + (dg(ah, bl) + dg(al, bh))


def _segsum(x, ones_bd):
    hi, lo = _split_bf16(x)
    return _dot(hi, ones_bd) + _dot(lo, ones_bd)


def _block_ones(width, seg):
    r = np.arange(width) // seg
    return jnp.asarray((r[:, None] == r[None, :]).astype(np.float32), dtype=BF16)


def _mod_kernel(c_ref, w_ref, b_ref, o_ref):
    c = c_ref[...]
    s = c * jax.nn.sigmoid(c)
    o_ref[...] = _dot_hp(s, w_ref[...]) + b_ref[...]


def _modulation(c, c_ctx, mod_w, mod_b):
    bsz, d = c.shape
    rows = SUBLANES * pl.cdiv(bsz + 1, SUBLANES)
    cc = jnp.zeros((rows, d), F32).at[:bsz].set(c).at[bsz].set(c_ctx)
    n = mod_w.shape[1]
    tn = n // 4
    out = pl.pallas_call(
        _mod_kernel,
        grid=(n // tn,),
        in_specs=[pl.BlockSpec((rows, d), lambda j: (0, 0)),
                  pl.BlockSpec((d, tn), lambda j: (0, j)),
                  pl.BlockSpec((1, tn), lambda j: (0, j))],
        out_specs=pl.BlockSpec((rows, tn), lambda j: (0, j)),
        out_shape=jax.ShapeDtypeStruct((rows, n), F32),
        compiler_params=_cparams(("arbitrary",)),
        name="adaln_mod",
    )(cc, mod_w, mod_b.reshape(1, n))
    lat = out[:bsz].reshape(bsz, N_MOD, d)
    ctx = jnp.broadcast_to(out[bsz].reshape(1, N_MOD, d), (bsz, N_MOD, d))
    return jnp.stack([ctx, lat], axis=1)


def _norm_mod(x, g, mod, row):
    ms = jnp.mean(x * x, axis=-1, keepdims=True)
    y = x * lax.rsqrt(ms + NORM_EPS) * g
    return y * (1.0 + mod[row + 1:row + 2]) + mod[row:row + 1]


def _normmod_matmul_kernel(x_ref, g_ref, mod_ref, w_ref, o1_ref, o2_ref):
    h = _norm_mod(x_ref[0], g_ref[...], mod_ref[0, 0], 0)
    z = _dot(h.astype(BF16), w_ref[...])
    n1 = o1_ref.shape[2]
    o1_ref[0] = z[:, :n1]
    o2_ref[0] = z[:, n1:]


def _normmod_matmul(x, g, mod, w, tm, ctx_tiles, n1):
    bsz, rows, d = x.shape
    n = w.shape[1]
    widths = (n1, n - n1)
    return pl.pallas_call(
        _normmod_matmul_kernel,
        grid=(bsz, rows // tm),
        in_specs=[pl.BlockSpec((1, tm, d), lambda b, t: (b, t, 0)),
                  pl.BlockSpec((1, d), lambda b, t: (0, 0)),
                  pl.BlockSpec((1, 1, N_MOD, d),
                               lambda b, t: (b, jnp.where(t >= ctx_tiles, 1, 0), 0, 0)),
                  pl.BlockSpec((d, n), lambda b, t: (0, 0))],
        out_specs=[pl.BlockSpec((1, tm, wd), lambda b, t: (b, t, 0)) for wd in widths],
        out_shape=[jax.ShapeDtypeStruct((bsz, rows, wd), F32) for wd in widths],
        compiler_params=_cparams(("parallel", "parallel")),
        name="normmod_proj",
    )(x, g.reshape(1, d), mod, w.astype(BF16))


def _rwkv_prep_kernel(z_ref, zp_ref, zn_ref, mup_ref, mun_ref, w0_ref, w2_ref, a0_ref, a2_ref,
                      g2_ref, kk_ref, ka_ref, rk_ref, ones_ref,
                      r_o, v_o, g_o, bonus_o, kkn_o, lwf_o, kdf_o, bf_o, lwr_o, kdr_o, br_o,
                      *, tm, ctx_tiles):
    t = pl.program_id(1)
    nt = pl.num_programs(1)
    z = z_ref[0][:, :A_COLS]
    zp_row = zp_ref[0][SUBLANES - 1:SUBLANES, :A_COLS]
    zn_row = zn_ref[0][0:1, :A_COLS]
    first = jnp.logical_or(t == 0, t == ctx_tiles)
    last = jnp.logical_or(t == ctx_tiles - 1, t == nt - 1)
    zp_row = jnp.where(first, 0.0, zp_row)
    zn_row = jnp.where(last, 0.0, zn_row)
    ridx = lax.broadcasted_iota(jnp.int32, (tm, 1), 0)
    z_prev = jnp.where(ridx == 0, zp_row, pltpu.roll(z, 1, 0))
    z_next = jnp.where(ridx == tm - 1, zn_row, pltpu.roll(z, tm - 1, 0))
    zs = z + mup_ref[...] * (z_prev - z) + mun_ref[...] * (z_next - z)

    w = A_WIDTH
    r = zs[:, 0:w]
    k = zs[:, w:2 * w]
    v = zs[:, 2 * w:3 * w]
    o = 3 * w
    xw = zs[:, o:o + 2 * A_RANK_W]
    o += 2 * A_RANK_W
    xa = zs[:, o:o + 2 * A_RANK_A]
    o += 2 * A_RANK_A
    xg = zs[:, o:o + A_RANK_G]
    ones_bd = ones_ref[...]

    g_o[0] = _dot_hp(jax.nn.sigmoid(xg), g2_ref[...])
    kk = k * kk_ref[...]
    nrm = jnp.sqrt(_segsum(kk * kk, ones_bd))
    kk = kk / jnp.maximum(nrm, 1e-12)
    r_o[0] = r
    v_o[0] = v
    kkn_o[0] = kk
    bonus = jnp.zeros_like(r)
    tw = jnp.tanh(xw)
    for d, (lw_o, kd_o, b_o) in enumerate(((lwf_o, kdf_o, bf_o), (lwr_o, kdr_o, br_o))):
        wl = w0_ref[d:d + 1, :] + _dot_hp(tw[:, d * A_RANK_W:(d + 1) * A_RANK_W], w2_ref[d])
        w_log = -jax.nn.softplus(-wl) - 0.5
        lw_o[0] = -jnp.exp(w_log)
        a_gate = jax.nn.sigmoid(
            a0_ref[d:d + 1, :] + _dot_hp(xa[:, d * A_RANK_A:(d + 1) * A_RANK_A], a2_ref[d]))
        k_d = k * (1.0 + (a_gate - 1.0) * ka_ref[...])
        kd_o[0] = k_d
        b_o[0] = kk * a_gate
        bonus = bonus + _segsum(r * k_d * rk_ref[...], ones_bd) * v
    bonus_o[0] = bonus


def _rwkv_prep(z, p, tm, ctx_tiles):
    bsz, rows, ncol = z.shape
    w = A_WIDTH
    hb = tm // SUBLANES
    nblk8 = rows // SUBLANES
    row_spec = lambda width: pl.BlockSpec((1, width), lambda b, t: (0, 0))
    full = lambda shape: pl.BlockSpec(shape, lambda b, t: (0,) * len(shape))
    out_spec = pl.BlockSpec((1, tm, w), lambda b, t: (b, t, 0))
    out_shape = jax.ShapeDtypeStruct((bsz, rows, w), F32)
    kern = functools.partial(_rwkv_prep_kernel, tm=tm, ctx_tiles=ctx_tiles)
    return pl.pallas_call(
        kern,
        grid=(bsz, rows // tm),
        in_specs=[pl.BlockSpec((1, tm, ncol), lambda b, t: (b, t, 0)),
                  pl.BlockSpec((1, SUBLANES, ncol),
                               lambda b, t: (b, jnp.maximum(t * hb - 1, 0), 0)),
                  pl.BlockSpec((1, SUBLANES, ncol),
                               lambda b, t: (b, jnp.minimum((t + 1) * hb, nblk8 - 1), 0)),
                  row_spec(A_COLS), row_spec(A_COLS),
                  full((2, w)), full((2, A_RANK_W, w)), full((2, w)), full((2, A_RANK_A, w)),
                  full((A_RANK_G, w)), row_spec(w), row_spec(w), row_spec(w), full((w, w))],
        out_specs=[out_spec] * 11,
        out_shape=[out_shape] * 11,
        compiler_params=_cparams(("parallel", "parallel")),
        name="rwkv_prep",
    )(z, z, z, p["shift_prev"].reshape(1, -1), p["shift_next"].reshape(1, -1),
      p["w0"], p["w2"], p["a0"], p["a2"], p["g2"], p["k_k"].reshape(1, w),
      p["k_a"].reshape(1, w), p["r_k"].reshape(1, w), _block_ones(w, HEAD_DIM))


SCAN_HEADS = 4


def _scan_kernel(rf, vf, kkf, lwf, kdf, bf, rr, vr, kkr, lwr, kdr, br, yf_o, yr_o, s_ref, *, c):
    @pl.when(pl.program_id(2) == 0)
    def _():
        s_ref[...] = jnp.zeros_like(s_ref)

    ri = lax.broadcasted_iota(jnp.int32, (c, c), 0)
    ci = lax.broadcasted_iota(jnp.int32, (c, c), 1)
    ri2 = lax.broadcasted_iota(jnp.int32, (c, 2 * c), 0)
    ci2 = lax.broadcasted_iota(jnp.int32, (c, 2 * c), 1)
    ci2 = jnp.where(ci2 >= c, ci2 - c, ci2)
    dirs = ((rf, vf, kkf, lwf, kdf, bf, ci <= ri, ci < ri, ci2 <= ri2, c - 1),
            (rr, vr, kkr, lwr, kdr, br, ci >= ri, ci > ri, ci2 >= ri2, 0))
    lhs, rhs, vs, ss, pend, incl, strict = [], [], [], [], [], [], []
    for d, (r_, v_, kk_, lw_, kd_, b_, m_incl, m_strict, m_incl2, edge) in enumerate(dirs):
        lw = lw_[0]
        tri = m_incl.astype(F32).astype(BF16)
        lw_h = lw.astype(BF16)
        lw_m, lw_l = _split_bf16(lw - lw_h.astype(F32))
        lcum = _dot(tri, lw_h) + (_dot(tri, lw_m) + _dot(tri, lw_l))
        p_in = jnp.exp(lcum)
        p_inv = jnp.exp(-lcum)
        lhs_all = jnp.concatenate([-kk_[0] * jnp.exp(lcum - lw), r_[0] * p_in], axis=0)
        rhs_all = jnp.concatenate([b_[0] * p_inv, kd_[0] * p_inv], axis=0)
        v_all = v_[0]
        for h in range(SCAN_HEADS):
            sl = slice(h * HEAD_DIM, (h + 1) * HEAD_DIM)
            lhs.append(lhs_all[:, sl])
            rhs.append(rhs_all[:, sl])
            vs.append(v_all[:, sl])
            ss.append(s_ref[d, h])
            pend.append(p_in[edge:edge + 1, sl])
            incl.append(m_incl2)
            strict.append(m_strict)
    n = len(lhs)
    idx = range(n)
    lhs2 = [_split_bf16(lhs[i]) for i in idx]
    rhs2 = [_split_bf16(rhs[i]) for i in idx]
    v2 = [_split_bf16(vs[i]) for i in idx]
    s2 = [_split_bf16(ss[i]) for i in idx]
    g = [_mm3(lhs2[i], rhs2[i], _NT) for i in idx]
    xs = [_mm3(lhs2[i], s2[i], _NT) for i in idx]
    a_ak = [_split_bf16(jnp.where(strict[i], g[i][:c, c:], 0.0)) for i in idx]
    x = [xs[i][:c] + _mm3(a_ak[i], v2[i]) for i in idx]
    apow = [_split_bf16(jnp.where(strict[i], g[i][:c, :c], 0.0)) for i in idx]
    steps = int(math.log2(c))
    for k in range(steps):
        x2 = [_split_bf16(x[i]) for i in idx]
        x = [x[i] + _mm3(apow[i], x2[i]) for i in idx]
        if k + 1 < steps:
            apow = [_split_bf16(_mm3(apow[i], apow[i])) for i in idx]
    x2 = [_split_bf16(x[i]) for i in idx]
    sav = [tuple(jnp.concatenate([x2[i][t], v2[i][t]], axis=0) for t in range(2)) for i in idx]
    a_r = [_split_bf16(jnp.where(incl[i], g[i][c:], 0.0)) for i in idx]
    y = [xs[i][c:] + _mm3(a_r[i], sav[i]) for i in idx]
    s_new = [(ss[i] + _mm3(sav[i], rhs2[i], _TN)) * pend[i] for i in idx]
    for d, y_o in enumerate((yf_o, yr_o)):
        for h in range(SCAN_HEADS):
            s_ref[d, h] = s_new[d * SCAN_HEADS + h]
        y_o[0] = jnp.concatenate(y[d * SCAN_HEADS:(d + 1) * SCAN_HEADS], axis=1)


def _rwkv_scan(r, v, kk, lwf, kdf, bf, lwr, kdr, br, ctx_len):
    bsz, rows, w = r.shape
    c = SCAN_CHUNK
    nch = rows // c
    cch = ctx_len // c
    lanes = SCAN_HEADS * HEAD_DIM
    fwd = lambda b, hp, i: (b, i, hp)
    rev = lambda b, hp, i: (b, jnp.where(i < cch, cch - 1 - i, nch - 1 + cch - i), hp)
    blk = lambda im: pl.BlockSpec((1, c, lanes), im)
    kern = functools.partial(_scan_kernel, c=c)
    out_shape = jax.ShapeDtypeStruct((bsz, rows, w), F32)
    return pl.pallas_call(
        kern,
        grid=(bsz, w // lanes, nch),
        in_specs=[blk(fwd)] * 6 + [blk(rev)] * 6,
        out_specs=[blk(fwd), blk(rev)],
        out_shape=[out_shape, out_shape],
        scratch_shapes=[pltpu.VMEM((2, SCAN_HEADS, HEAD_DIM, HEAD_DIM), F32)],
        compiler_params=_cparams(("parallel", "parallel", "arbitrary")),
        name="rwkv_scan",
    )(r, v, kk, lwf, kdf, bf, r, v, kk, lwr, kdr, br)


def _rwkv_finish_kernel(yf, yr, bonus, g, lg, lb, ones_ref, o_ref):
    y = yf[0] + yr[0]
    ones_bd = ones_ref[...]
    inv = 1.0 / HEAD_DIM
    mu = _segsum(y, ones_bd) * inv
    yc = y - mu
    var = _segsum(yc * yc, ones_bd) * inv
    yn = yc * lax.rsqrt(var + A_LNX_EPS) * lg[...] + lb[...]
    o_ref[0] = ((yn + bonus[0]) * g[0]).astype(o_ref.dtype)


def _rwkv_finish(yf, yr, bonus, g, lnx_g, lnx_b, tm):
    bsz, rows, w = yf.shape
    blk = pl.BlockSpec((1, tm, w), lambda b, t: (b, t, 0))
    row = pl.BlockSpec((1, w), lambda b, t: (0, 0))
    return pl.pallas_call(
        _rwkv_finish_kernel,
        grid=(bsz, rows // tm),
        in_specs=[blk, blk, blk, blk, row, row, pl.BlockSpec((w, w), lambda b, t: (0, 0))],
        out_specs=blk,
        out_shape=jax.ShapeDtypeStruct((bsz, rows, w), BF16),
        compiler_params=_cparams(("parallel", "parallel")),
        name="rwkv_finish",
    )(yf, yr, bonus, g, lnx_g.reshape(1, w), lnx_b.reshape(1, w), _block_ones(w, HEAD_DIM))


def _fnet_chan_kernel(z_ref, m_ref, o_ref):
    o_ref[0] = _dot(z_ref[0].astype(BF16), m_ref[...]).astype(o_ref.dtype)


def _fnet_chan(z, tm):
    bsz, rows, ncol = z.shape
    j = np.arange(B_WIDTH)
    same = (j[:, None] // B_GROUP_DIM) == (j[None, :] // B_GROUP_DIM)
    ang = 2.0 * np.pi * ((j[:, None] % B_GROUP_DIM) * (j[None, :] % B_GROUP_DIM) % B_GROUP_DIM) / B_GROUP_DIM
    m = np.concatenate([np.where(same, np.cos(ang), 0.0), np.where(same, np.sin(ang), 0.0)], axis=1)
    return pl.pallas_call(
        _fnet_chan_kernel,
        grid=(bsz, rows // tm),
        in_specs=[pl.BlockSpec((1, tm, ncol), lambda b, t: (b, t, 0)),
                  pl.BlockSpec((B_WIDTH, 2 * B_WIDTH), lambda b, t: (0, 0))],
        out_specs=pl.BlockSpec((1, tm, 2 * B_WIDTH), lambda b, t: (b, t, 0)),
        out_shape=jax.ShapeDtypeStruct((bsz, rows, 2 * B_WIDTH), BF16),
        compiler_params=_cparams(("parallel", "parallel")),
        name="fnet_chan",
    )(z, jnp.asarray(m, dtype=BF16))


def _fnet_seq_kernel(fcs_ref, cb_ref, sb_ref, c0_ref, s0_ref, o_ref, acc_ref, *, scale, nb):
    k = pl.program_id(1)

    @pl.when(k == 0)
    def _():
        acc_ref[...] = jnp.zeros_like(acc_ref)

    cb = cb_ref[...]
    sb = sb_ref[...]
    c0 = c0_ref[0]
    s0 = s0_ref[0]
    cm = (c0 * cb - s0 * sb).astype(BF16)
    sm = (s0 * cb + c0 * sb).astype(BF16)
    for b in range(nb):
        fcs = fcs_ref[b]
        acc_ref[b] += _dot(cm, fcs[:, :B_WIDTH]) - _dot(sm, fcs[:, B_WIDTH:])

    @pl.when(k == pl.num_programs(1) - 1)
    def _():
        o_ref[...] = (acc_ref[...] * scale).astype(o_ref.dtype)


def _fnet_seq(fcs, row0, length, ts, tk):
    bsz = fcs.shape[0]
    ds = jnp.arange(ts, dtype=jnp.int32)[:, None]
    tt = jnp.arange(length, dtype=jnp.int32)[None, :]
    ang = ((ds * tt) % length).astype(F32) * (2.0 * math.pi / length)
    cb, sb = jnp.cos(ang), jnp.sin(ang)
    s0 = (jnp.arange(length // ts, dtype=jnp.int32) * ts)[:, None]
    ang0 = ((s0 * tt) % length).astype(F32) * (2.0 * math.pi / length)
    c0, sn0 = jnp.cos(ang0)[:, None, :], jnp.sin(ang0)[:, None, :]
    kern = functools.partial(_fnet_seq_kernel, scale=1.0 / math.sqrt(length * B_GROUP_DIM), nb=bsz)
    koff = row0 // tk
    return pl.pallas_call(
        kern,
        grid=(length // ts, length // tk),
        in_specs=[pl.BlockSpec((bsz, tk, 2 * B_WIDTH), lambda s, k: (0, k + koff, 0)),
                  pl.BlockSpec((ts, tk), lambda s, k: (0, k)),
                  pl.BlockSpec((ts, tk), lambda s, k: (0, k)),
                  pl.BlockSpec((1, 1, tk), lambda s, k: (s, 0, k)),
                  pl.BlockSpec((1, 1, tk), lambda s, k: (s, 0, k))],
        out_specs=pl.BlockSpec((bsz, ts, B_WIDTH), lambda s, k: (0, s, 0)),
        out_shape=jax.ShapeDtypeStruct((bsz, length, B_WIDTH), BF16),
        scratch_shapes=[pltpu.VMEM((bsz, ts, B_WIDTH), F32)],
        compiler_params=_cparams(("parallel", "arbitrary")),
        name="fnet_seq",
    )(fcs, cb, sb, c0, sn0)


def _proj_residual_kernel(o1_ref, o2_ref, x_ref, mod_ref, w1_ref, w2_ref, out_ref, *, o1_transposed):
    o1 = o1_ref[0]
    if o1_transposed:
        o1 = o1.astype(F32).T.astype(BF16)
    mix = _dot(o1, w1_ref[...]) + _dot(o2_ref[0], w2_ref[...])
    gate = mod_ref[0, 0][2:3]
    out_ref[0] = x_ref[0] + gate * mix


def _proj_residual(o1, o2, x, mod, w_out, tm, tile_off, ctx_tiles, o1_transposed=False):
    bsz, rows, w2 = o2.shape
    w1 = o1.shape[1] if o1_transposed else o1.shape[2]
    d = x.shape[2]
    o1_spec = (pl.BlockSpec((1, w1, tm), lambda b, t: (b, 0, t)) if o1_transposed
               else pl.BlockSpec((1, tm, w1), lambda b, t: (b, t, 0)))
    kern = functools.partial(_proj_residual_kernel, o1_transposed=o1_transposed)
    return pl.pallas_call(
        kern,
        grid=(bsz, rows // tm),
        in_specs=[o1_spec,
                  pl.BlockSpec((1, tm, w2), lambda b, t: (b, t, 0)),
                  pl.BlockSpec((1, tm, d), lambda b, t: (b, t + tile_off, 0)),
                  pl.BlockSpec((1, 1, N_MOD, d),
                               lambda b, t: (b, jnp.where(t + tile_off >= ctx_tiles, 1, 0), 0, 0)),
                  pl.BlockSpec((w1, d), lambda b, t: (0, 0)),
                  pl.BlockSpec((w2, d), lambda b, t: (0, 0))],
        out_specs=pl.BlockSpec((1, tm, d), lambda b, t: (b, t, 0)),
        out_shape=jax.ShapeDtypeStruct((bsz, rows, d), F32),
        compiler_params=_cparams(("parallel", "parallel")),
        name="proj_residual",
    )(o1, o2, x, mod, w_out[:w1].astype(BF16), w_out[w1:].astype(BF16))


def _merge_exchange_network(n):
    pairs = []
    p = 1
    while p < n:
        k = p
        while k >= 1:
            for j in range(k % p, n - k, 2 * k):
                for i in range(min(k, n - j - k)):
                    if (i + j) // (2 * p) == (i + j + k) // (2 * p):
                        pairs.append((i + j, i + j + k))
            k //= 2
        p *= 2
    return pairs


def _top16_rows(s):
    nslab = s.shape[0] // SUBLANES
    slabs = [s[k * SUBLANES:(k + 1) * SUBLANES] for k in range(nslab)]
    for a, b in _merge_exchange_network(nslab):
        slabs[a], slabs[b] = jnp.maximum(slabs[a], slabs[b]), jnp.minimum(slabs[a], slabs[b])
    rows = []
    for t in range(PK_TOPK):
        m = jnp.max(slabs[0], axis=0, keepdims=True)
        rows.append(m)
        hit = slabs[0] >= m
        for k in range(nslab - 1 - t):
            slabs[k] = jnp.where(hit, slabs[k + 1], slabs[k])
    return rows


_PAIR_IDX = [(i, j) for i in range(PK_TOPK) for j in range(PK_TOPK) if (i + 1) * (j + 1) <= PK_TOPK]
BF16_SUBLANES = 2 * SUBLANES
PEER_SUB_E1 = 1
PEER_STEP_E1 = 16
PEER_BATCH_ROWS = 1


def _rows_bf16(row):
    packed = jnp.broadcast_to(row, (BF16_SUBLANES, row.shape[1])).astype(BF16)
    return jnp.tile(packed, (N_KEYS // BF16_SUBLANES, 1))


def _peer_kernel(x_ref, g_ref, mod_ref, pqh_ref, pql_ref, sk1h_ref, sk1l_ref, sk2h_ref, sk2l_ref,
                 pu_ref, pvt_ref, gf_ref,
                 out_ref, ht_ref, r2_ref, e2_ref, c1_ref, w1_ref, s_ref, cand_ref, act_ref, acc_ref,
                 *, nb, tm, e1_per_blk, final_norm):
    eb = pl.program_id(2)
    last = pl.num_programs(2) - 1
    lanes = nb * tm
    sub = PEER_SUB_E1 * N_KEYS
    nsub = e1_per_blk // PEER_SUB_E1

    def activation(blk, i, hu):
        acts = []
        for jj in range(PEER_SUB_E1):
            e1 = blk * e1_per_blk + PEER_SUB_E1 * i + jj
            gate = jnp.zeros((N_KEYS, lanes), BF16)
            for hd in range(PK_HEADS):
                count = _rows_bf16(c1_ref[hd, pl.ds(e1, 1), :])
                weight = _rows_bf16(w1_ref[hd, pl.ds(e1, 1), :])
                gate = gate + jnp.where(r2_ref[hd] < count, e2_ref[hd] * weight, 0)
            u = hu[jj * N_KEYS:(jj + 1) * N_KEYS]
            act = 0.5 * u * (1.0 + lax.erf(u * (1.0 / math.sqrt(2.0))))
            acts.append(act.astype(BF16) * gate)
        return jnp.concatenate(acts, axis=0)

    def up_projections(ht):
        return [_dot(pu_ref[i * sub:(i + 1) * sub, :], ht) for i in range(nsub)]

    def store_activations(blk, hus):
        for i in range(nsub):
            act_ref[(blk + 1) % 2, i * sub:(i + 1) * sub, :] = activation(blk, i, hus[i])

    @pl.when(eb == 0)
    def _():
        h = jnp.concatenate([_norm_mod(x_ref[i], g_ref[...], mod_ref[i, 0], 3) for i in range(nb)],
                            axis=0)
        ht = h.T
        hhi, hlo = _split_bf16(ht)
        ht_ref[...] = hhi
        qt = _mm3((pqh_ref[...], pql_ref[...]), (hhi, hlo))
        qs = [_split_bf16(qt[i * PK_HALF:(i + 1) * PK_HALF]) for i in range(2 * PK_HEADS)]
        for hd in range(PK_HEADS):
            s_ref[2 * hd] = _mm3((sk1h_ref[hd], sk1l_ref[hd]), qs[2 * hd])
            s_ref[2 * hd + 1] = _mm3((sk2h_ref[hd], sk2l_ref[hd]), qs[2 * hd + 1])
        hus = up_projections(hhi)
        for hd in range(PK_HEADS):
            s1 = s_ref[2 * hd]
            s2 = s_ref[2 * hd + 1]
            v1 = _top16_rows(s1)
            v2 = _top16_rows(s2)
            cand_ref[...] = jnp.full_like(cand_ref, NEG_INF)
            for n, (i, j) in enumerate(_PAIR_IDX):
                cand_ref[n:n + 1, :] = v1[i] + v2[j]
            cand = cand_ref[...]
            top = v1[0] + v2[0]
            zsum = jnp.zeros_like(top)
            tau = top
            for _ in range(PK_TOPK):
                m = jnp.max(cand, axis=0, keepdims=True)
                zsum = zsum + jnp.exp(m - top)
                tau = m
                cand = jnp.where(cand >= m, NEG_INF, cand)
            rank2 = jnp.full((N_KEYS, lanes), float(PK_TOPK), F32)
            for j in reversed(range(PK_TOPK)):
                rank2 = jnp.where(s2 >= v2[j], float(j), rank2)
            count1 = jnp.zeros((N_KEYS, lanes), F32)
            for i in reversed(range(PK_TOPK)):
                cnt = jnp.zeros_like(top)
                for j in range(PK_TOPK // (i + 1)):
                    cnt = cnt + jnp.where(v1[i] + v2[j] >= tau, 1.0, 0.0)
                count1 = jnp.where(s1 >= v1[i], cnt, count1)
            r2_ref[hd] = rank2.astype(BF16)
            c1_ref[hd] = count1
            e2_ref[hd] = jnp.exp(s2 - v2[0]).astype(BF16)
            w1_ref[hd] = jnp.exp(s1 - v1[0]) / zsum
        acc_ref[...] = jnp.zeros_like(acc_ref)
        store_activations(0, hus)

    @pl.when(jnp.logical_and(eb > 0, eb < last))
    def _():
        down = _dot(pvt_ref[0], act_ref[eb % 2])
        store_activations(eb, up_projections(ht_ref[...]))
        acc_ref[...] += down

    @pl.when(eb == last)
    def _():
        acc_t = (acc_ref[...] + _dot(pvt_ref[0], act_ref[eb % 2])).T
        for i in range(nb):
            y = x_ref[i] + mod_ref[i, 0][5:6] * acc_t[i * tm:(i + 1) * tm]
            if final_norm:
                ms = jnp.mean(y * y, axis=-1, keepdims=True)
                y = y * lax.rsqrt(ms + NORM_EPS) * gf_ref[...]
            out_ref[i] = y


def _peer(x, g, mod, p, tm, tile_off, ctx_tiles, final_norm, norm_f):
    bsz, rows, d = x.shape
    n_exp = p["pu"].shape[0]
    e1_per_blk = PEER_STEP_E1
    eblk = e1_per_blk * N_KEYS
    nblk = n_exp // eblk
    pqt = p["pq"].T
    pqh = pqt.astype(BF16)
    pql = (pqt - pqh.astype(F32)).astype(BF16)
    nq = pqt.shape[0]
    pvt = p["pv"].astype(BF16).reshape(nblk, eblk, d).transpose(0, 2, 1)
    nb = PEER_BATCH_ROWS if bsz % PEER_BATCH_ROWS == 0 else 1
    lanes = nb * tm
    kern = functools.partial(_peer_kernel, nb=nb, tm=tm, e1_per_blk=e1_per_blk,
                             final_norm=final_norm)
    const = lambda shape: pl.BlockSpec(shape, lambda b, t, e: (0,) * len(shape))
    return pl.pallas_call(
        kern,
        grid=(bsz // nb, rows // tm, nblk + 1),
        in_specs=[pl.BlockSpec((nb, tm, d), lambda b, t, e: (b, t, 0)),
                  const((1, d)),
                  pl.BlockSpec((nb, 1, N_MOD, d),
                               lambda b, t, e: (b, jnp.where(t + tile_off >= ctx_tiles, 1, 0), 0, 0)),
                  const((nq, d)), const((nq, d)),
                  const((PK_HEADS, N_KEYS, PK_HALF)), const((PK_HEADS, N_KEYS, PK_HALF)),
                  const((PK_HEADS, N_KEYS, PK_HALF)), const((PK_HEADS, N_KEYS, PK_HALF)),
                  pl.BlockSpec((eblk, d), lambda b, t, e: (jnp.minimum(e, nblk - 1), 0)),
                  pl.BlockSpec((1, d, eblk), lambda b, t, e: (jnp.maximum(e - 1, 0), 0, 0)),
                  const((1, d))],
        out_specs=pl.BlockSpec((nb, tm, d), lambda b, t, e: (b, t, 0)),
        out_shape=jax.ShapeDtypeStruct((bsz, rows, d), F32),
        scratch_shapes=[pltpu.VMEM((d, lanes), BF16),
                        pltpu.VMEM((PK_HEADS, N_KEYS, lanes), BF16),
                        pltpu.VMEM((PK_HEADS, N_KEYS, lanes), BF16),
                        pltpu.VMEM((PK_HEADS, N_KEYS, lanes), F32),
                        pltpu.VMEM((PK_HEADS, N_KEYS, lanes), F32),
                        pltpu.VMEM((2 * PK_HEADS, N_KEYS, lanes), F32),
                        pltpu.VMEM((SUBLANES * pl.cdiv(len(_PAIR_IDX), SUBLANES), lanes), F32),
                        pltpu.VMEM((2, eblk, lanes), BF16),
                        pltpu.VMEM((d, lanes), F32)],
        compiler_params=_cparams(("parallel", "parallel", "arbitrary")),
        name="peer",
    )(x, g.reshape(1, d), mod, pqh, pql, *_split_bf16(p["sk1"]), *_split_bf16(p["sk2"]),
      p["pu"].astype(BF16), pvt, norm_f.reshape(1, d))


def _rope_tables(rows, ctx_len):
    t = jnp.arange(rows - ctx_len, dtype=jnp.int32)
    inv = ROPE_THETA ** (-jnp.arange(0, ROPE_AXIS_DIM, 2, dtype=F32) / ROPE_AXIS_DIM)
    ang_r = (t // GRID_W).astype(F32)[:, None] * inv
    ang_c = (t % GRID_W).astype(F32)[:, None] * inv
    cos = jnp.concatenate([jnp.cos(ang_r)] * 2 + [jnp.cos(ang_c)] * 2, axis=1)
    sin = jnp.concatenate([-jnp.sin(ang_r), jnp.sin(ang_r), -jnp.sin(ang_c), jnp.sin(ang_c)], axis=1)
    cos = jnp.concatenate([jnp.ones((ctx_len, HEAD_DIM), F32), cos], axis=0)
    sin = jnp.concatenate([jnp.zeros((ctx_len, HEAD_DIM), F32), sin], axis=0)
    return jnp.tile(cos, (1, 2)), jnp.tile(sin, (1, 2))


def _head_norm_rope(x, gain, ones_bd, cos, sin):
    ms = _segsum(x * x, ones_bd) * (1.0 / HEAD_DIM)
    y = x * lax.rsqrt(ms + NORM_EPS) * gain
    outs = []
    half = ROPE_AXIS_DIM // 2
    lane = lax.broadcasted_iota(jnp.int32, (1, LANES), 1)
    first_half = (lane % ROPE_AXIS_DIM) < half
    for i in range(x.shape[1] // LANES):
        yc = y[:, i * LANES:(i + 1) * LANES]
        partner = jnp.where(first_half, pltpu.roll(yc, LANES - half, 1), pltpu.roll(yc, half, 1))
        outs.append(yc * cos + partner * sin)
    return jnp.concatenate(outs, axis=1)


def _attn_prep_kernel(z_ref, qn_ref, kn_ref, cos_ref, sin_ref, onesq_ref, onesk_ref,
                      q_o, k_o, v_o):
    z = z_ref[0]
    cos = cos_ref[...]
    sin = sin_ref[...]
    q = _head_norm_rope(z[:, :C_WIDTH], qn_ref[...], onesq_ref[...], cos, sin)
    k = _head_norm_rope(z[:, C_WIDTH:C_WIDTH + KV_WIDTH], kn_ref[...], onesk_ref[...], cos, sin)
    q_o[0] = (q * (ATTN_SCALE * math.log2(math.e))).T.astype(q_o.dtype)
    for j in range(C_KV_HEADS):
        k_o[0, j] = k[:, j * HEAD_DIM:(j + 1) * HEAD_DIM].astype(k_o.dtype)
    v_o[0] = z[:, C_WIDTH + KV_WIDTH:C_WIDTH + 2 * KV_WIDTH].T.astype(v_o.dtype)


def _attn_prep(z, q_norm, k_norm, ctx_len, tm):
    bsz, rows, ncol = z.shape
    cos, sin = _rope_tables(rows, ctx_len)
    qn = jnp.tile(q_norm, C_HEADS).reshape(1, C_WIDTH)
    kn = jnp.tile(k_norm, C_KV_HEADS).reshape(1, KV_WIDTH)
    const = lambda shape: pl.BlockSpec(shape, lambda b, t: (0,) * len(shape))
    return pl.pallas_call(
        _attn_prep_kernel,
        grid=(bsz, rows // tm),
        in_specs=[pl.BlockSpec((1, tm, ncol), lambda b, t: (b, t, 0)),
                  const((1, C_WIDTH)), const((1, KV_WIDTH)),
                  pl.BlockSpec((tm, LANES), lambda b, t: (t, 0)),
                  pl.BlockSpec((tm, LANES), lambda b, t: (t, 0)),
                  const((C_WIDTH, C_WIDTH)), const((KV_WIDTH, KV_WIDTH))],
        out_specs=[pl.BlockSpec((1, C_WIDTH, tm), lambda b, t: (b, 0, t)),
                   pl.BlockSpec((1, C_KV_HEADS, tm, HEAD_DIM), lambda b, t: (b, 0, t, 0)),
                   pl.BlockSpec((1, KV_WIDTH, tm), lambda b, t: (b, 0, t))],
        out_shape=[jax.ShapeDtypeStruct((bsz, C_WIDTH, rows), BF16),
                   jax.ShapeDtypeStruct((bsz, C_KV_HEADS, rows, HEAD_DIM), BF16),
                   jax.ShapeDtypeStruct((bsz, KV_WIDTH, rows), BF16)],
        compiler_params=_cparams(("parallel", "parallel")),
        name="attn_prep",
    )(z, qn, kn, cos, sin, _block_ones(C_WIDTH, HEAD_DIM), _block_ones(KV_WIDTH, HEAD_DIM))


def _attn_kernel(qt_ref, k_ref, vt_ref, o_ref, m_ref, l_ref, acc_ref, *, kc):
    kt = pl.program_id(3)

    @pl.when(kt == 0)
    def _():
        m_ref[...] = jnp.full_like(m_ref, NEG_INF)
        l_ref[...] = jnp.zeros_like(l_ref)
        acc_ref[...] = jnp.zeros_like(acc_ref)

    grp = range(C_GROUP)
    qts = [qt_ref[0, g * HEAD_DIM:(g + 1) * HEAD_DIM, :] for g in grp]
    nck = k_ref.shape[2] // kc

    def scores(c):
        kb = k_ref[0, 0, c * kc:(c + 1) * kc, :]
        return [_dot(kb, qts[g]) for g in grp]

    m = [m_ref[g] for g in grp]
    l = [l_ref[g] for g in grp]
    acc = [acc_ref[g] for g in grp]
    s_next = scores(0)
    for c in range(nck):
        s = s_next
        if c + 1 < nck:
            s_next = scores(c + 1)
        vt = vt_ref[0, :, c * kc:(c + 1) * kc]
        m_new = [jnp.maximum(m[g], jnp.max(s[g], axis=0, keepdims=True)) for g in grp]
        p = [jnp.exp2(s[g] - m_new[g]) for g in grp]
        pv = [_dot(vt, p[g].astype(BF16)) for g in grp]
        for g in grp:
            alpha = jnp.exp2(m[g] - m_new[g])
            l[g] = alpha * l[g] + jnp.sum(p[g], axis=0, keepdims=True)
            acc[g] = alpha * acc[g] + pv[g]
        m = m_new
    for g in grp:
        m_ref[g] = m[g]
        l_ref[g] = l[g]
        acc_ref[g] = acc[g]

    @pl.when(kt == pl.num_programs(3) - 1)
    def _():
        o_ref[0] = jnp.concatenate(
            [acc_ref[g] / l_ref[g] for g in grp], axis=0).astype(o_ref.dtype)


ATTN_MAX_KEY_TILE = 2816
ATTN_MAX_KEY_CHUNK = 256


def _key_tile(rows, limit=ATTN_MAX_KEY_TILE):
    return max(t for t in range(LANES, min(rows, limit) + 1, LANES) if rows % t == 0)


def _attention(qt, k, vt, ctx_len, tq, tk):
    bsz, _, rows = qt.shape
    seq = rows - ctx_len
    qoff = ctx_len // tq
    gw = C_GROUP * HEAD_DIM
    kern = functools.partial(_attn_kernel, kc=_key_tile(tk, ATTN_MAX_KEY_CHUNK))
    return pl.pallas_call(
        kern,
        grid=(bsz, C_KV_HEADS, seq // tq, rows // tk),
        in_specs=[pl.BlockSpec((1, gw, tq), lambda b, j, i, kk: (b, j, i + qoff)),
                  pl.BlockSpec((1, 1, tk, HEAD_DIM), lambda b, j, i, kk: (b, j, kk, 0)),
                  pl.BlockSpec((1, HEAD_DIM, tk), lambda b, j, i, kk: (b, j, kk))],
        out_specs=pl.BlockSpec((1, gw, tq), lambda b, j, i, kk: (b, j, i)),
        out_shape=jax.ShapeDtypeStruct((bsz, C_WIDTH, seq), BF16),
        scratch_shapes=[pltpu.VMEM((C_GROUP, 1, tq), F32),
                        pltpu.VMEM((C_GROUP, 1, tq), F32),
                        pltpu.VMEM((C_GROUP, HEAD_DIM, tq), F32)],
        compiler_params=_cparams(("parallel", "parallel", "parallel", "arbitrary")),
        name="attention",
    )(qt, k, vt)


def _conv_kernel(zc_ref, zp_ref, zn_ref, w_ref, b_ref, lg_ref, lb_ref, o_ref, ybuf, *, tm, halo):
    t = pl.program_id(1)
    nt = pl.num_programs(1)

    def glu(u):
        return u[:, :D_WIDTH] * jax.nn.sigmoid(u[:, D_WIDTH:])

    ybuf[0:halo, :] = jnp.where(t == 0, 0.0, glu(zp_ref[0]))
    ybuf[halo:halo + tm, :] = glu(zc_ref[0])
    ybuf[halo + tm:, :] = jnp.where(t == nt - 1, 0.0, glu(zn_ref[0]))
    acc = jnp.zeros((tm, D_WIDTH), F32)
    for j in range(D_CONV_WIDTH):
        off = halo - D_PAD + j
        acc = acc + w_ref[j:j + 1, :] * ybuf[off:off + tm, :]
    y = acc + b_ref[...]
    mu = jnp.mean(y, axis=-1, keepdims=True)
    yc = y - mu
    var = jnp.mean(yc * yc, axis=-1, keepdims=True)
    yn = yc * lax.rsqrt(var + LN_EPS) * lg_ref[...] + lb_ref[...]
    o_ref[0] = (yn * jax.nn.sigmoid(yn)).astype(o_ref.dtype)


def _conformer_conv(z, dw_w, dw_b, cn_g, cn_b, ctx_len, tm):
    bsz, rows, ncol = z.shape
    seq = rows - ctx_len
    halo = 2 * SUBLANES
    hb = tm // halo
    off = ctx_len // tm
    offh = ctx_len // halo
    nh = seq // halo
    kern = functools.partial(_conv_kernel, tm=tm, halo=halo)
    row = lambda w: pl.BlockSpec((1, w), lambda b, t: (0, 0))
    return pl.pallas_call(
        kern,
        grid=(bsz, seq // tm),
        in_specs=[pl.BlockSpec((1, tm, ncol), lambda b, t: (b, t + off, 0)),
                  pl.BlockSpec((1, halo, ncol),
                               lambda b, t: (b, offh + jnp.maximum(t * hb - 1, 0), 0)),
                  pl.BlockSpec((1, halo, ncol),
                               lambda b, t: (b, offh + jnp.minimum((t + 1) * hb, nh - 1), 0)),
                  pl.BlockSpec((D_CONV_WIDTH, D_WIDTH), lambda b, t: (0, 0)),
                  row(D_WIDTH), row(D_WIDTH), row(D_WIDTH)],
        out_specs=pl.BlockSpec((1, tm, D_WIDTH), lambda b, t: (b, t, 0)),
        out_shape=jax.ShapeDtypeStruct((bsz, seq, D_WIDTH), BF16),
        scratch_shapes=[pltpu.VMEM((tm + 2 * halo, D_WIDTH), F32)],
        compiler_params=_cparams(("parallel", "parallel")),
        name="conformer_conv",
    )(z, z, z, dw_w, dw_b.reshape(1, -1), cn_g.reshape(1, -1), cn_b.reshape(1, -1))


def _forward(x, c, ctx, c_ctx, l0, l1, norm_f):
    bsz, seq, d = x.shape
    ctx_len = ctx.shape[1]
    tm = min(256, ctx_len)
    ctx_tiles = ctx_len // tm
    xs = jnp.concatenate([ctx, x], axis=1)

    mod = _modulation(c, c_ctx, l0["mod_w"], l0["mod_b"])
    z, zf = _normmod_matmul(xs, l0["norm1"], mod, l0["w_in"], tm, ctx_tiles, A_COLS)
    r, v, g, bonus, kk, lwf, kdf, bf, lwr, kdr, br = _rwkv_prep(z, l0, tm, ctx_tiles)
    yf, yr = _rwkv_scan(r, v, kk, lwf, kdf, bf, lwr, kdr, br, ctx_len)
    o_rwkv = _rwkv_finish(yf, yr, bonus, g, l0["lnx_g"], l0["lnx_b"], tm)
    fcs = _fnet_chan(zf, tm)
    ts_c = min(256, ctx_len)
    ts_l = min(512, seq)
    f_ctx = _fnet_seq(fcs, 0, ctx_len, ts_c, ts_c)
    f_lat = _fnet_seq(fcs[:, ctx_len:], 0, seq, ts_l, ts_l)
    o_fnet = jnp.concatenate([f_ctx, f_lat], axis=1)
    xs = _proj_residual(o_rwkv, o_fnet, xs, mod, l0["w_out"], tm, 0, ctx_tiles)
    xs = _peer(xs, l0["norm2"], mod, l0, tm, 0, ctx_tiles, False, norm_f)

    mod = _modulation(c, c_ctx, l1["mod_w"], l1["mod_b"])
    z, zu = _normmod_matmul(xs, l1["norm1"], mod, l1["w_in"], tm, ctx_tiles,
                            C_WIDTH + 2 * KV_WIDTH)
    qt, k, vt = _attn_prep(z, l1["q_norm"], l1["k_norm"], ctx_len, tm)
    o_attn = _attention(qt, k, vt, ctx_len, tm, _key_tile(ctx_len + seq))
    o_conv = _conformer_conv(zu, l1["dw_w"], l1["dw_b"], l1["cn_g"], l1["cn_b"], ctx_len, tm)
    xl = _proj_residual(o_attn, o_conv, xs, mod, l1["w_out"], tm, ctx_tiles, ctx_tiles,
                        o1_transposed=True)
    return _peer(xl, l1["norm2"], mod, l1, tm, ctx_tiles, ctx_tiles, True, norm_f)


def kernel(x, c, ctx, c_ctx, l0_mod_w, l0_mod_b, l0_norm1, l0_w_in, l0_shift_prev, l0_shift_next, l0_w0, l0_w2, l0_a0, l0_a2, l0_g2, l0_k_k, l0_k_a, l0_r_k, l0_lnx_g, l0_lnx_b, l0_w_out, l0_norm2, l0_pq, l0_sk1, l0_sk2, l0_pu, l0_pv, l1_mod_w, l1_mod_b, l1_norm1, l1_w_in, l1_q_norm, l1_k_norm, l1_dw_w, l1_dw_b, l1_cn_g, l1_cn_b, l1_w_out, l1_norm2, l1_pq, l1_sk1, l1_sk2, l1_pu, l1_pv, norm_f):
    l0 = dict(mod_w=l0_mod_w, mod_b=l0_mod_b, norm1=l0_norm1, w_in=l0_w_in,
              shift_prev=l0_shift_prev, shift_next=l0_shift_next, w0=l0_w0, w2=l0_w2,
              a0=l0_a0, a2=l0_a2, g2=l0_g2, k_k=l0_k_k, k_a=l0_k_a, r_k=l0_r_k,
              lnx_g=l0_lnx_g, lnx_b=l0_lnx_b, w_out=l0_w_out, norm2=l0_norm2,
              pq=l0_pq, sk1=l0_sk1, sk2=l0_sk2, pu=l0_pu, pv=l0_pv)
    l1 = dict(mod_w=l1_mod_w, mod_b=l1_mod_b, norm1=l1_norm1, w_in=l1_w_in,
              q_norm=l1_q_norm, k_norm=l1_k_norm, dw_w=l1_dw_w, dw_b=l1_dw_b,
              cn_g=l1_cn_g, cn_b=l1_cn_b, w_out=l1_w_out, norm2=l1_norm2,
              pq=l1_pq, sk1=l1_sk1, sk2=l1_sk2, pu=l1_pu, pv=l1_pv)
    return _forward(x, c, ctx, c_ctx, l0, l1, norm_f)
```

```python
import functools
import math

import jax
import jax.numpy as jnp
import numpy as np
from jax import lax
from jax.experimental import pallas as pl
from jax.experimental.pallas import tpu as pltpu

F32 = jnp.float32
BF16 = jnp.bfloat16
HIGHEST = lax.Precision.HIGHEST

LANES = 128
SUBLANES = 8
VMEM_LIMIT_BYTES = 56 * 1024 * 1024

N_MOD = 6
NORM_EPS = 1e-6
LN_EPS = 1e-5
GRID_W = 64
HEAD_DIM = 64
A_WIDTH = 768
A_HEADS = A_WIDTH // HEAD_DIM
A_RANK_W = 64
A_RANK_A = 64
A_RANK_G = 128
A_LNX_EPS = 64e-5
A_COLS = 3 * A_WIDTH + 2 * A_RANK_W + 2 * A_RANK_A + A_RANK_G
B_WIDTH = 256
B_GROUP_DIM = 64
C_WIDTH = 768
C_HEADS = 12
C_KV_HEADS = 4
C_GROUP = C_HEADS // C_KV_HEADS
KV_WIDTH = C_KV_HEADS * HEAD_DIM
ROPE_AXIS_DIM = HEAD_DIM // 2
ROPE_THETA = 10000.0
ATTN_SCALE = HEAD_DIM ** -0.5
D_WIDTH = 256
D_CONV_WIDTH = 31
D_PAD = D_CONV_WIDTH // 2
PK_HEADS = 8
PK_DIM = 256
PK_HALF = 128
N_KEYS = 128
PK_TOPK = 16

SCAN_CHUNK = 64
NEG_INF = float("-inf")


def _cparams(semantics):
    return pltpu.CompilerParams(dimension_semantics=semantics,
                                vmem_limit_bytes=VMEM_LIMIT_BYTES)


def _split_bf16(x):
    hi = x.astype(BF16)
    lo = (x - hi.astype(F32)).astype(BF16)
    return hi, lo


def _dot(a, b):
    return jnp.dot(a, b, preferred_element_type=F32)


def _dot_hp(a, b):
    return _mm3(_split_bf16(a), _split_bf16(b))


def _dot_nt(a, b, precision=None):
    return lax.dot_general(a, b, (((1,), (1,)), ((), ())),
                           preferred_element_type=F32, precision=precision)


def _dot_tn(a, b, precision=None):
    return lax.dot_general(a, b, (((0,), (0,)), ((), ())),
                           preferred_element_type=F32, precision=precision)


_NN = (((1,), (0,)), ((), ()))
_NT = (((1,), (1,)), ((), ()))
_TN = (((0,), (0,)), ((), ()))


def _mm3(a, b, dims=_NN):
    (ah, al), (bh, bl) = a, b
    dg = lambda x, y: lax.dot_general(x, y, dims, preferred_element_type=F32)
    return dg(ah, bh) + (dg(ah, bl) + dg(al, bh))


def _segsum(x, ones_bd):
    hi, lo = _split_bf16(x)
    return _dot(hi, ones_bd) + _dot(lo, ones_bd)


def _block_ones(width, seg):
    r = np.arange(width) // seg
    return jnp.asarray((r[:, None] == r[None, :]).astype(np.float32), dtype=BF16)


def _mod_kernel(c_ref, w_ref, b_ref, o_ref):
    c = c_ref[...]
    s = c * jax.nn.sigmoid(c)
    o_ref[...] = _dot_hp(s, w_ref[...]) + b_ref[...]


def _modulation(c, c_ctx, mod_w, mod_b):
    bsz, d = c.shape
    rows = SUBLANES * pl.cdiv(bsz + 1, SUBLANES)
    cc = jnp.zeros((rows, d), F32).at[:bsz].set(c).at[bsz].set(c_ctx)
    n = mod_w.shape[1]
    tn = n // 4
    out = pl.pallas_call(
        _mod_kernel,
        grid=(n // tn,),
        in_specs=[pl.BlockSpec((rows, d), lambda j: (0, 0)),
                  pl.BlockSpec((d, tn), lambda j: (0, j)),
                  pl.BlockSpec((1, tn), lambda j: (0, j))],
        out_specs=pl.BlockSpec((rows, tn), lambda j: (0, j)),
        out_shape=jax.ShapeDtypeStruct((rows, n), F32),
        compiler_params=_cparams(("arbitrary",)),
        name="adaln_mod",
    )(cc, mod_w, mod_b.reshape(1, n))
    lat = out[:bsz].reshape(bsz, N_MOD, d)
    ctx = jnp.broadcast_to(out[bsz].reshape(1, N_MOD, d), (bsz, N_MOD, d))
    return jnp.stack([ctx, lat], axis=1)


def _norm_mod(x, g, mod, row):
    ms = jnp.mean(x * x, axis=-1, keepdims=True)
    y = x * lax.rsqrt(ms + NORM_EPS) * g
    return y * (1.0 + mod[row + 1:row + 2]) + mod[row:row + 1]


def _normmod_matmul_kernel(x_ref, g_ref, mod_ref, w_ref, o1_ref, o2_ref):
    h = _norm_mod(x_ref[0], g_ref[...], mod_ref[0, 0], 0)
    z = _dot(h.astype(BF16), w_ref[...])
    n1 = o1_ref.shape[2]
    o1_ref[0] = z[:, :n1]
    o2_ref[0] = z[:, n1:]


def _normmod_matmul(x, g, mod, w, tm, ctx_tiles, n1):
    bsz, rows, d = x.shape
    n = w.shape[1]
    widths = (n1, n - n1)
    return pl.pallas_call(
        _normmod_matmul_kernel,
        grid=(bsz, rows // tm),
        in_specs=[pl.BlockSpec((1, tm, d), lambda b, t: (b, t, 0)),
                  pl.BlockSpec((1, d), lambda b, t: (0, 0)),
                  pl.BlockSpec((1, 1, N_MOD, d),
                               lambda b, t: (b, jnp.where(t >= ctx_tiles, 1, 0), 0, 0)),
                  pl.BlockSpec((d, n), lambda b, t: (0, 0))],
        out_specs=[pl.BlockSpec((1, tm, wd), lambda b, t: (b, t, 0)) for wd in widths],
        out_shape=[jax.ShapeDtypeStruct((bsz, rows, wd), F32) for wd in widths],
        compiler_params=_cparams(("parallel", "parallel")),
        name="normmod_proj",
    )(x, g.reshape(1, d), mod, w.astype(BF16))


def _rwkv_prep_kernel(z_ref, zp_ref, zn_ref, mup_ref, mun_ref, w0_ref, w2_ref, a0_ref, a2_ref,
                      g2_ref, kk_ref, ka_ref, rk_ref, ones_ref,
                      r_o, v_o, g_o, bonus_o, kkn_o, lwf_o, kdf_o, bf_o, lwr_o, kdr_o, br_o,
                      *, tm, ctx_tiles):
    t = pl.program_id(1)
    nt = pl.num_programs(1)
    z = z_ref[0][:, :A_COLS]
    zp_row = zp_ref[0][SUBLANES - 1:SUBLANES, :A_COLS]
    zn_row = zn_ref[0][0:1, :A_COLS]
    first = jnp.logical_or(t == 0, t == ctx_tiles)
    last = jnp.logical_or(t == ctx_tiles - 1, t == nt - 1)
    zp_row = jnp.where(first, 0.0, zp_row)
    zn_row = jnp.where(last, 0.0, zn_row)
    ridx = lax.broadcasted_iota(jnp.int32, (tm, 1), 0)
    z_prev = jnp.where(ridx == 0, zp_row, pltpu.roll(z, 1, 0))
    z_next = jnp.where(ridx == tm - 1, zn_row, pltpu.roll(z, tm - 1, 0))
    zs = z + mup_ref[...] * (z_prev - z) + mun_ref[...] * (z_next - z)

    w = A_WIDTH
    r = zs[:, 0:w]
    k = zs[:, w:2 * w]
    v = zs[:, 2 * w:3 * w]
    o = 3 * w
    xw = zs[:, o:o + 2 * A_RANK_W]
    o += 2 * A_RANK_W
    xa = zs[:, o:o + 2 * A_RANK_A]
    o += 2 * A_RANK_A
    xg = zs[:, o:o + A_RANK_G]
    ones_bd = ones_ref[...]

    g_o[0] = _dot_hp(jax.nn.sigmoid(xg), g2_ref[...])
    kk = k * kk_ref[...]
    nrm = jnp.sqrt(_segsum(kk * kk, ones_bd))
    kk = kk / jnp.maximum(nrm, 1e-12)
    r_o[0] = r
    v_o[0] = v
    kkn_o[0] = kk
    bonus = jnp.zeros_like(r)
    tw = jnp.tanh(xw)
    for d, (lw_o, kd_o, b_o) in enumerate(((lwf_o, kdf_o, bf_o), (lwr_o, kdr_o, br_o))):
        wl = w0_ref[d:d + 1, :] + _dot_hp(tw[:, d * A_RANK_W:(d + 1) * A_RANK_W], w2_ref[d])
        w_log = -jax.nn.softplus(-wl) - 0.5
        lw_o[0] = -jnp.exp(w_log)
        a_gate = jax.nn.sigmoid(
            a0_ref[d:d + 1, :] + _dot_hp(xa[:, d * A_RANK_A:(d + 1) * A_RANK_A], a2_ref[d]))
        k_d = k * (1.0 + (a_gate - 1.0) * ka_ref[...])
        kd_o[0] = k_d
        b_o[0] = kk * a_gate
        bonus = bonus + _segsum(r * k_d * rk_ref[...], ones_bd) * v
    bonus_o[0] = bonus


def _rwkv_prep(z, p, tm, ctx_tiles):
    bsz, rows, ncol = z.shape
    w = A_WIDTH
    hb = tm // SUBLANES
    nblk8 = rows // SUBLANES
    row_spec = lambda width: pl.BlockSpec((1, width), lambda b, t: (0, 0))
    full = lambda shape: pl.BlockSpec(shape, lambda b, t: (0,) * len(shape))
    out_spec = pl.BlockSpec((1, tm, w), lambda b, t: (b, t, 0))
    out_shape = jax.ShapeDtypeStruct((bsz, rows, w), F32)
    kern = functools.partial(_rwkv_prep_kernel, tm=tm, ctx_tiles=ctx_tiles)
    return pl.pallas_call(
        kern,
        grid=(bsz, rows // tm),
        in_specs=[pl.BlockSpec((1, tm, ncol), lambda b, t: (b, t, 0)),
                  pl.BlockSpec((1, SUBLANES, ncol),
                               lambda b, t: (b, jnp.maximum(t * hb - 1, 0), 0)),
                  pl.BlockSpec((1, SUBLANES, ncol),
                               lambda b, t: (b, jnp.minimum((t + 1) * hb, nblk8 - 1), 0)),
                  row_spec(A_COLS), row_spec(A_COLS),
                  full((2, w)), full((2, A_RANK_W, w)), full((2, w)), full((2, A_RANK_A, w)),
                  full((A_RANK_G, w)), row_spec(w), row_spec(w), row_spec(w), full((w, w))],
        out_specs=[out_spec] * 11,
        out_shape=[out_shape] * 11,
        compiler_params=_cparams(("parallel", "parallel")),
        name="rwkv_prep",
    )(z, z, z, p["shift_prev"].reshape(1, -1), p["shift_next"].reshape(1, -1),
      p["w0"], p["w2"], p["a0"], p["a2"], p["g2"], p["k_k"].reshape(1, w),
      p["k_a"].reshape(1, w), p["r_k"].reshape(1, w), _block_ones(w, HEAD_DIM))


SCAN_HEADS = 12


def _scan_kernel(rf, vf, kkf, lwf, kdf, bf, rr, vr, kkr, lwr, kdr, br, yf_o, yr_o, s_ref, *, c):
    @pl.when(pl.program_id(2) == 0)
    def _():
        s_ref[...] = jnp.zeros_like(s_ref)

    ri = lax.broadcasted_iota(jnp.int32, (c, c), 0)
    ci = lax.broadcasted_iota(jnp.int32, (c, c), 1)
    ri2 = lax.broadcasted_iota(jnp.int32, (c, 2 * c), 0)
    ci2 = lax.broadcasted_iota(jnp.int32, (c, 2 * c), 1)
    ci2 = jnp.where(ci2 >= c, ci2 - c, ci2)
    dirs = ((rf, vf, kkf, lwf, kdf, bf, ci <= ri, ci < ri, ci2 <= ri2, c - 1),
            (rr, vr, kkr, lwr, kdr, br, ci >= ri, ci > ri, ci2 >= ri2, 0))
    lhs, rhs, vs, ss, pend, incl, strict = [], [], [], [], [], [], []
    for d, (r_, v_, kk_, lw_, kd_, b_, m_incl, m_strict, m_incl2, edge) in enumerate(dirs):
        lw = lw_[0]
        tri = m_incl.astype(F32).astype(BF16)
        lw_h = lw.astype(BF16)
        lw_m, lw_l = _split_bf16(lw - lw_h.astype(F32))
        lcum = _dot(tri, lw_h) + (_dot(tri, lw_m) + _dot(tri, lw_l))
        p_in = jnp.exp(lcum)
        p_inv = jnp.exp(-lcum)
        lhs_all = jnp.concatenate([-kk_[0] * jnp.exp(lcum - lw), r_[0] * p_in], axis=0)
        rhs_all = jnp.concatenate([b_[0] * p_inv, kd_[0] * p_inv], axis=0)
        v_all = v_[0]
        for h in range(SCAN_HEADS):
            sl = slice(h * HEAD_DIM, (h + 1) * HEAD_DIM)
            lhs.append(lhs_all[:, sl])
            rhs.append(rhs_all[:, sl])
            vs.append(v_all[:, sl])
            ss.append(s_ref[d, h])
            pend.append(p_in[edge:edge + 1, sl])
            incl.append(m_incl2)
            strict.append(m_strict)
    n = len(lhs)
    idx = range(n)
    lhs2 = [_split_bf16(lhs[i]) for i in idx]
    rhs2 = [_split_bf16(rhs[i]) for i in idx]
    v2 = [_split_bf16(vs[i]) for i in idx]
    s2 = [_split_bf16(ss[i]) for i in idx]
    g = [_mm3(lhs2[i], rhs2[i], _NT) for i in idx]
    xs = [_mm3(lhs2[i], s2[i], _NT) for i in idx]
    a_ak = [_split_bf16(jnp.where(strict[i], g[i][:c, c:], 0.0)) for i in idx]
    x = [xs[i][:c] + _mm3(a_ak[i], v2[i]) for i in idx]
    apow = [_split_bf16(jnp.where(strict[i], g[i][:c, :c], 0.0)) for i in idx]
    steps = int(math.log2(c))
    for k in range(steps):
        x2 = [_split_bf16(x[i]) for i in idx]
        x = [x[i] + _mm3(apow[i], x2[i]) for i in idx]
        if k + 1 < steps:
            apow = [_split_bf16(_mm3(apow[i], apow[i])) for i in idx]
    x2 = [_split_bf16(x[i]) for i in idx]
    sav = [tuple(jnp.concatenate([x2[i][t], v2[i][t]], axis=0) for t in range(2)) for i in idx]
    a_r = [_split_bf16(jnp.where(incl[i], g[i][c:], 0.0)) for i in idx]
    y = [xs[i][c:] + _mm3(a_r[i], sav[i]) for i in idx]
    s_new = [(ss[i] + _mm3(sav[i], rhs2[i], _TN)) * pend[i] for i in idx]
    for d, y_o in enumerate((yf_o, yr_o)):
        for h in range(SCAN_HEADS):
            s_ref[d, h] = s_new[d * SCAN_HEADS + h]
        y_o[0] = jnp.concatenate(y[d * SCAN_HEADS:(d + 1) * SCAN_HEADS], axis=1)


def _rwkv_scan(r, v, kk, lwf, kdf, bf, lwr, kdr, br, ctx_len):
    bsz, rows, w = r.shape
    c = SCAN_CHUNK
    nch = rows // c
    cch = ctx_len // c
    lanes = SCAN_HEADS * HEAD_DIM
    fwd = lambda b, hp, i: (b, i, hp)
    rev = lambda b, hp, i: (b, jnp.where(i < cch, cch - 1 - i, nch - 1 + cch - i), hp)
    blk = lambda im: pl.BlockSpec((1, c, lanes), im)
    kern = functools.partial(_scan_kernel, c=c)
    out_shape = jax.ShapeDtypeStruct((bsz, rows, w), F32)
    return pl.pallas_call(
        kern,
        grid=(bsz, w // lanes, nch),
        in_specs=[blk(fwd)] * 6 + [blk(rev)] * 6,
        out_specs=[blk(fwd), blk(rev)],
        out_shape=[out_shape, out_shape],
        scratch_shapes=[pltpu.VMEM((2, SCAN_HEADS, HEAD_DIM, HEAD_DIM), F32)],
        compiler_params=_cparams(("parallel", "parallel", "arbitrary")),
        name="rwkv_scan",
    )(r, v, kk, lwf, kdf, bf, r, v, kk, lwr, kdr, br)


def _rwkv_finish_kernel(yf, yr, bonus, g, lg, lb, ones_ref, o_ref):
    y = yf[0] + yr[0]
    ones_bd = ones_ref[...]
    inv = 1.0 / HEAD_DIM
    mu = _segsum(y, ones_bd) * inv
    yc = y - mu
    var = _segsum(yc * yc, ones_bd) * inv
    yn = yc * lax.rsqrt(var + A_LNX_EPS) * lg[...] + lb[...]
    o_ref[0] = ((yn + bonus[0]) * g[0]).astype(o_ref.dtype)


def _rwkv_finish(yf, yr, bonus, g, lnx_g, lnx_b, tm):
    bsz, rows, w = yf.shape
    blk = pl.BlockSpec((1, tm, w), lambda b, t: (b, t, 0))
    row = pl.BlockSpec((1, w), lambda b, t: (0, 0))
    return pl.pallas_call(
        _rwkv_finish_kernel,
        grid=(bsz, rows // tm),
        in_specs=[blk, blk, blk, blk, row, row, pl.BlockSpec((w, w), lambda b, t: (0, 0))],
        out_specs=blk,
        out_shape=jax.ShapeDtypeStruct((bsz, rows, w), BF16),
        compiler_params=_cparams(("parallel", "parallel")),
        name="rwkv_finish",
    )(yf, yr, bonus, g, lnx_g.reshape(1, w), lnx_b.reshape(1, w), _block_ones(w, HEAD_DIM))


def _fnet_chan_kernel(z_ref, m_ref, o_ref):
    o_ref[0] = _dot(z_ref[0].astype(BF16), m_ref[...]).astype(o_ref.dtype)


def _fnet_chan(z, tm):
    bsz, rows, ncol = z.shape
    j = np.arange(B_WIDTH)
    same = (j[:, None] // B_GROUP_DIM) == (j[None, :] // B_GROUP_DIM)
    ang = 2.0 * np.pi * ((j[:, None] % B_GROUP_DIM) * (j[None, :] % B_GROUP_DIM) % B_GROUP_DIM) / B_GROUP_DIM
    m = np.concatenate([np.where(same, np.cos(ang), 0.0), np.where(same, np.sin(ang), 0.0)], axis=1)
    return pl.pallas_call(
        _fnet_chan_kernel,
        grid=(bsz, rows // tm),
        in_specs=[pl.BlockSpec((1, tm, ncol), lambda b, t: (b, t, 0)),
                  pl.BlockSpec((B_WIDTH, 2 * B_WIDTH), lambda b, t: (0, 0))],
        out_specs=pl.BlockSpec((1, tm, 2 * B_WIDTH), lambda b, t: (b, t, 0)),
        out_shape=jax.ShapeDtypeStruct((bsz, rows, 2 * B_WIDTH), BF16),
        compiler_params=_cparams(("parallel", "parallel")),
        name="fnet_chan",
    )(z, jnp.asarray(m, dtype=BF16))


def _fnet_seq_kernel(fcs_ref, cb_ref, sb_ref, c0_ref, s0_ref, o_ref, acc_ref, *, scale, nb):
    k = pl.program_id(1)

    @pl.when(k == 0)
    def _():
        acc_ref[...] = jnp.zeros_like(acc_ref)

    cb = cb_ref[...]
    sb = sb_ref[...]
    c0 = c0_ref[0]
    s0 = s0_ref[0]
    cm = (c0 * cb - s0 * sb).astype(BF16)
    sm = (s0 * cb + c0 * sb).astype(BF16)
    for b in range(nb):
        fcs = fcs_ref[b]
        acc_ref[b] += _dot(cm, fcs[:, :B_WIDTH]) - _dot(sm, fcs[:, B_WIDTH:])

    @pl.when(k == pl.num_programs(1) - 1)
    def _():
        o_ref[...] = (acc_ref[...] * scale).astype(o_ref.dtype)


def _fnet_seq(fcs, row0, length, ts, tk):
    bsz = fcs.shape[0]
    ds = jnp.arange(ts, dtype=jnp.int32)[:, None]
    tt = jnp.arange(length, dtype=jnp.int32)[None, :]
    ang = ((ds * tt) % length).astype(F32) * (2.0 * math.pi / length)
    cb, sb = jnp.cos(ang), jnp.sin(ang)
    s0 = (jnp.arange(length // ts, dtype=jnp.int32) * ts)[:, None]
    ang0 = ((s0 * tt) % length).astype(F32) * (2.0 * math.pi / length)
    c0, sn0 = jnp.cos(ang0)[:, None, :], jnp.sin(ang0)[:, None, :]
    kern = functools.partial(_fnet_seq_kernel, scale=1.0 / math.sqrt(length * B_GROUP_DIM), nb=bsz)
    koff = row0 // tk
    return pl.pallas_call(
        kern,
        grid=(length // ts, length // tk),
        in_specs=[pl.BlockSpec((bsz, tk, 2 * B_WIDTH), lambda s, k: (0, k + koff, 0)),
                  pl.BlockSpec((ts, tk), lambda s, k: (0, k)),
                  pl.BlockSpec((ts, tk), lambda s, k: (0, k)),
                  pl.BlockSpec((1, 1, tk), lambda s, k: (s, 0, k)),
                  pl.BlockSpec((1, 1, tk), lambda s, k: (s, 0, k))],
        out_specs=pl.BlockSpec((bsz, ts, B_WIDTH), lambda s, k: (0, s, 0)),
        out_shape=jax.ShapeDtypeStruct((bsz, length, B_WIDTH), BF16),
        scratch_shapes=[pltpu.VMEM((bsz, ts, B_WIDTH), F32)],
        compiler_params=_cparams(("parallel", "arbitrary")),
        name="fnet_seq",
    )(fcs, cb, sb, c0, sn0)


def _proj_residual_kernel(o1_ref, o2_ref, x_ref, mod_ref, w1_ref, w2_ref, out_ref, *, o1_transposed):
    o1 = o1_ref[0]
    if o1_transposed:
        o1 = o1.astype(F32).T.astype(BF16)
    mix = _dot(o1, w1_ref[...]) + _dot(o2_ref[0], w2_ref[...])
    gate = mod_ref[0, 0][2:3]
    out_ref[0] = x_ref[0] + gate * mix


def _proj_residual(o1, o2, x, mod, w_out, tm, tile_off, ctx_tiles, o1_transposed=False):
    bsz, rows, w2 = o2.shape
    w1 = o1.shape[1] if o1_transposed else o1.shape[2]
    d = x.shape[2]
    o1_spec = (pl.BlockSpec((1, w1, tm), lambda b, t: (b, 0, t)) if o1_transposed
               else pl.BlockSpec((1, tm, w1), lambda b, t: (b, t, 0)))
    kern = functools.partial(_proj_residual_kernel, o1_transposed=o1_transposed)
    return pl.pallas_call(
        kern,
        grid=(bsz, rows // tm),
        in_specs=[o1_spec,
                  pl.BlockSpec((1, tm, w2), lambda b, t: (b, t, 0)),
                  pl.BlockSpec((1, tm, d), lambda b, t: (b, t + tile_off, 0)),
                  pl.BlockSpec((1, 1, N_MOD, d),
                               lambda b, t: (b, jnp.where(t + tile_off >= ctx_tiles, 1, 0), 0, 0)),
                  pl.BlockSpec((w1, d), lambda b, t: (0, 0)),
                  pl.BlockSpec((w2, d), lambda b, t: (0, 0))],
        out_specs=pl.BlockSpec((1, tm, d), lambda b, t: (b, t, 0)),
        out_shape=jax.ShapeDtypeStruct((bsz, rows, d), F32),
        compiler_params=_cparams(("parallel", "parallel")),
        name="proj_residual",
    )(o1, o2, x, mod, w_out[:w1].astype(BF16), w_out[w1:].astype(BF16))


def _merge_exchange_network(n):
    pairs = []
    p = 1
    while p < n:
        k = p
        while k >= 1:
            for j in range(k % p, n - k, 2 * k):
                for i in range(min(k, n - j - k)):
                    if (i + j) // (2 * p) == (i + j + k) // (2 * p):
                        pairs.append((i + j, i + j + k))
            k //= 2
        p *= 2
    return pairs


def _top16_rows(s):
    nslab = s.shape[0] // SUBLANES
    slabs = [s[k * SUBLANES:(k + 1) * SUBLANES] for k in range(nslab)]
    for a, b in _merge_exchange_network(nslab):
        slabs[a], slabs[b] = jnp.maximum(slabs[a], slabs[b]), jnp.minimum(slabs[a], slabs[b])
    rows = []
    for t in range(PK_TOPK):
        m = jnp.max(slabs[0], axis=0, keepdims=True)
        rows.append(m)
        hit = slabs[0] >= m
        for k in range(nslab - 1 - t):
            slabs[k] = jnp.where(hit, slabs[k + 1], slabs[k])
    return rows


_PAIR_IDX = [(i, j) for i in range(PK_TOPK) for j in range(PK_TOPK) if (i + 1) * (j + 1) <= PK_TOPK]
BF16_SUBLANES = 2 * SUBLANES
PEER_SUB_E1 = 1
PEER_STEP_E1 = 16


def _rows_bf16(row):
    packed = jnp.broadcast_to(row, (BF16_SUBLANES, row.shape[1])).astype(BF16)
    return jnp.tile(packed, (N_KEYS // BF16_SUBLANES, 1))


def _peer_kernel(x_ref, g_ref, mod_ref, pqh_ref, pql_ref, sk1h_ref, sk1l_ref, sk2h_ref, sk2l_ref,
                 pu_ref, pvt_ref, gf_ref,
                 out_ref, ht_ref, r2_ref, e2_ref, c1_ref, w1_ref, s_ref, cand_ref, acc_ref,
                 *, tm, e1_per_blk, final_norm):
    eb = pl.program_id(2)

    @pl.when(eb == 0)
    def _():
        h = _norm_mod(x_ref[0], g_ref[...], mod_ref[0, 0], 3)
        ht = h.T
        hhi, hlo = _split_bf16(ht)
        ht_ref[...] = hhi
        qt = _mm3((pqh_ref[...], pql_ref[...]), (hhi, hlo))
        qs = [_split_bf16(qt[i * PK_HALF:(i + 1) * PK_HALF]) for i in range(2 * PK_HEADS)]
        for hd in range(PK_HEADS):
            s_ref[2 * hd] = _mm3((sk1h_ref[hd], sk1l_ref[hd]), qs[2 * hd])
            s_ref[2 * hd + 1] = _mm3((sk2h_ref[hd], sk2l_ref[hd]), qs[2 * hd + 1])
        for hd in range(PK_HEADS):
            s1 = s_ref[2 * hd]
            s2 = s_ref[2 * hd + 1]
            v1 = _top16_rows(s1)
            v2 = _top16_rows(s2)
            cand_ref[...] = jnp.full_like(cand_ref, NEG_INF)
            for n, (i, j) in enumerate(_PAIR_IDX):
                cand_ref[n:n + 1, :] = v1[i] + v2[j]
            cand = cand_ref[...]
            top = v1[0] + v2[0]
            zsum = jnp.zeros_like(top)
            tau = top
            for _ in range(PK_TOPK):
                m = jnp.max(cand, axis=0, keepdims=True)
                zsum = zsum + jnp.exp(m - top)
                tau = m
                cand = jnp.where(cand >= m, NEG_INF, cand)
            rank2 = jnp.full((N_KEYS, tm), float(PK_TOPK), F32)
            for j in reversed(range(PK_TOPK)):
                rank2 = jnp.where(s2 >= v2[j], float(j), rank2)
            count1 = jnp.zeros((N_KEYS, tm), F32)
            for i in reversed(range(PK_TOPK)):
                cnt = jnp.zeros_like(top)
                for j in range(PK_TOPK // (i + 1)):
                    cnt = cnt + jnp.where(v1[i] + v2[j] >= tau, 1.0, 0.0)
                count1 = jnp.where(s1 >= v1[i], cnt, count1)
            r2_ref[hd] = rank2.astype(BF16)
            c1_ref[hd] = count1
            e2_ref[hd] = jnp.exp(s2 - v2[0]).astype(BF16)
            w1_ref[hd] = jnp.exp(s1 - v1[0]) / zsum
        acc_ref[...] = jnp.zeros_like(acc_ref)

    sub = PEER_SUB_E1 * N_KEYS
    nsub = e1_per_blk // PEER_SUB_E1
    ht = ht_ref[...]

    def up(i):
        return _dot(pu_ref[i * sub:(i + 1) * sub, :], ht)

    def activation(i, hu):
        acts = []
        for jj in range(PEER_SUB_E1):
            e1 = eb * e1_per_blk + PEER_SUB_E1 * i + jj
            gate = jnp.zeros((N_KEYS, tm), BF16)
            for hd in range(PK_HEADS):
                count = _rows_bf16(c1_ref[hd, pl.ds(e1, 1), :])
                weight = _rows_bf16(w1_ref[hd, pl.ds(e1, 1), :])
                gate = gate + jnp.where(r2_ref[hd] < count, e2_ref[hd] * weight, 0)
            u = hu[jj * N_KEYS:(jj + 1) * N_KEYS]
            act = 0.5 * u * (1.0 + lax.erf(u * (1.0 / math.sqrt(2.0))))
            acts.append(act.astype(BF16) * gate)
        return jnp.concatenate(acts, axis=0)

    hus = [up(i) for i in range(nsub)]
    acts = [activation(i, hus[i]) for i in range(nsub)]
    acc_ref[...] += _dot(pvt_ref[...], jnp.concatenate(acts, axis=0))

    @pl.when(eb == pl.num_programs(2) - 1)
    def _():
        y = x_ref[0] + mod_ref[0, 0][5:6] * acc_ref[...].T
        if final_norm:
            ms = jnp.mean(y * y, axis=-1, keepdims=True)
            y = y * lax.rsqrt(ms + NORM_EPS) * gf_ref[...]
        out_ref[0] = y


def _peer(x, g, mod, p, tm, tile_off, ctx_tiles, final_norm, norm_f):
    bsz, rows, d = x.shape
    n_exp = p["pu"].shape[0]
    e1_per_blk = PEER_STEP_E1
    eblk = e1_per_blk * N_KEYS
    pqt = p["pq"].T
    pqh = pqt.astype(BF16)
    pql = (pqt - pqh.astype(F32)).astype(BF16)
    nq = pqt.shape[0]
    kern = functools.partial(_peer_kernel, tm=tm, e1_per_blk=e1_per_blk, final_norm=final_norm)
    const = lambda shape: pl.BlockSpec(shape, lambda b, t, e: (0,) * len(shape))
    return pl.pallas_call(
        kern,
        grid=(bsz, rows // tm, n_exp // eblk),
        in_specs=[pl.BlockSpec((1, tm, d), lambda b, t, e: (b, t, 0)),
                  const((1, d)),
                  pl.BlockSpec((1, 1, N_MOD, d),
                               lambda b, t, e: (b, jnp.where(t + tile_off >= ctx_tiles, 1, 0), 0, 0)),
                  const((nq, d)), const((nq, d)),
                  const((PK_HEADS, N_KEYS, PK_HALF)), const((PK_HEADS, N_KEYS, PK_HALF)),
                  const((PK_HEADS, N_KEYS, PK_HALF)), const((PK_HEADS, N_KEYS, PK_HALF)),
                  pl.BlockSpec((eblk, d), lambda b, t, e: (e, 0)),
                  pl.BlockSpec((d, eblk), lambda b, t, e: (0, e)),
                  const((1, d))],
        out_specs=pl.BlockSpec((1, tm, d), lambda b, t, e: (b, t, 0)),
        out_shape=jax.ShapeDtypeStruct((bsz, rows, d), F32),
        scratch_shapes=[pltpu.VMEM((d, tm), BF16),
                        pltpu.VMEM((PK_HEADS, N_KEYS, tm), BF16),
                        pltpu.VMEM((PK_HEADS, N_KEYS, tm), BF16),
                        pltpu.VMEM((PK_HEADS, N_KEYS, tm), F32),
                        pltpu.VMEM((PK_HEADS, N_KEYS, tm), F32),
                        pltpu.VMEM((2 * PK_HEADS, N_KEYS, tm), F32),
                        pltpu.VMEM((SUBLANES * pl.cdiv(len(_PAIR_IDX), SUBLANES), tm), F32),
                        pltpu.VMEM((d, tm), F32)],
        compiler_params=_cparams(("parallel", "parallel", "arbitrary")),
        name="peer",
    )(x, g.reshape(1, d), mod, pqh, pql, *_split_bf16(p["sk1"]), *_split_bf16(p["sk2"]),
      p["pu"].astype(BF16), p["pv"].T.astype(BF16), norm_f.reshape(1, d))


def _rope_tables(rows, ctx_len):
    t = jnp.arange(rows - ctx_len, dtype=jnp.int32)
    inv = ROPE_THETA ** (-jnp.arange(0, ROPE_AXIS_DIM, 2, dtype=F32) / ROPE_AXIS_DIM)
    ang_r = (t // GRID_W).astype(F32)[:, None] * inv
    ang_c = (t % GRID_W).astype(F32)[:, None] * inv
    cos = jnp.concatenate([jnp.cos(ang_r)] * 2 + [jnp.cos(ang_c)] * 2, axis=1)
    sin = jnp.concatenate([-jnp.sin(ang_r), jnp.sin(ang_r), -jnp.sin(ang_c), jnp.sin(ang_c)], axis=1)
    cos = jnp.concatenate([jnp.ones((ctx_len, HEAD_DIM), F32), cos], axis=0)
    sin = jnp.concatenate([jnp.zeros((ctx_len, HEAD_DIM), F32), sin], axis=0)
    return jnp.tile(cos, (1, 2)), jnp.tile(sin, (1, 2))


def _head_norm_rope(x, gain, ones_bd, cos, sin):
    ms = _segsum(x * x, ones_bd) * (1.0 / HEAD_DIM)
    y = x * lax.rsqrt(ms + NORM_EPS) * gain
    outs = []
    half = ROPE_AXIS_DIM // 2
    lane = lax.broadcasted_iota(jnp.int32, (1, LANES), 1)
    first_half = (lane % ROPE_AXIS_DIM) < half
    for i in range(x.shape[1] // LANES):
        yc = y[:, i * LANES:(i + 1) * LANES]
        partner = jnp.where(first_half, pltpu.roll(yc, LANES - half, 1), pltpu.roll(yc, half, 1))
        outs.append(yc * cos + partner * sin)
    return jnp.concatenate(outs, axis=1)


def _attn_prep_kernel(z_ref, qn_ref, kn_ref, cos_ref, sin_ref, onesq_ref, onesk_ref,
                      q_o, k_o, v_o):
    z = z_ref[0]
    cos = cos_ref[...]
    sin = sin_ref[...]
    q = _head_norm_rope(z[:, :C_WIDTH], qn_ref[...], onesq_ref[...], cos, sin)
    k = _head_norm_rope(z[:, C_WIDTH:C_WIDTH + KV_WIDTH], kn_ref[...], onesk_ref[...], cos, sin)
    q_o[0] = (q * (ATTN_SCALE * math.log2(math.e))).T.astype(q_o.dtype)
    for j in range(C_KV_HEADS):
        k_o[0, j] = k[:, j * HEAD_DIM:(j + 1) * HEAD_DIM].astype(k_o.dtype)
    v_o[0] = z[:, C_WIDTH + KV_WIDTH:C_WIDTH + 2 * KV_WIDTH].T.astype(v_o.dtype)


def _attn_prep(z, q_norm, k_norm, ctx_len, tm):
    bsz, rows, ncol = z.shape
    cos, sin = _rope_tables(rows, ctx_len)
    qn = jnp.tile(q_norm, C_HEADS).reshape(1, C_WIDTH)
    kn = jnp.tile(k_norm, C_KV_HEADS).reshape(1, KV_WIDTH)
    const = lambda shape: pl.BlockSpec(shape, lambda b, t: (0,) * len(shape))
    return pl.pallas_call(
        _attn_prep_kernel,
        grid=(bsz, rows // tm),
        in_specs=[pl.BlockSpec((1, tm, ncol), lambda b, t: (b, t, 0)),
                  const((1, C_WIDTH)), const((1, KV_WIDTH)),
                  pl.BlockSpec((tm, LANES), lambda b, t: (t, 0)),
                  pl.BlockSpec((tm, LANES), lambda b, t: (t, 0)),
                  const((C_WIDTH, C_WIDTH)), const((KV_WIDTH, KV_WIDTH))],
        out_specs=[pl.BlockSpec((1, C_WIDTH, tm), lambda b, t: (b, 0, t)),
                   pl.BlockSpec((1, C_KV_HEADS, tm, HEAD_DIM), lambda b, t: (b, 0, t, 0)),
                   pl.BlockSpec((1, KV_WIDTH, tm), lambda b, t: (b, 0, t))],
        out_shape=[jax.ShapeDtypeStruct((bsz, C_WIDTH, rows), BF16),
                   jax.ShapeDtypeStruct((bsz, C_KV_HEADS, rows, HEAD_DIM), BF16),
                   jax.ShapeDtypeStruct((bsz, KV_WIDTH, rows), BF16)],
        compiler_params=_cparams(("parallel", "parallel")),
        name="attn_prep",
    )(z, qn, kn, cos, sin, _block_ones(C_WIDTH, HEAD_DIM), _block_ones(KV_WIDTH, HEAD_DIM))


def _attn_kernel(qt_ref, k_ref, vt_ref, o_ref, m_ref, l_ref, acc_ref, *, kc):
    kt = pl.program_id(3)

    @pl.when(kt == 0)
    def _():
        m_ref[...] = jnp.full_like(m_ref, NEG_INF)
        l_ref[...] = jnp.zeros_like(l_ref)
        acc_ref[...] = jnp.zeros_like(acc_ref)

    grp = range(C_GROUP)
    qts = [qt_ref[0, g * HEAD_DIM:(g + 1) * HEAD_DIM, :] for g in grp]
    nck = k_ref.shape[2] // kc

    def scores(c):
        kb = k_ref[0, 0, c * kc:(c + 1) * kc, :]
        return [_dot(kb, qts[g]) for g in grp]

    m = [m_ref[g] for g in grp]
    l = [l_ref[g] for g in grp]
    acc = [acc_ref[g] for g in grp]
    s_next = scores(0)
    for c in range(nck):
        s = s_next
        if c + 1 < nck:
            s_next = scores(c + 1)
        vt = vt_ref[0, :, c * kc:(c + 1) * kc]
        m_new = [jnp.maximum(m[g], jnp.max(s[g], axis=0, keepdims=True)) for g in grp]
        p = [jnp.exp2(s[g] - m_new[g]) for g in grp]
        pv = [_dot(vt, p[g].astype(BF16)) for g in grp]
        for g in grp:
            alpha = jnp.exp2(m[g] - m_new[g])
            l[g] = alpha * l[g] + jnp.sum(p[g], axis=0, keepdims=True)
            acc[g] = alpha * acc[g] + pv[g]
        m = m_new
    for g in grp:
        m_ref[g] = m[g]
        l_ref[g] = l[g]
        acc_ref[g] = acc[g]

    @pl.when(kt == pl.num_programs(3) - 1)
    def _():
        o_ref[0] = jnp.concatenate(
            [acc_ref[g] / l_ref[g] for g in grp], axis=0).astype(o_ref.dtype)


ATTN_MAX_KEY_TILE = 2816
ATTN_MAX_KEY_CHUNK = 256


def _key_tile(rows, limit=ATTN_MAX_KEY_TILE):
    return max(t for t in range(LANES, min(rows, limit) + 1, LANES) if rows % t == 0)


def _attention(qt, k, vt, ctx_len, tq, tk):
    bsz, _, rows = qt.shape
    seq = rows - ctx_len
    qoff = ctx_len // tq
    gw = C_GROUP * HEAD_DIM
    kern = functools.partial(_attn_kernel, kc=_key_tile(tk, ATTN_MAX_KEY_CHUNK))
    return pl.pallas_call(
        kern,
        grid=(bsz, C_KV_HEADS, seq // tq, rows // tk),
        in_specs=[pl.BlockSpec((1, gw, tq), lambda b, j, i, kk: (b, j, i + qoff)),
                  pl.BlockSpec((1, 1, tk, HEAD_DIM), lambda b, j, i, kk: (b, j, kk, 0)),
                  pl.BlockSpec((1, HEAD_DIM, tk), lambda b, j, i, kk: (b, j, kk))],
        out_specs=pl.BlockSpec((1, gw, tq), lambda b, j, i, kk: (b, j, i)),
        out_shape=jax.ShapeDtypeStruct((bsz, C_WIDTH, seq), BF16),
        scratch_shapes=[pltpu.VMEM((C_GROUP, 1, tq), F32),
                        pltpu.VMEM((C_GROUP, 1, tq), F32),
                        pltpu.VMEM((C_GROUP, HEAD_DIM, tq), F32)],
        compiler_params=_cparams(("parallel", "parallel", "parallel", "arbitrary")),
        name="attention",
    )(qt, k, vt)


def _conv_kernel(zc_ref, zp_ref, zn_ref, w_ref, b_ref, lg_ref, lb_ref, o_ref, ybuf, *, tm, halo):
    t = pl.program_id(1)
    nt = pl.num_programs(1)

    def glu(u):
        return u[:, :D_WIDTH] * jax.nn.sigmoid(u[:, D_WIDTH:])

    ybuf[0:halo, :] = jnp.where(t == 0, 0.0, glu(zp_ref[0]))
    ybuf[halo:halo + tm, :] = glu(zc_ref[0])
    ybuf[halo + tm:, :] = jnp.where(t == nt - 1, 0.0, glu(zn_ref[0]))
    acc = jnp.zeros((tm, D_WIDTH), F32)
    for j in range(D_CONV_WIDTH):
        off = halo - D_PAD + j
        acc = acc + w_ref[j:j + 1, :] * ybuf[off:off + tm, :]
    y = acc + b_ref[...]
    mu = jnp.mean(y, axis=-1, keepdims=True)
    yc = y - mu
    var = jnp.mean(yc * yc, axis=-1, keepdims=True)
    yn = yc * lax.rsqrt(var + LN_EPS) * lg_ref[...] + lb_ref[...]
    o_ref[0] = (yn * jax.nn.sigmoid(yn)).astype(o_ref.dtype)


def _conformer_conv(z, dw_w, dw_b, cn_g, cn_b, ctx_len, tm):
    bsz, rows, ncol = z.shape
    seq = rows - ctx_len
    halo = 2 * SUBLANES
    hb = tm // halo
    off = ctx_len // tm
    offh = ctx_len // halo
    nh = seq // halo
    kern = functools.partial(_conv_kernel, tm=tm, halo=halo)
    row = lambda w: pl.BlockSpec((1, w), lambda b, t: (0, 0))
    return pl.pallas_call(
        kern,
        grid=(bsz, seq // tm),
        in_specs=[pl.BlockSpec((1, tm, ncol), lambda b, t: (b, t + off, 0)),
                  pl.BlockSpec((1, halo, ncol),
                               lambda b, t: (b, offh + jnp.maximum(t * hb - 1, 0), 0)),
                  pl.BlockSpec((1, halo, ncol),
                               lambda b, t: (b, offh + jnp.minimum((t + 1) * hb, nh - 1), 0)),
                  pl.BlockSpec((D_CONV_WIDTH, D_WIDTH), lambda b, t: (0, 0)),
                  row(D_WIDTH), row(D_WIDTH), row(D_WIDTH)],
        out_specs=pl.BlockSpec((1, tm, D_WIDTH), lambda b, t: (b, t, 0)),
        out_shape=jax.ShapeDtypeStruct((bsz, seq, D_WIDTH), BF16),
        scratch_shapes=[pltpu.VMEM((tm + 2 * halo, D_WIDTH), F32)],
        compiler_params=_cparams(("parallel", "parallel")),
        name="conformer_conv",
    )(z, z, z, dw_w, dw_b.reshape(1, -1), cn_g.reshape(1, -1), cn_b.reshape(1, -1))


def _forward(x, c, ctx, c_ctx, l0, l1, norm_f):
    bsz, seq, d = x.shape
    ctx_len = ctx.shape[1]
    tm = min(256, ctx_len)
    ctx_tiles = ctx_len // tm
    xs = jnp.concatenate([ctx, x], axis=1)

    mod = _modulation(c, c_ctx, l0["mod_w"], l0["mod_b"])
    z, zf = _normmod_matmul(xs, l0["norm1"], mod, l0["w_in"], tm, ctx_tiles, A_COLS)
    r, v, g, bonus, kk, lwf, kdf, bf, lwr, kdr, br = _rwkv_prep(z, l0, tm, ctx_tiles)
    yf, yr = _rwkv_scan(r, v, kk, lwf, kdf, bf, lwr, kdr, br, ctx_len)
    o_rwkv = _rwkv_finish(yf, yr, bonus, g, l0["lnx_g"], l0["lnx_b"], tm)
    fcs = _fnet_chan(zf, tm)
    ts_c = min(256, ctx_len)
    ts_l = min(512, seq)
    f_ctx = _fnet_seq(fcs, 0, ctx_len, ts_c, ts_c)
    f_lat = _fnet_seq(fcs[:, ctx_len:], 0, seq, ts_l, ts_l)
    o_fnet = jnp.concatenate([f_ctx, f_lat], axis=1)
    xs = _proj_residual(o_rwkv, o_fnet, xs, mod, l0["w_out"], tm, 0, ctx_tiles)
    xs = _peer(xs, l0["norm2"], mod, l0, tm, 0, ctx_tiles, False, norm_f)

    mod = _modulation(c, c_ctx, l1["mod_w"], l1["mod_b"])
    z, zu = _normmod_matmul(xs, l1["norm1"], mod, l1["w_in"], tm, ctx_tiles,
                            C_WIDTH + 2 * KV_WIDTH)
    qt, k, vt = _attn_prep(z, l1["q_norm"], l1["k_norm"], ctx_len, tm)
    o_attn = _attention(qt, k, vt, ctx_len, tm, _key_tile(ctx_len + seq))
    o_conv = _conformer_conv(zu, l1["dw_w"], l1["dw_b"], l1["cn_g"], l1["cn_b"], ctx_len, tm)
    xl = _proj_residual(o_attn, o_conv, xs, mod, l1["w_out"], tm, ctx_tiles, ctx_tiles,
                        o1_transposed=True)
    return _peer(xl, l1["norm2"], mod, l1, tm, ctx_tiles, ctx_tiles, True, norm_f)


def kernel(x, c, ctx, c_ctx, l0_mod_w, l0_mod_b, l0_norm1, l0_w_in, l0_shift_prev, l0_shift_next, l0_w0, l0_w2, l0_a0, l0_a2, l0_g2, l0_k_k, l0_k_a, l0_r_k, l0_lnx_g, l0_lnx_b, l0_w_out, l0_norm2, l0_pq, l0_sk1, l0_sk2, l0_pu, l0_pv, l1_mod_w, l1_mod_b, l1_norm1, l1_w_in, l1_q_norm, l1_k_norm, l1_dw_w, l1_dw_b, l1_cn_g, l1_cn_b, l1_w_out, l1_norm2, l1_pq, l1_sk1, l1_sk2, l1_pu, l1_pv, norm_f):
    l0 = dict(mod_w=l0_mod_w, mod_b=l0_mod_b, norm1=l0_norm1, w_in=l0_w_in,
              shift_prev=l0_shift_prev, shift_next=l0_shift_next, w0=l0_w0, w2=l0_w2,
              a0=l0_a0, a2=l0_a2, g2=l0_g2, k_k=l0_k_k, k_a=l0_k_a, r_k=l0_r_k,
              lnx_g=l0_lnx_g, lnx_b=l0_lnx_b, w_out=l0_w_out, norm2=l0_norm2,
              pq=l0_pq, sk1=l0_sk1, sk2=l0_sk2, pu=l0_pu, pv=l0_pv)
    l1 = dict(mod_w=l1_mod_w, mod_b=l1_mod_b, norm1=l1_norm1, w_in=l1_w_in,
              q_norm=l1_q_norm, k_norm=l1_k_norm, dw_w=l1_dw_w, dw_b=l1_dw_b,
              cn_g=l1_cn_g, cn_b=l1_cn_b, w_out=l1_w_out, norm2=l1_norm2,
              pq=l1_pq, sk1=l1_sk1, sk2=l1_sk2, pu=l1_pu, pv=l1_pv)
    return _forward(x, c, ctx, c_ctx, l0, l1, norm_f)
```

```python
import functools
import math

import jax
import jax.numpy as jnp
import numpy as np
from jax import lax
from jax.experimental import pallas as pl
from jax.experimental.pallas import tpu as pltpu

F32 = jnp.float32
BF16 = jnp.bfloat16
HIGHEST = lax.Precision.HIGHEST

LANES = 128
SUBLANES = 8
VMEM_LIMIT_BYTES = 56 * 1024 * 1024

N_MOD = 6
NORM_EPS = 1e-6
LN_EPS = 1e-5
GRID_W = 64
HEAD_DIM = 64
A_WIDTH = 768
A_HEADS = A_WIDTH // HEAD_DIM
A_RANK_W = 64
A_RANK_A = 64
A_RANK_G = 128
A_LNX_EPS = 64e-5
A_COLS = 3 * A_WIDTH + 2 * A_RANK_W + 2 * A_RANK_A + A_RANK_G
B_WIDTH = 256
B_GROUP_DIM = 64
C_WIDTH = 768
C_HEADS = 12
C_KV_HEADS = 4
C_GROUP = C_HEADS // C_KV_HEADS
KV_WIDTH = C_KV_HEADS * HEAD_DIM
ROPE_AXIS_DIM = HEAD_DIM // 2
ROPE_THETA = 10000.0
ATTN_SCALE = HEAD_DIM ** -0.5
D_WIDTH = 256
D_CONV_WIDTH = 31
D_PAD = D_CONV_WIDTH // 2
PK_HEADS = 8
PK_DIM = 256
PK_HALF = 128
N_KEYS = 128
PK_TOPK = 16

SCAN_CHUNK = 64
NEG_INF = float("-inf")


def _cparams(semantics):
    return pltpu.CompilerParams(dimension_semantics=semantics,
                                vmem_limit_bytes=VMEM_LIMIT_BYTES)


def _split_bf16(x):
    hi = x.astype(BF16)
    lo = (x - hi.astype(F32)).astype(BF16)
    return hi, lo


def _dot(a, b):
    return jnp.dot(a, b, preferred_element_type=F32)


def _dot_hp(a, b):
    return _mm3(_split_bf16(a), _split_bf16(b))


def _dot_nt(a, b, precision=None):
    return lax.dot_general(a, b, (((1,), (1,)), ((), ())),
                           preferred_element_type=F32, precision=precision)


def _dot_tn(a, b, precision=None):
    return lax.dot_general(a, b, (((0,), (0,)), ((), ())),
                           preferred_element_type=F32, precision=precision)


_NN = (((1,), (0,)), ((), ()))
_NT = (((1,), (1,)), ((), ()))
_TN = (((0,), (0,)), ((), ()))


def _mm3(a, b, dims=_NN):
    (ah, al), (bh, bl) = a, b
    dg = lambda x, y: lax.dot_general(x, y, dims, preferred_element_type=F32)
    return dg(ah, bh) + (dg(ah, bl) + dg(al, bh))


def _segsum(x, ones_bd):
    hi, lo = _split_bf16(x)
    return _dot(hi, ones_bd) + _dot(lo, ones_bd)


def _block_ones(width, seg):
    r = np.arange(width) // seg
    return jnp.asarray((r[:, None] == r[None, :]).astype(np.float32), dtype=BF16)


def _mod_kernel(c_ref, w_ref, b_ref, o_ref):
    c = c_ref[...]
    s = c * jax.nn.sigmoid(c)
    o_ref[...] = _dot_hp(s, w_ref[...]) + b_ref[...]


def _modulation(c, c_ctx, mod_w, mod_b):
    bsz, d = c.shape
    rows = SUBLANES * pl.cdiv(bsz + 1, SUBLANES)
    cc = jnp.zeros((rows, d), F32).at[:bsz].set(c).at[bsz].set(c_ctx)
    n = mod_w.shape[1]
    tn = n // 4
    out = pl.pallas_call(
        _mod_kernel,
        grid=(n // tn,),
        in_specs=[pl.BlockSpec((rows, d), lambda j: (0, 0)),
                  pl.BlockSpec((d, tn), lambda j: (0, j)),
                  pl.BlockSpec((1, tn), lambda j: (0, j))],
        out_specs=pl.BlockSpec((rows, tn), lambda j: (0, j)),
        out_shape=jax.ShapeDtypeStruct((rows, n), F32),
        compiler_params=_cparams(("arbitrary",)),
        name="adaln_mod",
    )(cc, mod_w, mod_b.reshape(1, n))
    lat = out[:bsz].reshape(bsz, N_MOD, d)
    ctx = jnp.broadcast_to(out[bsz].reshape(1, N_MOD, d), (bsz, N_MOD, d))
    return jnp.stack([ctx, lat], axis=1)


def _norm_mod(x, g, mod, row):
    ms = jnp.mean(x * x, axis=-1, keepdims=True)
    y = x * lax.rsqrt(ms + NORM_EPS) * g
    return y * (1.0 + mod[row + 1:row + 2]) + mod[row:row + 1]


def _normmod_matmul_kernel(x_ref, g_ref, mod_ref, w_ref, o1_ref, o2_ref):
    h = _norm_mod(x_ref[0], g_ref[...], mod_ref[0, 0], 0)
    z = _dot(h.astype(BF16), w_ref[...])
    n1 = o1_ref.shape[2]
    o1_ref[0] = z[:, :n1]
    o2_ref[0] = z[:, n1:]


def _normmod_matmul(x, g, mod, w, tm, ctx_tiles, n1):
    bsz, rows, d = x.shape
    n = w.shape[1]
    widths = (n1, n - n1)
    return pl.pallas_call(
        _normmod_matmul_kernel,
        grid=(bsz, rows // tm),
        in_specs=[pl.BlockSpec((1, tm, d), lambda b, t: (b, t, 0)),
                  pl.BlockSpec((1, d), lambda b, t: (0, 0)),
                  pl.BlockSpec((1, 1, N_MOD, d),
                               lambda b, t: (b, jnp.where(t >= ctx_tiles, 1, 0), 0, 0)),
                  pl.BlockSpec((d, n), lambda b, t: (0, 0))],
        out_specs=[pl.BlockSpec((1, tm, wd), lambda b, t: (b, t, 0)) for wd in widths],
        out_shape=[jax.ShapeDtypeStruct((bsz, rows, wd), F32) for wd in widths],
        compiler_params=_cparams(("parallel", "parallel")),
        name="normmod_proj",
    )(x, g.reshape(1, d), mod, w.astype(BF16))


def _rwkv_prep_kernel(z_ref, zp_ref, zn_ref, mup_ref, mun_ref, w0_ref, w2_ref, a0_ref, a2_ref,
                      g2_ref, kk_ref, ka_ref, rk_ref, ones_ref,
                      r_o, v_o, g_o, bonus_o, kkn_o, lwf_o, kdf_o, bf_o, lwr_o, kdr_o, br_o,
                      *, tm, ctx_tiles):
    t = pl.program_id(1)
    nt = pl.num_programs(1)
    z = z_ref[0][:, :A_COLS]
    zp_row = zp_ref[0][SUBLANES - 1:SUBLANES, :A_COLS]
    zn_row = zn_ref[0][0:1, :A_COLS]
    first = jnp.logical_or(t == 0, t == ctx_tiles)
    last = jnp.logical_or(t == ctx_tiles - 1, t == nt - 1)
    zp_row = jnp.where(first, 0.0, zp_row)
    zn_row = jnp.where(last, 0.0, zn_row)
    ridx = lax.broadcasted_iota(jnp.int32, (tm, 1), 0)
    z_prev = jnp.where(ridx == 0, zp_row, pltpu.roll(z, 1, 0))
    z_next = jnp.where(ridx == tm - 1, zn_row, pltpu.roll(z, tm - 1, 0))
    zs = z + mup_ref[...] * (z_prev - z) + mun_ref[...] * (z_next - z)

    w = A_WIDTH
    r = zs[:, 0:w]
    k = zs[:, w:2 * w]
    v = zs[:, 2 * w:3 * w]
    o = 3 * w
    xw = zs[:, o:o + 2 * A_RANK_W]
    o += 2 * A_RANK_W
    xa = zs[:, o:o + 2 * A_RANK_A]
    o += 2 * A_RANK_A
    xg = zs[:, o:o + A_RANK_G]
    ones_bd = ones_ref[...]

    g_o[0] = _dot_hp(jax.nn.sigmoid(xg), g2_ref[...])
    kk = k * kk_ref[...]
    nrm = jnp.sqrt(_segsum(kk * kk, ones_bd))
    kk = kk / jnp.maximum(nrm, 1e-12)
    r_o[0] = r
    v_o[0] = v
    kkn_o[0] = kk
    k_sum = jnp.zeros_like(r)
    tw = jnp.tanh(xw)
    for d, (lw_o, kd_o, b_o) in enumerate(((lwf_o, kdf_o, bf_o), (lwr_o, kdr_o, br_o))):
        wl = w0_ref[d:d + 1, :] + _dot_hp(tw[:, d * A_RANK_W:(d + 1) * A_RANK_W], w2_ref[d])
        w_log = -jax.nn.softplus(-wl) - 0.5
        lw_o[0] = -jnp.exp(w_log)
        a_gate = jax.nn.sigmoid(
            a0_ref[d:d + 1, :] + _dot_hp(xa[:, d * A_RANK_A:(d + 1) * A_RANK_A], a2_ref[d]))
        k_d = k * (1.0 + (a_gate - 1.0) * ka_ref[...])
        kd_o[0] = k_d
        b_o[0] = kk * a_gate
        k_sum = k_sum + k_d
    bonus_o[0] = _segsum(r * k_sum * rk_ref[...], ones_bd) * v


def _rwkv_prep(z, p, tm, ctx_tiles):
    bsz, rows, ncol = z.shape
    w = A_WIDTH
    hb = tm // SUBLANES
    nblk8 = rows // SUBLANES
    row_spec = lambda width: pl.BlockSpec((1, width), lambda b, t: (0, 0))
    full = lambda shape: pl.BlockSpec(shape, lambda b, t: (0,) * len(shape))
    out_spec = pl.BlockSpec((1, tm, w), lambda b, t: (b, t, 0))
    out_shape = jax.ShapeDtypeStruct((bsz, rows, w), F32)
    kern = functools.partial(_rwkv_prep_kernel, tm=tm, ctx_tiles=ctx_tiles)
    return pl.pallas_call(
        kern,
        grid=(bsz, rows // tm),
        in_specs=[pl.BlockSpec((1, tm, ncol), lambda b, t: (b, t, 0)),
                  pl.BlockSpec((1, SUBLANES, ncol),
                               lambda b, t: (b, jnp.maximum(t * hb - 1, 0), 0)),
                  pl.BlockSpec((1, SUBLANES, ncol),
                               lambda b, t: (b, jnp.minimum((t + 1) * hb, nblk8 - 1), 0)),
                  row_spec(A_COLS), row_spec(A_COLS),
                  full((2, w)), full((2, A_RANK_W, w)), full((2, w)), full((2, A_RANK_A, w)),
                  full((A_RANK_G, w)), row_spec(w), row_spec(w), row_spec(w), full((w, w))],
        out_specs=[out_spec] * 11,
        out_shape=[out_shape] * 11,
        compiler_params=_cparams(("parallel", "parallel")),
        name="rwkv_prep",
    )(z, z, z, p["shift_prev"].reshape(1, -1), p["shift_next"].reshape(1, -1),
      p["w0"], p["w2"], p["a0"], p["a2"], p["g2"], p["k_k"].reshape(1, w),
      p["k_a"].reshape(1, w), p["r_k"].reshape(1, w), _block_ones(w, HEAD_DIM))


SCAN_HEADS = 12


def _scan_kernel(rf, vf, kkf, lwf, kdf, bf, rr, vr, kkr, lwr, kdr, br, yf_o, yr_o, s_ref, *, c):
    @pl.when(pl.program_id(2) == 0)
    def _():
        s_ref[...] = jnp.zeros_like(s_ref)

    ri = lax.broadcasted_iota(jnp.int32, (c, c), 0)
    ci = lax.broadcasted_iota(jnp.int32, (c, c), 1)
    ri2 = lax.broadcasted_iota(jnp.int32, (c, 2 * c), 0)
    ci2 = lax.broadcasted_iota(jnp.int32, (c, 2 * c), 1)
    ci2 = jnp.where(ci2 >= c, ci2 - c, ci2)
    dirs = ((rf, vf, kkf, lwf, kdf, bf, ci <= ri, ci < ri, ci2 <= ri2, c - 1),
            (rr, vr, kkr, lwr, kdr, br, ci >= ri, ci > ri, ci2 >= ri2, 0))
    lhs, rhs, vs, ss, pend, incl, strict = [], [], [], [], [], [], []
    for d, (r_, v_, kk_, lw_, kd_, b_, m_incl, m_strict, m_incl2, edge) in enumerate(dirs):
        lw = lw_[0]
        tri = m_incl.astype(F32).astype(BF16)
        lw_h = lw.astype(BF16)
        lw_m, lw_l = _split_bf16(lw - lw_h.astype(F32))
        lcum = _dot(tri, lw_h) + (_dot(tri, lw_m) + _dot(tri, lw_l))
        p_in = jnp.exp(lcum)
        p_inv = jnp.exp(-lcum)
        lhs_all = jnp.concatenate([-kk_[0] * jnp.exp(lcum - lw), r_[0] * p_in], axis=0)
        rhs_all = jnp.concatenate([b_[0] * p_inv, kd_[0] * p_inv], axis=0)
        v_all = v_[0]
        for h in range(SCAN_HEADS):
            sl = slice(h * HEAD_DIM, (h + 1) * HEAD_DIM)
            lhs.append(lhs_all[:, sl])
            rhs.append(rhs_all[:, sl])
            vs.append(v_all[:, sl])
            ss.append(s_ref[d, h])
            pend.append(p_in[edge:edge + 1, sl])
            incl.append(m_incl2)
            strict.append(m_strict)
    n = len(lhs)
    idx = range(n)
    lhs2 = [_split_bf16(lhs[i]) for i in idx]
    rhs2 = [_split_bf16(rhs[i]) for i in idx]
    v2 = [_split_bf16(vs[i]) for i in idx]
    s2 = [_split_bf16(ss[i]) for i in idx]
    g = [_mm3(lhs2[i], rhs2[i], _NT) for i in idx]
    xs = [_mm3(lhs2[i], s2[i], _NT) for i in idx]
    a_ak = [_split_bf16(jnp.where(strict[i], g[i][:c, c:], 0.0)) for i in idx]
    x = [xs[i][:c] + _mm3(a_ak[i], v2[i]) for i in idx]
    apow = [_split_bf16(jnp.where(strict[i], g[i][:c, :c], 0.0)) for i in idx]
    steps = int(math.log2(c))
    for k in range(steps):
        x2 = [_split_bf16(x[i]) for i in idx]
        x = [x[i] + _mm3(apow[i], x2[i]) for i in idx]
        if k + 1 < steps:
            apow = [_split_bf16(_mm3(apow[i], apow[i])) for i in idx]
    x2 = [_split_bf16(x[i]) for i in idx]
    sav = [tuple(jnp.concatenate([x2[i][t], v2[i][t]], axis=0) for t in range(2)) for i in idx]
    a_r = [_split_bf16(jnp.where(incl[i], g[i][c:], 0.0)) for i in idx]
    y = [xs[i][c:] + _mm3(a_r[i], sav[i]) for i in idx]
    s_new = [(ss[i] + _mm3(sav[i], rhs2[i], _TN)) * pend[i] for i in idx]
    for d, y_o in enumerate((yf_o, yr_o)):
        for h in range(SCAN_HEADS):
            s_ref[d, h] = s_new[d * SCAN_HEADS + h]
        y_o[0] = jnp.concatenate(y[d * SCAN_HEADS:(d + 1) * SCAN_HEADS], axis=1)


def _rwkv_scan(r, v, kk, lwf, kdf, bf, lwr, kdr, br, ctx_len):
    bsz, rows, w = r.shape
    c = SCAN_CHUNK
    nch = rows // c
    cch = ctx_len // c
    lanes = SCAN_HEADS * HEAD_DIM
    fwd = lambda b, hp, i: (b, i, hp)
    rev = lambda b, hp, i: (b, jnp.where(i < cch, cch - 1 - i, nch - 1 + cch - i), hp)
    blk = lambda im: pl.BlockSpec((1, c, lanes), im)
    kern = functools.partial(_scan_kernel, c=c)
    out_shape = jax.ShapeDtypeStruct((bsz, rows, w), F32)
    return pl.pallas_call(
        kern,
        grid=(bsz, w // lanes, nch),
        in_specs=[blk(fwd)] * 6 + [blk(rev)] * 6,
        out_specs=[blk(fwd), blk(rev)],
        out_shape=[out_shape, out_shape],
        scratch_shapes=[pltpu.VMEM((2, SCAN_HEADS, HEAD_DIM, HEAD_DIM), F32)],
        compiler_params=_cparams(("parallel", "parallel", "arbitrary")),
        name="rwkv_scan",
    )(r, v, kk, lwf, kdf, bf, r, v, kk, lwr, kdr, br)


def _rwkv_finish_kernel(yf, yr, bonus, g, lg, lb, ones_ref, o_ref):
    y = yf[0] + yr[0]
    ones_bd = ones_ref[...]
    inv = 1.0 / HEAD_DIM
    mu = _segsum(y, ones_bd) * inv
    yc = y - mu
    var = _segsum(yc * yc, ones_bd) * inv
    yn = yc * lax.rsqrt(var + A_LNX_EPS) * lg[...] + lb[...]
    o_ref[0] = ((yn + bonus[0]) * g[0]).astype(o_ref.dtype)


def _rwkv_finish(yf, yr, bonus, g, lnx_g, lnx_b, tm):
    bsz, rows, w = yf.shape
    blk = pl.BlockSpec((1, tm, w), lambda b, t: (b, t, 0))
    row = pl.BlockSpec((1, w), lambda b, t: (0, 0))
    return pl.pallas_call(
        _rwkv_finish_kernel,
        grid=(bsz, rows // tm),
        in_specs=[blk, blk, blk, blk, row, row, pl.BlockSpec((w, w), lambda b, t: (0, 0))],
        out_specs=blk,
        out_shape=jax.ShapeDtypeStruct((bsz, rows, w), BF16),
        compiler_params=_cparams(("parallel", "parallel")),
        name="rwkv_finish",
    )(yf, yr, bonus, g, lnx_g.reshape(1, w), lnx_b.reshape(1, w), _block_ones(w, HEAD_DIM))


def _fnet_chan_kernel(z_ref, m_ref, o_ref):
    o_ref[0] = _dot(z_ref[0].astype(BF16), m_ref[...]).astype(o_ref.dtype)


def _fnet_chan(z, tm):
    bsz, rows, ncol = z.shape
    j = np.arange(B_WIDTH)
    same = (j[:, None] // B_GROUP_DIM) == (j[None, :] // B_GROUP_DIM)
    ang = 2.0 * np.pi * ((j[:, None] % B_GROUP_DIM) * (j[None, :] % B_GROUP_DIM) % B_GROUP_DIM) / B_GROUP_DIM
    m = np.concatenate([np.where(same, np.cos(ang), 0.0), np.where(same, np.sin(ang), 0.0)], axis=1)
    return pl.pallas_call(
        _fnet_chan_kernel,
        grid=(bsz, rows // tm),
        in_specs=[pl.BlockSpec((1, tm, ncol), lambda b, t: (b, t, 0)),
                  pl.BlockSpec((B_WIDTH, 2 * B_WIDTH), lambda b, t: (0, 0))],
        out_specs=pl.BlockSpec((1, tm, 2 * B_WIDTH), lambda b, t: (b, t, 0)),
        out_shape=jax.ShapeDtypeStruct((bsz, rows, 2 * B_WIDTH), BF16),
        compiler_params=_cparams(("parallel", "parallel")),
        name="fnet_chan",
    )(z, jnp.asarray(m, dtype=BF16))


def _fnet_seq_kernel(fcs_ref, cb_ref, sb_ref, c0_ref, s0_ref, o_ref, acc_ref, *, scale, nb):
    k = pl.program_id(1)

    @pl.when(k == 0)
    def _():
        acc_ref[...] = jnp.zeros_like(acc_ref)

    cb = cb_ref[...]
    sb = sb_ref[...]
    c0 = c0_ref[0]
    s0 = s0_ref[0]
    cm = (c0 * cb - s0 * sb).astype(BF16)
    sm = (s0 * cb + c0 * sb).astype(BF16)
    for b in range(nb):
        fcs = fcs_ref[b]
        acc_ref[b] += _dot(cm, fcs[:, :B_WIDTH]) - _dot(sm, fcs[:, B_WIDTH:])

    @pl.when(k == pl.num_programs(1) - 1)
    def _():
        o_ref[...] = (acc_ref[...] * scale).astype(o_ref.dtype)


def _fnet_seq(fcs, row0, length, ts, tk):
    bsz = fcs.shape[0]
    ds = jnp.arange(ts, dtype=jnp.int32)[:, None]
    tt = jnp.arange(length, dtype=jnp.int32)[None, :]
    ang = ((ds * tt) % length).astype(F32) * (2.0 * math.pi / length)
    cb, sb = jnp.cos(ang), jnp.sin(ang)
    s0 = (jnp.arange(length // ts, dtype=jnp.int32) * ts)[:, None]
    ang0 = ((s0 * tt) % length).astype(F32) * (2.0 * math.pi / length)
    c0, sn0 = jnp.cos(ang0)[:, None, :], jnp.sin(ang0)[:, None, :]
    kern = functools.partial(_fnet_seq_kernel, scale=1.0 / math.sqrt(length * B_GROUP_DIM), nb=bsz)
    koff = row0 // tk
    return pl.pallas_call(
        kern,
        grid=(length // ts, length // tk),
        in_specs=[pl.BlockSpec((bsz, tk, 2 * B_WIDTH), lambda s, k: (0, k + koff, 0)),
                  pl.BlockSpec((ts, tk), lambda s, k: (0, k)),
                  pl.BlockSpec((ts, tk), lambda s, k: (0, k)),
                  pl.BlockSpec((1, 1, tk), lambda s, k: (s, 0, k)),
                  pl.BlockSpec((1, 1, tk), lambda s, k: (s, 0, k))],
        out_specs=pl.BlockSpec((bsz, ts, B_WIDTH), lambda s, k: (0, s, 0)),
        out_shape=jax.ShapeDtypeStruct((bsz, length, B_WIDTH), BF16),
        scratch_shapes=[pltpu.VMEM((bsz, ts, B_WIDTH), F32)],
        compiler_params=_cparams(("parallel", "arbitrary")),
        name="fnet_seq",
    )(fcs, cb, sb, c0, sn0)


def _proj_residual_kernel(o1_ref, o2_ref, x_ref, mod_ref, w1_ref, w2_ref, out_ref, *, o1_transposed):
    o1 = o1_ref[0]
    if o1_transposed:
        o1 = o1.astype(F32).T.astype(BF16)
    mix = _dot(o1, w1_ref[...]) + _dot(o2_ref[0], w2_ref[...])
    gate = mod_ref[0, 0][2:3]
    out_ref[0] = x_ref[0] + gate * mix


def _proj_residual(o1, o2, x, mod, w_out, tm, tile_off, ctx_tiles, o1_transposed=False):
    bsz, rows, w2 = o2.shape
    w1 = o1.shape[1] if o1_transposed else o1.shape[2]
    d = x.shape[2]
    o1_spec = (pl.BlockSpec((1, w1, tm), lambda b, t: (b, 0, t)) if o1_transposed
               else pl.BlockSpec((1, tm, w1), lambda b, t: (b, t, 0)))
    kern = functools.partial(_proj_residual_kernel, o1_transposed=o1_transposed)
    return pl.pallas_call(
        kern,
        grid=(bsz, rows // tm),
        in_specs=[o1_spec,
                  pl.BlockSpec((1, tm, w2), lambda b, t: (b, t, 0)),
                  pl.BlockSpec((1, tm, d), lambda b, t: (b, t + tile_off, 0)),
                  pl.BlockSpec((1, 1, N_MOD, d),
                               lambda b, t: (b, jnp.where(t + tile_off >= ctx_tiles, 1, 0), 0, 0)),
                  pl.BlockSpec((w1, d), lambda b, t: (0, 0)),
                  pl.BlockSpec((w2, d), lambda b, t: (0, 0))],
        out_specs=pl.BlockSpec((1, tm, d), lambda b, t: (b, t, 0)),
        out_shape=jax.ShapeDtypeStruct((bsz, rows, d), F32),
        compiler_params=_cparams(("parallel", "parallel")),
        name="proj_residual",
    )(o1, o2, x, mod, w_out[:w1].astype(BF16), w_out[w1:].astype(BF16))


def _merge_exchange_network(n):
    pairs = []
    p = 1
    while p < n:
        k = p
        while k >= 1:
            for j in range(k % p, n - k, 2 * k):
                for i in range(min(k, n - j - k)):
                    if (i + j) // (2 * p) == (i + j + k) // (2 * p):
                        pairs.append((i + j, i + j + k))
            k //= 2
        p *= 2
    return pairs


def _top16_rows(s):
    nslab = s.shape[0] // SUBLANES
    slabs = [s[k * SUBLANES:(k + 1) * SUBLANES] for k in range(nslab)]
    for a, b in _merge_exchange_network(nslab):
        slabs[a], slabs[b] = jnp.maximum(slabs[a], slabs[b]), jnp.minimum(slabs[a], slabs[b])
    rows = []
    for t in range(PK_TOPK):
        m = jnp.max(slabs[0], axis=0, keepdims=True)
        rows.append(m)
        hit = slabs[0] >= m
        for k in range(nslab - 1 - t):
            slabs[k] = jnp.where(hit, slabs[k + 1], slabs[k])
    return rows


_PAIR_IDX = [(i, j) for i in range(PK_TOPK) for j in range(PK_TOPK) if (i + 1) * (j + 1) <= PK_TOPK]
BF16_SUBLANES = 2 * SUBLANES
PEER_SUB_E1 = 1
PEER_STEP_E1 = 16


def _rows_bf16(row):
    packed = jnp.broadcast_to(row, (BF16_SUBLANES, row.shape[1])).astype(BF16)
    return jnp.tile(packed, (N_KEYS // BF16_SUBLANES, 1))


def _peer_kernel(x_ref, g_ref, mod_ref, pqh_ref, pql_ref, sk1h_ref, sk1l_ref, sk2h_ref, sk2l_ref,
                 pu_ref, pvt_ref, gf_ref,
                 out_ref, ht_ref, r2_ref, e2_ref, c1_ref, w1_ref, s_ref, cand_ref, acc_ref,
                 *, tm, e1_per_blk, final_norm):
    eb = pl.program_id(2)

    @pl.when(eb == 0)
    def _():
        h = _norm_mod(x_ref[0], g_ref[...], mod_ref[0, 0], 3)
        ht = h.T
        hhi, hlo = _split_bf16(ht)
        ht_ref[...] = hhi
        qt = _mm3((pqh_ref[...], pql_ref[...]), (hhi, hlo))
        qs = [_split_bf16(qt[i * PK_HALF:(i + 1) * PK_HALF]) for i in range(2 * PK_HEADS)]
        for hd in range(PK_HEADS):
            s_ref[2 * hd] = _mm3((sk1h_ref[hd], sk1l_ref[hd]), qs[2 * hd])
            s_ref[2 * hd + 1] = _mm3((sk2h_ref[hd], sk2l_ref[hd]), qs[2 * hd + 1])
        for hd in range(PK_HEADS):
            s1 = s_ref[2 * hd]
            s2 = s_ref[2 * hd + 1]
            v1 = _top16_rows(s1)
            v2 = _top16_rows(s2)
            cand_ref[...] = jnp.full_like(cand_ref, NEG_INF)
            for n, (i, j) in enumerate(_PAIR_IDX):
                cand_ref[n:n + 1, :] = v1[i] + v2[j]
            cand = cand_ref[...]
            top = v1[0] + v2[0]
            zsum = jnp.zeros_like(top)
            tau = top
            for _ in range(PK_TOPK):
                m = jnp.max(cand, axis=0, keepdims=True)
                zsum = zsum + jnp.exp(m - top)
                tau = m
                cand = jnp.where(cand >= m, NEG_INF, cand)
            rank2 = jnp.full((N_KEYS, tm), float(PK_TOPK), F32)
            for j in reversed(range(PK_TOPK)):
                rank2 = jnp.where(s2 >= v2[j], float(j), rank2)
            count1 = jnp.zeros((N_KEYS, tm), F32)
            for i in reversed(range(PK_TOPK)):
                cnt = jnp.zeros_like(top)
                for j in range(PK_TOPK // (i + 1)):
                    cnt = cnt + jnp.where(v1[i] + v2[j] >= tau, 1.0, 0.0)
                count1 = jnp.where(s1 >= v1[i], cnt, count1)
            r2_ref[hd] = rank2.astype(BF16)
            c1_ref[hd] = count1
            e2_ref[hd] = jnp.exp(s2 - v2[0]).astype(BF16)
            w1_ref[hd] = jnp.exp(s1 - v1[0]) / zsum
        acc_ref[...] = jnp.zeros_like(acc_ref)

    sub = PEER_SUB_E1 * N_KEYS
    nsub = e1_per_blk // PEER_SUB_E1
    ht = ht_ref[...]

    def up(i):
        return _dot(pu_ref[i * sub:(i + 1) * sub, :], ht)

    def activation(i, hu):
        acts = []
        for jj in range(PEER_SUB_E1):
            e1 = eb * e1_per_blk + PEER_SUB_E1 * i + jj
            gate = jnp.zeros((N_KEYS, tm), BF16)
            for hd in range(PK_HEADS):
                count = _rows_bf16(c1_ref[hd, pl.ds(e1, 1), :])
                weight = _rows_bf16(w1_ref[hd, pl.ds(e1, 1), :])
                gate = gate + jnp.where(r2_ref[hd] < count, e2_ref[hd] * weight, 0)
            u = hu[jj * N_KEYS:(jj + 1) * N_KEYS]
            act = 0.5 * u * (1.0 + lax.erf(u * (1.0 / math.sqrt(2.0))))
            acts.append(act.astype(BF16) * gate)
        return jnp.concatenate(acts, axis=0)

    hus = [up(i) for i in range(nsub)]
    acts = [activation(i, hus[i]) for i in range(nsub)]
    acc_ref[...] += _dot(pvt_ref[...], jnp.concatenate(acts, axis=0))

    @pl.when(eb == pl.num_programs(2) - 1)
    def _():
        y = x_ref[0] + mod_ref[0, 0][5:6] * acc_ref[...].T
        if final_norm:
            ms = jnp.mean(y * y, axis=-1, keepdims=True)
            y = y * lax.rsqrt(ms + NORM_EPS) * gf_ref[...]
        out_ref[0] = y


def _peer(x, g, mod, p, tm, tile_off, ctx_tiles, final_norm, norm_f):
    bsz, rows, d = x.shape
    n_exp = p["pu"].shape[0]
    e1_per_blk = PEER_STEP_E1
    eblk = e1_per_blk * N_KEYS
    pqt = p["pq"].T
    pqh = pqt.astype(BF16)
    pql = (pqt - pqh.astype(F32)).astype(BF16)
    nq = pqt.shape[0]
    kern = functools.partial(_peer_kernel, tm=tm, e1_per_blk=e1_per_blk, final_norm=final_norm)
    const = lambda shape: pl.BlockSpec(shape, lambda b, t, e: (0,) * len(shape))
    return pl.pallas_call(
        kern,
        grid=(bsz, rows // tm, n_exp // eblk),
        in_specs=[pl.BlockSpec((1, tm, d), lambda b, t, e: (b, t, 0)),
                  const((1, d)),
                  pl.BlockSpec((1, 1, N_MOD, d),
                               lambda b, t, e: (b, jnp.where(t + tile_off >= ctx_tiles, 1, 0), 0, 0)),
                  const((nq, d)), const((nq, d)),
                  const((PK_HEADS, N_KEYS, PK_HALF)), const((PK_HEADS, N_KEYS, PK_HALF)),
                  const((PK_HEADS, N_KEYS, PK_HALF)), const((PK_HEADS, N_KEYS, PK_HALF)),
                  pl.BlockSpec((eblk, d), lambda b, t, e: (e, 0)),
                  pl.BlockSpec((d, eblk), lambda b, t, e: (0, e)),
                  const((1, d))],
        out_specs=pl.BlockSpec((1, tm, d), lambda b, t, e: (b, t, 0)),
        out_shape=jax.ShapeDtypeStruct((bsz, rows, d), F32),
        scratch_shapes=[pltpu.VMEM((d, tm), BF16),
                        pltpu.VMEM((PK_HEADS, N_KEYS, tm), BF16),
                        pltpu.VMEM((PK_HEADS, N_KEYS, tm), BF16),
                        pltpu.VMEM((PK_HEADS, N_KEYS, tm), F32),
                        pltpu.VMEM((PK_HEADS, N_KEYS, tm), F32),
                        pltpu.VMEM((2 * PK_HEADS, N_KEYS, tm), F32),
                        pltpu.VMEM((SUBLANES * pl.cdiv(len(_PAIR_IDX), SUBLANES), tm), F32),
                        pltpu.VMEM((d, tm), F32)],
        compiler_params=_cparams(("parallel", "parallel", "arbitrary")),
        name="peer",
    )(x, g.reshape(1, d), mod, pqh, pql, *_split_bf16(p["sk1"]), *_split_bf16(p["sk2"]),
      p["pu"].astype(BF16), p["pv"].T.astype(BF16), norm_f.reshape(1, d))


def _rope_tables(rows, ctx_len):
    t = jnp.arange(rows - ctx_len, dtype=jnp.int32)
    inv = ROPE_THETA ** (-jnp.arange(0, ROPE_AXIS_DIM, 2, dtype=F32) / ROPE_AXIS_DIM)
    ang_r = (t // GRID_W).astype(F32)[:, None] * inv
    ang_c = (t % GRID_W).astype(F32)[:, None] * inv
    cos = jnp.concatenate([jnp.cos(ang_r)] * 2 + [jnp.cos(ang_c)] * 2, axis=1)
    sin = jnp.concatenate([-jnp.sin(ang_r), jnp.sin(ang_r), -jnp.sin(ang_c), jnp.sin(ang_c)], axis=1)
    cos = jnp.concatenate([jnp.ones((ctx_len, HEAD_DIM), F32), cos], axis=0)
    sin = jnp.concatenate([jnp.zeros((ctx_len, HEAD_DIM), F32), sin], axis=0)
    return jnp.tile(cos, (1, 2)), jnp.tile(sin, (1, 2))


def _head_norm_rope(x, gain, ones_bd, cos, sin):
    ms = _segsum(x * x, ones_bd) * (1.0 / HEAD_DIM)
    y = x * lax.rsqrt(ms + NORM_EPS) * gain
    outs = []
    half = ROPE_AXIS_DIM // 2
    lane = lax.broadcasted_iota(jnp.int32, (1, LANES), 1)
    first_half = (lane % ROPE_AXIS_DIM) < half
    for i in range(x.shape[1] // LANES):
        yc = y[:, i * LANES:(i + 1) * LANES]
        partner = jnp.where(first_half, pltpu.roll(yc, LANES - half, 1), pltpu.roll(yc, half, 1))
        outs.append(yc * cos + partner * sin)
    return jnp.concatenate(outs, axis=1)


def _attn_prep_kernel(z_ref, qn_ref, kn_ref, cos_ref, sin_ref, onesq_ref, onesk_ref,
                      q_o, k_o, v_o):
    z = z_ref[0]
    cos = cos_ref[...]
    sin = sin_ref[...]
    q = _head_norm_rope(z[:, :C_WIDTH], qn_ref[...], onesq_ref[...], cos, sin)
    k = _head_norm_rope(z[:, C_WIDTH:C_WIDTH + KV_WIDTH], kn_ref[...], onesk_ref[...], cos, sin)
    q_o[0] = (q * (ATTN_SCALE * math.log2(math.e))).T.astype(q_o.dtype)
    for j in range(C_KV_HEADS):
        k_o[0, j] = k[:, j * HEAD_DIM:(j + 1) * HEAD_DIM].astype(k_o.dtype)
    v_o[0] = z[:, C_WIDTH + KV_WIDTH:C_WIDTH + 2 * KV_WIDTH].T.astype(v_o.dtype)


def _attn_prep(z, q_norm, k_norm, ctx_len, tm):
    bsz, rows, ncol = z.shape
    cos, sin = _rope_tables(rows, ctx_len)
    qn = jnp.tile(q_norm, C_HEADS).reshape(1, C_WIDTH)
    kn = jnp.tile(k_norm, C_KV_HEADS).reshape(1, KV_WIDTH)
    const = lambda shape: pl.BlockSpec(shape, lambda b, t: (0,) * len(shape))
    return pl.pallas_call(
        _attn_prep_kernel,
        grid=(bsz, rows // tm),
        in_specs=[pl.BlockSpec((1, tm, ncol), lambda b, t: (b, t, 0)),
                  const((1, C_WIDTH)), const((1, KV_WIDTH)),
                  pl.BlockSpec((tm, LANES), lambda b, t: (t, 0)),
                  pl.BlockSpec((tm, LANES), lambda b, t: (t, 0)),
                  const((C_WIDTH, C_WIDTH)), const((KV_WIDTH, KV_WIDTH))],
        out_specs=[pl.BlockSpec((1, C_WIDTH, tm), lambda b, t: (b, 0, t)),
                   pl.BlockSpec((1, C_KV_HEADS, tm, HEAD_DIM), lambda b, t: (b, 0, t, 0)),
                   pl.BlockSpec((1, KV_WIDTH, tm), lambda b, t: (b, 0, t))],
        out_shape=[jax.ShapeDtypeStruct((bsz, C_WIDTH, rows), BF16),
                   jax.ShapeDtypeStruct((bsz, C_KV_HEADS, rows, HEAD_DIM), BF16),
                   jax.ShapeDtypeStruct((bsz, KV_WIDTH, rows), BF16)],
        compiler_params=_cparams(("parallel", "parallel")),
        name="attn_prep",
    )(z, qn, kn, cos, sin, _block_ones(C_WIDTH, HEAD_DIM), _block_ones(KV_WIDTH, HEAD_DIM))


def _attn_kernel(qt_ref, k_ref, vt_ref, o_ref, m_ref, l_ref, acc_ref, *, kc):
    kt = pl.program_id(3)

    @pl.when(kt == 0)
    def _():
        m_ref[...] = jnp.full_like(m_ref, NEG_INF)
        l_ref[...] = jnp.zeros_like(l_ref)
        acc_ref[...] = jnp.zeros_like(acc_ref)

    nkv = k_ref.shape[1]
    grp = range(nkv * C_GROUP)
    qts = [qt_ref[0, g * HEAD_DIM:(g + 1) * HEAD_DIM, :] for g in grp]
    nck = k_ref.shape[2] // kc

    def scores(c):
        kbs = [k_ref[0, j, c * kc:(c + 1) * kc, :] for j in range(nkv)]
        return [_dot(kbs[g // C_GROUP], qts[g]) for g in grp]

    m = [m_ref[g] for g in grp]
    l = [l_ref[g] for g in grp]
    acc = [acc_ref[g] for g in grp]
    s_next = scores(0)
    for c in range(nck):
        s = s_next
        if c + 1 < nck:
            s_next = scores(c + 1)
        vts = [vt_ref[0, j * HEAD_DIM:(j + 1) * HEAD_DIM, c * kc:(c + 1) * kc]
               for j in range(nkv)]
        m_new = [jnp.maximum(m[g], jnp.max(s[g], axis=0, keepdims=True)) for g in grp]
        p = [jnp.exp2(s[g] - m_new[g]) for g in grp]
        pv = [_dot(vts[g // C_GROUP], p[g].astype(BF16)) for g in grp]
        for g in grp:
            alpha = jnp.exp2(m[g] - m_new[g])
            l[g] = alpha * l[g] + jnp.sum(p[g], axis=0, keepdims=True)
            acc[g] = alpha * acc[g] + pv[g]
        m = m_new
    for g in grp:
        m_ref[g] = m[g]
        l_ref[g] = l[g]
        acc_ref[g] = acc[g]

    @pl.when(kt == pl.num_programs(3) - 1)
    def _():
        o_ref[0] = jnp.concatenate(
            [acc_ref[g] / l_ref[g] for g in grp], axis=0).astype(o_ref.dtype)


ATTN_MAX_KEY_TILE = 8448
ATTN_MAX_KEY_CHUNK = 256
ATTN_KV_PER_STEP = 1


def _key_tile(rows, limit=ATTN_MAX_KEY_TILE):
    return max(t for t in range(LANES, min(rows, limit) + 1, LANES) if rows % t == 0)


def _attention(qt, k, vt, ctx_len, tq, tk):
    bsz, _, rows = qt.shape
    seq = rows - ctx_len
    qoff = ctx_len // tq
    nkv = ATTN_KV_PER_STEP
    nq = nkv * C_GROUP
    gw = nq * HEAD_DIM
    kern = functools.partial(_attn_kernel, kc=_key_tile(tk, ATTN_MAX_KEY_CHUNK))
    return pl.pallas_call(
        kern,
        grid=(bsz, C_KV_HEADS // nkv, seq // tq, rows // tk),
        in_specs=[pl.BlockSpec((1, gw, tq), lambda b, j, i, kk: (b, j, i + qoff)),
                  pl.BlockSpec((1, nkv, tk, HEAD_DIM), lambda b, j, i, kk: (b, j, kk, 0)),
                  pl.BlockSpec((1, nkv * HEAD_DIM, tk), lambda b, j, i, kk: (b, j, kk))],
        out_specs=pl.BlockSpec((1, gw, tq), lambda b, j, i, kk: (b, j, i)),
        out_shape=jax.ShapeDtypeStruct((bsz, C_WIDTH, seq), BF16),
        scratch_shapes=[pltpu.VMEM((nq, 1, tq), F32),
                        pltpu.VMEM((nq, 1, tq), F32),
                        pltpu.VMEM((nq, HEAD_DIM, tq), F32)],
        compiler_params=_cparams(("parallel", "parallel", "parallel", "arbitrary")),
        name="attention",
    )(qt, k, vt)


def _conv_kernel(zc_ref, zp_ref, zn_ref, w_ref, b_ref, lg_ref, lb_ref, o_ref, ybuf, *, tm, halo):
    t = pl.program_id(1)
    nt = pl.num_programs(1)

    def glu(u):
        return u[:, :D_WIDTH] * jax.nn.sigmoid(u[:, D_WIDTH:])

    ybuf[0:halo, :] = jnp.where(t == 0, 0.0, glu(zp_ref[0]))
    ybuf[halo:halo + tm, :] = glu(zc_ref[0])
    ybuf[halo + tm:, :] = jnp.where(t == nt - 1, 0.0, glu(zn_ref[0]))
    acc = jnp.zeros((tm, D_WIDTH), F32)
    for j in range(D_CONV_WIDTH):
        off = halo - D_PAD + j
        acc = acc + w_ref[j:j + 1, :] * ybuf[off:off + tm, :]
    y = acc + b_ref[...]
    mu = jnp.mean(y, axis=-1, keepdims=True)
    yc = y - mu
    var = jnp.mean(yc * yc, axis=-1, keepdims=True)
    yn = yc * lax.rsqrt(var + LN_EPS) * lg_ref[...] + lb_ref[...]
    o_ref[0] = (yn * jax.nn.sigmoid(yn)).astype(o_ref.dtype)


def _conformer_conv(z, dw_w, dw_b, cn_g, cn_b, ctx_len, tm):
    bsz, rows, ncol = z.shape
    seq = rows - ctx_len
    halo = 2 * SUBLANES
    hb = tm // halo
    off = ctx_len // tm
    offh = ctx_len // halo
    nh = seq // halo
    kern = functools.partial(_conv_kernel, tm=tm, halo=halo)
    row = lambda w: pl.BlockSpec((1, w), lambda b, t: (0, 0))
    return pl.pallas_call(
        kern,
        grid=(bsz, seq // tm),
        in_specs=[pl.BlockSpec((1, tm, ncol), lambda b, t: (b, t + off, 0)),
                  pl.BlockSpec((1, halo, ncol),
                               lambda b, t: (b, offh + jnp.maximum(t * hb - 1, 0), 0)),
                  pl.BlockSpec((1, halo, ncol),
                               lambda b, t: (b, offh + jnp.minimum((t + 1) * hb, nh - 1), 0)),
                  pl.BlockSpec((D_CONV_WIDTH, D_WIDTH), lambda b, t: (0, 0)),
                  row(D_WIDTH), row(D_WIDTH), row(D_WIDTH)],
        out_specs=pl.BlockSpec((1, tm, D_WIDTH), lambda b, t: (b, t, 0)),
        out_shape=jax.ShapeDtypeStruct((bsz, seq, D_WIDTH), BF16),
        scratch_shapes=[pltpu.VMEM((tm + 2 * halo, D_WIDTH), F32)],
        compiler_params=_cparams(("parallel", "parallel")),
        name="conformer_conv",
    )(z, z, z, dw_w, dw_b.reshape(1, -1), cn_g.reshape(1, -1), cn_b.reshape(1, -1))


def _forward(x, c, ctx, c_ctx, l0, l1, norm_f):
    bsz, seq, d = x.shape
    ctx_len = ctx.shape[1]
    tm = min(256, ctx_len)
    ctx_tiles = ctx_len // tm
    xs = jnp.concatenate([ctx, x], axis=1)

    mod = _modulation(c, c_ctx, l0["mod_w"], l0["mod_b"])
    z, zf = _normmod_matmul(xs, l0["norm1"], mod, l0["w_in"], tm, ctx_tiles, A_COLS)
    r, v, g, bonus, kk, lwf, kdf, bf, lwr, kdr, br = _rwkv_prep(z, l0, tm, ctx_tiles)
    yf, yr = _rwkv_scan(r, v, kk, lwf, kdf, bf, lwr, kdr, br, ctx_len)
    o_rwkv = _rwkv_finish(yf, yr, bonus, g, l0["lnx_g"], l0["lnx_b"], tm)
    fcs = _fnet_chan(zf, tm)
    ts_c = min(256, ctx_len)
    ts_l = min(512, seq)
    f_ctx = _fnet_seq(fcs, 0, ctx_len, ts_c, ts_c)
    f_lat = _fnet_seq(fcs[:, ctx_len:], 0, seq, ts_l, ts_l)
    o_fnet = jnp.concatenate([f_ctx, f_lat], axis=1)
    xs = _proj_residual(o_rwkv, o_fnet, xs, mod, l0["w_out"], tm, 0, ctx_tiles)
    xs = _peer(xs, l0["norm2"], mod, l0, tm, 0, ctx_tiles, False, norm_f)

    mod = _modulation(c, c_ctx, l1["mod_w"], l1["mod_b"])
    z, zu = _normmod_matmul(xs, l1["norm1"], mod, l1["w_in"], tm, ctx_tiles,
                            C_WIDTH + 2 * KV_WIDTH)
    qt, k, vt = _attn_prep(z, l1["q_norm"], l1["k_norm"], ctx_len, tm)
    o_attn = _attention(qt, k, vt, ctx_len, tm, _key_tile(ctx_len + seq))
    o_conv = _conformer_conv(zu, l1["dw_w"], l1["dw_b"], l1["cn_g"], l1["cn_b"], ctx_len, tm)
    xl = _proj_residual(o_attn, o_conv, xs, mod, l1["w_out"], tm, ctx_tiles, ctx_tiles,
                        o1_transposed=True)
    return _peer(xl, l1["norm2"], mod, l1, tm, ctx_tiles, ctx_tiles, True, norm_f)


def kernel(x, c, ctx, c_ctx, l0_mod_w, l0_mod_b, l0_norm1, l0_w_in, l0_shift_prev, l0_shift_next, l0_w0, l0_w2, l0_a0, l0_a2, l0_g2, l0_k_k, l0_k_a, l0_r_k, l0_lnx_g, l0_lnx_b, l0_w_out, l0_norm2, l0_pq, l0_sk1, l0_sk2, l0_pu, l0_pv, l1_mod_w, l1_mod_b, l1_norm1, l1_w_in, l1_q_norm, l1_k_norm, l1_dw_w, l1_dw_b, l1_cn_g, l1_cn_b, l1_w_out, l1_norm2, l1_pq, l1_sk1, l1_sk2, l1_pu, l1_pv, norm_f):
    l0 = dict(mod_w=l0_mod_w, mod_b=l0_mod_b, norm1=l0_norm1, w_in=l0_w_in,
              shift_prev=l0_shift_prev, shift_next=l0_shift_next, w0=l0_w0, w2=l0_w2,
              a0=l0_a0, a2=l0_a2, g2=l0_g2, k_k=l0_k_k, k_a=l0_k_a, r_k=l0_r_k,
              lnx_g=l0_lnx_g, lnx_b=l0_lnx_b, w_out=l0_w_out, norm2=l0_norm2,
              pq=l0_pq, sk1=l0_sk1, sk2=l0_sk2, pu=l0_pu, pv=l0_pv)
    l1 = dict(mod_w=l1_mod_w, mod_b=l1_mod_b, norm1=l1_norm1, w_in=l1_w_in,
              q_norm=l1_q_norm, k_norm=l1_k_norm, dw_w=l1_dw_w, dw_b=l1_dw_b,
              cn_g=l1_cn_g, cn_b=l1_cn_b, w_out=l1_w_out, norm2=l1_norm2,
              pq=l1_pq, sk1=l1_sk1, sk2=l1_sk2, pu=l1_pu, pv=l1_pv)
    return _forward(x, c, ctx, c_ctx, l0, l1, norm_f)
```

```python
import functools
import math

import jax
import jax.numpy as jnp
import numpy as np
from jax import lax
from jax.experimental import pallas as pl
from jax.experimental.pallas import tpu as pltpu

F32 = jnp.float32
BF16 = jnp.bfloat16

LANES = 128
SUBLANES = 8
VMEM_LIMIT_BYTES = 56 * 1024 * 1024

N_MOD = 6
NORM_EPS = 1e-6
LN_EPS = 1e-5
GRID_W = 64
HEAD_DIM = 64
A_WIDTH = 768
A_HEADS = A_WIDTH // HEAD_DIM
A_RANK_W = 64
A_RANK_A = 64
A_RANK_G = 128
A_LNX_EPS = 64e-5
A_COLS = 3 * A_WIDTH + 2 * A_RANK_W + 2 * A_RANK_A + A_RANK_G
B_WIDTH = 256
B_GROUP_DIM = 64
C_WIDTH = 768
C_HEADS = 12
C_KV_HEADS = 4
C_GROUP = C_HEADS // C_KV_HEADS
KV_WIDTH = C_KV_HEADS * HEAD_DIM
ROPE_AXIS_DIM = HEAD_DIM // 2
ROPE_THETA = 10000.0
ATTN_SCALE = HEAD_DIM ** -0.5
D_WIDTH = 256
D_CONV_WIDTH = 31
D_PAD = D_CONV_WIDTH // 2
PK_HEADS = 8
PK_DIM = 256
PK_HALF = 128
N_KEYS = 128
PK_TOPK = 16

SCAN_CHUNK = 64
NEG_INF = float("-inf")


def _cparams(semantics):
    return pltpu.CompilerParams(dimension_semantics=semantics,
                                vmem_limit_bytes=VMEM_LIMIT_BYTES)


def _split_bf16(x):
    hi = x.astype(BF16)
    lo = (x - hi.astype(F32)).astype(BF16)
    return hi, lo


def _dot(a, b):
    return jnp.dot(a, b, preferred_element_type=F32)


def _dot_hp(a, b):
    return _mm3(_split_bf16(a), _split_bf16(b))


_NN = (((1,), (0,)), ((), ()))
_NT = (((1,), (1,)), ((), ()))
_TN = (((0,), (0,)), ((), ()))


def _mm3(a, b, dims=_NN):
    (ah, al), (bh, bl) = a, b
    dg = lambda x, y: lax.dot_general(x, y, dims, preferred_element_type=F32)
    return dg(ah, bh) + (dg(ah, bl) + dg(al, bh))


def _segsum(x, ones_bd):
    hi, lo = _split_bf16(x)
    return _dot(hi, ones_bd) + _dot(lo, ones_bd)


def _block_ones(width, seg):
    r = np.arange(width) // seg
    return jnp.asarray((r[:, None] == r[None, :]).astype(np.float32), dtype=BF16)


def _mod_kernel(c_ref, w_ref, b_ref, o_ref):
    c = c_ref[...]
    s = c * jax.nn.sigmoid(c)
    o_ref[...] = _dot_hp(s, w_ref[...]) + b_ref[...]


def _modulation(c, c_ctx, mod_w, mod_b):
    bsz, d = c.shape
    rows = SUBLANES * pl.cdiv(bsz + 1, SUBLANES)
    cc = jnp.zeros((rows, d), F32).at[:bsz].set(c).at[bsz].set(c_ctx)
    n = mod_w.shape[1]
    tn = n // 4
    out = pl.pallas_call(
        _mod_kernel,
        grid=(n // tn,),
        in_specs=[pl.BlockSpec((rows, d), lambda j: (0, 0)),
                  pl.BlockSpec((d, tn), lambda j: (0, j)),
                  pl.BlockSpec((1, tn), lambda j: (0, j))],
        out_specs=pl.BlockSpec((rows, tn), lambda j: (0, j)),
        out_shape=jax.ShapeDtypeStruct((rows, n), F32),
        compiler_params=_cparams(("arbitrary",)),
        name="adaln_mod",
    )(cc, mod_w, mod_b.reshape(1, n))
    lat = out[:bsz].reshape(bsz, N_MOD, d)
    ctx = jnp.broadcast_to(out[bsz].reshape(1, N_MOD, d), (bsz, N_MOD, d))
    return jnp.stack([ctx, lat], axis=1)


def _norm_mod(x, g, mod, row):
    ms = jnp.mean(x * x, axis=-1, keepdims=True)
    y = x * lax.rsqrt(ms + NORM_EPS) * g
    return y * (1.0 + mod[row + 1:row + 2]) + mod[row:row + 1]


def _normmod_matmul_kernel(x_ref, g_ref, mod_ref, w_ref, o1_ref, o2_ref):
    h = _norm_mod(x_ref[0], g_ref[...], mod_ref[0, 0], 0)
    z = _dot(h.astype(BF16), w_ref[...])
    n1 = o1_ref.shape[2]
    o1_ref[0] = z[:, :n1]
    o2_ref[0] = z[:, n1:]


def _normmod_matmul(x, g, mod, w, tm, ctx_tiles, n1):
    bsz, rows, d = x.shape
    n = w.shape[1]
    widths = (n1, n - n1)
    return pl.pallas_call(
        _normmod_matmul_kernel,
        grid=(bsz, rows // tm),
        in_specs=[pl.BlockSpec((1, tm, d), lambda b, t: (b, t, 0)),
                  pl.BlockSpec((1, d), lambda b, t: (0, 0)),
                  pl.BlockSpec((1, 1, N_MOD, d),
                               lambda b, t: (b, jnp.where(t >= ctx_tiles, 1, 0), 0, 0)),
                  pl.BlockSpec((d, n), lambda b, t: (0, 0))],
        out_specs=[pl.BlockSpec((1, tm, wd), lambda b, t: (b, t, 0)) for wd in widths],
        out_shape=[jax.ShapeDtypeStruct((bsz, rows, wd), F32) for wd in widths],
        compiler_params=_cparams(("parallel", "parallel")),
        name="normmod_proj",
    )(x, g.reshape(1, d), mod, w.astype(BF16))


def _rwkv_prep_kernel(z_ref, zp_ref, zn_ref, mup_ref, mun_ref, w0_ref, w2_ref, a0_ref, a2_ref,
                      g2_ref, kk_ref, ka_ref, rk_ref, ones_ref,
                      r_o, v_o, g_o, bonus_o, kkn_o, lwf_o, kdf_o, bf_o, lwr_o, kdr_o, br_o,
                      *, tm, ctx_tiles):
    t = pl.program_id(1)
    nt = pl.num_programs(1)
    z = z_ref[0][:, :A_COLS]
    zp_row = zp_ref[0][SUBLANES - 1:SUBLANES, :A_COLS]
    zn_row = zn_ref[0][0:1, :A_COLS]
    first = jnp.logical_or(t == 0, t == ctx_tiles)
    last = jnp.logical_or(t == ctx_tiles - 1, t == nt - 1)
    zp_row = jnp.where(first, 0.0, zp_row)
    zn_row = jnp.where(last, 0.0, zn_row)
    ridx = lax.broadcasted_iota(jnp.int32, (tm, 1), 0)
    z_prev = jnp.where(ridx == 0, zp_row, pltpu.roll(z, 1, 0))
    z_next = jnp.where(ridx == tm - 1, zn_row, pltpu.roll(z, tm - 1, 0))
    zs = z + mup_ref[...] * (z_prev - z) + mun_ref[...] * (z_next - z)

    w = A_WIDTH
    r = zs[:, 0:w]
    k = zs[:, w:2 * w]
    v = zs[:, 2 * w:3 * w]
    o = 3 * w
    xw = zs[:, o:o + 2 * A_RANK_W]
    o += 2 * A_RANK_W
    xa = zs[:, o:o + 2 * A_RANK_A]
    o += 2 * A_RANK_A
    xg = zs[:, o:o + A_RANK_G]
    ones_bd = ones_ref[...]

    g_o[0] = _dot_hp(jax.nn.sigmoid(xg), g2_ref[...])
    kk = k * kk_ref[...]
    nrm = jnp.sqrt(_segsum(kk * kk, ones_bd))
    kk = kk / jnp.maximum(nrm, 1e-12)
    r_o[0] = r
    v_o[0] = v
    kkn_o[0] = kk
    k_sum = jnp.zeros_like(r)
    tw = jnp.tanh(xw)
    for d, (lw_o, kd_o, b_o) in enumerate(((lwf_o, kdf_o, bf_o), (lwr_o, kdr_o, br_o))):
        wl = w0_ref[d:d + 1, :] + _dot_hp(tw[:, d * A_RANK_W:(d + 1) * A_RANK_W], w2_ref[d])
        w_log = -jax.nn.softplus(-wl) - 0.5
        lw_o[0] = -jnp.exp(w_log)
        a_gate = jax.nn.sigmoid(
            a0_ref[d:d + 1, :] + _dot_hp(xa[:, d * A_RANK_A:(d + 1) * A_RANK_A], a2_ref[d]))
        k_d = k * (1.0 + (a_gate - 1.0) * ka_ref[...])
        kd_o[0] = k_d
        b_o[0] = kk * a_gate
        k_sum = k_sum + k_d
    bonus_o[0] = _segsum(r * k_sum * rk_ref[...], ones_bd) * v


def _rwkv_prep(z, p, tm, ctx_tiles):
    bsz, rows, ncol = z.shape
    w = A_WIDTH
    hb = tm // SUBLANES
    nblk8 = rows // SUBLANES
    row_spec = lambda width: pl.BlockSpec((1, width), lambda b, t: (0, 0))
    full = lambda shape: pl.BlockSpec(shape, lambda b, t: (0,) * len(shape))
    out_spec = pl.BlockSpec((1, tm, w), lambda b, t: (b, t, 0))
    out_shape = jax.ShapeDtypeStruct((bsz, rows, w), F32)
    kern = functools.partial(_rwkv_prep_kernel, tm=tm, ctx_tiles=ctx_tiles)
    return pl.pallas_call(
        kern,
        grid=(bsz, rows // tm),
        in_specs=[pl.BlockSpec((1, tm, ncol), lambda b, t: (b, t, 0)),
                  pl.BlockSpec((1, SUBLANES, ncol),
                               lambda b, t: (b, jnp.maximum(t * hb - 1, 0), 0)),
                  pl.BlockSpec((1, SUBLANES, ncol),
                               lambda b, t: (b, jnp.minimum((t + 1) * hb, nblk8 - 1), 0)),
                  row_spec(A_COLS), row_spec(A_COLS),
                  full((2, w)), full((2, A_RANK_W, w)), full((2, w)), full((2, A_RANK_A, w)),
                  full((A_RANK_G, w)), row_spec(w), row_spec(w), row_spec(w), full((w, w))],
        out_specs=[out_spec] * 11,
        out_shape=[out_shape] * 11,
        compiler_params=_cparams(("parallel", "parallel")),
        name="rwkv_prep",
    )(z, z, z, p["shift_prev"].reshape(1, -1), p["shift_next"].reshape(1, -1),
      p["w0"], p["w2"], p["a0"], p["a2"], p["g2"], p["k_k"].reshape(1, w),
      p["k_a"].reshape(1, w), p["r_k"].reshape(1, w), _block_ones(w, HEAD_DIM))


SCAN_HEADS = 12


def _scan_kernel(rf, vf, kkf, lwf, kdf, bf, rr, vr, kkr, lwr, kdr, br, yf_o, yr_o, s_ref, *, c):
    @pl.when(pl.program_id(2) == 0)
    def _():
        s_ref[...] = jnp.zeros_like(s_ref)

    ri = lax.broadcasted_iota(jnp.int32, (c, c), 0)
    ci = lax.broadcasted_iota(jnp.int32, (c, c), 1)
    ri2 = lax.broadcasted_iota(jnp.int32, (c, 2 * c), 0)
    ci2 = lax.broadcasted_iota(jnp.int32, (c, 2 * c), 1)
    ci2 = jnp.where(ci2 >= c, ci2 - c, ci2)
    dirs = ((rf, vf, kkf, lwf, kdf, bf, ci <= ri, ci < ri, ci2 <= ri2, c - 1),
            (rr, vr, kkr, lwr, kdr, br, ci >= ri, ci > ri, ci2 >= ri2, 0))
    lhs, rhs, vs, ss, pend, incl, strict = [], [], [], [], [], [], []
    for d, (r_, v_, kk_, lw_, kd_, b_, m_incl, m_strict, m_incl2, edge) in enumerate(dirs):
        lw = lw_[0]
        tri = m_incl.astype(F32).astype(BF16)
        lw_h = lw.astype(BF16)
        lw_m, lw_l = _split_bf16(lw - lw_h.astype(F32))
        lcum = _dot(tri, lw_h) + (_dot(tri, lw_m) + _dot(tri, lw_l))
        p_in = jnp.exp(lcum)
        p_inv = jnp.exp(-lcum)
        lhs_all = jnp.concatenate([-kk_[0] * jnp.exp(lcum - lw), r_[0] * p_in], axis=0)
        rhs_all = jnp.concatenate([b_[0] * p_inv, kd_[0] * p_inv], axis=0)
        v_all = v_[0]
        for h in range(SCAN_HEADS):
            sl = slice(h * HEAD_DIM, (h + 1) * HEAD_DIM)
            lhs.append(lhs_all[:, sl])
            rhs.append(rhs_all[:, sl])
            vs.append(v_all[:, sl])
            ss.append(s_ref[d, h])
            pend.append(p_in[edge:edge + 1, sl])
            incl.append(m_incl2)
            strict.append(m_strict)
    n = len(lhs)
    idx = range(n)
    lhs2 = [_split_bf16(lhs[i]) for i in idx]
    rhs2 = [_split_bf16(rhs[i]) for i in idx]
    v2 = [_split_bf16(vs[i]) for i in idx]
    s2 = [_split_bf16(ss[i]) for i in idx]
    g = [_mm3(lhs2[i], rhs2[i], _NT) for i in idx]
    xs = [_mm3(lhs2[i], s2[i], _NT) for i in idx]
    a_ak = [_split_bf16(jnp.where(strict[i], g[i][:c, c:], 0.0)) for i in idx]
    x = [xs[i][:c] + _mm3(a_ak[i], v2[i]) for i in idx]
    apow = [_split_bf16(jnp.where(strict[i], g[i][:c, :c], 0.0)) for i in idx]
    steps = int(math.log2(c))
    for k in range(steps):
        x2 = [_split_bf16(x[i]) for i in idx]
        x = [x[i] + _mm3(apow[i], x2[i]) for i in idx]
        if k + 1 < steps:
            apow = [_split_bf16(_mm3(apow[i], apow[i])) for i in idx]
    x2 = [_split_bf16(x[i]) for i in idx]
    sav = [tuple(jnp.concatenate([x2[i][t], v2[i][t]], axis=0) for t in range(2)) for i in idx]
    a_r = [_split_bf16(jnp.where(incl[i], g[i][c:], 0.0)) for i in idx]
    y = [xs[i][c:] + _mm3(a_r[i], sav[i]) for i in idx]
    s_new = [(ss[i] + _mm3(sav[i], rhs2[i], _TN)) * pend[i] for i in idx]
    for d, y_o in enumerate((yf_o, yr_o)):
        for h in range(SCAN_HEADS):
            s_ref[d, h] = s_new[d * SCAN_HEADS + h]
        y_o[0] = jnp.concatenate(y[d * SCAN_HEADS:(d + 1) * SCAN_HEADS], axis=1)


def _rwkv_scan(r, v, kk, lwf, kdf, bf, lwr, kdr, br, ctx_len):
    bsz, rows, w = r.shape
    c = SCAN_CHUNK
    nch = rows // c
    cch = ctx_len // c
    lanes = SCAN_HEADS * HEAD_DIM
    fwd = lambda b, hp, i: (b, i, hp)
    rev = lambda b, hp, i: (b, jnp.where(i < cch, cch - 1 - i, nch - 1 + cch - i), hp)
    blk = lambda im: pl.BlockSpec((1, c, lanes), im)
    kern = functools.partial(_scan_kernel, c=c)
    out_shape = jax.ShapeDtypeStruct((bsz, rows, w), F32)
    return pl.pallas_call(
        kern,
        grid=(bsz, w // lanes, nch),
        in_specs=[blk(fwd)] * 6 + [blk(rev)] * 6,
        out_specs=[blk(fwd), blk(rev)],
        out_shape=[out_shape, out_shape],
        scratch_shapes=[pltpu.VMEM((2, SCAN_HEADS, HEAD_DIM, HEAD_DIM), F32)],
        compiler_params=_cparams(("parallel", "parallel", "arbitrary")),
        name="rwkv_scan",
    )(r, v, kk, lwf, kdf, bf, r, v, kk, lwr, kdr, br)


def _rwkv_finish_kernel(yf, yr, bonus, g, lg, lb, ones_ref, o_ref):
    y = yf[0] + yr[0]
    ones_bd = ones_ref[...]
    inv = 1.0 / HEAD_DIM
    mu = _segsum(y, ones_bd) * inv
    yc = y - mu
    var = _segsum(yc * yc, ones_bd) * inv
    yn = yc * lax.rsqrt(var + A_LNX_EPS) * lg[...] + lb[...]
    o_ref[0] = ((yn + bonus[0]) * g[0]).astype(o_ref.dtype)


def _rwkv_finish(yf, yr, bonus, g, lnx_g, lnx_b, tm):
    bsz, rows, w = yf.shape
    blk = pl.BlockSpec((1, tm, w), lambda b, t: (b, t, 0))
    row = pl.BlockSpec((1, w), lambda b, t: (0, 0))
    return pl.pallas_call(
        _rwkv_finish_kernel,
        grid=(bsz, rows // tm),
        in_specs=[blk, blk, blk, blk, row, row, pl.BlockSpec((w, w), lambda b, t: (0, 0))],
        out_specs=blk,
        out_shape=jax.ShapeDtypeStruct((bsz, rows, w), BF16),
        compiler_params=_cparams(("parallel", "parallel")),
        name="rwkv_finish",
    )(yf, yr, bonus, g, lnx_g.reshape(1, w), lnx_b.reshape(1, w), _block_ones(w, HEAD_DIM))


def _fnet_chan_kernel(z_ref, m_ref, o_ref):
    o_ref[0] = _dot(z_ref[0].astype(BF16), m_ref[...]).astype(o_ref.dtype)


def _fnet_chan(z, tm):
    bsz, rows, ncol = z.shape
    j = np.arange(B_WIDTH)
    same = (j[:, None] // B_GROUP_DIM) == (j[None, :] // B_GROUP_DIM)
    ang = 2.0 * np.pi * ((j[:, None] % B_GROUP_DIM) * (j[None, :] % B_GROUP_DIM) % B_GROUP_DIM) / B_GROUP_DIM
    m = np.concatenate([np.where(same, np.cos(ang), 0.0), np.where(same, np.sin(ang), 0.0)], axis=1)
    return pl.pallas_call(
        _fnet_chan_kernel,
        grid=(bsz, rows // tm),
        in_specs=[pl.BlockSpec((1, tm, ncol), lambda b, t: (b, t, 0)),
                  pl.BlockSpec((B_WIDTH, 2 * B_WIDTH), lambda b, t: (0, 0))],
        out_specs=pl.BlockSpec((1, tm, 2 * B_WIDTH), lambda b, t: (b, t, 0)),
        out_shape=jax.ShapeDtypeStruct((bsz, rows, 2 * B_WIDTH), BF16),
        compiler_params=_cparams(("parallel", "parallel")),
        name="fnet_chan",
    )(z, jnp.asarray(m, dtype=BF16))


def _fnet_seq_kernel(fcs_ref, cb_ref, sb_ref, c0_ref, s0_ref, o_ref, acc_ref, *, scale, nb):
    k = pl.program_id(1)

    @pl.when(k == 0)
    def _():
        acc_ref[...] = jnp.zeros_like(acc_ref)

    cb = cb_ref[...]
    sb = sb_ref[...]
    c0 = c0_ref[0]
    s0 = s0_ref[0]
    cm = (c0 * cb - s0 * sb).astype(BF16)
    sm = (s0 * cb + c0 * sb).astype(BF16)
    for b in range(nb):
        fcs = fcs_ref[b]
        acc_ref[b] += _dot(cm, fcs[:, :B_WIDTH]) - _dot(sm, fcs[:, B_WIDTH:])

    @pl.when(k == pl.num_programs(1) - 1)
    def _():
        o_ref[...] = (acc_ref[...] * scale).astype(o_ref.dtype)


def _fnet_seq(fcs, row0, length, ts, tk):
    bsz = fcs.shape[0]
    ds = jnp.arange(ts, dtype=jnp.int32)[:, None]
    tt = jnp.arange(length, dtype=jnp.int32)[None, :]
    ang = ((ds * tt) % length).astype(F32) * (2.0 * math.pi / length)
    cb, sb = jnp.cos(ang), jnp.sin(ang)
    s0 = (jnp.arange(length // ts, dtype=jnp.int32) * ts)[:, None]
    ang0 = ((s0 * tt) % length).astype(F32) * (2.0 * math.pi / length)
    c0, sn0 = jnp.cos(ang0)[:, None, :], jnp.sin(ang0)[:, None, :]
    kern = functools.partial(_fnet_seq_kernel, scale=1.0 / math.sqrt(length * B_GROUP_DIM), nb=bsz)
    koff = row0 // tk
    return pl.pallas_call(
        kern,
        grid=(length // ts, length // tk),
        in_specs=[pl.BlockSpec((bsz, tk, 2 * B_WIDTH), lambda s, k: (0, k + koff, 0)),
                  pl.BlockSpec((ts, tk), lambda s, k: (0, k)),
                  pl.BlockSpec((ts, tk), lambda s, k: (0, k)),
                  pl.BlockSpec((1, 1, tk), lambda s, k: (s, 0, k)),
                  pl.BlockSpec((1, 1, tk), lambda s, k: (s, 0, k))],
        out_specs=pl.BlockSpec((bsz, ts, B_WIDTH), lambda s, k: (0, s, 0)),
        out_shape=jax.ShapeDtypeStruct((bsz, length, B_WIDTH), BF16),
        scratch_shapes=[pltpu.VMEM((bsz, ts, B_WIDTH), F32)],
        compiler_params=_cparams(("parallel", "arbitrary")),
        name="fnet_seq",
    )(fcs, cb, sb, c0, sn0)


def _proj_residual_kernel(o1_ref, o2_ref, x_ref, mod_ref, w1_ref, w2_ref, out_ref, *, o1_transposed):
    o1 = o1_ref[0]
    if o1_transposed:
        o1 = o1.astype(F32).T.astype(BF16)
    mix = _dot(o1, w1_ref[...]) + _dot(o2_ref[0], w2_ref[...])
    gate = mod_ref[0, 0][2:3]
    out_ref[0] = x_ref[0] + gate * mix


def _proj_residual(o1, o2, x, mod, w_out, tm, tile_off, ctx_tiles, o1_transposed=False):
    bsz, rows, w2 = o2.shape
    w1 = o1.shape[1] if o1_transposed else o1.shape[2]
    d = x.shape[2]
    o1_spec = (pl.BlockSpec((1, w1, tm), lambda b, t: (b, 0, t)) if o1_transposed
               else pl.BlockSpec((1, tm, w1), lambda b, t: (b, t, 0)))
    kern = functools.partial(_proj_residual_kernel, o1_transposed=o1_transposed)
    return pl.pallas_call(
        kern,
        grid=(bsz, rows // tm),
        in_specs=[o1_spec,
                  pl.BlockSpec((1, tm, w2), lambda b, t: (b, t, 0)),
                  pl.BlockSpec((1, tm, d), lambda b, t: (b, t + tile_off, 0)),
                  pl.BlockSpec((1, 1, N_MOD, d),
                               lambda b, t: (b, jnp.where(t + tile_off >= ctx_tiles, 1, 0), 0, 0)),
                  pl.BlockSpec((w1, d), lambda b, t: (0, 0)),
                  pl.BlockSpec((w2, d), lambda b, t: (0, 0))],
        out_specs=pl.BlockSpec((1, tm, d), lambda b, t: (b, t, 0)),
        out_shape=jax.ShapeDtypeStruct((bsz, rows, d), F32),
        compiler_params=_cparams(("parallel", "parallel")),
        name="proj_residual",
    )(o1, o2, x, mod, w_out[:w1].astype(BF16), w_out[w1:].astype(BF16))


def _merge_exchange_network(n):
    pairs = []
    p = 1
    while p < n:
        k = p
        while k >= 1:
            for j in range(k % p, n - k, 2 * k):
                for i in range(min(k, n - j - k)):
                    if (i + j) // (2 * p) == (i + j + k) // (2 * p):
                        pairs.append((i + j, i + j + k))
            k //= 2
        p *= 2
    return pairs


def _top16_rows(s):
    nslab = s.shape[0] // SUBLANES
    slabs = [s[k * SUBLANES:(k + 1) * SUBLANES] for k in range(nslab)]
    for a, b in _merge_exchange_network(nslab):
        slabs[a], slabs[b] = jnp.maximum(slabs[a], slabs[b]), jnp.minimum(slabs[a], slabs[b])
    rows = []
    for t in range(PK_TOPK):
        m = jnp.max(slabs[0], axis=0, keepdims=True)
        rows.append(m)
        hit = slabs[0] >= m
        for k in range(nslab - 1 - t):
            slabs[k] = jnp.where(hit, slabs[k + 1], slabs[k])
    return rows


_PAIR_IDX = [(i, j) for i in range(PK_TOPK) for j in range(PK_TOPK) if (i + 1) * (j + 1) <= PK_TOPK]
BF16_SUBLANES = 2 * SUBLANES
PEER_SUB_E1 = 1
PEER_STEP_E1 = 16


def _rows_bf16(row):
    packed = jnp.broadcast_to(row, (BF16_SUBLANES, row.shape[1])).astype(BF16)
    return jnp.tile(packed, (N_KEYS // BF16_SUBLANES, 1))


def _peer_kernel(x_ref, g_ref, mod_ref, pqh_ref, pql_ref, sk1h_ref, sk1l_ref, sk2h_ref, sk2l_ref,
                 pu_ref, pvt_ref, gf_ref,
                 out_ref, ht_ref, r2_ref, e2_ref, c1_ref, w1_ref, s_ref, cand_ref, acc_ref,
                 *, tm, e1_per_blk, final_norm):
    eb = pl.program_id(2)

    @pl.when(eb == 0)
    def _():
        h = _norm_mod(x_ref[0], g_ref[...], mod_ref[0, 0], 3)
        ht = h.T
        hhi, hlo = _split_bf16(ht)
        ht_ref[...] = hhi
        qt = _mm3((pqh_ref[...], pql_ref[...]), (hhi, hlo))
        qs = [_split_bf16(qt[i * PK_HALF:(i + 1) * PK_HALF]) for i in range(2 * PK_HEADS)]
        for hd in range(PK_HEADS):
            s_ref[2 * hd] = _mm3((sk1h_ref[hd], sk1l_ref[hd]), qs[2 * hd])
            s_ref[2 * hd + 1] = _mm3((sk2h_ref[hd], sk2l_ref[hd]), qs[2 * hd + 1])
        for hd in range(PK_HEADS):
            s1 = s_ref[2 * hd]
            s2 = s_ref[2 * hd + 1]
            v1 = _top16_rows(s1)
            v2 = _top16_rows(s2)
            cand_ref[...] = jnp.full_like(cand_ref, NEG_INF)
            for n, (i, j) in enumerate(_PAIR_IDX):
                cand_ref[n:n + 1, :] = v1[i] + v2[j]
            cand = cand_ref[...]
            top = v1[0] + v2[0]
            zsum = jnp.zeros_like(top)
            tau = top
            for _ in range(PK_TOPK):
                m = jnp.max(cand, axis=0, keepdims=True)
                zsum = zsum + jnp.exp(m - top)
                tau = m
                cand = jnp.where(cand >= m, NEG_INF, cand)
            rank2 = jnp.full((N_KEYS, tm), float(PK_TOPK), F32)
            for j in reversed(range(PK_TOPK)):
                rank2 = jnp.where(s2 >= v2[j], float(j), rank2)
            count1 = jnp.zeros((N_KEYS, tm), F32)
            for i in reversed(range(PK_TOPK)):
                cnt = jnp.zeros_like(top)
                for j in range(PK_TOPK // (i + 1)):
                    cnt = cnt + jnp.where(v1[i] + v2[j] >= tau, 1.0, 0.0)
                count1 = jnp.where(s1 >= v1[i], cnt, count1)
            r2_ref[hd] = rank2.astype(BF16)
            c1_ref[hd] = count1
            e2_ref[hd] = jnp.exp(s2 - v2[0]).astype(BF16)
            w1_ref[hd] = jnp.exp(s1 - v1[0]) / zsum
        acc_ref[...] = jnp.zeros_like(acc_ref)

    sub = PEER_SUB_E1 * N_KEYS
    nsub = e1_per_blk // PEER_SUB_E1
    ht = ht_ref[...]

    def up(i):
        return _dot(pu_ref[i * sub:(i + 1) * sub, :], ht)

    def activation(i, hu):
        acts = []
        for jj in range(PEER_SUB_E1):
            e1 = eb * e1_per_blk + PEER_SUB_E1 * i + jj
            gate = jnp.zeros((N_KEYS, tm), BF16)
            for hd in range(PK_HEADS):
                count = _rows_bf16(c1_ref[hd, pl.ds(e1, 1), :])
                weight = _rows_bf16(w1_ref[hd, pl.ds(e1, 1), :])
                gate = gate + jnp.where(r2_ref[hd] < count, e2_ref[hd] * weight, 0)
            u = hu[jj * N_KEYS:(jj + 1) * N_KEYS]
            act = 0.5 * u * (1.0 + lax.erf(u * (1.0 / math.sqrt(2.0))))
            acts.append(act.astype(BF16) * gate)
        return jnp.concatenate(acts, axis=0)

    hus = [up(i) for i in range(nsub)]
    acts = [activation(i, hus[i]) for i in range(nsub)]
    acc_ref[...] += _dot(pvt_ref[...], jnp.concatenate(acts, axis=0))

    @pl.when(eb == pl.num_programs(2) - 1)
    def _():
        y = x_ref[0] + mod_ref[0, 0][5:6] * acc_ref[...].T
        if final_norm:
            ms = jnp.mean(y * y, axis=-1, keepdims=True)
            y = y * lax.rsqrt(ms + NORM_EPS) * gf_ref[...]
        out_ref[0] = y


def _peer(x, g, mod, p, tm, tile_off, ctx_tiles, final_norm, norm_f):
    bsz, rows, d = x.shape
    n_exp = p["pu"].shape[0]
    e1_per_blk = PEER_STEP_E1
    eblk = e1_per_blk * N_KEYS
    pqt = p["pq"].T
    pqh = pqt.astype(BF16)
    pql = (pqt - pqh.astype(F32)).astype(BF16)
    nq = pqt.shape[0]
    kern = functools.partial(_peer_kernel, tm=tm, e1_per_blk=e1_per_blk, final_norm=final_norm)
    const = lambda shape: pl.BlockSpec(shape, lambda b, t, e: (0,) * len(shape))
    return pl.pallas_call(
        kern,
        grid=(bsz, rows // tm, n_exp // eblk),
        in_specs=[pl.BlockSpec((1, tm, d), lambda b, t, e: (b, t, 0)),
                  const((1, d)),
                  pl.BlockSpec((1, 1, N_MOD, d),
                               lambda b, t, e: (b, jnp.where(t + tile_off >= ctx_tiles, 1, 0), 0, 0)),
                  const((nq, d)), const((nq, d)),
                  const((PK_HEADS, N_KEYS, PK_HALF)), const((PK_HEADS, N_KEYS, PK_HALF)),
                  const((PK_HEADS, N_KEYS, PK_HALF)), const((PK_HEADS, N_KEYS, PK_HALF)),
                  pl.BlockSpec((eblk, d), lambda b, t, e: (e, 0)),
                  pl.BlockSpec((d, eblk), lambda b, t, e: (0, e)),
                  const((1, d))],
        out_specs=pl.BlockSpec((1, tm, d), lambda b, t, e: (b, t, 0)),
        out_shape=jax.ShapeDtypeStruct((bsz, rows, d), F32),
        scratch_shapes=[pltpu.VMEM((d, tm), BF16),
                        pltpu.VMEM((PK_HEADS, N_KEYS, tm), BF16),
                        pltpu.VMEM((PK_HEADS, N_KEYS, tm), BF16),
                        pltpu.VMEM((PK_HEADS, N_KEYS, tm), F32),
                        pltpu.VMEM((PK_HEADS, N_KEYS, tm), F32),
                        pltpu.VMEM((2 * PK_HEADS, N_KEYS, tm), F32),
                        pltpu.VMEM((SUBLANES * pl.cdiv(len(_PAIR_IDX), SUBLANES), tm), F32),
                        pltpu.VMEM((d, tm), F32)],
        compiler_params=_cparams(("parallel", "parallel", "arbitrary")),
        name="peer",
    )(x, g.reshape(1, d), mod, pqh, pql, *_split_bf16(p["sk1"]), *_split_bf16(p["sk2"]),
      p["pu"].astype(BF16), p["pv"].astype(BF16).T, norm_f.reshape(1, d))


def _rope_tables(rows, ctx_len):
    t = jnp.arange(rows - ctx_len, dtype=jnp.int32)
    inv = ROPE_THETA ** (-jnp.arange(0, ROPE_AXIS_DIM, 2, dtype=F32) / ROPE_AXIS_DIM)
    ang_r = (t // GRID_W).astype(F32)[:, None] * inv
    ang_c = (t % GRID_W).astype(F32)[:, None] * inv
    cos = jnp.concatenate([jnp.cos(ang_r)] * 2 + [jnp.cos(ang_c)] * 2, axis=1)
    sin = jnp.concatenate([-jnp.sin(ang_r), jnp.sin(ang_r), -jnp.sin(ang_c), jnp.sin(ang_c)], axis=1)
    cos = jnp.concatenate([jnp.ones((ctx_len, HEAD_DIM), F32), cos], axis=0)
    sin = jnp.concatenate([jnp.zeros((ctx_len, HEAD_DIM), F32), sin], axis=0)
    return jnp.tile(cos, (1, 2)), jnp.tile(sin, (1, 2))


def _head_norm_rope(x, gain, ones_bd, cos, sin):
    ms = _segsum(x * x, ones_bd) * (1.0 / HEAD_DIM)
    y = x * lax.rsqrt(ms + NORM_EPS) * gain
    outs = []
    half = ROPE_AXIS_DIM // 2
    lane = lax.broadcasted_iota(jnp.int32, (1, LANES), 1)
    first_half = (lane % ROPE_AXIS_DIM) < half
    for i in range(x.shape[1] // LANES):
        yc = y[:, i * LANES:(i + 1) * LANES]
        partner = jnp.where(first_half, pltpu.roll(yc, LANES - half, 1), pltpu.roll(yc, half, 1))
        outs.append(yc * cos + partner * sin)
    return jnp.concatenate(outs, axis=1)


def _attn_prep_kernel(z_ref, qn_ref, kn_ref, cos_ref, sin_ref, onesq_ref, onesk_ref,
                      q_o, k_o, v_o):
    z = z_ref[0]
    cos = cos_ref[...]
    sin = sin_ref[...]
    q = _head_norm_rope(z[:, :C_WIDTH], qn_ref[...], onesq_ref[...], cos, sin)
    k = _head_norm_rope(z[:, C_WIDTH:C_WIDTH + KV_WIDTH], kn_ref[...], onesk_ref[...], cos, sin)
    q_o[0] = (q * (ATTN_SCALE * math.log2(math.e))).T.astype(q_o.dtype)
    for j in range(C_KV_HEADS):
        k_o[0, j] = k[:, j * HEAD_DIM:(j + 1) * HEAD_DIM].astype(k_o.dtype)
    v_o[0] = z[:, C_WIDTH + KV_WIDTH:C_WIDTH + 2 * KV_WIDTH].T.astype(v_o.dtype)


def _attn_prep(z, q_norm, k_norm, ctx_len, tm):
    bsz, rows, ncol = z.shape
    cos, sin = _rope_tables(rows, ctx_len)
    qn = jnp.tile(q_norm, C_HEADS).reshape(1, C_WIDTH)
    kn = jnp.tile(k_norm, C_KV_HEADS).reshape(1, KV_WIDTH)
    const = lambda shape: pl.BlockSpec(shape, lambda b, t: (0,) * len(shape))
    return pl.pallas_call(
        _attn_prep_kernel,
        grid=(bsz, rows // tm),
        in_specs=[pl.BlockSpec((1, tm, ncol), lambda b, t: (b, t, 0)),
                  const((1, C_WIDTH)), const((1, KV_WIDTH)),
                  pl.BlockSpec((tm, LANES), lambda b, t: (t, 0)),
                  pl.BlockSpec((tm, LANES), lambda b, t: (t, 0)),
                  const((C_WIDTH, C_WIDTH)), const((KV_WIDTH, KV_WIDTH))],
        out_specs=[pl.BlockSpec((1, C_WIDTH, tm), lambda b, t: (b, 0, t)),
                   pl.BlockSpec((1, C_KV_HEADS, tm, HEAD_DIM), lambda b, t: (b, 0, t, 0)),
                   pl.BlockSpec((1, KV_WIDTH, tm), lambda b, t: (b, 0, t))],
        out_shape=[jax.ShapeDtypeStruct((bsz, C_WIDTH, rows), BF16),
                   jax.ShapeDtypeStruct((bsz, C_KV_HEADS, rows, HEAD_DIM), BF16),
                   jax.ShapeDtypeStruct((bsz, KV_WIDTH, rows), BF16)],
        compiler_params=_cparams(("parallel", "parallel")),
        name="attn_prep",
    )(z, qn, kn, cos, sin, _block_ones(C_WIDTH, HEAD_DIM), _block_ones(KV_WIDTH, HEAD_DIM))


def _attn_kernel(qt_ref, k_ref, vt_ref, o_ref, m_ref, l_ref, acc_ref, *, kc):
    kt = pl.program_id(3)

    @pl.when(kt == 0)
    def _():
        m_ref[...] = jnp.full_like(m_ref, NEG_INF)
        l_ref[...] = jnp.zeros_like(l_ref)
        acc_ref[...] = jnp.zeros_like(acc_ref)

    nkv = k_ref.shape[1]
    grp = range(nkv * C_GROUP)
    qts = [qt_ref[0, g * HEAD_DIM:(g + 1) * HEAD_DIM, :] for g in grp]
    nck = k_ref.shape[2] // kc

    def scores(c):
        kbs = [k_ref[0, j, c * kc:(c + 1) * kc, :] for j in range(nkv)]
        return [_dot(kbs[g // C_GROUP], qts[g]) for g in grp]

    m = [m_ref[g] for g in grp]
    l = [l_ref[g] for g in grp]
    acc = [acc_ref[g] for g in grp]
    s_next = scores(0)
    for c in range(nck):
        s = s_next
        if c + 1 < nck:
            s_next = scores(c + 1)
        vts = [vt_ref[0, j * HEAD_DIM:(j + 1) * HEAD_DIM, c * kc:(c + 1) * kc]
               for j in range(nkv)]
        m_new = [jnp.maximum(m[g], jnp.max(s[g], axis=0, keepdims=True)) for g in grp]
        p = [jnp.exp2(s[g] - m_new[g]) for g in grp]
        pv = [_dot(vts[g // C_GROUP], p[g].astype(BF16)) for g in grp]
        for g in grp:
            alpha = jnp.exp2(m[g] - m_new[g])
            l[g] = alpha * l[g] + jnp.sum(p[g], axis=0, keepdims=True)
            acc[g] = alpha * acc[g] + pv[g]
        m = m_new
    for g in grp:
        m_ref[g] = m[g]
        l_ref[g] = l[g]
        acc_ref[g] = acc[g]

    @pl.when(kt == pl.num_programs(3) - 1)
    def _():
        o_ref[0] = jnp.concatenate(
            [acc_ref[g] / l_ref[g] for g in grp], axis=0).astype(o_ref.dtype)


ATTN_MAX_KEY_TILE = 8448
ATTN_MAX_KEY_CHUNK = 256
ATTN_KV_PER_STEP = 1


def _key_tile(rows, limit=ATTN_MAX_KEY_TILE):
    return max(t for t in range(LANES, min(rows, limit) + 1, LANES) if rows % t == 0)


def _attention(qt, k, vt, ctx_len, tq, tk):
    bsz, _, rows = qt.shape
    seq = rows - ctx_len
    qoff = ctx_len // tq
    nkv = ATTN_KV_PER_STEP
    nq = nkv * C_GROUP
    gw = nq * HEAD_DIM
    kern = functools.partial(_attn_kernel, kc=_key_tile(tk, ATTN_MAX_KEY_CHUNK))
    return pl.pallas_call(
        kern,
        grid=(bsz, C_KV_HEADS // nkv, seq // tq, rows // tk),
        in_specs=[pl.BlockSpec((1, gw, tq), lambda b, j, i, kk: (b, j, i + qoff)),
                  pl.BlockSpec((1, nkv, tk, HEAD_DIM), lambda b, j, i, kk: (b, j, kk, 0)),
                  pl.BlockSpec((1, nkv * HEAD_DIM, tk), lambda b, j, i, kk: (b, j, kk))],
        out_specs=pl.BlockSpec((1, gw, tq), lambda b, j, i, kk: (b, j, i)),
        out_shape=jax.ShapeDtypeStruct((bsz, C_WIDTH, seq), BF16),
        scratch_shapes=[pltpu.VMEM((nq, 1, tq), F32),
                        pltpu.VMEM((nq, 1, tq), F32),
                        pltpu.VMEM((nq, HEAD_DIM, tq), F32)],
        compiler_params=_cparams(("parallel", "parallel", "parallel", "arbitrary")),
        name="attention",
    )(qt, k, vt)


def _conv_kernel(zc_ref, zp_ref, zn_ref, w_ref, b_ref, lg_ref, lb_ref, o_ref, ybuf, *, tm, halo):
    t = pl.program_id(1)
    nt = pl.num_programs(1)

    def glu(u):
        return u[:, :D_WIDTH] * jax.nn.sigmoid(u[:, D_WIDTH:])

    ybuf[0:halo, :] = jnp.where(t == 0, 0.0, glu(zp_ref[0]))
    ybuf[halo:halo + tm, :] = glu(zc_ref[0])
    ybuf[halo + tm:, :] = jnp.where(t == nt - 1, 0.0, glu(zn_ref[0]))
    acc = jnp.zeros((tm, D_WIDTH), F32)
    for j in range(D_CONV_WIDTH):
        off = halo - D_PAD + j
        acc = acc + w_ref[j:j + 1, :] * ybuf[off:off + tm, :]
    y = acc + b_ref[...]
    mu = jnp.mean(y, axis=-1, keepdims=True)
    yc = y - mu
    var = jnp.mean(yc * yc, axis=-1, keepdims=True)
    yn = yc * lax.rsqrt(var + LN_EPS) * lg_ref[...] + lb_ref[...]
    o_ref[0] = (yn * jax.nn.sigmoid(yn)).astype(o_ref.dtype)


def _conformer_conv(z, dw_w, dw_b, cn_g, cn_b, ctx_len, tm):
    bsz, rows, ncol = z.shape
    seq = rows - ctx_len
    halo = 2 * SUBLANES
    hb = tm // halo
    off = ctx_len // tm
    offh = ctx_len // halo
    nh = seq // halo
    kern = functools.partial(_conv_kernel, tm=tm, halo=halo)
    row = lambda w: pl.BlockSpec((1, w), lambda b, t: (0, 0))
    return pl.pallas_call(
        kern,
        grid=(bsz, seq // tm),
        in_specs=[pl.BlockSpec((1, tm, ncol), lambda b, t: (b, t + off, 0)),
                  pl.BlockSpec((1, halo, ncol),
                               lambda b, t: (b, offh + jnp.maximum(t * hb - 1, 0), 0)),
                  pl.BlockSpec((1, halo, ncol),
                               lambda b, t: (b, offh + jnp.minimum((t + 1) * hb, nh - 1), 0)),
                  pl.BlockSpec((D_CONV_WIDTH, D_WIDTH), lambda b, t: (0, 0)),
                  row(D_WIDTH), row(D_WIDTH), row(D_WIDTH)],
        out_specs=pl.BlockSpec((1, tm, D_WIDTH), lambda b, t: (b, t, 0)),
        out_shape=jax.ShapeDtypeStruct((bsz, seq, D_WIDTH), BF16),
        scratch_shapes=[pltpu.VMEM((tm + 2 * halo, D_WIDTH), F32)],
        compiler_params=_cparams(("parallel", "parallel")),
        name="conformer_conv",
    )(z, z, z, dw_w, dw_b.reshape(1, -1), cn_g.reshape(1, -1), cn_b.reshape(1, -1))


def _forward(x, c, ctx, c_ctx, l0, l1, norm_f):
    bsz, seq, d = x.shape
    ctx_len = ctx.shape[1]
    tm = min(256, ctx_len)
    ctx_tiles = ctx_len // tm
    xs = jnp.concatenate([ctx, x], axis=1)

    mod = _modulation(c, c_ctx, l0["mod_w"], l0["mod_b"])
    z, zf = _normmod_matmul(xs, l0["norm1"], mod, l0["w_in"], tm, ctx_tiles, A_COLS)
    r, v, g, bonus, kk, lwf, kdf, bf, lwr, kdr, br = _rwkv_prep(z, l0, tm, ctx_tiles)
    yf, yr = _rwkv_scan(r, v, kk, lwf, kdf, bf, lwr, kdr, br, ctx_len)
    o_rwkv = _rwkv_finish(yf, yr, bonus, g, l0["lnx_g"], l0["lnx_b"], tm)
    fcs = _fnet_chan(zf, tm)
    ts_c = min(256, ctx_len)
    ts_l = min(512, seq)
    f_ctx = _fnet_seq(fcs, 0, ctx_len, ts_c, ts_c)
    f_lat = _fnet_seq(fcs[:, ctx_len:], 0, seq, ts_l, ts_l)
    o_fnet = jnp.concatenate([f_ctx, f_lat], axis=1)
    xs = _proj_residual(o_rwkv, o_fnet, xs, mod, l0["w_out"], tm, 0, ctx_tiles)
    xs = _peer(xs, l0["norm2"], mod, l0, tm, 0, ctx_tiles, False, norm_f)

    mod = _modulation(c, c_ctx, l1["mod_w"], l1["mod_b"])
    z, zu = _normmod_matmul(xs, l1["norm1"], mod, l1["w_in"], tm, ctx_tiles,
                            C_WIDTH + 2 * KV_WIDTH)
    qt, k, vt = _attn_prep(z, l1["q_norm"], l1["k_norm"], ctx_len, tm)
    o_attn = _attention(qt, k, vt, ctx_len, tm, _key_tile(ctx_len + seq))
    o_conv = _conformer_conv(zu, l1["dw_w"], l1["dw_b"], l1["cn_g"], l1["cn_b"], ctx_len, tm)
    xl = _proj_residual(o_attn, o_conv, xs, mod, l1["w_out"], tm, ctx_tiles, ctx_tiles,
                        o1_transposed=True)
    return _peer(xl, l1["norm2"], mod, l1, tm, ctx_tiles, ctx_tiles, True, norm_f)


def kernel(x, c, ctx, c_ctx, l0_mod_w, l0_mod_b, l0_norm1, l0_w_in, l0_shift_prev, l0_shift_next, l0_w0, l0_w2, l0_a0, l0_a2, l0_g2, l0_k_k, l0_k_a, l0_r_k, l0_lnx_g, l0_lnx_b, l0_w_out, l0_norm2, l0_pq, l0_sk1, l0_sk2, l0_pu, l0_pv, l1_mod_w, l1_mod_b, l1_norm1, l1_w_in, l1_q_norm, l1_k_norm, l1_dw_w, l1_dw_b, l1_cn_g, l1_cn_b, l1_w_out, l1_norm2, l1_pq, l1_sk1, l1_sk2, l1_pu, l1_pv, norm_f):
    l0 = dict(mod_w=l0_mod_w, mod_b=l0_mod_b, norm1=l0_norm1, w_in=l0_w_in,
              shift_prev=l0_shift_prev, shift_next=l0_shift_next, w0=l0_w0, w2=l0_w2,
              a0=l0_a0, a2=l0_a2, g2=l0_g2, k_k=l0_k_k, k_a=l0_k_a, r_k=l0_r_k,
              lnx_g=l0_lnx_g, lnx_b=l0_lnx_b, w_out=l0_w_out, norm2=l0_norm2,
              pq=l0_pq, sk1=l0_sk1, sk2=l0_sk2, pu=l0_pu, pv=l0_pv)
    l1 = dict(mod_w=l1_mod_w, mod_b=l1_mod_b, norm1=l1_norm1, w_in=l1_w_in,
              q_norm=l1_q_norm, k_norm=l1_k_norm, dw_w=l1_dw_w, dw_b=l1_dw_b,
              cn_g=l1_cn_g, cn_b=l1_cn_b, w_out=l1_w_out, norm2=l1_norm2,
              pq=l1_pq, sk1=l1_sk1, sk2=l1_sk2, pu=l1_pu, pv=l1_pv)
    return _forward(x, c, ctx, c_ctx, l0, l1, norm_f)
```

```python
import functools
import math

import jax
import jax.numpy as jnp
import numpy as np
from jax import lax
from jax.experimental import pallas as pl
from jax.experimental.pallas import tpu as pltpu

F32 = jnp.float32
BF16 = jnp.bfloat16

LANES = 128
SUBLANES = 8
VMEM_LIMIT_BYTES = 56 * 1024 * 1024

N_MOD = 6
NORM_EPS = 1e-6
LN_EPS = 1e-5
GRID_W = 64
HEAD_DIM = 64
A_WIDTH = 768
A_HEADS = A_WIDTH // HEAD_DIM
A_RANK_W = 64
A_RANK_A = 64
A_RANK_G = 128
A_LNX_EPS = 64e-5
A_COLS = 3 * A_WIDTH + 2 * A_RANK_W + 2 * A_RANK_A + A_RANK_G
B_WIDTH = 256
B_GROUP_DIM = 64
C_WIDTH = 768
C_HEADS = 12
C_KV_HEADS = 4
C_GROUP = C_HEADS // C_KV_HEADS
KV_WIDTH = C_KV_HEADS * HEAD_DIM
ROPE_AXIS_DIM = HEAD_DIM // 2
ROPE_THETA = 10000.0
ATTN_SCALE = HEAD_DIM ** -0.5
D_WIDTH = 256
D_CONV_WIDTH = 31
D_PAD = D_CONV_WIDTH // 2
PK_HEADS = 8
PK_DIM = 256
PK_HALF = 128
N_KEYS = 128
PK_TOPK = 16

SCAN_CHUNK = 64
NEG_INF = float("-inf")


def _cparams(semantics):
    return pltpu.CompilerParams(dimension_semantics=semantics,
                                vmem_limit_bytes=VMEM_LIMIT_BYTES)


def _split_bf16(x):
    hi = x.astype(BF16)
    lo = (x - hi.astype(F32)).astype(BF16)
    return hi, lo


def _dot(a, b):
    return jnp.dot(a, b, preferred_element_type=F32)


def _dot_hp(a, b):
    return _mm3(_split_bf16(a), _split_bf16(b))


_NN = (((1,), (0,)), ((), ()))
_NT = (((1,), (1,)), ((), ()))
_TN = (((0,), (0,)), ((), ()))


def _mm3(a, b, dims=_NN):
    (ah, al), (bh, bl) = a, b
    dg = lambda x, y: lax.dot_general(x, y, dims, preferred_element_type=F32)
    return dg(ah, bh) + (dg(ah, bl) + dg(al, bh))


def _segsum(x, ones_bd):
    hi, lo = _split_bf16(x)
    return _dot(hi, ones_bd) + _dot(lo, ones_bd)


def _block_ones(width, seg):
    r = np.arange(width) // seg
    return jnp.asarray((r[:, None] == r[None, :]).astype(np.float32), dtype=BF16)


def _mod_kernel(c_ref, w_ref, b_ref, o_ref):
    c = c_ref[...]
    s = c * jax.nn.sigmoid(c)
    o_ref[...] = _dot_hp(s, w_ref[...]) + b_ref[...]


def _modulation(c, c_ctx, mod_w, mod_b):
    bsz, d = c.shape
    rows = SUBLANES * pl.cdiv(bsz + 1, SUBLANES)
    cc = jnp.zeros((rows, d), F32).at[:bsz].set(c).at[bsz].set(c_ctx)
    n = mod_w.shape[1]
    tn = n // 4
    out = pl.pallas_call(
        _mod_kernel,
        grid=(n // tn,),
        in_specs=[pl.BlockSpec((rows, d), lambda j: (0, 0)),
                  pl.BlockSpec((d, tn), lambda j: (0, j)),
                  pl.BlockSpec((1, tn), lambda j: (0, j))],
        out_specs=pl.BlockSpec((rows, tn), lambda j: (0, j)),
        out_shape=jax.ShapeDtypeStruct((rows, n), F32),
        compiler_params=_cparams(("arbitrary",)),
        name="adaln_mod",
    )(cc, mod_w, mod_b.reshape(1, n))
    lat = out[:bsz].reshape(bsz, N_MOD, d)
    ctx = jnp.broadcast_to(out[bsz].reshape(1, N_MOD, d), (bsz, N_MOD, d))
    return jnp.stack([ctx, lat], axis=1)


def _norm_mod(x, g, mod, row):
    ms = jnp.mean(x * x, axis=-1, keepdims=True)
    y = x * lax.rsqrt(ms + NORM_EPS) * g
    return y * (1.0 + mod[row + 1:row + 2]) + mod[row:row + 1]


def _normmod_matmul_kernel(x_ref, g_ref, mod_ref, w_ref, o1_ref, o2_ref):
    h = _norm_mod(x_ref[0], g_ref[...], mod_ref[0, 0], 0)
    z = _dot(h.astype(BF16), w_ref[...])
    n1 = o1_ref.shape[2]
    o1_ref[0] = z[:, :n1]
    o2_ref[0] = z[:, n1:]


def _normmod_matmul(x, g, mod, w, tm, ctx_tiles, n1):
    bsz, rows, d = x.shape
    n = w.shape[1]
    widths = (n1, n - n1)
    return pl.pallas_call(
        _normmod_matmul_kernel,
        grid=(bsz, rows // tm),
        in_specs=[pl.BlockSpec((1, tm, d), lambda b, t: (b, t, 0)),
                  pl.BlockSpec((1, d), lambda b, t: (0, 0)),
                  pl.BlockSpec((1, 1, N_MOD, d),
                               lambda b, t: (b, jnp.where(t >= ctx_tiles, 1, 0), 0, 0)),
                  pl.BlockSpec((d, n), lambda b, t: (0, 0))],
        out_specs=[pl.BlockSpec((1, tm, wd), lambda b, t: (b, t, 0)) for wd in widths],
        out_shape=[jax.ShapeDtypeStruct((bsz, rows, wd), F32) for wd in widths],
        compiler_params=_cparams(("parallel", "parallel")),
        name="normmod_proj",
    )(x, g.reshape(1, d), mod, w.astype(BF16))


def _rwkv_prep_kernel(z_ref, zp_ref, zn_ref, mup_ref, mun_ref, w0_ref, w2_ref, a0_ref, a2_ref,
                      g2_ref, kk_ref, ka_ref, rk_ref, ones_ref,
                      r_o, v_o, g_o, bonus_o, kkn_o, lwf_o, kdf_o, bf_o, lwr_o, kdr_o, br_o,
                      *, tm, ctx_tiles):
    t = pl.program_id(1)
    nt = pl.num_programs(1)
    z = z_ref[0][:, :A_COLS]
    zp_row = zp_ref[0][SUBLANES - 1:SUBLANES, :A_COLS]
    zn_row = zn_ref[0][0:1, :A_COLS]
    first = jnp.logical_or(t == 0, t == ctx_tiles)
    last = jnp.logical_or(t == ctx_tiles - 1, t == nt - 1)
    zp_row = jnp.where(first, 0.0, zp_row)
    zn_row = jnp.where(last, 0.0, zn_row)
    ridx = lax.broadcasted_iota(jnp.int32, (tm, 1), 0)
    z_prev = jnp.where(ridx == 0, zp_row, pltpu.roll(z, 1, 0))
    z_next = jnp.where(ridx == tm - 1, zn_row, pltpu.roll(z, tm - 1, 0))
    zs = z + mup_ref[...] * (z_prev - z) + mun_ref[...] * (z_next - z)

    w = A_WIDTH
    r = zs[:, 0:w]
    k = zs[:, w:2 * w]
    v = zs[:, 2 * w:3 * w]
    o = 3 * w
    xw = zs[:, o:o + 2 * A_RANK_W]
    o += 2 * A_RANK_W
    xa = zs[:, o:o + 2 * A_RANK_A]
    o += 2 * A_RANK_A
    xg = zs[:, o:o + A_RANK_G]
    ones_bd = ones_ref[...]

    g_o[0] = _dot_hp(jax.nn.sigmoid(xg), g2_ref[...])
    kk = k * kk_ref[...]
    nrm = jnp.sqrt(_segsum(kk * kk, ones_bd))
    kk = kk / jnp.maximum(nrm, 1e-12)
    r_o[0] = r
    v_o[0] = v
    kkn_o[0] = kk
    k_sum = jnp.zeros_like(r)
    tw = jnp.tanh(xw)
    for d, (lw_o, kd_o, b_o) in enumerate(((lwf_o, kdf_o, bf_o), (lwr_o, kdr_o, br_o))):
        wl = w0_ref[d:d + 1, :] + _dot_hp(tw[:, d * A_RANK_W:(d + 1) * A_RANK_W], w2_ref[d])
        w_log = -jax.nn.softplus(-wl) - 0.5
        lw_o[0] = -jnp.exp(w_log)
        a_gate = jax.nn.sigmoid(
            a0_ref[d:d + 1, :] + _dot_hp(xa[:, d * A_RANK_A:(d + 1) * A_RANK_A], a2_ref[d]))
        k_d = k * (1.0 + (a_gate - 1.0) * ka_ref[...])
        kd_o[0] = k_d
        b_o[0] = kk * a_gate
        k_sum = k_sum + k_d
    bonus_o[0] = _segsum(r * k_sum * rk_ref[...], ones_bd) * v


def _rwkv_prep(z, p, tm, ctx_tiles):
    bsz, rows, ncol = z.shape
    w = A_WIDTH
    hb = tm // SUBLANES
    nblk8 = rows // SUBLANES
    row_spec = lambda width: pl.BlockSpec((1, width), lambda b, t: (0, 0))
    full = lambda shape: pl.BlockSpec(shape, lambda b, t: (0,) * len(shape))
    out_spec = pl.BlockSpec((1, tm, w), lambda b, t: (b, t, 0))
    out_shape = jax.ShapeDtypeStruct((bsz, rows, w), F32)
    kern = functools.partial(_rwkv_prep_kernel, tm=tm, ctx_tiles=ctx_tiles)
    return pl.pallas_call(
        kern,
        grid=(bsz, rows // tm),
        in_specs=[pl.BlockSpec((1, tm, ncol), lambda b, t: (b, t, 0)),
                  pl.BlockSpec((1, SUBLANES, ncol),
                               lambda b, t: (b, jnp.maximum(t * hb - 1, 0), 0)),
                  pl.BlockSpec((1, SUBLANES, ncol),
                               lambda b, t: (b, jnp.minimum((t + 1) * hb, nblk8 - 1), 0)),
                  row_spec(A_COLS), row_spec(A_COLS),
                  full((2, w)), full((2, A_RANK_W, w)), full((2, w)), full((2, A_RANK_A, w)),
                  full((A_RANK_G, w)), row_spec(w), row_spec(w), row_spec(w), full((w, w))],
        out_specs=[out_spec] * 11,
        out_shape=[out_shape] * 11,
        compiler_params=_cparams(("parallel", "parallel")),
        name="rwkv_prep",
    )(z, z, z, p["shift_prev"].reshape(1, -1), p["shift_next"].reshape(1, -1),
      p["w0"], p["w2"], p["a0"], p["a2"], p["g2"], p["k_k"].reshape(1, w),
      p["k_a"].reshape(1, w), p["r_k"].reshape(1, w), _block_ones(w, HEAD_DIM))


SCAN_HEADS = 12


def _scan_kernel(rf, vf, kkf, lwf, kdf, bf, rr, vr, kkr, lwr, kdr, br, yf_o, yr_o, s_ref, *, c):
    @pl.when(pl.program_id(2) == 0)
    def _():
        s_ref[...] = jnp.zeros_like(s_ref)

    ri = lax.broadcasted_iota(jnp.int32, (c, c), 0)
    ci = lax.broadcasted_iota(jnp.int32, (c, c), 1)
    ri2 = lax.broadcasted_iota(jnp.int32, (c, 2 * c), 0)
    ci2 = lax.broadcasted_iota(jnp.int32, (c, 2 * c), 1)
    ci2 = jnp.where(ci2 >= c, ci2 - c, ci2)
    dirs = ((rf, vf, kkf, lwf, kdf, bf, ci <= ri, ci < ri, ci2 <= ri2, c - 1),
            (rr, vr, kkr, lwr, kdr, br, ci >= ri, ci > ri, ci2 >= ri2, 0))
    lhs, rhs, vs, ss, pend, incl, strict = [], [], [], [], [], [], []
    for d, (r_, v_, kk_, lw_, kd_, b_, m_incl, m_strict, m_incl2, edge) in enumerate(dirs):
        lw = lw_[0]
        tri = m_incl.astype(F32).astype(BF16)
        lw_h = lw.astype(BF16)
        lw_m, lw_l = _split_bf16(lw - lw_h.astype(F32))
        lcum = _dot(tri, lw_h) + (_dot(tri, lw_m) + _dot(tri, lw_l))
        p_in = jnp.exp(lcum)
        p_inv = jnp.exp(-lcum)
        lhs_all = jnp.concatenate([-kk_[0] * jnp.exp(lcum - lw), r_[0] * p_in], axis=0)
        rhs_all = jnp.concatenate([b_[0] * p_inv, kd_[0] * p_inv], axis=0)
        v_all = v_[0]
        for h in range(SCAN_HEADS):
            sl = slice(h * HEAD_DIM, (h + 1) * HEAD_DIM)
            lhs.append(lhs_all[:, sl])
            rhs.append(rhs_all[:, sl])
            vs.append(v_all[:, sl])
            ss.append(s_ref[d, h])
            pend.append(p_in[edge:edge + 1, sl])
            incl.append(m_incl2)
            strict.append(m_strict)
    n = len(lhs)
    idx = range(n)
    lhs2 = [_split_bf16(lhs[i]) for i in idx]
    rhs2 = [_split_bf16(rhs[i]) for i in idx]
    v2 = [_split_bf16(vs[i]) for i in idx]
    s2 = [_split_bf16(ss[i]) for i in idx]
    g = [_mm3(lhs2[i], rhs2[i], _NT) for i in idx]
    xs = [_mm3(lhs2[i], s2[i], _NT) for i in idx]
    a_ak = [_split_bf16(jnp.where(strict[i], g[i][:c, c:], 0.0)) for i in idx]
    x = [xs[i][:c] + _mm3(a_ak[i], v2[i]) for i in idx]
    apow = [_split_bf16(jnp.where(strict[i], g[i][:c, :c], 0.0)) for i in idx]
    steps = int(math.log2(c))
    for k in range(steps):
        x2 = [_split_bf16(x[i]) for i in idx]
        x = [x[i] + _mm3(apow[i], x2[i]) for i in idx]
        if k + 1 < steps:
            apow = [_split_bf16(_mm3(apow[i], apow[i])) for i in idx]
    x2 = [_split_bf16(x[i]) for i in idx]
    sav = [tuple(jnp.concatenate([x2[i][t], v2[i][t]], axis=0) for t in range(2)) for i in idx]
    a_r = [_split_bf16(jnp.where(incl[i], g[i][c:], 0.0)) for i in idx]
    y = [xs[i][c:] + _mm3(a_r[i], sav[i]) for i in idx]
    s_new = [(ss[i] + _mm3(sav[i], rhs2[i], _TN)) * pend[i] for i in idx]
    for d, y_o in enumerate((yf_o, yr_o)):
        for h in range(SCAN_HEADS):
            s_ref[d, h] = s_new[d * SCAN_HEADS + h]
        y_o[0] = jnp.concatenate(y[d * SCAN_HEADS:(d + 1) * SCAN_HEADS], axis=1)


def _rwkv_scan(r, v, kk, lwf, kdf, bf, lwr, kdr, br, ctx_len):
    bsz, rows, w = r.shape
    c = SCAN_CHUNK
    nch = rows // c
    cch = ctx_len // c
    lanes = SCAN_HEADS * HEAD_DIM
    fwd = lambda b, hp, i: (b, i, hp)
    rev = lambda b, hp, i: (b, jnp.where(i < cch, cch - 1 - i, nch - 1 + cch - i), hp)
    blk = lambda im: pl.BlockSpec((1, c, lanes), im)
    kern = functools.partial(_scan_kernel, c=c)
    out_shape = jax.ShapeDtypeStruct((bsz, rows, w), F32)
    return pl.pallas_call(
        kern,
        grid=(bsz, w // lanes, nch),
        in_specs=[blk(fwd)] * 6 + [blk(rev)] * 6,
        out_specs=[blk(fwd), blk(rev)],
        out_shape=[out_shape, out_shape],
        scratch_shapes=[pltpu.VMEM((2, SCAN_HEADS, HEAD_DIM, HEAD_DIM), F32)],
        compiler_params=_cparams(("parallel", "parallel", "arbitrary")),
        name="rwkv_scan",
    )(r, v, kk, lwf, kdf, bf, r, v, kk, lwr, kdr, br)


def _rwkv_finish_kernel(yf, yr, bonus, g, lg, lb, ones_ref, o_ref):
    y = yf[0] + yr[0]
    ones_bd = ones_ref[...]
    inv = 1.0 / HEAD_DIM
    mu = _segsum(y, ones_bd) * inv
    yc = y - mu
    var = _segsum(yc * yc, ones_bd) * inv
    yn = yc * lax.rsqrt(var + A_LNX_EPS) * lg[...] + lb[...]
    o_ref[0] = ((yn + bonus[0]) * g[0]).astype(o_ref.dtype)


def _rwkv_finish(yf, yr, bonus, g, lnx_g, lnx_b, tm):
    bsz, rows, w = yf.shape
    blk = pl.BlockSpec((1, tm, w), lambda b, t: (b, t, 0))
    row = pl.BlockSpec((1, w), lambda b, t: (0, 0))
    return pl.pallas_call(
        _rwkv_finish_kernel,
        grid=(bsz, rows // tm),
        in_specs=[blk, blk, blk, blk, row, row, pl.BlockSpec((w, w), lambda b, t: (0, 0))],
        out_specs=blk,
        out_shape=jax.ShapeDtypeStruct((bsz, rows, w), BF16),
        compiler_params=_cparams(("parallel", "parallel")),
        name="rwkv_finish",
    )(yf, yr, bonus, g, lnx_g.reshape(1, w), lnx_b.reshape(1, w), _block_ones(w, HEAD_DIM))


def _fnet_chan_kernel(z_ref, m_ref, o_ref):
    o_ref[0] = _dot(z_ref[0].astype(BF16), m_ref[...]).astype(o_ref.dtype)


def _fnet_chan(z, tm):
    bsz, rows, ncol = z.shape
    j = np.arange(B_WIDTH)
    same = (j[:, None] // B_GROUP_DIM) == (j[None, :] // B_GROUP_DIM)
    ang = 2.0 * np.pi * ((j[:, None] % B_GROUP_DIM) * (j[None, :] % B_GROUP_DIM) % B_GROUP_DIM) / B_GROUP_DIM
    m = np.concatenate([np.where(same, np.cos(ang), 0.0), np.where(same, np.sin(ang), 0.0)], axis=1)
    return pl.pallas_call(
        _fnet_chan_kernel,
        grid=(bsz, rows // tm),
        in_specs=[pl.BlockSpec((1, tm, ncol), lambda b, t: (b, t, 0)),
                  pl.BlockSpec((B_WIDTH, 2 * B_WIDTH), lambda b, t: (0, 0))],
        out_specs=pl.BlockSpec((1, tm, 2 * B_WIDTH), lambda b, t: (b, t, 0)),
        out_shape=jax.ShapeDtypeStruct((bsz, rows, 2 * B_WIDTH), BF16),
        compiler_params=_cparams(("parallel", "parallel")),
        name="fnet_chan",
    )(z, jnp.asarray(m, dtype=BF16))


def _fnet_seq_kernel(fcs_ref, cb_ref, sb_ref, c0_ref, s0_ref, o_ref, acc_ref, *, scale, nb):
    k = pl.program_id(1)

    @pl.when(k == 0)
    def _():
        acc_ref[...] = jnp.zeros_like(acc_ref)

    cb = cb_ref[...]
    sb = sb_ref[...]
    c0 = c0_ref[0]
    s0 = s0_ref[0]
    cm = (c0 * cb - s0 * sb).astype(BF16)
    sm = (s0 * cb + c0 * sb).astype(BF16)
    for b in range(nb):
        fcs = fcs_ref[b]
        acc_ref[b] += _dot(cm, fcs[:, :B_WIDTH]) - _dot(sm, fcs[:, B_WIDTH:])

    @pl.when(k == pl.num_programs(1) - 1)
    def _():
        o_ref[...] = (acc_ref[...] * scale).astype(o_ref.dtype)


def _fnet_seq(fcs, row0, length, ts, tk):
    bsz = fcs.shape[0]
    ds = jnp.arange(ts, dtype=jnp.int32)[:, None]
    tt = jnp.arange(length, dtype=jnp.int32)[None, :]
    ang = ((ds * tt) % length).astype(F32) * (2.0 * math.pi / length)
    cb, sb = jnp.cos(ang), jnp.sin(ang)
    s0 = (jnp.arange(length // ts, dtype=jnp.int32) * ts)[:, None]
    ang0 = ((s0 * tt) % length).astype(F32) * (2.0 * math.pi / length)
    c0, sn0 = jnp.cos(ang0)[:, None, :], jnp.sin(ang0)[:, None, :]
    kern = functools.partial(_fnet_seq_kernel, scale=1.0 / math.sqrt(length * B_GROUP_DIM), nb=bsz)
    koff = row0 // tk
    return pl.pallas_call(
        kern,
        grid=(length // ts, length // tk),
        in_specs=[pl.BlockSpec((bsz, tk, 2 * B_WIDTH), lambda s, k: (0, k + koff, 0)),
                  pl.BlockSpec((ts, tk), lambda s, k: (0, k)),
                  pl.BlockSpec((ts, tk), lambda s, k: (0, k)),
                  pl.BlockSpec((1, 1, tk), lambda s, k: (s, 0, k)),
                  pl.BlockSpec((1, 1, tk), lambda s, k: (s, 0, k))],
        out_specs=pl.BlockSpec((bsz, ts, B_WIDTH), lambda s, k: (0, s, 0)),
        out_shape=jax.ShapeDtypeStruct((bsz, length, B_WIDTH), BF16),
        scratch_shapes=[pltpu.VMEM((bsz, ts, B_WIDTH), F32)],
        compiler_params=_cparams(("parallel", "arbitrary")),
        name="fnet_seq",
    )(fcs, cb, sb, c0, sn0)


def _proj_residual_kernel(o1_ref, o2_ref, x_ref, mod_ref, w1_ref, w2_ref, out_ref, *, o1_transposed):
    o1 = o1_ref[0]
    if o1_transposed:
        o1 = o1.astype(F32).T.astype(BF16)
    mix = _dot(o1, w1_ref[...]) + _dot(o2_ref[0], w2_ref[...])
    gate = mod_ref[0, 0][2:3]
    out_ref[0] = x_ref[0] + gate * mix


def _proj_residual(o1, o2, x, mod, w_out, tm, tile_off, ctx_tiles, o1_transposed=False):
    bsz, rows, w2 = o2.shape
    w1 = o1.shape[1] if o1_transposed else o1.shape[2]
    d = x.shape[2]
    o1_spec = (pl.BlockSpec((1, w1, tm), lambda b, t: (b, 0, t)) if o1_transposed
               else pl.BlockSpec((1, tm, w1), lambda b, t: (b, t, 0)))
    kern = functools.partial(_proj_residual_kernel, o1_transposed=o1_transposed)
    return pl.pallas_call(
        kern,
        grid=(bsz, rows // tm),
        in_specs=[o1_spec,
                  pl.BlockSpec((1, tm, w2), lambda b, t: (b, t, 0)),
                  pl.BlockSpec((1, tm, d), lambda b, t: (b, t + tile_off, 0)),
                  pl.BlockSpec((1, 1, N_MOD, d),
                               lambda b, t: (b, jnp.where(t + tile_off >= ctx_tiles, 1, 0), 0, 0)),
                  pl.BlockSpec((w1, d), lambda b, t: (0, 0)),
                  pl.BlockSpec((w2, d), lambda b, t: (0, 0))],
        out_specs=pl.BlockSpec((1, tm, d), lambda b, t: (b, t, 0)),
        out_shape=jax.ShapeDtypeStruct((bsz, rows, d), F32),
        compiler_params=_cparams(("parallel", "parallel")),
        name="proj_residual",
    )(o1, o2, x, mod, w_out[:w1].astype(BF16), w_out[w1:].astype(BF16))


def _merge_exchange_network(n):
    pairs = []
    p = 1
    while p < n:
        k = p
        while k >= 1:
            for j in range(k % p, n - k, 2 * k):
                for i in range(min(k, n - j - k)):
                    if (i + j) // (2 * p) == (i + j + k) // (2 * p):
                        pairs.append((i + j, i + j + k))
            k //= 2
        p *= 2
    return pairs


def _top16_rows(s):
    nslab = s.shape[0] // SUBLANES
    slabs = [s[k * SUBLANES:(k + 1) * SUBLANES] for k in range(nslab)]
    for a, b in _merge_exchange_network(nslab):
        slabs[a], slabs[b] = jnp.maximum(slabs[a], slabs[b]), jnp.minimum(slabs[a], slabs[b])
    rows = []
    for t in range(PK_TOPK):
        m = jnp.max(slabs[0], axis=0, keepdims=True)
        rows.append(m)
        hit = slabs[0] >= m
        for k in range(nslab - 1 - t):
            slabs[k] = jnp.where(hit, slabs[k + 1], slabs[k])
    return rows


_PAIR_IDX = [(i, j) for i in range(PK_TOPK) for j in range(PK_TOPK) if (i + 1) * (j + 1) <= PK_TOPK]
BF16_SUBLANES = 2 * SUBLANES
PEER_SUB_E1 = 1
PEER_STEP_E1 = 16


def _rows_bf16(row):
    packed = jnp.broadcast_to(row, (BF16_SUBLANES, row.shape[1])).astype(BF16)
    return jnp.tile(packed, (N_KEYS // BF16_SUBLANES, 1))


def _peer_kernel(x_ref, g_ref, mod_ref, pqh_ref, pql_ref, sk1h_ref, sk1l_ref, sk2h_ref, sk2l_ref,
                 pu_ref, pvt_ref, gf_ref,
                 out_ref, ht_ref, r2_ref, e2_ref, c1_ref, w1_ref, s_ref, cand_ref, acc_ref,
                 *, tm, e1_per_blk, final_norm):
    eb = pl.program_id(2)

    @pl.when(eb == 0)
    def _():
        h = _norm_mod(x_ref[0], g_ref[...], mod_ref[0, 0], 3)
        ht = h.T
        hhi, hlo = _split_bf16(ht)
        ht_ref[...] = hhi
        qt = _mm3((pqh_ref[...], pql_ref[...]), (hhi, hlo))
        qs = [_split_bf16(qt[i * PK_HALF:(i + 1) * PK_HALF]) for i in range(2 * PK_HEADS)]
        for hd in range(PK_HEADS):
            s_ref[2 * hd] = _mm3((sk1h_ref[hd], sk1l_ref[hd]), qs[2 * hd])
            s_ref[2 * hd + 1] = _mm3((sk2h_ref[hd], sk2l_ref[hd]), qs[2 * hd + 1])
        for hd in range(PK_HEADS):
            s1 = s_ref[2 * hd]
            s2 = s_ref[2 * hd + 1]
            v1 = _top16_rows(s1)
            v2 = _top16_rows(s2)
            cand_ref[...] = jnp.full_like(cand_ref, NEG_INF)
            for n, (i, j) in enumerate(_PAIR_IDX):
                cand_ref[n:n + 1, :] = v1[i] + v2[j]
            cand = cand_ref[...]
            top = v1[0] + v2[0]
            zsum = jnp.zeros_like(top)
            tau = top
            for _ in range(PK_TOPK):
                m = jnp.max(cand, axis=0, keepdims=True)
                zsum = zsum + jnp.exp(m - top)
                tau = m
                cand = jnp.where(cand >= m, NEG_INF, cand)
            rank2 = jnp.full((N_KEYS, tm), float(PK_TOPK), F32)
            for j in reversed(range(PK_TOPK)):
                rank2 = jnp.where(s2 >= v2[j], float(j), rank2)
            count1 = jnp.zeros((N_KEYS, tm), F32)
            for i in reversed(range(PK_TOPK)):
                cnt = jnp.zeros_like(top)
                for j in range(PK_TOPK // (i + 1)):
                    cnt = cnt + jnp.where(v1[i] + v2[j] >= tau, 1.0, 0.0)
                count1 = jnp.where(s1 >= v1[i], cnt, count1)
            r2_ref[hd] = rank2.astype(BF16)
            c1_ref[hd] = count1
            e2_ref[hd] = jnp.exp(s2 - v2[0]).astype(BF16)
            w1_ref[hd] = jnp.exp(s1 - v1[0]) / zsum
        acc_ref[...] = jnp.zeros_like(acc_ref)

    sub = PEER_SUB_E1 * N_KEYS
    nsub = e1_per_blk // PEER_SUB_E1
    ht = ht_ref[...]

    def up(i):
        return _dot(pu_ref[i * sub:(i + 1) * sub, :], ht)

    def activation(i, hu):
        acts = []
        for jj in range(PEER_SUB_E1):
            e1 = eb * e1_per_blk + PEER_SUB_E1 * i + jj
            gate = jnp.zeros((N_KEYS, tm), BF16)
            for hd in range(PK_HEADS):
                count = _rows_bf16(c1_ref[hd, pl.ds(e1, 1), :])
                weight = _rows_bf16(w1_ref[hd, pl.ds(e1, 1), :])
                gate = gate + jnp.where(r2_ref[hd] < count, e2_ref[hd] * weight, 0)
            u = hu[jj * N_KEYS:(jj + 1) * N_KEYS]
            act = 0.5 * u * (1.0 + lax.erf(u * (1.0 / math.sqrt(2.0))))
            acts.append(act.astype(BF16) * gate)
        return jnp.concatenate(acts, axis=0)

    hus = [up(i) for i in range(nsub)]
    acts = [activation(i, hus[i]) for i in range(nsub)]
    acc_ref[...] += _dot(pvt_ref[...], jnp.concatenate(acts, axis=0))

    @pl.when(eb == pl.num_programs(2) - 1)
    def _():
        y = x_ref[0] + mod_ref[0, 0][5:6] * acc_ref[...].T
        if final_norm:
            ms = jnp.mean(y * y, axis=-1, keepdims=True)
            y = y * lax.rsqrt(ms + NORM_EPS) * gf_ref[...]
        out_ref[0] = y


def _peer(x, g, mod, p, tm, tile_off, ctx_tiles, final_norm, norm_f):
    bsz, rows, d = x.shape
    n_exp = p["pu"].shape[0]
    e1_per_blk = PEER_STEP_E1
    eblk = e1_per_blk * N_KEYS
    pqt = p["pq"].T
    pqh = pqt.astype(BF16)
    pql = (pqt - pqh.astype(F32)).astype(BF16)
    nq = pqt.shape[0]
    kern = functools.partial(_peer_kernel, tm=tm, e1_per_blk=e1_per_blk, final_norm=final_norm)
    const = lambda shape: pl.BlockSpec(shape, lambda b, t, e: (0,) * len(shape))
    return pl.pallas_call(
        kern,
        grid=(bsz, rows // tm, n_exp // eblk),
        in_specs=[pl.BlockSpec((1, tm, d), lambda b, t, e: (b, t, 0)),
                  const((1, d)),
                  pl.BlockSpec((1, 1, N_MOD, d),
                               lambda b, t, e: (b, jnp.where(t + tile_off >= ctx_tiles, 1, 0), 0, 0)),
                  const((nq, d)), const((nq, d)),
                  const((PK_HEADS, N_KEYS, PK_HALF)), const((PK_HEADS, N_KEYS, PK_HALF)),
                  const((PK_HEADS, N_KEYS, PK_HALF)), const((PK_HEADS, N_KEYS, PK_HALF)),
                  pl.BlockSpec((eblk, d), lambda b, t, e: (e, 0)),
                  pl.BlockSpec((d, eblk), lambda b, t, e: (0, e)),
                  const((1, d))],
        out_specs=pl.BlockSpec((1, tm, d), lambda b, t, e: (b, t, 0)),
        out_shape=jax.ShapeDtypeStruct((bsz, rows, d), F32),
        scratch_shapes=[pltpu.VMEM((d, tm), BF16),
                        pltpu.VMEM((PK_HEADS, N_KEYS, tm), BF16),
                        pltpu.VMEM((PK_HEADS, N_KEYS, tm), BF16),
                        pltpu.VMEM((PK_HEADS, N_KEYS, tm), F32),
                        pltpu.VMEM((PK_HEADS, N_KEYS, tm), F32),
                        pltpu.VMEM((2 * PK_HEADS, N_KEYS, tm), F32),
                        pltpu.VMEM((SUBLANES * pl.cdiv(len(_PAIR_IDX), SUBLANES), tm), F32),
                        pltpu.VMEM((d, tm), F32)],
        compiler_params=_cparams(("parallel", "parallel", "arbitrary")),
        name="peer",
    )(x, g.reshape(1, d), mod, pqh, pql, *_split_bf16(p["sk1"]), *_split_bf16(p["sk2"]),
      p["pu"].astype(BF16), p["pv"].astype(BF16).T, norm_f.reshape(1, d))


def _rope_tables(rows, ctx_len):
    t = jnp.arange(rows - ctx_len, dtype=jnp.int32)
    inv = ROPE_THETA ** (-jnp.arange(0, ROPE_AXIS_DIM, 2, dtype=F32) / ROPE_AXIS_DIM)
    ang_r = (t // GRID_W).astype(F32)[:, None] * inv
    ang_c = (t % GRID_W).astype(F32)[:, None] * inv
    cos = jnp.concatenate([jnp.cos(ang_r)] * 2 + [jnp.cos(ang_c)] * 2, axis=1)
    sin = jnp.concatenate([-jnp.sin(ang_r), jnp.sin(ang_r), -jnp.sin(ang_c), jnp.sin(ang_c)], axis=1)
    cos = jnp.concatenate([jnp.ones((ctx_len, HEAD_DIM), F32), cos], axis=0)
    sin = jnp.concatenate([jnp.zeros((ctx_len, HEAD_DIM), F32), sin], axis=0)
    return jnp.tile(cos, (1, 2)), jnp.tile(sin, (1, 2))


def _head_norm_rope(x, gain, ones_bd, cos, sin):
    ms = _segsum(x * x, ones_bd) * (1.0 / HEAD_DIM)
    y = x * lax.rsqrt(ms + NORM_EPS) * gain
    outs = []
    half = ROPE_AXIS_DIM // 2
    lane = lax.broadcasted_iota(jnp.int32, (1, LANES), 1)
    first_half = (lane % ROPE_AXIS_DIM) < half
    for i in range(x.shape[1] // LANES):
        yc = y[:, i * LANES:(i + 1) * LANES]
        partner = jnp.where(first_half, pltpu.roll(yc, LANES - half, 1), pltpu.roll(yc, half, 1))
        outs.append(yc * cos + partner * sin)
    return jnp.concatenate(outs, axis=1)


def _attn_prep_kernel(z_ref, qn_ref, kn_ref, cos_ref, sin_ref, onesq_ref, onesk_ref,
                      q_o, k_o, v_o):
    z = z_ref[0]
    cos = cos_ref[...]
    sin = sin_ref[...]
    q = _head_norm_rope(z[:, :C_WIDTH], qn_ref[...], onesq_ref[...], cos, sin)
    k = _head_norm_rope(z[:, C_WIDTH:C_WIDTH + KV_WIDTH], kn_ref[...], onesk_ref[...], cos, sin)
    q_o[0] = (q * (ATTN_SCALE * math.log2(math.e))).T.astype(q_o.dtype)
    for j in range(C_KV_HEADS):
        k_o[0, j] = k[:, j * HEAD_DIM:(j + 1) * HEAD_DIM].astype(k_o.dtype)
    v_o[0] = z[:, C_WIDTH + KV_WIDTH:C_WIDTH + 2 * KV_WIDTH].T.astype(v_o.dtype)


def _attn_prep(z, q_norm, k_norm, ctx_len, tm):
    bsz, rows, ncol = z.shape
    cos, sin = _rope_tables(rows, ctx_len)
    qn = jnp.tile(q_norm, C_HEADS).reshape(1, C_WIDTH)
    kn = jnp.tile(k_norm, C_KV_HEADS).reshape(1, KV_WIDTH)
    const = lambda shape: pl.BlockSpec(shape, lambda b, t: (0,) * len(shape))
    return pl.pallas_call(
        _attn_prep_kernel,
        grid=(bsz, rows // tm),
        in_specs=[pl.BlockSpec((1, tm, ncol), lambda b, t: (b, t, 0)),
                  const((1, C_WIDTH)), const((1, KV_WIDTH)),
                  pl.BlockSpec((tm, LANES), lambda b, t: (t, 0)),
                  pl.BlockSpec((tm, LANES), lambda b, t: (t, 0)),
                  const((C_WIDTH, C_WIDTH)), const((KV_WIDTH, KV_WIDTH))],
        out_specs=[pl.BlockSpec((1, C_WIDTH, tm), lambda b, t: (b, 0, t)),
                   pl.BlockSpec((1, C_KV_HEADS, tm, HEAD_DIM), lambda b, t: (b, 0, t, 0)),
                   pl.BlockSpec((1, KV_WIDTH, tm), lambda b, t: (b, 0, t))],
        out_shape=[jax.ShapeDtypeStruct((bsz, C_WIDTH, rows), BF16),
                   jax.ShapeDtypeStruct((bsz, C_KV_HEADS, rows, HEAD_DIM), BF16),
                   jax.ShapeDtypeStruct((bsz, KV_WIDTH, rows), BF16)],
        compiler_params=_cparams(("parallel", "parallel")),
        name="attn_prep",
    )(z, qn, kn, cos, sin, _block_ones(C_WIDTH, HEAD_DIM), _block_ones(KV_WIDTH, HEAD_DIM))


def _attn_kernel(qt_ref, k_ref, vt_ref, o_ref, m_ref, l_ref, acc_ref, *, kc):
    kt = pl.program_id(3)

    @pl.when(kt == 0)
    def _():
        m_ref[...] = jnp.full_like(m_ref, NEG_INF)
        l_ref[...] = jnp.zeros_like(l_ref)
        acc_ref[...] = jnp.zeros_like(acc_ref)

    nkv = k_ref.shape[1]
    grp = range(nkv * C_GROUP)
    qts = [qt_ref[0, g * HEAD_DIM:(g + 1) * HEAD_DIM, :] for g in grp]
    nck = k_ref.shape[2] // kc

    def scores(c):
        kbs = [k_ref[0, j, c * kc:(c + 1) * kc, :] for j in range(nkv)]
        return [_dot(kbs[g // C_GROUP], qts[g]) for g in grp]

    m = [m_ref[g] for g in grp]
    l = [l_ref[g] for g in grp]
    acc = [acc_ref[g] for g in grp]
    s_next = scores(0)
    for c in range(nck):
        s = s_next
        if c + 1 < nck:
            s_next = scores(c + 1)
        vts = [vt_ref[0, j * HEAD_DIM:(j + 1) * HEAD_DIM, c * kc:(c + 1) * kc]
               for j in range(nkv)]
        m_new = [jnp.maximum(m[g], jnp.max(s[g], axis=0, keepdims=True)) for g in grp]
        p = [jnp.exp2(s[g] - m_new[g]) for g in grp]
        pv = [_dot(vts[g // C_GROUP], p[g].astype(BF16)) for g in grp]
        for g in grp:
            alpha = jnp.exp2(m[g] - m_new[g])
            l[g] = alpha * l[g] + jnp.sum(p[g], axis=0, keepdims=True)
            acc[g] = alpha * acc[g] + pv[g]
        m = m_new
    for g in grp:
        m_ref[g] = m[g]
        l_ref[g] = l[g]
        acc_ref[g] = acc[g]

    @pl.when(kt == pl.num_programs(3) - 1)
    def _():
        o_ref[0] = jnp.concatenate(
            [acc_ref[g] / l_ref[g] for g in grp], axis=0).astype(o_ref.dtype)


ATTN_MAX_KEY_TILE = 8448
ATTN_MAX_KEY_CHUNK = 256
ATTN_KV_PER_STEP = 1


def _key_tile(rows, limit=ATTN_MAX_KEY_TILE):
    return max(t for t in range(LANES, min(rows, limit) + 1, LANES) if rows % t == 0)


def _attention(qt, k, vt, ctx_len, tq, tk):
    bsz, _, rows = qt.shape
    seq = rows - ctx_len
    qoff = ctx_len // tq
    nkv = ATTN_KV_PER_STEP
    nq = nkv * C_GROUP
    gw = nq * HEAD_DIM
    kern = functools.partial(_attn_kernel, kc=_key_tile(tk, ATTN_MAX_KEY_CHUNK))
    return pl.pallas_call(
        kern,
        grid=(bsz, C_KV_HEADS // nkv, seq // tq, rows // tk),
        in_specs=[pl.BlockSpec((1, gw, tq), lambda b, j, i, kk: (b, j, i + qoff)),
                  pl.BlockSpec((1, nkv, tk, HEAD_DIM), lambda b, j, i, kk: (b, j, kk, 0)),
                  pl.BlockSpec((1, nkv * HEAD_DIM, tk), lambda b, j, i, kk: (b, j, kk))],
        out_specs=pl.BlockSpec((1, gw, tq), lambda b, j, i, kk: (b, j, i)),
        out_shape=jax.ShapeDtypeStruct((bsz, C_WIDTH, seq), BF16),
        scratch_shapes=[pltpu.VMEM((nq, 1, tq), F32),
                        pltpu.VMEM((nq, 1, tq), F32),
                        pltpu.VMEM((nq, HEAD_DIM, tq), F32)],
        compiler_params=_cparams(("parallel", "parallel", "parallel", "arbitrary")),
        name="attention",
    )(qt, k, vt)


def _conv_kernel(zc_ref, zp_ref, zn_ref, w_ref, b_ref, lg_ref, lb_ref, o_ref, ybuf, *, tm, halo):
    t = pl.program_id(1)
    nt = pl.num_programs(1)

    def glu(u):
        return u[:, :D_WIDTH] * jax.nn.sigmoid(u[:, D_WIDTH:])

    ybuf[0:halo, :] = jnp.where(t == 0, 0.0, glu(zp_ref[0]))
    ybuf[halo:halo + tm, :] = glu(zc_ref[0])
    ybuf[halo + tm:, :] = jnp.where(t == nt - 1, 0.0, glu(zn_ref[0]))
    acc = jnp.zeros((tm, D_WIDTH), F32)
    for j in range(D_CONV_WIDTH):
        off = halo - D_PAD + j
        acc = acc + w_ref[j:j + 1, :] * ybuf[off:off + tm, :]
    y = acc + b_ref[...]
    mu = jnp.mean(y, axis=-1, keepdims=True)
    yc = y - mu
    var = jnp.mean(yc * yc, axis=-1, keepdims=True)
    yn = yc * lax.rsqrt(var + LN_EPS) * lg_ref[...] + lb_ref[...]
    o_ref[0] = (yn * jax.nn.sigmoid(yn)).astype(o_ref.dtype)


def _conformer_conv(z, dw_w, dw_b, cn_g, cn_b, ctx_len, tm):
    bsz, rows, ncol = z.shape
    seq = rows - ctx_len
    halo = 2 * SUBLANES
    hb = tm // halo
    off = ctx_len // tm
    offh = ctx_len // halo
    nh = seq // halo
    kern = functools.partial(_conv_kernel, tm=tm, halo=halo)
    row = lambda w: pl.BlockSpec((1, w), lambda b, t: (0, 0))
    return pl.pallas_call(
        kern,
        grid=(bsz, seq // tm),
        in_specs=[pl.BlockSpec((1, tm, ncol), lambda b, t: (b, t + off, 0)),
                  pl.BlockSpec((1, halo, ncol),
                               lambda b, t: (b, offh + jnp.maximum(t * hb - 1, 0), 0)),
                  pl.BlockSpec((1, halo, ncol),
                               lambda b, t: (b, offh + jnp.minimum((t + 1) * hb, nh - 1), 0)),
                  pl.BlockSpec((D_CONV_WIDTH, D_WIDTH), lambda b, t: (0, 0)),
                  row(D_WIDTH), row(D_WIDTH), row(D_WIDTH)],
        out_specs=pl.BlockSpec((1, tm, D_WIDTH), lambda b, t: (b, t, 0)),
        out_shape=jax.ShapeDtypeStruct((bsz, seq, D_WIDTH), BF16),
        scratch_shapes=[pltpu.VMEM((tm + 2 * halo, D_WIDTH), F32)],
        compiler_params=_cparams(("parallel", "parallel")),
        name="conformer_conv",
    )(z, z, z, dw_w, dw_b.reshape(1, -1), cn_g.reshape(1, -1), cn_b.reshape(1, -1))


def _forward(x, c, ctx, c_ctx, l0, l1, norm_f):
    bsz, seq, d = x.shape
    ctx_len = ctx.shape[1]
    tm = min(256, ctx_len)
    ctx_tiles = ctx_len // tm
    xs = jnp.concatenate([ctx, x], axis=1)

    mod = _modulation(c, c_ctx, l0["mod_w"], l0["mod_b"])
    z, zf = _normmod_matmul(xs, l0["norm1"], mod, l0["w_in"], tm, ctx_tiles, A_COLS)
    r, v, g, bonus, kk, lwf, kdf, bf, lwr, kdr, br = _rwkv_prep(z, l0, tm, ctx_tiles)
    yf, yr = _rwkv_scan(r, v, kk, lwf, kdf, bf, lwr, kdr, br, ctx_len)
    o_rwkv = _rwkv_finish(yf, yr, bonus, g, l0["lnx_g"], l0["lnx_b"], tm)
    fcs = _fnet_chan(zf, tm)
    ts_c = min(256, ctx_len)
    ts_l = min(1024, seq)
    f_ctx = _fnet_seq(fcs, 0, ctx_len, ts_c, ts_c)
    f_lat = _fnet_seq(fcs[:, ctx_len:], 0, seq, ts_l, min(512, seq))
    o_fnet = jnp.concatenate([f_ctx, f_lat], axis=1)
    xs = _proj_residual(o_rwkv, o_fnet, xs, mod, l0["w_out"], tm, 0, ctx_tiles)
    xs = _peer(xs, l0["norm2"], mod, l0, tm, 0, ctx_tiles, False, norm_f)

    mod = _modulation(c, c_ctx, l1["mod_w"], l1["mod_b"])
    z, zu = _normmod_matmul(xs, l1["norm1"], mod, l1["w_in"], tm, ctx_tiles,
                            C_WIDTH + 2 * KV_WIDTH)
    qt, k, vt = _attn_prep(z, l1["q_norm"], l1["k_norm"], ctx_len, tm)
    o_attn = _attention(qt, k, vt, ctx_len, tm, _key_tile(ctx_len + seq))
    o_conv = _conformer_conv(zu, l1["dw_w"], l1["dw_b"], l1["cn_g"], l1["cn_b"], ctx_len, tm)
    xl = _proj_residual(o_attn, o_conv, xs, mod, l1["w_out"], tm, ctx_tiles, ctx_tiles,
                        o1_transposed=True)
    return _peer(xl, l1["norm2"], mod, l1, tm, ctx_tiles, ctx_tiles, True, norm_f)


def kernel(x, c, ctx, c_ctx, l0_mod_w, l0_mod_b, l0_norm1, l0_w_in, l0_shift_prev, l0_shift_next, l0_w0, l0_w2, l0_a0, l0_a2, l0_g2, l0_k_k, l0_k_a, l0_r_k, l0_lnx_g, l0_lnx_b, l0_w_out, l0_norm2, l0_pq, l0_sk1, l0_sk2, l0_pu, l0_pv, l1_mod_w, l1_mod_b, l1_norm1, l1_w_in, l1_q_norm, l1_k_norm, l1_dw_w, l1_dw_b, l1_cn_g, l1_cn_b, l1_w_out, l1_norm2, l1_pq, l1_sk1, l1_sk2, l1_pu, l1_pv, norm_f):
    l0 = dict(mod_w=l0_mod_w, mod_b=l0_mod_b, norm1=l0_norm1, w_in=l0_w_in,
              shift_prev=l0_shift_prev, shift_next=l0_shift_next, w0=l0_w0, w2=l0_w2,
              a0=l0_a0, a2=l0_a2, g2=l0_g2, k_k=l0_k_k, k_a=l0_k_a, r_k=l0_r_k,
              lnx_g=l0_lnx_g, lnx_b=l0_lnx_b, w_out=l0_w_out, norm2=l0_norm2,
              pq=l0_pq, sk1=l0_sk1, sk2=l0_sk2, pu=l0_pu, pv=l0_pv)
    l1 = dict(mod_w=l1_mod_w, mod_b=l1_mod_b, norm1=l1_norm1, w_in=l1_w_in,
              q_norm=l1_q_norm, k_norm=l1_k_norm, dw_w=l1_dw_w, dw_b=l1_dw_b,
              cn_g=l1_cn_g, cn_b=l1_cn_b, w_out=l1_w_out, norm2=l1_norm2,
              pq=l1_pq, sk1=l1_sk1, sk2=l1_sk2, pu=l1_pu, pv=l1_pv)
    return _forward(x, c, ctx, c_ctx, l0, l1, norm_f)
```

```python
import functools
import math

import jax
import jax.numpy as jnp
import numpy as np
from jax import lax
from jax.experimental import pallas as pl
from jax.experimental.pallas import tpu as pltpu

F32 = jnp.float32
BF16 = jnp.bfloat16

LANES = 128
SUBLANES = 8
VMEM_LIMIT_BYTES = 56 * 1024 * 1024

N_MOD = 6
NORM_EPS = 1e-6
LN_EPS = 1e-5
GRID_W = 64
HEAD_DIM = 64
A_WIDTH = 768
A_HEADS = A_WIDTH // HEAD_DIM
A_RANK_W = 64
A_RANK_A = 64
A_RANK_G = 128
A_LNX_EPS = 64e-5
A_COLS = 3 * A_WIDTH + 2 * A_RANK_W + 2 * A_RANK_A + A_RANK_G
B_WIDTH = 256
B_GROUP_DIM = 64
C_WIDTH = 768
C_HEADS = 12
C_KV_HEADS = 4
C_GROUP = C_HEADS // C_KV_HEADS
KV_WIDTH = C_KV_HEADS * HEAD_DIM
ROPE_AXIS_DIM = HEAD_DIM // 2
ROPE_THETA = 10000.0
ATTN_SCALE = HEAD_DIM ** -0.5
D_WIDTH = 256
D_CONV_WIDTH = 31
D_PAD = D_CONV_WIDTH // 2
PK_HEADS = 8
PK_DIM = 256
PK_HALF = 128
N_KEYS = 128
PK_TOPK = 16

SCAN_CHUNK = 64
NEG_INF = float("-inf")


def _cparams(semantics):
    return pltpu.CompilerParams(dimension_semantics=semantics,
                                vmem_limit_bytes=VMEM_LIMIT_BYTES)


def _split_bf16(x):
    hi = x.astype(BF16)
    lo = (x - hi.astype(F32)).astype(BF16)
    return hi, lo


def _dot(a, b):
    return jnp.dot(a, b, preferred_element_type=F32)


def _dot_hp(a, b):
    return _mm3(_split_bf16(a), _split_bf16(b))


_NN = (((1,), (0,)), ((), ()))
_NT = (((1,), (1,)), ((), ()))
_TN = (((0,), (0,)), ((), ()))


def _mm3(a, b, dims=_NN):
    (ah, al), (bh, bl) = a, b
    dg = lambda x, y: lax.dot_general(x, y, dims, preferred_element_type=F32)
    return dg(ah, bh) + (dg(ah, bl) + dg(al, bh))


def _segsum(x, ones_bd):
    hi, lo = _split_bf16(x)
    return _dot(hi, ones_bd) + _dot(lo, ones_bd)


def _block_ones(width, seg):
    r = np.arange(width) // seg
    return jnp.asarray((r[:, None] == r[None, :]).astype(np.float32), dtype=BF16)


def _mod_kernel(c_ref, w_ref, b_ref, o_ref):
    c = c_ref[...]
    s = c * jax.nn.sigmoid(c)
    o_ref[...] = _dot_hp(s, w_ref[...]) + b_ref[...]


def _modulation(c, c_ctx, mod_w, mod_b):
    bsz, d = c.shape
    rows = SUBLANES * pl.cdiv(bsz + 1, SUBLANES)
    cc = jnp.zeros((rows, d), F32).at[:bsz].set(c).at[bsz].set(c_ctx)
    n = mod_w.shape[1]
    tn = n // 4
    out = pl.pallas_call(
        _mod_kernel,
        grid=(n // tn,),
        in_specs=[pl.BlockSpec((rows, d), lambda j: (0, 0)),
                  pl.BlockSpec((d, tn), lambda j: (0, j)),
                  pl.BlockSpec((1, tn), lambda j: (0, j))],
        out_specs=pl.BlockSpec((rows, tn), lambda j: (0, j)),
        out_shape=jax.ShapeDtypeStruct((rows, n), F32),
        compiler_params=_cparams(("arbitrary",)),
        name="adaln_mod",
    )(cc, mod_w, mod_b.reshape(1, n))
    lat = out[:bsz].reshape(bsz, N_MOD, d)
    ctx = jnp.broadcast_to(out[bsz].reshape(1, N_MOD, d), (bsz, N_MOD, d))
    return jnp.stack([ctx, lat], axis=1)


def _norm_mod(x, g, mod, row):
    ms = jnp.mean(x * x, axis=-1, keepdims=True)
    y = x * lax.rsqrt(ms + NORM_EPS) * g
    return y * (1.0 + mod[row + 1:row + 2]) + mod[row:row + 1]


def _normmod_matmul_kernel(x_ref, g_ref, mod_ref, w_ref, o1_ref, o2_ref):
    h = _norm_mod(x_ref[0], g_ref[...], mod_ref[0, 0], 0)
    z = _dot(h.astype(BF16), w_ref[...])
    n1 = o1_ref.shape[2]
    o1_ref[0] = z[:, :n1]
    o2_ref[0] = z[:, n1:]


def _normmod_matmul(x, g, mod, w, tm, ctx_tiles, n1):
    bsz, rows, d = x.shape
    n = w.shape[1]
    widths = (n1, n - n1)
    return pl.pallas_call(
        _normmod_matmul_kernel,
        grid=(bsz, rows // tm),
        in_specs=[pl.BlockSpec((1, tm, d), lambda b, t: (b, t, 0)),
                  pl.BlockSpec((1, d), lambda b, t: (0, 0)),
                  pl.BlockSpec((1, 1, N_MOD, d),
                               lambda b, t: (b, jnp.where(t >= ctx_tiles, 1, 0), 0, 0)),
                  pl.BlockSpec((d, n), lambda b, t: (0, 0))],
        out_specs=[pl.BlockSpec((1, tm, wd), lambda b, t: (b, t, 0)) for wd in widths],
        out_shape=[jax.ShapeDtypeStruct((bsz, rows, wd), F32) for wd in widths],
        compiler_params=_cparams(("parallel", "parallel")),
        name="normmod_proj",
    )(x, g.reshape(1, d), mod, w.astype(BF16))


def _rwkv_prep_kernel(z_ref, zp_ref, zn_ref, mup_ref, mun_ref, w0_ref, w2_ref, a0_ref, a2_ref,
                      g2_ref, kk_ref, ka_ref, rk_ref, ones_ref,
                      r_o, v_o, g_o, bonus_o, kkn_o, lwf_o, kdf_o, bf_o, lwr_o, kdr_o, br_o,
                      *, tm, ctx_tiles):
    t = pl.program_id(1)
    nt = pl.num_programs(1)
    z = z_ref[0][:, :A_COLS]
    zp_row = zp_ref[0][SUBLANES - 1:SUBLANES, :A_COLS]
    zn_row = zn_ref[0][0:1, :A_COLS]
    first = jnp.logical_or(t == 0, t == ctx_tiles)
    last = jnp.logical_or(t == ctx_tiles - 1, t == nt - 1)
    zp_row = jnp.where(first, 0.0, zp_row)
    zn_row = jnp.where(last, 0.0, zn_row)
    ridx = lax.broadcasted_iota(jnp.int32, (tm, 1), 0)
    z_prev = jnp.where(ridx == 0, zp_row, pltpu.roll(z, 1, 0))
    z_next = jnp.where(ridx == tm - 1, zn_row, pltpu.roll(z, tm - 1, 0))
    zs = z + mup_ref[...] * (z_prev - z) + mun_ref[...] * (z_next - z)

    w = A_WIDTH
    r = zs[:, 0:w]
    k = zs[:, w:2 * w]
    v = zs[:, 2 * w:3 * w]
    o = 3 * w
    xw = zs[:, o:o + 2 * A_RANK_W]
    o += 2 * A_RANK_W
    xa = zs[:, o:o + 2 * A_RANK_A]
    o += 2 * A_RANK_A
    xg = zs[:, o:o + A_RANK_G]
    ones_bd = ones_ref[...]

    g_o[0] = _dot_hp(jax.nn.sigmoid(xg), g2_ref[...])
    kk = k * kk_ref[...]
    nrm = jnp.sqrt(_segsum(kk * kk, ones_bd))
    kk = kk / jnp.maximum(nrm, 1e-12)
    r_o[0] = r
    v_o[0] = v
    kkn_o[0] = kk
    k_sum = jnp.zeros_like(r)
    tw = jnp.tanh(xw)
    for d, (lw_o, kd_o, b_o) in enumerate(((lwf_o, kdf_o, bf_o), (lwr_o, kdr_o, br_o))):
        wl = w0_ref[d:d + 1, :] + _dot_hp(tw[:, d * A_RANK_W:(d + 1) * A_RANK_W], w2_ref[d])
        w_log = -jax.nn.softplus(-wl) - 0.5
        lw_o[0] = -jnp.exp(w_log)
        a_gate = jax.nn.sigmoid(
            a0_ref[d:d + 1, :] + _dot_hp(xa[:, d * A_RANK_A:(d + 1) * A_RANK_A], a2_ref[d]))
        k_d = k * (1.0 + (a_gate - 1.0) * ka_ref[...])
        kd_o[0] = k_d
        b_o[0] = kk * a_gate
        k_sum = k_sum + k_d
    bonus_o[0] = _segsum(r * k_sum * rk_ref[...], ones_bd) * v


def _rwkv_prep(z, p, tm, ctx_tiles):
    bsz, rows, ncol = z.shape
    w = A_WIDTH
    hb = tm // SUBLANES
    nblk8 = rows // SUBLANES
    row_spec = lambda width: pl.BlockSpec((1, width), lambda b, t: (0, 0))
    full = lambda shape: pl.BlockSpec(shape, lambda b, t: (0,) * len(shape))
    out_spec = pl.BlockSpec((1, tm, w), lambda b, t: (b, t, 0))
    out_shape = jax.ShapeDtypeStruct((bsz, rows, w), F32)
    kern = functools.partial(_rwkv_prep_kernel, tm=tm, ctx_tiles=ctx_tiles)
    return pl.pallas_call(
        kern,
        grid=(bsz, rows // tm),
        in_specs=[pl.BlockSpec((1, tm, ncol), lambda b, t: (b, t, 0)),
                  pl.BlockSpec((1, SUBLANES, ncol),
                               lambda b, t: (b, jnp.maximum(t * hb - 1, 0), 0)),
                  pl.BlockSpec((1, SUBLANES, ncol),
                               lambda b, t: (b, jnp.minimum((t + 1) * hb, nblk8 - 1), 0)),
                  row_spec(A_COLS), row_spec(A_COLS),
                  full((2, w)), full((2, A_RANK_W, w)), full((2, w)), full((2, A_RANK_A, w)),
                  full((A_RANK_G, w)), row_spec(w), row_spec(w), row_spec(w), full((w, w))],
        out_specs=[out_spec] * 11,
        out_shape=[out_shape] * 11,
        compiler_params=_cparams(("parallel", "parallel")),
        name="rwkv_prep",
    )(z, z, z, p["shift_prev"].reshape(1, -1), p["shift_next"].reshape(1, -1),
      p["w0"], p["w2"], p["a0"], p["a2"], p["g2"], p["k_k"].reshape(1, w),
      p["k_a"].reshape(1, w), p["r_k"].reshape(1, w), _block_ones(w, HEAD_DIM))


SCAN_HEADS = 12


def _scan_kernel(rf, vf, kkf, lwf, kdf, bf, rr, vr, kkr, lwr, kdr, br, yf_o, yr_o, s_ref, *, c):
    @pl.when(pl.program_id(2) == 0)
    def _():
        s_ref[...] = jnp.zeros_like(s_ref)

    ri = lax.broadcasted_iota(jnp.int32, (c, c), 0)
    ci = lax.broadcasted_iota(jnp.int32, (c, c), 1)
    ri2 = lax.broadcasted_iota(jnp.int32, (c, 2 * c), 0)
    ci2 = lax.broadcasted_iota(jnp.int32, (c, 2 * c), 1)
    ci2 = jnp.where(ci2 >= c, ci2 - c, ci2)
    dirs = ((rf, vf, kkf, lwf, kdf, bf, ci <= ri, ci < ri, ci2 <= ri2, c - 1),
            (rr, vr, kkr, lwr, kdr, br, ci >= ri, ci > ri, ci2 >= ri2, 0))
    lhs, rhs, vs, ss, pend, incl, strict = [], [], [], [], [], [], []
    for d, (r_, v_, kk_, lw_, kd_, b_, m_incl, m_strict, m_incl2, edge) in enumerate(dirs):
        lw = lw_[0]
        tri = m_incl.astype(F32).astype(BF16)
        lw_h = lw.astype(BF16)
        lw_m, lw_l = _split_bf16(lw - lw_h.astype(F32))
        lcum = _dot(tri, lw_h) + (_dot(tri, lw_m) + _dot(tri, lw_l))
        p_in = jnp.exp(lcum)
        p_inv = jnp.exp(-lcum)
        lhs_all = jnp.concatenate([-kk_[0] * jnp.exp(lcum - lw), r_[0] * p_in], axis=0)
        rhs_all = jnp.concatenate([b_[0] * p_inv, kd_[0] * p_inv], axis=0)
        v_all = v_[0]
        for h in range(SCAN_HEADS):
            sl = slice(h * HEAD_DIM, (h + 1) * HEAD_DIM)
            lhs.append(lhs_all[:, sl])
            rhs.append(rhs_all[:, sl])
            vs.append(v_all[:, sl])
            ss.append(s_ref[d, h])
            pend.append(p_in[edge:edge + 1, sl])
            incl.append(m_incl2)
            strict.append(m_strict)
    n = len(lhs)
    idx = range(n)
    lhs2 = [_split_bf16(lhs[i]) for i in idx]
    rhs2 = [_split_bf16(rhs[i]) for i in idx]
    v2 = [_split_bf16(vs[i]) for i in idx]
    s2 = [_split_bf16(ss[i]) for i in idx]
    g = [_mm3(lhs2[i], rhs2[i], _NT) for i in idx]
    xs = [_mm3(lhs2[i], s2[i], _NT) for i in idx]
    a_ak = [_split_bf16(jnp.where(strict[i], g[i][:c, c:], 0.0)) for i in idx]
    x = [xs[i][:c] + _mm3(a_ak[i], v2[i]) for i in idx]
    apow = [_split_bf16(jnp.where(strict[i], g[i][:c, :c], 0.0)) for i in idx]
    steps = int(math.log2(c))
    for k in range(steps):
        x2 = [_split_bf16(x[i]) for i in idx]
        x = [x[i] + _mm3(apow[i], x2[i]) for i in idx]
        if k + 1 < steps:
            apow = [_split_bf16(_mm3(apow[i], apow[i])) for i in idx]
    x2 = [_split_bf16(x[i]) for i in idx]
    sav = [tuple(jnp.concatenate([x2[i][t], v2[i][t]], axis=0) for t in range(2)) for i in idx]
    a_r = [_split_bf16(jnp.where(incl[i], g[i][c:], 0.0)) for i in idx]
    y = [xs[i][c:] + _mm3(a_r[i], sav[i]) for i in idx]
    s_new = [(ss[i] + _mm3(sav[i], rhs2[i], _TN)) * pend[i] for i in idx]
    for d, y_o in enumerate((yf_o, yr_o)):
        for h in range(SCAN_HEADS):
            s_ref[d, h] = s_new[d * SCAN_HEADS + h]
        y_o[0] = jnp.concatenate(y[d * SCAN_HEADS:(d + 1) * SCAN_HEADS], axis=1)


def _rwkv_scan(r, v, kk, lwf, kdf, bf, lwr, kdr, br, ctx_len):
    bsz, rows, w = r.shape
    c = SCAN_CHUNK
    nch = rows // c
    cch = ctx_len // c
    lanes = SCAN_HEADS * HEAD_DIM
    fwd = lambda b, hp, i: (b, i, hp)
    rev = lambda b, hp, i: (b, jnp.where(i < cch, cch - 1 - i, nch - 1 + cch - i), hp)
    blk = lambda im: pl.BlockSpec((1, c, lanes), im)
    kern = functools.partial(_scan_kernel, c=c)
    out_shape = jax.ShapeDtypeStruct((bsz, rows, w), F32)
    return pl.pallas_call(
        kern,
        grid=(bsz, w // lanes, nch),
        in_specs=[blk(fwd)] * 6 + [blk(rev)] * 6,
        out_specs=[blk(fwd), blk(rev)],
        out_shape=[out_shape, out_shape],
        scratch_shapes=[pltpu.VMEM((2, SCAN_HEADS, HEAD_DIM, HEAD_DIM), F32)],
        compiler_params=_cparams(("parallel", "parallel", "arbitrary")),
        name="rwkv_scan",
    )(r, v, kk, lwf, kdf, bf, r, v, kk, lwr, kdr, br)


def _rwkv_finish_kernel(yf, yr, bonus, g, lg, lb, ones_ref, o_ref):
    y = yf[0] + yr[0]
    ones_bd = ones_ref[...]
    inv = 1.0 / HEAD_DIM
    mu = _segsum(y, ones_bd) * inv
    yc = y - mu
    var = _segsum(yc * yc, ones_bd) * inv
    yn = yc * lax.rsqrt(var + A_LNX_EPS) * lg[...] + lb[...]
    o_ref[0] = ((yn + bonus[0]) * g[0]).astype(o_ref.dtype)


def _rwkv_finish(yf, yr, bonus, g, lnx_g, lnx_b, tm):
    bsz, rows, w = yf.shape
    blk = pl.BlockSpec((1, tm, w), lambda b, t: (b, t, 0))
    row = pl.BlockSpec((1, w), lambda b, t: (0, 0))
    return pl.pallas_call(
        _rwkv_finish_kernel,
        grid=(bsz, rows // tm),
        in_specs=[blk, blk, blk, blk, row, row, pl.BlockSpec((w, w), lambda b, t: (0, 0))],
        out_specs=blk,
        out_shape=jax.ShapeDtypeStruct((bsz, rows, w), BF16),
        compiler_params=_cparams(("parallel", "parallel")),
        name="rwkv_finish",
    )(yf, yr, bonus, g, lnx_g.reshape(1, w), lnx_b.reshape(1, w), _block_ones(w, HEAD_DIM))


def _fnet_chan_kernel(z_ref, m_ref, o_ref):
    o_ref[0] = _dot(z_ref[0].astype(BF16), m_ref[...]).astype(o_ref.dtype)


def _fnet_chan(z, tm):
    bsz, rows, ncol = z.shape
    j = np.arange(B_WIDTH)
    same = (j[:, None] // B_GROUP_DIM) == (j[None, :] // B_GROUP_DIM)
    ang = 2.0 * np.pi * ((j[:, None] % B_GROUP_DIM) * (j[None, :] % B_GROUP_DIM) % B_GROUP_DIM) / B_GROUP_DIM
    m = np.concatenate([np.where(same, np.cos(ang), 0.0), np.where(same, np.sin(ang), 0.0)], axis=1)
    return pl.pallas_call(
        _fnet_chan_kernel,
        grid=(bsz, rows // tm),
        in_specs=[pl.BlockSpec((1, tm, ncol), lambda b, t: (b, t, 0)),
                  pl.BlockSpec((B_WIDTH, 2 * B_WIDTH), lambda b, t: (0, 0))],
        out_specs=pl.BlockSpec((1, tm, 2 * B_WIDTH), lambda b, t: (b, t, 0)),
        out_shape=jax.ShapeDtypeStruct((bsz, rows, 2 * B_WIDTH), BF16),
        compiler_params=_cparams(("parallel", "parallel")),
        name="fnet_chan",
    )(z, jnp.asarray(m, dtype=BF16))


def _fnet_seq_kernel(fcs_ref, cb_ref, sb_ref, c0_ref, s0_ref, o_ref, acc_ref, *, scale, nb):
    k = pl.program_id(1)

    @pl.when(k == 0)
    def _():
        acc_ref[...] = jnp.zeros_like(acc_ref)

    cb = cb_ref[...]
    sb = sb_ref[...]
    c0 = c0_ref[0]
    s0 = s0_ref[0]
    cm = (c0 * cb - s0 * sb).astype(BF16)
    sm = (s0 * cb + c0 * sb).astype(BF16)
    for b in range(nb):
        fcs = fcs_ref[b]
        acc_ref[b] += _dot(cm, fcs[:, :B_WIDTH]) - _dot(sm, fcs[:, B_WIDTH:])

    @pl.when(k == pl.num_programs(1) - 1)
    def _():
        o_ref[...] = (acc_ref[...] * scale).astype(o_ref.dtype)


def _fnet_seq(fcs, row0, length, ts, tk):
    bsz = fcs.shape[0]
    ds = jnp.arange(ts, dtype=jnp.int32)[:, None]
    tt = jnp.arange(length, dtype=jnp.int32)[None, :]
    ang = ((ds * tt) % length).astype(F32) * (2.0 * math.pi / length)
    cb, sb = jnp.cos(ang), jnp.sin(ang)
    s0 = (jnp.arange(length // ts, dtype=jnp.int32) * ts)[:, None]
    ang0 = ((s0 * tt) % length).astype(F32) * (2.0 * math.pi / length)
    c0, sn0 = jnp.cos(ang0)[:, None, :], jnp.sin(ang0)[:, None, :]
    kern = functools.partial(_fnet_seq_kernel, scale=1.0 / math.sqrt(length * B_GROUP_DIM), nb=bsz)
    koff = row0 // tk
    return pl.pallas_call(
        kern,
        grid=(length // ts, length // tk),
        in_specs=[pl.BlockSpec((bsz, tk, 2 * B_WIDTH), lambda s, k: (0, k + koff, 0)),
                  pl.BlockSpec((ts, tk), lambda s, k: (0, k)),
                  pl.BlockSpec((ts, tk), lambda s, k: (0, k)),
                  pl.BlockSpec((1, 1, tk), lambda s, k: (s, 0, k)),
                  pl.BlockSpec((1, 1, tk), lambda s, k: (s, 0, k))],
        out_specs=pl.BlockSpec((bsz, ts, B_WIDTH), lambda s, k: (0, s, 0)),
        out_shape=jax.ShapeDtypeStruct((bsz, length, B_WIDTH), BF16),
        scratch_shapes=[pltpu.VMEM((bsz, ts, B_WIDTH), F32)],
        compiler_params=_cparams(("parallel", "arbitrary")),
        name="fnet_seq",
    )(fcs, cb, sb, c0, sn0)


def _proj_residual_kernel(o1_ref, o2_ref, x_ref, mod_ref, w1_ref, w2_ref, out_ref, *, o1_transposed):
    o1 = o1_ref[0]
    if o1_transposed:
        o1 = o1.astype(F32).T.astype(BF16)
    mix = _dot(o1, w1_ref[...]) + _dot(o2_ref[0], w2_ref[...])
    gate = mod_ref[0, 0][2:3]
    out_ref[0] = x_ref[0] + gate * mix


def _proj_residual(o1, o2, x, mod, w_out, tm, tile_off, ctx_tiles, o1_transposed=False):
    bsz, rows, w2 = o2.shape
    w1 = o1.shape[1] if o1_transposed else o1.shape[2]
    d = x.shape[2]
    o1_spec = (pl.BlockSpec((1, w1, tm), lambda b, t: (b, 0, t)) if o1_transposed
               else pl.BlockSpec((1, tm, w1), lambda b, t: (b, t, 0)))
    kern = functools.partial(_proj_residual_kernel, o1_transposed=o1_transposed)
    return pl.pallas_call(
        kern,
        grid=(bsz, rows // tm),
        in_specs=[o1_spec,
                  pl.BlockSpec((1, tm, w2), lambda b, t: (b, t, 0)),
                  pl.BlockSpec((1, tm, d), lambda b, t: (b, t + tile_off, 0)),
                  pl.BlockSpec((1, 1, N_MOD, d),
                               lambda b, t: (b, jnp.where(t + tile_off >= ctx_tiles, 1, 0), 0, 0)),
                  pl.BlockSpec((w1, d), lambda b, t: (0, 0)),
                  pl.BlockSpec((w2, d), lambda b, t: (0, 0))],
        out_specs=pl.BlockSpec((1, tm, d), lambda b, t: (b, t, 0)),
        out_shape=jax.ShapeDtypeStruct((bsz, rows, d), F32),
        compiler_params=_cparams(("parallel", "parallel")),
        name="proj_residual",
    )(o1, o2, x, mod, w_out[:w1].astype(BF16), w_out[w1:].astype(BF16))


def _merge_exchange_network(n):
    pairs = []
    p = 1
    while p < n:
        k = p
        while k >= 1:
            for j in range(k % p, n - k, 2 * k):
                for i in range(min(k, n - j - k)):
                    if (i + j) // (2 * p) == (i + j + k) // (2 * p):
                        pairs.append((i + j, i + j + k))
            k //= 2
        p *= 2
    return pairs


def _top16_rows(s):
    nslab = s.shape[0] // SUBLANES
    slabs = [s[k * SUBLANES:(k + 1) * SUBLANES] for k in range(nslab)]
    for a, b in _merge_exchange_network(nslab):
        slabs[a], slabs[b] = jnp.maximum(slabs[a], slabs[b]), jnp.minimum(slabs[a], slabs[b])
    rows = []
    for t in range(PK_TOPK):
        m = jnp.max(slabs[0], axis=0, keepdims=True)
        rows.append(m)
        hit = slabs[0] >= m
        for k in range(nslab - 1 - t):
            slabs[k] = jnp.where(hit, slabs[k + 1], slabs[k])
    return rows


_PAIR_IDX = [(i, j) for i in range(PK_TOPK) for j in range(PK_TOPK) if (i + 1) * (j + 1) <= PK_TOPK]
BF16_SUBLANES = 2 * SUBLANES
PEER_SUB_E1 = 1
PEER_STEP_E1 = 16


def _rows_bf16(row):
    packed = jnp.broadcast_to(row, (BF16_SUBLANES, row.shape[1])).astype(BF16)
    return jnp.tile(packed, (N_KEYS // BF16_SUBLANES, 1))


def _peer_kernel(x_ref, g_ref, mod_ref, pqh_ref, pql_ref, sk1h_ref, sk1l_ref, sk2h_ref, sk2l_ref,
                 pu_ref, pvt_ref, gf_ref,
                 out_ref, ht_ref, r2_ref, e2_ref, c1_ref, w1_ref, s_ref, cand_ref, acc_ref,
                 *, tm, e1_per_blk, final_norm):
    eb = pl.program_id(2)

    @pl.when(eb == 0)
    def _():
        h = _norm_mod(x_ref[0], g_ref[...], mod_ref[0, 0], 3)
        ht = h.T
        hhi, hlo = _split_bf16(ht)
        ht_ref[...] = hhi
        qt = _mm3((pqh_ref[...], pql_ref[...]), (hhi, hlo))
        qs = [_split_bf16(qt[i * PK_HALF:(i + 1) * PK_HALF]) for i in range(2 * PK_HEADS)]
        for hd in range(PK_HEADS):
            s_ref[2 * hd] = _mm3((sk1h_ref[hd], sk1l_ref[hd]), qs[2 * hd])
            s_ref[2 * hd + 1] = _mm3((sk2h_ref[hd], sk2l_ref[hd]), qs[2 * hd + 1])
        for hd in range(PK_HEADS):
            s1 = s_ref[2 * hd]
            s2 = s_ref[2 * hd + 1]
            v1 = _top16_rows(s1)
            v2 = _top16_rows(s2)
            cand_ref[...] = jnp.full_like(cand_ref, NEG_INF)
            for n, (i, j) in enumerate(_PAIR_IDX):
                cand_ref[n:n + 1, :] = v1[i] + v2[j]
            cand = cand_ref[...]
            top = v1[0] + v2[0]
            zsum = jnp.zeros_like(top)
            tau = top
            for _ in range(PK_TOPK):
                m = jnp.max(cand, axis=0, keepdims=True)
                zsum = zsum + jnp.exp(m - top)
                tau = m
                cand = jnp.where(cand >= m, NEG_INF, cand)
            rank2 = jnp.full((N_KEYS, tm), float(PK_TOPK), F32)
            for j in reversed(range(PK_TOPK)):
                rank2 = jnp.where(s2 >= v2[j], float(j), rank2)
            count1 = jnp.zeros((N_KEYS, tm), F32)
            for i in reversed(range(PK_TOPK)):
                cnt = jnp.zeros_like(top)
                for j in range(PK_TOPK // (i + 1)):
                    cnt = cnt + jnp.where(v1[i] + v2[j] >= tau, 1.0, 0.0)
                count1 = jnp.where(s1 >= v1[i], cnt, count1)
            r2_ref[hd] = rank2.astype(BF16)
            c1_ref[hd] = count1
            e2_ref[hd] = jnp.exp(s2 - v2[0]).astype(BF16)
            w1_ref[hd] = jnp.exp(s1 - v1[0]) / zsum
        acc_ref[...] = jnp.zeros_like(acc_ref)

    sub = PEER_SUB_E1 * N_KEYS
    nsub = e1_per_blk // PEER_SUB_E1
    ht = ht_ref[...]

    def up(i):
        return _dot(pu_ref[i * sub:(i + 1) * sub, :], ht)

    def activation(i, hu):
        acts = []
        for jj in range(PEER_SUB_E1):
            e1 = eb * e1_per_blk + PEER_SUB_E1 * i + jj
            gate = jnp.zeros((N_KEYS, tm), BF16)
            for hd in range(PK_HEADS):
                count = _rows_bf16(c1_ref[hd, pl.ds(e1, 1), :])
                weight = _rows_bf16(w1_ref[hd, pl.ds(e1, 1), :])
                gate = gate + jnp.where(r2_ref[hd] < count, e2_ref[hd] * weight, 0)
            u = hu[jj * N_KEYS:(jj + 1) * N_KEYS]
            act = 0.5 * u * (1.0 + lax.erf(u * (1.0 / math.sqrt(2.0))))
            acts.append(act.astype(BF16) * gate)
        return jnp.concatenate(acts, axis=0)

    hus = [up(i) for i in range(nsub)]
    acts = [activation(i, hus[i]) for i in range(nsub)]
    acc_ref[...] += _dot(pvt_ref[...], jnp.concatenate(acts, axis=0))

    @pl.when(eb == pl.num_programs(2) - 1)
    def _():
        y = x_ref[0] + mod_ref[0, 0][5:6] * acc_ref[...].T
        if final_norm:
            ms = jnp.mean(y * y, axis=-1, keepdims=True)
            y = y * lax.rsqrt(ms + NORM_EPS) * gf_ref[...]
        out_ref[0] = y


def _peer(x, g, mod, p, tm, tile_off, ctx_tiles, final_norm, norm_f):
    bsz, rows, d = x.shape
    n_exp = p["pu"].shape[0]
    e1_per_blk = PEER_STEP_E1
    eblk = e1_per_blk * N_KEYS
    pqt = p["pq"].T
    pqh = pqt.astype(BF16)
    pql = (pqt - pqh.astype(F32)).astype(BF16)
    nq = pqt.shape[0]
    kern = functools.partial(_peer_kernel, tm=tm, e1_per_blk=e1_per_blk, final_norm=final_norm)
    const = lambda shape: pl.BlockSpec(shape, lambda b, t, e: (0,) * len(shape))
    return pl.pallas_call(
        kern,
        grid=(bsz, rows // tm, n_exp // eblk),
        in_specs=[pl.BlockSpec((1, tm, d), lambda b, t, e: (b, t, 0)),
                  const((1, d)),
                  pl.BlockSpec((1, 1, N_MOD, d),
                               lambda b, t, e: (b, jnp.where(t + tile_off >= ctx_tiles, 1, 0), 0, 0)),
                  const((nq, d)), const((nq, d)),
                  const((PK_HEADS, N_KEYS, PK_HALF)), const((PK_HEADS, N_KEYS, PK_HALF)),
                  const((PK_HEADS, N_KEYS, PK_HALF)), const((PK_HEADS, N_KEYS, PK_HALF)),
                  pl.BlockSpec((eblk, d), lambda b, t, e: (e, 0)),
                  pl.BlockSpec((d, eblk), lambda b, t, e: (0, e)),
                  const((1, d))],
        out_specs=pl.BlockSpec((1, tm, d), lambda b, t, e: (b, t, 0)),
        out_shape=jax.ShapeDtypeStruct((bsz, rows, d), F32),
        scratch_shapes=[pltpu.VMEM((d, tm), BF16),
                        pltpu.VMEM((PK_HEADS, N_KEYS, tm), BF16),
                        pltpu.VMEM((PK_HEADS, N_KEYS, tm), BF16),
                        pltpu.VMEM((PK_HEADS, N_KEYS, tm), F32),
                        pltpu.VMEM((PK_HEADS, N_KEYS, tm), F32),
                        pltpu.VMEM((2 * PK_HEADS, N_KEYS, tm), F32),
                        pltpu.VMEM((SUBLANES * pl.cdiv(len(_PAIR_IDX), SUBLANES), tm), F32),
                        pltpu.VMEM((d, tm), F32)],
        compiler_params=_cparams(("parallel", "parallel", "arbitrary")),
        name="peer",
    )(x, g.reshape(1, d), mod, pqh, pql, *_split_bf16(p["sk1"]), *_split_bf16(p["sk2"]),
      p["pu"].astype(BF16), p["pv"].astype(BF16).T, norm_f.reshape(1, d))


def _rope_tables(rows, ctx_len):
    t = jnp.arange(rows - ctx_len, dtype=jnp.int32)
    inv = ROPE_THETA ** (-jnp.arange(0, ROPE_AXIS_DIM, 2, dtype=F32) / ROPE_AXIS_DIM)
    ang_r = (t // GRID_W).astype(F32)[:, None] * inv
    ang_c = (t % GRID_W).astype(F32)[:, None] * inv
    cos = jnp.concatenate([jnp.cos(ang_r)] * 2 + [jnp.cos(ang_c)] * 2, axis=1)
    sin = jnp.concatenate([-jnp.sin(ang_r), jnp.sin(ang_r), -jnp.sin(ang_c), jnp.sin(ang_c)], axis=1)
    cos = jnp.concatenate([jnp.ones((ctx_len, HEAD_DIM), F32), cos], axis=0)
    sin = jnp.concatenate([jnp.zeros((ctx_len, HEAD_DIM), F32), sin], axis=0)
    return jnp.tile(cos, (1, 2)), jnp.tile(sin, (1, 2))


def _head_norm_rope(x, gain, ones_bd, cos, sin):
    ms = _segsum(x * x, ones_bd) * (1.0 / HEAD_DIM)
    y = x * lax.rsqrt(ms + NORM_EPS) * gain
    outs = []
    half = ROPE_AXIS_DIM // 2
    lane = lax.broadcasted_iota(jnp.int32, (1, LANES), 1)
    first_half = (lane % ROPE_AXIS_DIM) < half
    for i in range(x.shape[1] // LANES):
        yc = y[:, i * LANES:(i + 1) * LANES]
        partner = jnp.where(first_half, pltpu.roll(yc, LANES - half, 1), pltpu.roll(yc, half, 1))
        outs.append(yc * cos + partner * sin)
    return jnp.concatenate(outs, axis=1)


def _attn_prep_kernel(z_ref, qn_ref, kn_ref, cos_ref, sin_ref, onesq_ref, onesk_ref,
                      q_o, k_o, v_o):
    z = z_ref[0]
    cos = cos_ref[...]
    sin = sin_ref[...]
    q = _head_norm_rope(z[:, :C_WIDTH], qn_ref[...], onesq_ref[...], cos, sin)
    k = _head_norm_rope(z[:, C_WIDTH:C_WIDTH + KV_WIDTH], kn_ref[...], onesk_ref[...], cos, sin)
    q_o[0] = (q * (ATTN_SCALE * math.log2(math.e))).T.astype(q_o.dtype)
    for j in range(C_KV_HEADS):
        k_o[0, j] = k[:, j * HEAD_DIM:(j + 1) * HEAD_DIM].astype(k_o.dtype)
    v_o[0] = z[:, C_WIDTH + KV_WIDTH:C_WIDTH + 2 * KV_WIDTH].T.astype(v_o.dtype)


def _attn_prep(z, q_norm, k_norm, ctx_len, tm):
    bsz, rows, ncol = z.shape
    cos, sin = _rope_tables(rows, ctx_len)
    qn = jnp.tile(q_norm, C_HEADS).reshape(1, C_WIDTH)
    kn = jnp.tile(k_norm, C_KV_HEADS).reshape(1, KV_WIDTH)
    const = lambda shape: pl.BlockSpec(shape, lambda b, t: (0,) * len(shape))
    return pl.pallas_call(
        _attn_prep_kernel,
        grid=(bsz, rows // tm),
        in_specs=[pl.BlockSpec((1, tm, ncol), lambda b, t: (b, t, 0)),
                  const((1, C_WIDTH)), const((1, KV_WIDTH)),
                  pl.BlockSpec((tm, LANES), lambda b, t: (t, 0)),
                  pl.BlockSpec((tm, LANES), lambda b, t: (t, 0)),
                  const((C_WIDTH, C_WIDTH)), const((KV_WIDTH, KV_WIDTH))],
        out_specs=[pl.BlockSpec((1, C_WIDTH, tm), lambda b, t: (b, 0, t)),
                   pl.BlockSpec((1, C_KV_HEADS, tm, HEAD_DIM), lambda b, t: (b, 0, t, 0)),
                   pl.BlockSpec((1, KV_WIDTH, tm), lambda b, t: (b, 0, t))],
        out_shape=[jax.ShapeDtypeStruct((bsz, C_WIDTH, rows), BF16),
                   jax.ShapeDtypeStruct((bsz, C_KV_HEADS, rows, HEAD_DIM), BF16),
                   jax.ShapeDtypeStruct((bsz, KV_WIDTH, rows), BF16)],
        compiler_params=_cparams(("parallel", "parallel")),
        name="attn_prep",
    )(z, qn, kn, cos, sin, _block_ones(C_WIDTH, HEAD_DIM), _block_ones(KV_WIDTH, HEAD_DIM))


def _attn_kernel(qt_ref, k_ref, vt_ref, o_ref, m_ref, l_ref, acc_ref, *, kc):
    kt = pl.program_id(3)

    @pl.when(kt == 0)
    def _():
        m_ref[...] = jnp.full_like(m_ref, NEG_INF)
        l_ref[...] = jnp.zeros_like(l_ref)
        acc_ref[...] = jnp.zeros_like(acc_ref)

    nkv = k_ref.shape[1]
    grp = range(nkv * C_GROUP)
    qts = [qt_ref[0, g * HEAD_DIM:(g + 1) * HEAD_DIM, :] for g in grp]
    nck = k_ref.shape[2] // kc

    def scores(c):
        kbs = [k_ref[0, j, c * kc:(c + 1) * kc, :] for j in range(nkv)]
        return [_dot(kbs[g // C_GROUP], qts[g]) for g in grp]

    m = [m_ref[g] for g in grp]
    l = [l_ref[g] for g in grp]
    acc = [acc_ref[g] for g in grp]
    s_next = scores(0)
    for c in range(nck):
        s = s_next
        if c + 1 < nck:
            s_next = scores(c + 1)
        vts = [vt_ref[0, j * HEAD_DIM:(j + 1) * HEAD_DIM, c * kc:(c + 1) * kc]
               for j in range(nkv)]
        m_new = [jnp.maximum(m[g], jnp.max(s[g], axis=0, keepdims=True)) for g in grp]
        p = [jnp.exp2(s[g] - m_new[g]) for g in grp]
        pv = [_dot(vts[g // C_GROUP], p[g].astype(BF16)) for g in grp]
        for g in grp:
            alpha = jnp.exp2(m[g] - m_new[g])
            l[g] = alpha * l[g] + jnp.sum(p[g], axis=0, keepdims=True)
            acc[g] = alpha * acc[g] + pv[g]
        m = m_new
    for g in grp:
        m_ref[g] = m[g]
        l_ref[g] = l[g]
        acc_ref[g] = acc[g]

    @pl.when(kt == pl.num_programs(3) - 1)
    def _():
        o_ref[0] = jnp.concatenate(
            [acc_ref[g] / l_ref[g] for g in grp], axis=0).astype(o_ref.dtype)


ATTN_MAX_KEY_TILE = 8448
ATTN_MAX_KEY_CHUNK = 256
ATTN_KV_PER_STEP = 1


def _key_tile(rows, limit=ATTN_MAX_KEY_TILE):
    return max(t for t in range(LANES, min(rows, limit) + 1, LANES) if rows % t == 0)


def _attention(qt, k, vt, ctx_len, tq, tk):
    bsz, _, rows = qt.shape
    seq = rows - ctx_len
    qoff = ctx_len // tq
    nkv = ATTN_KV_PER_STEP
    nq = nkv * C_GROUP
    gw = nq * HEAD_DIM
    kern = functools.partial(_attn_kernel, kc=_key_tile(tk, ATTN_MAX_KEY_CHUNK))
    return pl.pallas_call(
        kern,
        grid=(bsz, C_KV_HEADS // nkv, seq // tq, rows // tk),
        in_specs=[pl.BlockSpec((1, gw, tq), lambda b, j, i, kk: (b, j, i + qoff)),
                  pl.BlockSpec((1, nkv, tk, HEAD_DIM), lambda b, j, i, kk: (b, j, kk, 0)),
                  pl.BlockSpec((1, nkv * HEAD_DIM, tk), lambda b, j, i, kk: (b, j, kk))],
        out_specs=pl.BlockSpec((1, gw, tq), lambda b, j, i, kk: (b, j, i)),
        out_shape=jax.ShapeDtypeStruct((bsz, C_WIDTH, seq), BF16),
        scratch_shapes=[pltpu.VMEM((nq, 1, tq), F32),
                        pltpu.VMEM((nq, 1, tq), F32),
                        pltpu.VMEM((nq, HEAD_DIM, tq), F32)],
        compiler_params=_cparams(("parallel", "parallel", "parallel", "arbitrary")),
        name="attention",
    )(qt, k, vt)


def _conv_kernel(zc_ref, zp_ref, zn_ref, w_ref, b_ref, lg_ref, lb_ref, o_ref, ybuf, *, tm, halo):
    t = pl.program_id(1)
    nt = pl.num_programs(1)

    def glu(u):
        return u[:, :D_WIDTH] * jax.nn.sigmoid(u[:, D_WIDTH:])

    ybuf[0:halo, :] = jnp.where(t == 0, 0.0, glu(zp_ref[0]))
    ybuf[halo:halo + tm, :] = glu(zc_ref[0])
    ybuf[halo + tm:, :] = jnp.where(t == nt - 1, 0.0, glu(zn_ref[0]))
    acc = jnp.zeros((tm, D_WIDTH), F32)
    for j in range(D_CONV_WIDTH):
        off = halo - D_PAD + j
        acc = acc + w_ref[j:j + 1, :] * ybuf[off:off + tm, :]
    y = acc + b_ref[...]
    mu = jnp.mean(y, axis=-1, keepdims=True)
    yc = y - mu
    var = jnp.mean(yc * yc, axis=-1, keepdims=True)
    yn = yc * lax.rsqrt(var + LN_EPS) * lg_ref[...] + lb_ref[...]
    o_ref[0] = (yn * jax.nn.sigmoid(yn)).astype(o_ref.dtype)


def _conformer_conv(z, dw_w, dw_b, cn_g, cn_b, ctx_len, tm):
    bsz, rows, ncol = z.shape
    seq = rows - ctx_len
    halo = 2 * SUBLANES
    hb = tm // halo
    off = ctx_len // tm
    offh = ctx_len // halo
    nh = seq // halo
    kern = functools.partial(_conv_kernel, tm=tm, halo=halo)
    row = lambda w: pl.BlockSpec((1, w), lambda b, t: (0, 0))
    return pl.pallas_call(
        kern,
        grid=(bsz, seq // tm),
        in_specs=[pl.BlockSpec((1, tm, ncol), lambda b, t: (b, t + off, 0)),
                  pl.BlockSpec((1, halo, ncol),
                               lambda b, t: (b, offh + jnp.maximum(t * hb - 1, 0), 0)),
                  pl.BlockSpec((1, halo, ncol),
                               lambda b, t: (b, offh + jnp.minimum((t + 1) * hb, nh - 1), 0)),
                  pl.BlockSpec((D_CONV_WIDTH, D_WIDTH), lambda b, t: (0, 0)),
                  row(D_WIDTH), row(D_WIDTH), row(D_WIDTH)],
        out_specs=pl.BlockSpec((1, tm, D_WIDTH), lambda b, t: (b, t, 0)),
        out_shape=jax.ShapeDtypeStruct((bsz, seq, D_WIDTH), BF16),
        scratch_shapes=[pltpu.VMEM((tm + 2 * halo, D_WIDTH), F32)],
        compiler_params=_cparams(("parallel", "parallel")),
        name="conformer_conv",
    )(z, z, z, dw_w, dw_b.reshape(1, -1), cn_g.reshape(1, -1), cn_b.reshape(1, -1))


def _forward(x, c, ctx, c_ctx, l0, l1, norm_f):
    bsz, seq, d = x.shape
    ctx_len = ctx.shape[1]
    tm = min(256, ctx_len)
    ctx_tiles = ctx_len // tm
    xs = jnp.concatenate([ctx, x], axis=1)

    mod = _modulation(c, c_ctx, l0["mod_w"], l0["mod_b"])
    z, zf = _normmod_matmul(xs, l0["norm1"], mod, l0["w_in"], tm, ctx_tiles, A_COLS)
    r, v, g, bonus, kk, lwf, kdf, bf, lwr, kdr, br = _rwkv_prep(z, l0, tm, ctx_tiles)
    yf, yr = _rwkv_scan(r, v, kk, lwf, kdf, bf, lwr, kdr, br, ctx_len)
    o_rwkv = _rwkv_finish(yf, yr, bonus, g, l0["lnx_g"], l0["lnx_b"], tm)
    fcs = _fnet_chan(zf, tm)
    ts_c = min(256, ctx_len)
    ts_l = min(2048, seq)
    f_ctx = _fnet_seq(fcs, 0, ctx_len, ts_c, ts_c)
    f_lat = _fnet_seq(fcs[:, ctx_len:], 0, seq, ts_l, min(512, seq))
    o_fnet = jnp.concatenate([f_ctx, f_lat], axis=1)
    xs = _proj_residual(o_rwkv, o_fnet, xs, mod, l0["w_out"], tm, 0, ctx_tiles)
    xs = _peer(xs, l0["norm2"], mod, l0, tm, 0, ctx_tiles, False, norm_f)

    mod = _modulation(c, c_ctx, l1["mod_w"], l1["mod_b"])
    z, zu = _normmod_matmul(xs, l1["norm1"], mod, l1["w_in"], tm, ctx_tiles,
                            C_WIDTH + 2 * KV_WIDTH)
    qt, k, vt = _attn_prep(z, l1["q_norm"], l1["k_norm"], ctx_len, tm)
    o_attn = _attention(qt, k, vt, ctx_len, tm, _key_tile(ctx_len + seq))
    o_conv = _conformer_conv(zu, l1["dw_w"], l1["dw_b"], l1["cn_g"], l1["cn_b"], ctx_len, tm)
    xl = _proj_residual(o_attn, o_conv, xs, mod, l1["w_out"], tm, ctx_tiles, ctx_tiles,
                        o1_transposed=True)
    return _peer(xl, l1["norm2"], mod, l1, tm, ctx_tiles, ctx_tiles, True, norm_f)


def kernel(x, c, ctx, c_ctx, l0_mod_w, l0_mod_b, l0_norm1, l0_w_in, l0_shift_prev, l0_shift_next, l0_w0, l0_w2, l0_a0, l0_a2, l0_g2, l0_k_k, l0_k_a, l0_r_k, l0_lnx_g, l0_lnx_b, l0_w_out, l0_norm2, l0_pq, l0_sk1, l0_sk2, l0_pu, l0_pv, l1_mod_w, l1_mod_b, l1_norm1, l1_w_in, l1_q_norm, l1_k_norm, l1_dw_w, l1_dw_b, l1_cn_g, l1_cn_b, l1_w_out, l1_norm2, l1_pq, l1_sk1, l1_sk2, l1_pu, l1_pv, norm_f):
    l0 = dict(mod_w=l0_mod_w, mod_b=l0_mod_b, norm1=l0_norm1, w_in=l0_w_in,
              shift_prev=l0_shift_prev, shift_next=l0_shift_next, w0=l0_w0, w2=l0_w2,
              a0=l0_a0, a2=l0_a2, g2=l0_g2, k_k=l0_k_k, k_a=l0_k_a, r_k=l0_r_k,
              lnx_g=l0_lnx_g, lnx_b=l0_lnx_b, w_out=l0_w_out, norm2=l0_norm2,
              pq=l0_pq, sk1=l0_sk1, sk2=l0_sk2, pu=l0_pu, pv=l0_pv)
    l1 = dict(mod_w=l1_mod_w, mod_b=l1_mod_b, norm1=l1_norm1, w_in=l1_w_in,
              q_norm=l1_q_norm, k_norm=l1_k_norm, dw_w=l1_dw_w, dw_b=l1_dw_b,
              cn_g=l1_cn_g, cn_b=l1_cn_b, w_out=l1_w_out, norm2=l1_norm2,
              pq=l1_pq, sk1=l1_sk1, sk2=l1_sk2, pu=l1_pu, pv=l1_pv)
    return _forward(x, c, ctx, c_ctx, l0, l1, norm_f)
```

```python
import functools
import math

import jax
import jax.numpy as jnp
import numpy as np
from jax import lax
from jax.experimental import pallas as pl
from jax.experimental.pallas import tpu as pltpu

F32 = jnp.float32
BF16 = jnp.bfloat16

LANES = 128
SUBLANES = 8
VMEM_LIMIT_BYTES = 56 * 1024 * 1024

N_MOD = 6
NORM_EPS = 1e-6
LN_EPS = 1e-5
GRID_W = 64
HEAD_DIM = 64
A_WIDTH = 768
A_HEADS = A_WIDTH // HEAD_DIM
A_RANK_W = 64
A_RANK_A = 64
A_RANK_G = 128
A_LNX_EPS = 64e-5
A_COLS = 3 * A_WIDTH + 2 * A_RANK_W + 2 * A_RANK_A + A_RANK_G
B_WIDTH = 256
B_GROUP_DIM = 64
C_WIDTH = 768
C_HEADS = 12
C_KV_HEADS = 4
C_GROUP = C_HEADS // C_KV_HEADS
KV_WIDTH = C_KV_HEADS * HEAD_DIM
ROPE_AXIS_DIM = HEAD_DIM // 2
ROPE_THETA = 10000.0
ATTN_SCALE = HEAD_DIM ** -0.5
D_WIDTH = 256
D_CONV_WIDTH = 31
D_PAD = D_CONV_WIDTH // 2
PK_HEADS = 8
PK_DIM = 256
PK_HALF = 128
N_KEYS = 128
PK_TOPK = 16

SCAN_CHUNK = 64
NEG_INF = float("-inf")


def _cparams(semantics):
    return pltpu.CompilerParams(dimension_semantics=semantics,
                                vmem_limit_bytes=VMEM_LIMIT_BYTES)


def _split_bf16(x):
    hi = x.astype(BF16)
    lo = (x - hi.astype(F32)).astype(BF16)
    return hi, lo


def _dot(a, b):
    return jnp.dot(a, b, preferred_element_type=F32)


def _dot_hp(a, b):
    return _mm3(_split_bf16(a), _split_bf16(b))


_NN = (((1,), (0,)), ((), ()))
_NT = (((1,), (1,)), ((), ()))
_TN = (((0,), (0,)), ((), ()))


def _mm3(a, b, dims=_NN):
    (ah, al), (bh, bl) = a, b
    dg = lambda x, y: lax.dot_general(x, y, dims, preferred_element_type=F32)
    return dg(ah, bh) + (dg(ah, bl) + dg(al, bh))


def _segsum(x, ones_bd):
    hi, lo = _split_bf16(x)
    return _dot(hi, ones_bd) + _dot(lo, ones_bd)


def _block_ones(width, seg):
    r = np.arange(width) // seg
    return jnp.asarray((r[:, None] == r[None, :]).astype(np.float32), dtype=BF16)


def _mod_kernel(c_ref, w_ref, b_ref, o_ref):
    c = c_ref[...]
    s = c * jax.nn.sigmoid(c)
    o_ref[...] = _dot_hp(s, w_ref[...]) + b_ref[...]


def _modulation(c, c_ctx, mod_w, mod_b):
    bsz, d = c.shape
    rows = SUBLANES * pl.cdiv(bsz + 1, SUBLANES)
    cc = jnp.zeros((rows, d), F32).at[:bsz].set(c).at[bsz].set(c_ctx)
    n = mod_w.shape[1]
    tn = n // 4
    out = pl.pallas_call(
        _mod_kernel,
        grid=(n // tn,),
        in_specs=[pl.BlockSpec((rows, d), lambda j: (0, 0)),
                  pl.BlockSpec((d, tn), lambda j: (0, j)),
                  pl.BlockSpec((1, tn), lambda j: (0, j))],
        out_specs=pl.BlockSpec((rows, tn), lambda j: (0, j)),
        out_shape=jax.ShapeDtypeStruct((rows, n), F32),
        compiler_params=_cparams(("arbitrary",)),
        name="adaln_mod",
    )(cc, mod_w, mod_b.reshape(1, n))
    lat = out[:bsz].reshape(bsz, N_MOD, d)
    ctx = jnp.broadcast_to(out[bsz].reshape(1, N_MOD, d), (bsz, N_MOD, d))
    return jnp.stack([ctx, lat], axis=1)


def _norm_mod(x, g, mod, row):
    ms = jnp.mean(x * x, axis=-1, keepdims=True)
    y = x * lax.rsqrt(ms + NORM_EPS) * g
    return y * (1.0 + mod[row + 1:row + 2]) + mod[row:row + 1]


def _normmod_matmul_kernel(x_ref, g_ref, mod_ref, w_ref, o1_ref, o2_ref):
    h = _norm_mod(x_ref[0], g_ref[...], mod_ref[0, 0], 0)
    z = _dot(h.astype(BF16), w_ref[...])
    n1 = o1_ref.shape[2]
    o1_ref[0] = z[:, :n1]
    o2_ref[0] = z[:, n1:]


def _normmod_matmul(x, g, mod, w, tm, ctx_tiles, n1):
    bsz, rows, d = x.shape
    n = w.shape[1]
    widths = (n1, n - n1)
    return pl.pallas_call(
        _normmod_matmul_kernel,
        grid=(bsz, rows // tm),
        in_specs=[pl.BlockSpec((1, tm, d), lambda b, t: (b, t, 0)),
                  pl.BlockSpec((1, d), lambda b, t: (0, 0)),
                  pl.BlockSpec((1, 1, N_MOD, d),
                               lambda b, t: (b, jnp.where(t >= ctx_tiles, 1, 0), 0, 0)),
                  pl.BlockSpec((d, n), lambda b, t: (0, 0))],
        out_specs=[pl.BlockSpec((1, tm, wd), lambda b, t: (b, t, 0)) for wd in widths],
        out_shape=[jax.ShapeDtypeStruct((bsz, rows, wd), F32) for wd in widths],
        compiler_params=_cparams(("parallel", "parallel")),
        name="normmod_proj",
    )(x, g.reshape(1, d), mod, w.astype(BF16))


def _rwkv_prep_kernel(z_ref, zp_ref, zn_ref, mup_ref, mun_ref, w0_ref, w2_ref, a0_ref, a2_ref,
                      g2_ref, kk_ref, ka_ref, rk_ref, ones_ref,
                      r_o, v_o, g_o, bonus_o, kkn_o, lwf_o, kdf_o, bf_o, lwr_o, kdr_o, br_o,
                      *, tm, ctx_tiles):
    t = pl.program_id(1)
    nt = pl.num_programs(1)
    z = z_ref[0][:, :A_COLS]
    zp_row = zp_ref[0][SUBLANES - 1:SUBLANES, :A_COLS]
    zn_row = zn_ref[0][0:1, :A_COLS]
    first = jnp.logical_or(t == 0, t == ctx_tiles)
    last = jnp.logical_or(t == ctx_tiles - 1, t == nt - 1)
    zp_row = jnp.where(first, 0.0, zp_row)
    zn_row = jnp.where(last, 0.0, zn_row)
    ridx = lax.broadcasted_iota(jnp.int32, (tm, 1), 0)
    z_prev = jnp.where(ridx == 0, zp_row, pltpu.roll(z, 1, 0))
    z_next = jnp.where(ridx == tm - 1, zn_row, pltpu.roll(z, tm - 1, 0))
    zs = z + mup_ref[...] * (z_prev - z) + mun_ref[...] * (z_next - z)

    w = A_WIDTH
    r = zs[:, 0:w]
    k = zs[:, w:2 * w]
    v = zs[:, 2 * w:3 * w]
    o = 3 * w
    xw = zs[:, o:o + 2 * A_RANK_W]
    o += 2 * A_RANK_W
    xa = zs[:, o:o + 2 * A_RANK_A]
    o += 2 * A_RANK_A
    xg = zs[:, o:o + A_RANK_G]
    ones_bd = ones_ref[...]

    g_o[0] = _dot_hp(jax.nn.sigmoid(xg), g2_ref[...])
    kk = k * kk_ref[...]
    nrm = jnp.sqrt(_segsum(kk * kk, ones_bd))
    kk = kk / jnp.maximum(nrm, 1e-12)
    r_o[0] = r
    v_o[0] = v
    kkn_o[0] = kk
    k_sum = jnp.zeros_like(r)
    tw = jnp.tanh(xw)
    for d, (lw_o, kd_o, b_o) in enumerate(((lwf_o, kdf_o, bf_o), (lwr_o, kdr_o, br_o))):
        wl = w0_ref[d:d + 1, :] + _dot_hp(tw[:, d * A_RANK_W:(d + 1) * A_RANK_W], w2_ref[d])
        w_log = -jax.nn.softplus(-wl) - 0.5
        lw_o[0] = -jnp.exp(w_log)
        a_gate = jax.nn.sigmoid(
            a0_ref[d:d + 1, :] + _dot_hp(xa[:, d * A_RANK_A:(d + 1) * A_RANK_A], a2_ref[d]))
        k_d = k * (1.0 + (a_gate - 1.0) * ka_ref[...])
        kd_o[0] = k_d
        b_o[0] = kk * a_gate
        k_sum = k_sum + k_d
    bonus_o[0] = _segsum(r * k_sum * rk_ref[...], ones_bd) * v


def _rwkv_prep(z, p, tm, ctx_tiles):
    bsz, rows, ncol = z.shape
    w = A_WIDTH
    hb = tm // SUBLANES
    nblk8 = rows // SUBLANES
    row_spec = lambda width: pl.BlockSpec((1, width), lambda b, t: (0, 0))
    full = lambda shape: pl.BlockSpec(shape, lambda b, t: (0,) * len(shape))
    out_spec = pl.BlockSpec((1, tm, w), lambda b, t: (b, t, 0))
    out_shape = jax.ShapeDtypeStruct((bsz, rows, w), F32)
    kern = functools.partial(_rwkv_prep_kernel, tm=tm, ctx_tiles=ctx_tiles)
    return pl.pallas_call(
        kern,
        grid=(bsz, rows // tm),
        in_specs=[pl.BlockSpec((1, tm, ncol), lambda b, t: (b, t, 0)),
                  pl.BlockSpec((1, SUBLANES, ncol),
                               lambda b, t: (b, jnp.maximum(t * hb - 1, 0), 0)),
                  pl.BlockSpec((1, SUBLANES, ncol),
                               lambda b, t: (b, jnp.minimum((t + 1) * hb, nblk8 - 1), 0)),
                  row_spec(A_COLS), row_spec(A_COLS),
                  full((2, w)), full((2, A_RANK_W, w)), full((2, w)), full((2, A_RANK_A, w)),
                  full((A_RANK_G, w)), row_spec(w), row_spec(w), row_spec(w), full((w, w))],
        out_specs=[out_spec] * 11,
        out_shape=[out_shape] * 11,
        compiler_params=_cparams(("parallel", "parallel")),
        name="rwkv_prep",
    )(z, z, z, p["shift_prev"].reshape(1, -1), p["shift_next"].reshape(1, -1),
      p["w0"], p["w2"], p["a0"], p["a2"], p["g2"], p["k_k"].reshape(1, w),
      p["k_a"].reshape(1, w), p["r_k"].reshape(1, w), _block_ones(w, HEAD_DIM))


SCAN_HEADS = 12


def _scan_kernel(rf, vf, kkf, lwf, kdf, bf, rr, vr, kkr, lwr, kdr, br, yf_o, yr_o, s_ref, *, c):
    @pl.when(pl.program_id(2) == 0)
    def _():
        s_ref[...] = jnp.zeros_like(s_ref)

    ri = lax.broadcasted_iota(jnp.int32, (c, c), 0)
    ci = lax.broadcasted_iota(jnp.int32, (c, c), 1)
    ri2 = lax.broadcasted_iota(jnp.int32, (c, 2 * c), 0)
    ci2 = lax.broadcasted_iota(jnp.int32, (c, 2 * c), 1)
    ci2 = jnp.where(ci2 >= c, ci2 - c, ci2)
    dirs = ((rf, vf, kkf, lwf, kdf, bf, ci <= ri, ci < ri, ci2 <= ri2, c - 1),
            (rr, vr, kkr, lwr, kdr, br, ci >= ri, ci > ri, ci2 >= ri2, 0))
    lhs, rhs, vs, ss, pend, incl, strict = [], [], [], [], [], [], []
    for d, (r_, v_, kk_, lw_, kd_, b_, m_incl, m_strict, m_incl2, edge) in enumerate(dirs):
        lw = lw_[0]
        tri = m_incl.astype(F32).astype(BF16)
        lw_h = lw.astype(BF16)
        lw_m, lw_l = _split_bf16(lw - lw_h.astype(F32))
        lcum = _dot(tri, lw_h) + (_dot(tri, lw_m) + _dot(tri, lw_l))
        p_in = jnp.exp(lcum)
        p_inv = jnp.exp(-lcum)
        lhs_all = jnp.concatenate([-kk_[0] * jnp.exp(lcum - lw), r_[0] * p_in], axis=0)
        rhs_all = jnp.concatenate([b_[0] * p_inv, kd_[0] * p_inv], axis=0)
        v_all = v_[0]
        for h in range(SCAN_HEADS):
            sl = slice(h * HEAD_DIM, (h + 1) * HEAD_DIM)
            lhs.append(lhs_all[:, sl])
            rhs.append(rhs_all[:, sl])
            vs.append(v_all[:, sl])
            ss.append(s_ref[d, h])
            pend.append(p_in[edge:edge + 1, sl])
            incl.append(m_incl2)
            strict.append(m_strict)
    n = len(lhs)
    idx = range(n)
    lhs2 = [_split_bf16(lhs[i]) for i in idx]
    rhs2 = [_split_bf16(rhs[i]) for i in idx]
    v2 = [_split_bf16(vs[i]) for i in idx]
    s2 = [_split_bf16(ss[i]) for i in idx]
    g = [_mm3(lhs2[i], rhs2[i], _NT) for i in idx]
    xs = [_mm3(lhs2[i], s2[i], _NT) for i in idx]
    a_ak = [_split_bf16(jnp.where(strict[i], g[i][:c, c:], 0.0)) for i in idx]
    x = [xs[i][:c] + _mm3(a_ak[i], v2[i]) for i in idx]
    apow = [_split_bf16(jnp.where(strict[i], g[i][:c, :c], 0.0)) for i in idx]
    steps = int(math.log2(c))
    for k in range(steps):
        x2 = [_split_bf16(x[i]) for i in idx]
        x = [x[i] + _mm3(apow[i], x2[i]) for i in idx]
        if k + 1 < steps:
            apow = [_split_bf16(_mm3(apow[i], apow[i])) for i in idx]
    x2 = [_split_bf16(x[i]) for i in idx]
    sav = [tuple(jnp.concatenate([x2[i][t], v2[i][t]], axis=0) for t in range(2)) for i in idx]
    a_r = [_split_bf16(jnp.where(incl[i], g[i][c:], 0.0)) for i in idx]
    y = [xs[i][c:] + _mm3(a_r[i], sav[i]) for i in idx]
    s_new = [(ss[i] + _mm3(sav[i], rhs2[i], _TN)) * pend[i] for i in idx]
    for d, y_o in enumerate((yf_o, yr_o)):
        for h in range(SCAN_HEADS):
            s_ref[d, h] = s_new[d * SCAN_HEADS + h]
        y_o[0] = jnp.concatenate(y[d * SCAN_HEADS:(d + 1) * SCAN_HEADS], axis=1)


def _rwkv_scan(r, v, kk, lwf, kdf, bf, lwr, kdr, br, ctx_len):
    bsz, rows, w = r.shape
    c = SCAN_CHUNK
    nch = rows // c
    cch = ctx_len // c
    lanes = SCAN_HEADS * HEAD_DIM
    fwd = lambda b, hp, i: (b, i, hp)
    rev = lambda b, hp, i: (b, jnp.where(i < cch, cch - 1 - i, nch - 1 + cch - i), hp)
    blk = lambda im: pl.BlockSpec((1, c, lanes), im)
    kern = functools.partial(_scan_kernel, c=c)
    out_shape = jax.ShapeDtypeStruct((bsz, rows, w), F32)
    return pl.pallas_call(
        kern,
        grid=(bsz, w // lanes, nch),
        in_specs=[blk(fwd)] * 6 + [blk(rev)] * 6,
        out_specs=[blk(fwd), blk(rev)],
        out_shape=[out_shape, out_shape],
        scratch_shapes=[pltpu.VMEM((2, SCAN_HEADS, HEAD_DIM, HEAD_DIM), F32)],
        compiler_params=_cparams(("parallel", "parallel", "arbitrary")),
        name="rwkv_scan",
    )(r, v, kk, lwf, kdf, bf, r, v, kk, lwr, kdr, br)


def _rwkv_finish_kernel(yf, yr, bonus, g, lg, lb, ones_ref, o_ref):
    y = yf[0] + yr[0]
    ones_bd = ones_ref[...]
    inv = 1.0 / HEAD_DIM
    mu = _segsum(y, ones_bd) * inv
    yc = y - mu
    var = _segsum(yc * yc, ones_bd) * inv
    yn = yc * lax.rsqrt(var + A_LNX_EPS) * lg[...] + lb[...]
    o_ref[0] = ((yn + bonus[0]) * g[0]).astype(o_ref.dtype)


def _rwkv_finish(yf, yr, bonus, g, lnx_g, lnx_b, tm):
    bsz, rows, w = yf.shape
    blk = pl.BlockSpec((1, tm, w), lambda b, t: (b, t, 0))
    row = pl.BlockSpec((1, w), lambda b, t: (0, 0))
    return pl.pallas_call(
        _rwkv_finish_kernel,
        grid=(bsz, rows // tm),
        in_specs=[blk, blk, blk, blk, row, row, pl.BlockSpec((w, w), lambda b, t: (0, 0))],
        out_specs=blk,
        out_shape=jax.ShapeDtypeStruct((bsz, rows, w), BF16),
        compiler_params=_cparams(("parallel", "parallel")),
        name="rwkv_finish",
    )(yf, yr, bonus, g, lnx_g.reshape(1, w), lnx_b.reshape(1, w), _block_ones(w, HEAD_DIM))


def _fnet_chan_kernel(z_ref, m_ref, o_ref):
    o_ref[0] = _dot(z_ref[0].astype(BF16), m_ref[...]).astype(o_ref.dtype)


def _fnet_chan(z, tm):
    bsz, rows, ncol = z.shape
    j = np.arange(B_WIDTH)
    same = (j[:, None] // B_GROUP_DIM) == (j[None, :] // B_GROUP_DIM)
    ang = 2.0 * np.pi * ((j[:, None] % B_GROUP_DIM) * (j[None, :] % B_GROUP_DIM) % B_GROUP_DIM) / B_GROUP_DIM
    m = np.concatenate([np.where(same, np.cos(ang), 0.0), np.where(same, np.sin(ang), 0.0)], axis=1)
    return pl.pallas_call(
        _fnet_chan_kernel,
        grid=(bsz, rows // tm),
        in_specs=[pl.BlockSpec((1, tm, ncol), lambda b, t: (b, t, 0)),
                  pl.BlockSpec((B_WIDTH, 2 * B_WIDTH), lambda b, t: (0, 0))],
        out_specs=pl.BlockSpec((1, tm, 2 * B_WIDTH), lambda b, t: (b, t, 0)),
        out_shape=jax.ShapeDtypeStruct((bsz, rows, 2 * B_WIDTH), BF16),
        compiler_params=_cparams(("parallel", "parallel")),
        name="fnet_chan",
    )(z, jnp.asarray(m, dtype=BF16))


def _fnet_seq_kernel(fcs_ref, cb_ref, sb_ref, c0_ref, s0_ref, o_ref, acc_ref, *, scale, nb):
    k = pl.program_id(1)

    @pl.when(k == 0)
    def _():
        acc_ref[...] = jnp.zeros_like(acc_ref)

    cb = cb_ref[...]
    sb = sb_ref[...]
    c0 = c0_ref[0]
    s0 = s0_ref[0]
    cm = (c0 * cb - s0 * sb).astype(BF16)
    sm = (s0 * cb + c0 * sb).astype(BF16)
    for b in range(nb):
        fcs = fcs_ref[b]
        acc_ref[b] += _dot(cm, fcs[:, :B_WIDTH]) - _dot(sm, fcs[:, B_WIDTH:])

    @pl.when(k == pl.num_programs(1) - 1)
    def _():
        o_ref[...] = (acc_ref[...] * scale).astype(o_ref.dtype)


def _fnet_seq(fcs, row0, length, ts, tk):
    bsz = fcs.shape[0]
    ds = jnp.arange(ts, dtype=jnp.int32)[:, None]
    tt = jnp.arange(length, dtype=jnp.int32)[None, :]
    ang = ((ds * tt) % length).astype(F32) * (2.0 * math.pi / length)
    cb, sb = jnp.cos(ang), jnp.sin(ang)
    s0 = (jnp.arange(length // ts, dtype=jnp.int32) * ts)[:, None]
    ang0 = ((s0 * tt) % length).astype(F32) * (2.0 * math.pi / length)
    c0, sn0 = jnp.cos(ang0)[:, None, :], jnp.sin(ang0)[:, None, :]
    kern = functools.partial(_fnet_seq_kernel, scale=1.0 / math.sqrt(length * B_GROUP_DIM), nb=bsz)
    koff = row0 // tk
    return pl.pallas_call(
        kern,
        grid=(length // ts, length // tk),
        in_specs=[pl.BlockSpec((bsz, tk, 2 * B_WIDTH), lambda s, k: (0, k + koff, 0)),
                  pl.BlockSpec((ts, tk), lambda s, k: (0, k)),
                  pl.BlockSpec((ts, tk), lambda s, k: (0, k)),
                  pl.BlockSpec((1, 1, tk), lambda s, k: (s, 0, k)),
                  pl.BlockSpec((1, 1, tk), lambda s, k: (s, 0, k))],
        out_specs=pl.BlockSpec((bsz, ts, B_WIDTH), lambda s, k: (0, s, 0)),
        out_shape=jax.ShapeDtypeStruct((bsz, length, B_WIDTH), BF16),
        scratch_shapes=[pltpu.VMEM((bsz, ts, B_WIDTH), F32)],
        compiler_params=_cparams(("parallel", "arbitrary")),
        name="fnet_seq",
    )(fcs, cb, sb, c0, sn0)


def _proj_residual_kernel(o1_ref, o2_ref, x_ref, mod_ref, w1_ref, w2_ref, out_ref, *, o1_transposed):
    o1 = o1_ref[0]
    if o1_transposed:
        o1 = o1.astype(F32).T.astype(BF16)
    mix = _dot(o1, w1_ref[...]) + _dot(o2_ref[0], w2_ref[...])
    gate = mod_ref[0, 0][2:3]
    out_ref[0] = x_ref[0] + gate * mix


def _proj_residual(o1, o2, x, mod, w_out, tm, tile_off, ctx_tiles, o1_transposed=False):
    bsz, rows, w2 = o2.shape
    w1 = o1.shape[1] if o1_transposed else o1.shape[2]
    d = x.shape[2]
    o1_spec = (pl.BlockSpec((1, w1, tm), lambda b, t: (b, 0, t)) if o1_transposed
               else pl.BlockSpec((1, tm, w1), lambda b, t: (b, t, 0)))
    kern = functools.partial(_proj_residual_kernel, o1_transposed=o1_transposed)
    return pl.pallas_call(
        kern,
        grid=(bsz, rows // tm),
        in_specs=[o1_spec,
                  pl.BlockSpec((1, tm, w2), lambda b, t: (b, t, 0)),
                  pl.BlockSpec((1, tm, d), lambda b, t: (b, t + tile_off, 0)),
                  pl.BlockSpec((1, 1, N_MOD, d),
                               lambda b, t: (b, jnp.where(t + tile_off >= ctx_tiles, 1, 0), 0, 0)),
                  pl.BlockSpec((w1, d), lambda b, t: (0, 0)),
                  pl.BlockSpec((w2, d), lambda b, t: (0, 0))],
        out_specs=pl.BlockSpec((1, tm, d), lambda b, t: (b, t, 0)),
        out_shape=jax.ShapeDtypeStruct((bsz, rows, d), F32),
        compiler_params=_cparams(("parallel", "parallel")),
        name="proj_residual",
    )(o1, o2, x, mod, w_out[:w1].astype(BF16), w_out[w1:].astype(BF16))


def _rwkv_proj_kernel(yf, yr, bonus, g, lg, lb, ones_ref, o2_ref, x_ref, mod_ref, w1_ref, w2_ref,
                      out_ref):
    y = yf[0] + yr[0]
    ones_bd = ones_ref[...]
    inv = 1.0 / HEAD_DIM
    mu = _segsum(y, ones_bd) * inv
    yc = y - mu
    var = _segsum(yc * yc, ones_bd) * inv
    yn = yc * lax.rsqrt(var + A_LNX_EPS) * lg[...] + lb[...]
    o1 = ((yn + bonus[0]) * g[0]).astype(BF16)
    mix = _dot(o1, w1_ref[...]) + _dot(o2_ref[0], w2_ref[...])
    out_ref[0] = x_ref[0] + mod_ref[0, 0][2:3] * mix


def _rwkv_proj_residual(yf, yr, bonus, g, lnx_g, lnx_b, o2, x, mod, w_out, tm, ctx_tiles):
    bsz, rows, w1 = yf.shape
    w2 = o2.shape[2]
    d = x.shape[2]
    blk = lambda w: pl.BlockSpec((1, tm, w), lambda b, t: (b, t, 0))
    const = lambda shape: pl.BlockSpec(shape, lambda b, t: (0,) * len(shape))
    return pl.pallas_call(
        _rwkv_proj_kernel,
        grid=(bsz, rows // tm),
        in_specs=[blk(w1), blk(w1), blk(w1), blk(w1), const((1, w1)), const((1, w1)),
                  const((w1, w1)), blk(w2), blk(d),
                  pl.BlockSpec((1, 1, N_MOD, d),
                               lambda b, t: (b, jnp.where(t >= ctx_tiles, 1, 0), 0, 0)),
                  const((w1, d)), const((w2, d))],
        out_specs=blk(d),
        out_shape=jax.ShapeDtypeStruct((bsz, rows, d), F32),
        compiler_params=_cparams(("parallel", "parallel")),
        name="rwkv_proj_residual",
    )(yf, yr, bonus, g, lnx_g.reshape(1, w1), lnx_b.reshape(1, w1), _block_ones(w1, HEAD_DIM),
      o2, x, mod, w_out[:w1].astype(BF16), w_out[w1:].astype(BF16))


def _merge_exchange_network(n):
    pairs = []
    p = 1
    while p < n:
        k = p
        while k >= 1:
            for j in range(k % p, n - k, 2 * k):
                for i in range(min(k, n - j - k)):
                    if (i + j) // (2 * p) == (i + j + k) // (2 * p):
                        pairs.append((i + j, i + j + k))
            k //= 2
        p *= 2
    return pairs


def _top16_rows(s):
    nslab = s.shape[0] // SUBLANES
    slabs = [s[k * SUBLANES:(k + 1) * SUBLANES] for k in range(nslab)]
    for a, b in _merge_exchange_network(nslab):
        slabs[a], slabs[b] = jnp.maximum(slabs[a], slabs[b]), jnp.minimum(slabs[a], slabs[b])
    rows = []
    for t in range(PK_TOPK):
        m = jnp.max(slabs[0], axis=0, keepdims=True)
        rows.append(m)
        hit = slabs[0] >= m
        for k in range(nslab - 1 - t):
            slabs[k] = jnp.where(hit, slabs[k + 1], slabs[k])
    return rows


_PAIR_IDX = [(i, j) for i in range(PK_TOPK) for j in range(PK_TOPK) if (i + 1) * (j + 1) <= PK_TOPK]
BF16_SUBLANES = 2 * SUBLANES
PEER_SUB_E1 = 1
PEER_STEP_E1 = 16


def _rows_bf16(row):
    packed = jnp.broadcast_to(row, (BF16_SUBLANES, row.shape[1])).astype(BF16)
    return jnp.tile(packed, (N_KEYS // BF16_SUBLANES, 1))


def _peer_kernel(x_ref, g_ref, mod_ref, pqh_ref, pql_ref, sk1h_ref, sk1l_ref, sk2h_ref, sk2l_ref,
                 pu_ref, pvt_ref, gf_ref,
                 out_ref, ht_ref, r2_ref, e2_ref, c1_ref, w1_ref, s_ref, cand_ref, acc_ref,
                 *, tm, e1_per_blk, final_norm):
    eb = pl.program_id(2)

    @pl.when(eb == 0)
    def _():
        h = _norm_mod(x_ref[0], g_ref[...], mod_ref[0, 0], 3)
        ht = h.T
        hhi, hlo = _split_bf16(ht)
        ht_ref[...] = hhi
        qt = _mm3((pqh_ref[...], pql_ref[...]), (hhi, hlo))
        qs = [_split_bf16(qt[i * PK_HALF:(i + 1) * PK_HALF]) for i in range(2 * PK_HEADS)]
        for hd in range(PK_HEADS):
            s_ref[2 * hd] = _mm3((sk1h_ref[hd], sk1l_ref[hd]), qs[2 * hd])
            s_ref[2 * hd + 1] = _mm3((sk2h_ref[hd], sk2l_ref[hd]), qs[2 * hd + 1])
        for hd in range(PK_HEADS):
            s1 = s_ref[2 * hd]
            s2 = s_ref[2 * hd + 1]
            v1 = _top16_rows(s1)
            v2 = _top16_rows(s2)
            cand_ref[...] = jnp.full_like(cand_ref, NEG_INF)
            for n, (i, j) in enumerate(_PAIR_IDX):
                cand_ref[n:n + 1, :] = v1[i] + v2[j]
            cand = cand_ref[...]
            top = v1[0] + v2[0]
            zsum = jnp.zeros_like(top)
            tau = top
            for _ in range(PK_TOPK):
                m = jnp.max(cand, axis=0, keepdims=True)
                zsum = zsum + jnp.exp(m - top)
                tau = m
                cand = jnp.where(cand >= m, NEG_INF, cand)
            rank2 = jnp.full((N_KEYS, tm), float(PK_TOPK), F32)
            for j in reversed(range(PK_TOPK)):
                rank2 = jnp.where(s2 >= v2[j], float(j), rank2)
            count1 = jnp.zeros((N_KEYS, tm), F32)
            for i in reversed(range(PK_TOPK)):
                cnt = jnp.zeros_like(top)
                for j in range(PK_TOPK // (i + 1)):
                    cnt = cnt + jnp.where(v1[i] + v2[j] >= tau, 1.0, 0.0)
                count1 = jnp.where(s1 >= v1[i], cnt, count1)
            r2_ref[hd] = rank2.astype(BF16)
            c1_ref[hd] = count1
            e2_ref[hd] = jnp.exp(s2 - v2[0]).astype(BF16)
            w1_ref[hd] = jnp.exp(s1 - v1[0]) / zsum
        acc_ref[...] = jnp.zeros_like(acc_ref)

    sub = PEER_SUB_E1 * N_KEYS
    nsub = e1_per_blk // PEER_SUB_E1
    ht = ht_ref[...]

    def up(i):
        return _dot(pu_ref[i * sub:(i + 1) * sub, :], ht)

    def activation(i, hu):
        acts = []
        for jj in range(PEER_SUB_E1):
            e1 = eb * e1_per_blk + PEER_SUB_E1 * i + jj
            gate = jnp.zeros((N_KEYS, tm), BF16)
            for hd in range(PK_HEADS):
                count = _rows_bf16(c1_ref[hd, pl.ds(e1, 1), :])
                weight = _rows_bf16(w1_ref[hd, pl.ds(e1, 1), :])
                gate = gate + jnp.where(r2_ref[hd] < count, e2_ref[hd] * weight, 0)
            u = hu[jj * N_KEYS:(jj + 1) * N_KEYS]
            act = 0.5 * u * (1.0 + lax.erf(u * (1.0 / math.sqrt(2.0))))
            acts.append(act.astype(BF16) * gate)
        return jnp.concatenate(acts, axis=0)

    hus = [up(i) for i in range(nsub)]
    acts = [activation(i, hus[i]) for i in range(nsub)]
    acc_ref[...] += _dot(pvt_ref[...], jnp.concatenate(acts, axis=0))

    @pl.when(eb == pl.num_programs(2) - 1)
    def _():
        y = x_ref[0] + mod_ref[0, 0][5:6] * acc_ref[...].T
        if final_norm:
            ms = jnp.mean(y * y, axis=-1, keepdims=True)
            y = y * lax.rsqrt(ms + NORM_EPS) * gf_ref[...]
        out_ref[0] = y


def _peer(x, g, mod, p, tm, tile_off, ctx_tiles, final_norm, norm_f):
    bsz, rows, d = x.shape
    n_exp = p["pu"].shape[0]
    e1_per_blk = PEER_STEP_E1
    eblk = e1_per_blk * N_KEYS
    pqt = p["pq"].T
    pqh = pqt.astype(BF16)
    pql = (pqt - pqh.astype(F32)).astype(BF16)
    nq = pqt.shape[0]
    kern = functools.partial(_peer_kernel, tm=tm, e1_per_blk=e1_per_blk, final_norm=final_norm)
    const = lambda shape: pl.BlockSpec(shape, lambda b, t, e: (0,) * len(shape))
    return pl.pallas_call(
        kern,
        grid=(bsz, rows // tm, n_exp // eblk),
        in_specs=[pl.BlockSpec((1, tm, d), lambda b, t, e: (b, t, 0)),
                  const((1, d)),
                  pl.BlockSpec((1, 1, N_MOD, d),
                               lambda b, t, e: (b, jnp.where(t + tile_off >= ctx_tiles, 1, 0), 0, 0)),
                  const((nq, d)), const((nq, d)),
                  const((PK_HEADS, N_KEYS, PK_HALF)), const((PK_HEADS, N_KEYS, PK_HALF)),
                  const((PK_HEADS, N_KEYS, PK_HALF)), const((PK_HEADS, N_KEYS, PK_HALF)),
                  pl.BlockSpec((eblk, d), lambda b, t, e: (e, 0)),
                  pl.BlockSpec((d, eblk), lambda b, t, e: (0, e)),
                  const((1, d))],
        out_specs=pl.BlockSpec((1, tm, d), lambda b, t, e: (b, t, 0)),
        out_shape=jax.ShapeDtypeStruct((bsz, rows, d), F32),
        scratch_shapes=[pltpu.VMEM((d, tm), BF16),
                        pltpu.VMEM((PK_HEADS, N_KEYS, tm), BF16),
                        pltpu.VMEM((PK_HEADS, N_KEYS, tm), BF16),
                        pltpu.VMEM((PK_HEADS, N_KEYS, tm), F32),
                        pltpu.VMEM((PK_HEADS, N_KEYS, tm), F32),
                        pltpu.VMEM((2 * PK_HEADS, N_KEYS, tm), F32),
                        pltpu.VMEM((SUBLANES * pl.cdiv(len(_PAIR_IDX), SUBLANES), tm), F32),
                        pltpu.VMEM((d, tm), F32)],
        compiler_params=_cparams(("parallel", "parallel", "arbitrary")),
        name="peer",
    )(x, g.reshape(1, d), mod, pqh, pql, *_split_bf16(p["sk1"]), *_split_bf16(p["sk2"]),
      p["pu"].astype(BF16), p["pv"].astype(BF16).T, norm_f.reshape(1, d))


def _rope_tables(rows, ctx_len):
    t = jnp.arange(rows - ctx_len, dtype=jnp.int32)
    inv = ROPE_THETA ** (-jnp.arange(0, ROPE_AXIS_DIM, 2, dtype=F32) / ROPE_AXIS_DIM)
    ang_r = (t // GRID_W).astype(F32)[:, None] * inv
    ang_c = (t % GRID_W).astype(F32)[:, None] * inv
    cos = jnp.concatenate([jnp.cos(ang_r)] * 2 + [jnp.cos(ang_c)] * 2, axis=1)
    sin = jnp.concatenate([-jnp.sin(ang_r), jnp.sin(ang_r), -jnp.sin(ang_c), jnp.sin(ang_c)], axis=1)
    cos = jnp.concatenate([jnp.ones((ctx_len, HEAD_DIM), F32), cos], axis=0)
    sin = jnp.concatenate([jnp.zeros((ctx_len, HEAD_DIM), F32), sin], axis=0)
    return jnp.tile(cos, (1, 2)), jnp.tile(sin, (1, 2))


def _head_norm_rope(x, gain, ones_bd, cos, sin):
    ms = _segsum(x * x, ones_bd) * (1.0 / HEAD_DIM)
    y = x * lax.rsqrt(ms + NORM_EPS) * gain
    outs = []
    half = ROPE_AXIS_DIM // 2
    lane = lax.broadcasted_iota(jnp.int32, (1, LANES), 1)
    first_half = (lane % ROPE_AXIS_DIM) < half
    for i in range(x.shape[1] // LANES):
        yc = y[:, i * LANES:(i + 1) * LANES]
        partner = jnp.where(first_half, pltpu.roll(yc, LANES - half, 1), pltpu.roll(yc, half, 1))
        outs.append(yc * cos + partner * sin)
    return jnp.concatenate(outs, axis=1)


def _attn_prep_kernel(z_ref, qn_ref, kn_ref, cos_ref, sin_ref, onesq_ref, onesk_ref,
                      q_o, k_o, v_o):
    z = z_ref[0]
    cos = cos_ref[...]
    sin = sin_ref[...]
    q = _head_norm_rope(z[:, :C_WIDTH], qn_ref[...], onesq_ref[...], cos, sin)
    k = _head_norm_rope(z[:, C_WIDTH:C_WIDTH + KV_WIDTH], kn_ref[...], onesk_ref[...], cos, sin)
    q_o[0] = (q * (ATTN_SCALE * math.log2(math.e))).T.astype(q_o.dtype)
    for j in range(C_KV_HEADS):
        k_o[0, j] = k[:, j * HEAD_DIM:(j + 1) * HEAD_DIM].astype(k_o.dtype)
    v_o[0] = z[:, C_WIDTH + KV_WIDTH:C_WIDTH + 2 * KV_WIDTH].T.astype(v_o.dtype)


def _attn_prep(z, q_norm, k_norm, ctx_len, tm):
    bsz, rows, ncol = z.shape
    cos, sin = _rope_tables(rows, ctx_len)
    qn = jnp.tile(q_norm, C_HEADS).reshape(1, C_WIDTH)
    kn = jnp.tile(k_norm, C_KV_HEADS).reshape(1, KV_WIDTH)
    const = lambda shape: pl.BlockSpec(shape, lambda b, t: (0,) * len(shape))
    return pl.pallas_call(
        _attn_prep_kernel,
        grid=(bsz, rows // tm),
        in_specs=[pl.BlockSpec((1, tm, ncol), lambda b, t: (b, t, 0)),
                  const((1, C_WIDTH)), const((1, KV_WIDTH)),
                  pl.BlockSpec((tm, LANES), lambda b, t: (t, 0)),
                  pl.BlockSpec((tm, LANES), lambda b, t: (t, 0)),
                  const((C_WIDTH, C_WIDTH)), const((KV_WIDTH, KV_WIDTH))],
        out_specs=[pl.BlockSpec((1, C_WIDTH, tm), lambda b, t: (b, 0, t)),
                   pl.BlockSpec((1, C_KV_HEADS, tm, HEAD_DIM), lambda b, t: (b, 0, t, 0)),
                   pl.BlockSpec((1, KV_WIDTH, tm), lambda b, t: (b, 0, t))],
        out_shape=[jax.ShapeDtypeStruct((bsz, C_WIDTH, rows), BF16),
                   jax.ShapeDtypeStruct((bsz, C_KV_HEADS, rows, HEAD_DIM), BF16),
                   jax.ShapeDtypeStruct((bsz, KV_WIDTH, rows), BF16)],
        compiler_params=_cparams(("parallel", "parallel")),
        name="attn_prep",
    )(z, qn, kn, cos, sin, _block_ones(C_WIDTH, HEAD_DIM), _block_ones(KV_WIDTH, HEAD_DIM))


def _attn_kernel(qt_ref, k_ref, vt_ref, o_ref, m_ref, l_ref, acc_ref, *, kc):
    kt = pl.program_id(3)

    @pl.when(kt == 0)
    def _():
        m_ref[...] = jnp.full_like(m_ref, NEG_INF)
        l_ref[...] = jnp.zeros_like(l_ref)
        acc_ref[...] = jnp.zeros_like(acc_ref)

    nkv = k_ref.shape[1]
    grp = range(nkv * C_GROUP)
    qts = [qt_ref[0, g * HEAD_DIM:(g + 1) * HEAD_DIM, :] for g in grp]
    nck = k_ref.shape[2] // kc

    def scores(c):
        kbs = [k_ref[0, j, c * kc:(c + 1) * kc, :] for j in range(nkv)]
        return [_dot(kbs[g // C_GROUP], qts[g]) for g in grp]

    m = [m_ref[g] for g in grp]
    l = [l_ref[g] for g in grp]
    acc = [acc_ref[g] for g in grp]
    s_next = scores(0)
    for c in range(nck):
        s = s_next
        if c + 1 < nck:
            s_next = scores(c + 1)
        vts = [vt_ref[0, j * HEAD_DIM:(j + 1) * HEAD_DIM, c * kc:(c + 1) * kc]
               for j in range(nkv)]
        m_new = [jnp.maximum(m[g], jnp.max(s[g], axis=0, keepdims=True)) for g in grp]
        p = [jnp.exp2(s[g] - m_new[g]) for g in grp]
        pv = [_dot(vts[g // C_GROUP], p[g].astype(BF16)) for g in grp]
        for g in grp:
            alpha = jnp.exp2(m[g] - m_new[g])
            l[g] = alpha * l[g] + jnp.sum(p[g], axis=0, keepdims=True)
            acc[g] = alpha * acc[g] + pv[g]
        m = m_new
    for g in grp:
        m_ref[g] = m[g]
        l_ref[g] = l[g]
        acc_ref[g] = acc[g]

    @pl.when(kt == pl.num_programs(3) - 1)
    def _():
        o_ref[0] = jnp.concatenate(
            [acc_ref[g] / l_ref[g] for g in grp], axis=0).astype(o_ref.dtype)


ATTN_MAX_KEY_TILE = 8448
ATTN_MAX_KEY_CHUNK = 256
ATTN_KV_PER_STEP = 1


def _key_tile(rows, limit=ATTN_MAX_KEY_TILE):
    return max(t for t in range(LANES, min(rows, limit) + 1, LANES) if rows % t == 0)


def _attention(qt, k, vt, ctx_len, tq, tk):
    bsz, _, rows = qt.shape
    seq = rows - ctx_len
    qoff = ctx_len // tq
    nkv = ATTN_KV_PER_STEP
    nq = nkv * C_GROUP
    gw = nq * HEAD_DIM
    kern = functools.partial(_attn_kernel, kc=_key_tile(tk, ATTN_MAX_KEY_CHUNK))
    return pl.pallas_call(
        kern,
        grid=(bsz, C_KV_HEADS // nkv, seq // tq, rows // tk),
        in_specs=[pl.BlockSpec((1, gw, tq), lambda b, j, i, kk: (b, j, i + qoff)),
                  pl.BlockSpec((1, nkv, tk, HEAD_DIM), lambda b, j, i, kk: (b, j, kk, 0)),
                  pl.BlockSpec((1, nkv * HEAD_DIM, tk), lambda b, j, i, kk: (b, j, kk))],
        out_specs=pl.BlockSpec((1, gw, tq), lambda b, j, i, kk: (b, j, i)),
        out_shape=jax.ShapeDtypeStruct((bsz, C_WIDTH, seq), BF16),
        scratch_shapes=[pltpu.VMEM((nq, 1, tq), F32),
                        pltpu.VMEM((nq, 1, tq), F32),
                        pltpu.VMEM((nq, HEAD_DIM, tq), F32)],
        compiler_params=_cparams(("parallel", "parallel", "parallel", "arbitrary")),
        name="attention",
    )(qt, k, vt)


def _conv_kernel(zc_ref, zp_ref, zn_ref, w_ref, b_ref, lg_ref, lb_ref, o_ref, ybuf, *, tm, halo):
    t = pl.program_id(1)
    nt = pl.num_programs(1)

    def glu(u):
        return u[:, :D_WIDTH] * jax.nn.sigmoid(u[:, D_WIDTH:])

    ybuf[0:halo, :] = jnp.where(t == 0, 0.0, glu(zp_ref[0]))
    ybuf[halo:halo + tm, :] = glu(zc_ref[0])
    ybuf[halo + tm:, :] = jnp.where(t == nt - 1, 0.0, glu(zn_ref[0]))
    acc = jnp.zeros((tm, D_WIDTH), F32)
    for j in range(D_CONV_WIDTH):
        off = halo - D_PAD + j
        acc = acc + w_ref[j:j + 1, :] * ybuf[off:off + tm, :]
    y = acc + b_ref[...]
    mu = jnp.mean(y, axis=-1, keepdims=True)
    yc = y - mu
    var = jnp.mean(yc * yc, axis=-1, keepdims=True)
    yn = yc * lax.rsqrt(var + LN_EPS) * lg_ref[...] + lb_ref[...]
    o_ref[0] = (yn * jax.nn.sigmoid(yn)).astype(o_ref.dtype)


def _conformer_conv(z, dw_w, dw_b, cn_g, cn_b, ctx_len, tm):
    bsz, rows, ncol = z.shape
    seq = rows - ctx_len
    halo = 2 * SUBLANES
    hb = tm // halo
    off = ctx_len // tm
    offh = ctx_len // halo
    nh = seq // halo
    kern = functools.partial(_conv_kernel, tm=tm, halo=halo)
    row = lambda w: pl.BlockSpec((1, w), lambda b, t: (0, 0))
    return pl.pallas_call(
        kern,
        grid=(bsz, seq // tm),
        in_specs=[pl.BlockSpec((1, tm, ncol), lambda b, t: (b, t + off, 0)),
                  pl.BlockSpec((1, halo, ncol),
                               lambda b, t: (b, offh + jnp.maximum(t * hb - 1, 0), 0)),
                  pl.BlockSpec((1, halo, ncol),
                               lambda b, t: (b, offh + jnp.minimum((t + 1) * hb, nh - 1), 0)),
                  pl.BlockSpec((D_CONV_WIDTH, D_WIDTH), lambda b, t: (0, 0)),
                  row(D_WIDTH), row(D_WIDTH), row(D_WIDTH)],
        out_specs=pl.BlockSpec((1, tm, D_WIDTH), lambda b, t: (b, t, 0)),
        out_shape=jax.ShapeDtypeStruct((bsz, seq, D_WIDTH), BF16),
        scratch_shapes=[pltpu.VMEM((tm + 2 * halo, D_WIDTH), F32)],
        compiler_params=_cparams(("parallel", "parallel")),
        name="conformer_conv",
    )(z, z, z, dw_w, dw_b.reshape(1, -1), cn_g.reshape(1, -1), cn_b.reshape(1, -1))


def _forward(x, c, ctx, c_ctx, l0, l1, norm_f):
    bsz, seq, d = x.shape
    ctx_len = ctx.shape[1]
    tm = min(256, ctx_len)
    ctx_tiles = ctx_len // tm
    xs = jnp.concatenate([ctx, x], axis=1)

    mod = _modulation(c, c_ctx, l0["mod_w"], l0["mod_b"])
    z, zf = _normmod_matmul(xs, l0["norm1"], mod, l0["w_in"], tm, ctx_tiles, A_COLS)
    r, v, g, bonus, kk, lwf, kdf, bf, lwr, kdr, br = _rwkv_prep(z, l0, tm, ctx_tiles)
    yf, yr = _rwkv_scan(r, v, kk, lwf, kdf, bf, lwr, kdr, br, ctx_len)
    fcs = _fnet_chan(zf, tm)
    ts_c = min(256, ctx_len)
    ts_l = min(1024, seq)
    f_ctx = _fnet_seq(fcs, 0, ctx_len, ts_c, ts_c)
    f_lat = _fnet_seq(fcs[:, ctx_len:], 0, seq, ts_l, min(512, seq))
    o_fnet = jnp.concatenate([f_ctx, f_lat], axis=1)
    xs = _rwkv_proj_residual(yf, yr, bonus, g, l0["lnx_g"], l0["lnx_b"], o_fnet, xs, mod,
                             l0["w_out"], tm, ctx_tiles)
    xs = _peer(xs, l0["norm2"], mod, l0, tm, 0, ctx_tiles, False, norm_f)

    mod = _modulation(c, c_ctx, l1["mod_w"], l1["mod_b"])
    z, zu = _normmod_matmul(xs, l1["norm1"], mod, l1["w_in"], tm, ctx_tiles,
                            C_WIDTH + 2 * KV_WIDTH)
    qt, k, vt = _attn_prep(z, l1["q_norm"], l1["k_norm"], ctx_len, tm)
    o_attn = _attention(qt, k, vt, ctx_len, tm, _key_tile(ctx_len + seq))
    o_conv = _conformer_conv(zu, l1["dw_w"], l1["dw_b"], l1["cn_g"], l1["cn_b"], ctx_len, tm)
    xl = _proj_residual(o_attn, o_conv, xs, mod, l1["w_out"], tm, ctx_tiles, ctx_tiles,
                        o1_transposed=True)
    return _peer(xl, l1["norm2"], mod, l1, tm, ctx_tiles, ctx_tiles, True, norm_f)


def kernel(x, c, ctx, c_ctx, l0_mod_w, l0_mod_b, l0_norm1, l0_w_in, l0_shift_prev, l0_shift_next, l0_w0, l0_w2, l0_a0, l0_a2, l0_g2, l0_k_k, l0_k_a, l0_r_k, l0_lnx_g, l0_lnx_b, l0_w_out, l0_norm2, l0_pq, l0_sk1, l0_sk2, l0_pu, l0_pv, l1_mod_w, l1_mod_b, l1_norm1, l1_w_in, l1_q_norm, l1_k_norm, l1_dw_w, l1_dw_b, l1_cn_g, l1_cn_b, l1_w_out, l1_norm2, l1_pq, l1_sk1, l1_sk2, l1_pu, l1_pv, norm_f):
    l0 = dict(mod_w=l0_mod_w, mod_b=l0_mod_b, norm1=l0_norm1, w_in=l0_w_in,
              shift_prev=l0_shift_prev, shift_next=l0_shift_next, w0=l0_w0, w2=l0_w2,
              a0=l0_a0, a2=l0_a2, g2=l0_g2, k_k=l0_k_k, k_a=l0_k_a, r_k=l0_r_k,
              lnx_g=l0_lnx_g, lnx_b=l0_lnx_b, w_out=l0_w_out, norm2=l0_norm2,
              pq=l0_pq, sk1=l0_sk1, sk2=l0_sk2, pu=l0_pu, pv=l0_pv)
    l1 = dict(mod_w=l1_mod_w, mod_b=l1_mod_b, norm1=l1_norm1, w_in=l1_w_in,
              q_norm=l1_q_norm, k_norm=l1_k_norm, dw_w=l1_dw_w, dw_b=l1_dw_b,
              cn_g=l1_cn_g, cn_b=l1_cn_b, w_out=l1_w_out, norm2=l1_norm2,
              pq=l1_pq, sk1=l1_sk1, sk2=l1_sk2, pu=l1_pu, pv=l1_pv)
    return _forward(x, c, ctx, c_ctx, l0, l1, norm_f)
```
